```python
import math, functools
import jax, jax.numpy as jnp
from jax import lax
import numpy as np

D_MODEL = 2048
BATCH = 2
SEQ = 4096
DEPTH = 4
DEC_BATCH = 8
DEC_SEQ = 4
PAST_LEN = 16384
PAGE_SIZE = 128

HEAD_DIM = 128
D_MIX = D_MODEL
M_WIDTH = D_MIX // 4
G_WIDTH = D_MIX // 4
N_WIDTH = D_MIX - M_WIDTH - G_WIDTH
H_M = M_WIDTH // HEAD_DIM
H_N = N_WIDTH // HEAD_DIM
H_G = G_WIDTH // HEAD_DIM
KV_GROUPS = 2
Q_PER_KV = H_N // KV_GROUPS
D_FF = 4 * D_MODEL
CONV_W = 4
CHUNK = 64
CMP_STRIDE = 16
CMP_LEN = 2 * CMP_STRIDE
CMP_HID = 256
SEL_BLK = 64
N_SELECT = 16
WINDOW = 512
QBLK = 128
GATE_RANK = 16
GLA_GATE_TEMP = 16.0
ROPE_THETA = 10000.0
EPS = 1e-6
TINY = 1e-30
FORCE_BONUS = 1e3
NEG_BIG = -1e9
SPLIT_SIZES = (2 * M_WIDTH, M_WIDTH, M_WIDTH, 2 * H_M,
               N_WIDTH, 6 * KV_GROUPS * HEAD_DIM, 3 * H_N,
               G_WIDTH, G_WIDTH, G_WIDTH, G_WIDTH, GATE_RANK)
IN_COLS = sum(SPLIT_SIZES)
SPLIT_OFFSETS = tuple(int(o) for o in np.cumsum(SPLIT_SIZES)[:-1])

kernel_name = 'hymba_mlstm_nsa_gla_decoder_step'


def rmsnorm(x, g):
    xf = x.astype(jnp.float32)
    y = xf * lax.rsqrt(jnp.mean(xf * xf, axis=-1, keepdims=True) + EPS)
    return (y * g.astype(jnp.float32)).astype(x.dtype)


def rope(x, pos):
    half = x.shape[-1] // 2
    inv_freq = jnp.exp(-math.log(ROPE_THETA) * jnp.arange(half, dtype=jnp.float32) / half)
    ang = pos.astype(jnp.float32)[:, None] * inv_freq[None, :]
    cos = jnp.cos(ang)[:, None, :]
    sin = jnp.sin(ang)[:, None, :]
    xf = x.astype(jnp.float32)
    x1, x2 = xf[..., :half], xf[..., half:]
    return jnp.concatenate([x1 * cos - x2 * sin, x2 * cos + x1 * sin], axis=-1).astype(x.dtype)


def masked_softmax(s, mask):
    s = jnp.where(mask, s, -jnp.inf)
    m = jnp.max(s, axis=-1, keepdims=True)
    m = jnp.where(jnp.isfinite(m), m, 0.0)
    p = jnp.exp(s - m)
    return p / jnp.maximum(jnp.sum(p, axis=-1, keepdims=True), TINY)


def chunked_scan(step, carry, xs, chunk):
    b, t = xs[0].shape[:2]
    c = chunk if t % chunk == 0 else t
    n = t // c
    split = lambda a: jnp.moveaxis(a.reshape((b, n, c) + a.shape[2:]), 1, 0)
    carry, ys = lax.scan(step, carry, tuple(split(a) for a in xs))
    return carry, jnp.moveaxis(ys, 0, 1).reshape((b, t) + ys.shape[3:])


def mlstm_chunk(carry, inp):
    c0, n0, m0 = carry
    q, k, v, li, lf = inp
    L = q.shape[1]
    b = jnp.cumsum(lf, axis=1).transpose(0, 2, 1)
    ig = li.transpose(0, 2, 1)
    causal = jnp.tril(jnp.ones((L, L), dtype=bool))
    logw = jnp.where(causal, b[..., :, None] - b[..., None, :] + ig[..., None, :], -jnp.inf)
    g = b + m0[..., None]
    m_row = jnp.maximum(jnp.max(logw, axis=-1), g)
    w = jnp.exp(logw - m_row[..., None]) * jnp.einsum('blhd,bshd->bhls', q, k)
    inter = jnp.exp(g - m_row)
    num = jnp.einsum('bhls,bshd->bhld', w, v) + inter[..., None] * jnp.einsum('bhed,blhd->bhle', c0, q)
    den = jnp.sum(w, axis=-1) + inter * jnp.einsum('bhd,blhd->bhl', n0, q)
    h = num / jnp.maximum(jnp.abs(den), jnp.exp(-m_row))[..., None]
    bl = b[..., -1]
    a = bl[..., None] - b + ig
    m_new = jnp.maximum(jnp.max(a, axis=-1), bl + m0)
    wa = jnp.exp(a - m_new[..., None])
    wc = jnp.exp(bl + m0 - m_new)
    c_new = wc[..., None, None] * c0 + jnp.einsum('bhs,bshe,bshd->bhed', wa, v, k)
    n_new = wc[..., None] * n0 + jnp.einsum('bhs,bshd->bhd', wa, k)
    return (c_new, n_new, m_new), h.transpose(0, 2, 1, 3)


def gla_chunk(s0, inp):
    q, k, v, la = inp
    L = q.shape[1]
    bc = jnp.cumsum(la, axis=1)
    causal = jnp.tril(jnp.ones((L, L), dtype=bool))[None, :, :, None, None]
    decay = jnp.exp(jnp.where(causal, bc[:, :, None] - bc[:, None, :], -jnp.inf))
    a = jnp.einsum('bthk,bshk,btshk->bhts', q, k, decay)
    o = jnp.einsum('bhts,bshv->bthv', a, v) + jnp.einsum('bthk,bhkv->bthv', q * jnp.exp(bc), s0)
    bl = bc[:, -1]
    s_new = jnp.exp(bl)[..., None] * s0 + jnp.einsum('bshk,bshv->bhkv', k * jnp.exp(bl[:, None] - bc), v)
    return s_new, o


def nsa_compress(x, pe, w1, w2):
    b, l = x.shape[:2]
    nb = l // CMP_STRIDE
    xb = x[:, :nb * CMP_STRIDE].reshape(b, nb, CMP_STRIDE, KV_GROUPS, HEAD_DIM)
    first = jnp.einsum('bnjgd,jde->bnge', xb, w1[:CMP_STRIDE])
    second = jnp.einsum('bnjgd,jde->bnge', xb, w1[CMP_STRIDE:])
    hid = first[:, :-1] + second[:, 1:] + jnp.einsum('jd,jde->e', pe, w1)
    return jnp.einsum('bnge,ed->bngd', jax.nn.gelu(hid), w2)


def nsa_context(rows, cmp_params):
    pe, w1, w2 = cmp_params
    b, l = rows.shape[:2]
    kc = nsa_compress(rows[:, :, 0], pe[0], w1[0], w2[0])
    vc = nsa_compress(rows[:, :, 1], pe[1], w1[1], w2[1])
    nc = kc.shape[1]
    ns = -(-l // SEL_BLK)
    sel = jnp.pad(rows[:, :, 2:4], ((0, 0), (0, ns * SEL_BLK - l), (0, 0), (0, 0), (0, 0)))
    sel = sel.reshape(b, ns, SEL_BLK, 2, KV_GROUPS, HEAD_DIM).transpose(3, 0, 4, 1, 2, 5)
    cmp_start = jnp.arange(nc) * CMP_STRIDE
    cmp_end = cmp_start + CMP_LEN - 1
    sel_start = jnp.arange(ns) * SEL_BLK
    cmp_to_sel = ((cmp_start[:, None] < sel_start[None, :] + SEL_BLK)
                  & (cmp_end[:, None] >= sel_start[None, :])).astype(jnp.float32)
    return kc, vc, cmp_end, cmp_to_sel, sel[0], sel[1]


def nsa_query_block(q, qpos, gates, kc, vc, cmp_end, cmp_to_sel, ksb, vsb, kw, vw, kwpos):
    b, t = q.shape[:2]
    f32 = jnp.float32
    qg = q.reshape(b, t, KV_GROUPS, Q_PER_KV, HEAD_DIM) * (HEAD_DIM ** -0.5)
    s = jnp.einsum('btgrd,bngd->btgrn', qg, kc).astype(f32)
    p_c = masked_softmax(s, (cmp_end[None, :] <= qpos[:, None])[None, :, None, None, :])
    o_c = jnp.einsum('btgrn,bngd->btgrd', p_c.astype(vc.dtype), vc)
    ns = ksb.shape[2]
    imp = jnp.einsum('btgn,nj->btgj', jnp.sum(p_c, axis=3), cmp_to_sel)
    j = jnp.arange(ns)
    cur = qpos // SEL_BLK
    elig = (j * SEL_BLK)[None, :] <= qpos[:, None]
    forced = (j[None, :] == 0) | (j[None, :] == cur[:, None]) | (j[None, :] == cur[:, None] - 1)
    score = jnp.where(elig[None, :, None, :], imp + jnp.where(forced, FORCE_BONUS, 0.0)[None, :, None, :], NEG_BIG)
    _, idx = lax.top_k(score, min(N_SELECT, ns))
    n = idx.shape[-1]
    idx_g = idx.transpose(0, 2, 1, 3).reshape(b, KV_GROUPS, t * n)
    take = jax.vmap(jax.vmap(lambda blocks, ids: blocks[ids]))
    gk = take(ksb, idx_g).reshape(b, KV_GROUPS, t, n * SEL_BLK, HEAD_DIM)
    gv = take(vsb, idx_g).reshape(b, KV_GROUPS, t, n * SEL_BLK, HEAD_DIM)
    kpos = (idx[..., None] * SEL_BLK + jnp.arange(SEL_BLK)).reshape(b, t, KV_GROUPS, n * SEL_BLK)
    s = jnp.einsum('btgrd,bgtmd->btgrm', qg, gk).astype(f32)
    p_s = masked_softmax(s, (kpos <= qpos[None, :, None, None])[:, :, :, None, :])
    o_s = jnp.einsum('btgrm,bgtmd->btgrd', p_s.astype(gv.dtype), gv)
    s = jnp.einsum('btgrd,bkgd->btgrk', qg, kw).astype(f32)
    d = qpos[:, None] - kwpos[None, :]
    wmask = (kwpos[None, :] >= 0) & (d >= 0) & (d < WINDOW)
    p_w = masked_softmax(s, wmask[None, :, None, None, :])
    o_w = jnp.einsum('btgrk,bkgd->btgrd', p_w.astype(vw.dtype), vw)
    g = jax.nn.sigmoid(gates.astype(f32)).reshape(b, t, KV_GROUPS, Q_PER_KV, 3)
    o = g[..., 0:1] * o_c + g[..., 1:2] * o_s + g[..., 2:3] * o_w
    return o.reshape(b, t, H_N * HEAD_DIM).astype(q.dtype)


def nsa_prep(n_q, n_kv, n_g, pos):
    b, t = n_q.shape[:2]
    q = rope(n_q.reshape(b, t, H_N, HEAD_DIM), pos)
    kv = n_kv.reshape(b, t, 6, KV_GROUPS, HEAD_DIM)
    rows = jnp.stack([rope(kv[:, :, 0], pos), kv[:, :, 1], rope(kv[:, :, 2], pos), kv[:, :, 3]], axis=2)
    win = jnp.stack([rope(kv[:, :, 4], pos), kv[:, :, 5]], axis=2)
    return q, n_g.reshape(b, t, H_N, 3), rows, win


def nsa_prompt(n_q, n_kv, n_g, cmp_params):
    b, s = n_q.shape[:2]
    pos = jnp.arange(s)
    q, gates, rows, win = nsa_prep(n_q, n_kv, n_g, pos)
    kc, vc, cmp_end, cmp_to_sel, ksb, vsb = nsa_context(rows, cmp_params)
    nb = s // QBLK
    kwp = jnp.pad(win, ((0, 0), (WINDOW, 0), (0, 0), (0, 0), (0, 0)))

    def one_block(args):
        qi, gi, bi = args
        start = bi * QBLK
        qpos = start + jnp.arange(QBLK)
        band = lax.dynamic_slice_in_dim(kwp, start, WINDOW + QBLK, axis=1)
        kwpos = start - WINDOW + jnp.arange(WINDOW + QBLK)
        return nsa_query_block(qi, qpos, gi, kc, vc, cmp_end, cmp_to_sel, ksb, vsb,
                               band[:, :, 0], band[:, :, 1], kwpos)

    blocks = lambda a: jnp.moveaxis(a.reshape((b, nb, QBLK) + a.shape[2:]), 1, 0)
    out = lax.map(one_block, (blocks(q), blocks(gates), jnp.arange(nb)))
    out = jnp.moveaxis(out, 0, 1).reshape(b, s, H_N * HEAD_DIM)
    return out, (rows, win[:, -min(WINDOW, s):])


def nsa_sample(n_q, n_kv, n_g, past_rows, win_buf, cmp_params):
    b, t = n_q.shape[:2]
    past_len = past_rows.shape[1]
    pos = past_len + jnp.arange(t)
    q, gates, rows, win = nsa_prep(n_q, n_kv, n_g, pos)
    full = jnp.concatenate([past_rows.astype(rows.dtype), rows], axis=1)
    kc, vc, cmp_end, cmp_to_sel, ksb, vsb = nsa_context(full, cmp_params)
    wl = win_buf.shape[1]
    kw_all = jnp.concatenate([win_buf.astype(win.dtype), win], axis=1)
    kwpos = past_len - wl + jnp.arange(wl + t)
    out = nsa_query_block(q, pos, gates, kc, vc, cmp_end, cmp_to_sel, ksb, vsb,
                          kw_all[:, :, 0], kw_all[:, :, 1], kwpos)
    return out, (rows, kw_all[:, -wl:])


def to_heads(a, n_heads):
    return a.reshape(a.shape[0], a.shape[1], n_heads, -1).astype(jnp.float32)


def mixer_block(h, conv_buf, c0, n0, m0, s0, nsa_fn, w_in, conv_w, m_gate_b, m_norm, g_w2, g_b, g_norm, w_out):
    b, t, _ = h.shape
    f32 = jnp.float32
    (u_qk, m_v, m_o, m_if, n_q, n_kv, n_g,
     g_q, g_k, g_v, g_r, g_lr) = jnp.split(h @ w_in, SPLIT_OFFSETS, axis=-1)
    u = jnp.concatenate([conv_buf.astype(u_qk.dtype), u_qk], axis=1)
    conv = conv_w[0] * u[:, 0:t]
    for j in range(1, CONV_W):
        conv = conv + conv_w[j] * u[:, j:j + t]
    conv = jax.nn.silu(conv)
    new_conv = u[:, t:]
    q_m, k_m = jnp.split(conv, 2, axis=-1)
    ig, fg = jnp.split(m_if.astype(f32) + m_gate_b.reshape(-1).astype(f32), 2, axis=-1)
    (c, n, m), hm = chunked_scan(
        mlstm_chunk, (c0.astype(f32), n0.astype(f32), m0.astype(f32)),
        (to_heads(q_m, H_M), to_heads(k_m, H_M) * (HEAD_DIM ** -0.5), to_heads(m_v, H_M),
         ig, jax.nn.log_sigmoid(fg)), CHUNK)
    hm = (rmsnorm(hm, m_norm.reshape(H_M, HEAD_DIM)) * jax.nn.sigmoid(to_heads(m_o, H_M))).reshape(b, t, M_WIDTH)
    hn, nsa_state = nsa_fn(n_q, n_kv, n_g)
    la = jax.nn.log_sigmoid(g_lr.astype(f32) @ g_w2.astype(f32) + g_b.astype(f32)) / GLA_GATE_TEMP
    s, hg = chunked_scan(
        gla_chunk, s0.astype(f32),
        (to_heads(g_q, H_G) * (HEAD_DIM ** -0.5), to_heads(g_k, H_G), to_heads(g_v, H_G),
         la.reshape(b, t, H_G, HEAD_DIM)), CHUNK)
    hg = rmsnorm(hg, g_norm.reshape(H_G, HEAD_DIM)).reshape(b, t, G_WIDTH) * jax.nn.silu(g_r.astype(f32))
    mix = jnp.concatenate([hm.astype(h.dtype), hn, hg.astype(h.dtype)], axis=-1)
    return mix @ w_out, new_conv, c, n, m, s, nsa_state


def sq_relu_ffn(x, w1, w2):
    return jnp.square(jax.nn.relu(x @ w1)) @ w2


def trunk_layer(x, conv_buf, c0, n0, m0, s0, nsa_fn, g_norms, w_in, conv_w, m_gate_b, m_norm,
                g_w2, g_b, g_norm, w_out, w_ff1, w_ff2):
    y, new_conv, c, n, m, s, nsa_state = mixer_block(
        rmsnorm(x, g_norms[0]), conv_buf, c0, n0, m0, s0, nsa_fn,
        w_in, conv_w, m_gate_b, m_norm, g_w2, g_b, g_norm, w_out)
    x = x + rmsnorm(y, g_norms[1])
    x = x + rmsnorm(sq_relu_ffn(rmsnorm(x, g_norms[2]), w_ff1, w_ff2), g_norms[3])
    return x, (nsa_state[0], nsa_state[1], c, n, m, new_conv, s)


def setup_inputs(seed: int = 0) -> dict:
    key = jax.random.key(seed)
    ks = jax.random.split(key, 32)
    f32 = jnp.float32
    n_pages = PAST_LEN // PAGE_SIZE
    n_used = DEC_BATCH * n_pages
    n_pool = n_used + max(1, n_used // 4)
    win_len = min(WINDOW, PAST_LEN)

    def nrm(k, shape, scale=1.0):
        return scale * jax.random.normal(k, shape, f32)

    page_table = jax.random.permutation(ks[2], n_pool)[:n_used].reshape(DEC_BATCH, n_pages).astype(jnp.int32)
    gate_b = jnp.concatenate([nrm(ks[10], (DEPTH, 1, H_M), 0.1),
                              3.0 + 3.0 * jax.random.uniform(ks[11], (DEPTH, 1, H_M), f32)], axis=1)
    return {
        'x_prompt': nrm(ks[0], (BATCH, SEQ, D_MODEL)),
        'x_sample': nrm(ks[1], (DEC_BATCH, DEC_SEQ, D_MODEL)),
        'cache_nsa_kv': nrm(ks[3], (DEPTH, n_pool, PAGE_SIZE, 4, KV_GROUPS, HEAD_DIM)),
        'state_nsa_win': nrm(ks[4], (DEPTH, DEC_BATCH, win_len, 2, KV_GROUPS, HEAD_DIM)),
        'state_mlstm_C': nrm(ks[5], (DEPTH, DEC_BATCH, H_M, HEAD_DIM, HEAD_DIM)),
        'state_mlstm_n': nrm(ks[6], (DEPTH, DEC_BATCH, H_M, HEAD_DIM)),
        'state_mlstm_m': nrm(ks[7], (DEPTH, DEC_BATCH, H_M), 0.5),
        'state_mlstm_conv': nrm(ks[8], (DEPTH, DEC_BATCH, CONV_W - 1, 2 * M_WIDTH)),
        'state_gla_S': nrm(ks[9], (DEPTH, DEC_BATCH, H_G, HEAD_DIM, HEAD_DIM)),
        'page_table': page_table,
        'norms': 1.0 + nrm(ks[12], (DEPTH, 4, D_MODEL), 0.05),
        'w_in': nrm(ks[13], (DEPTH, D_MODEL, IN_COLS), D_MODEL ** -0.5),
        'mlstm_conv_w': nrm(ks[14], (DEPTH, CONV_W, 2 * M_WIDTH), CONV_W ** -0.5),
        'mlstm_gate_b': gate_b,
        'mlstm_norm': 1.0 + nrm(ks[15], (DEPTH, M_WIDTH), 0.05),
        'nsa_cmp_pe': nrm(ks[16], (DEPTH, 2, CMP_LEN, HEAD_DIM), 0.5),
        'nsa_cmp_w1': nrm(ks[17], (DEPTH, 2, CMP_LEN, HEAD_DIM, CMP_HID), (CMP_LEN * HEAD_DIM) ** -0.5),
        'nsa_cmp_w2': nrm(ks[18], (DEPTH, 2, CMP_HID, HEAD_DIM), CMP_HID ** -0.5),
        'gla_gate_w2': nrm(ks[19], (DEPTH, GATE_RANK, G_WIDTH), GATE_RANK ** -0.5),
        'gla_gate_b': nrm(ks[20], (DEPTH, G_WIDTH), 0.1),
        'gla_norm': 1.0 + nrm(ks[21], (DEPTH, G_WIDTH), 0.05),
        'w_out': nrm(ks[22], (DEPTH, D_MIX, D_MODEL), D_MIX ** -0.5),
        'w_ff1': nrm(ks[23], (DEPTH, D_MODEL, D_FF), D_MODEL ** -0.5),
        'w_ff2': nrm(ks[24], (DEPTH, D_FF, D_MODEL), D_FF ** -0.5),
    }


def reference(x_prompt, x_sample, cache_nsa_kv, state_nsa_win, state_mlstm_C, state_mlstm_n,
              state_mlstm_m, state_mlstm_conv, state_gla_S, page_table, norms, w_in, mlstm_conv_w,
              mlstm_gate_b, mlstm_norm, nsa_cmp_pe, nsa_cmp_w1, nsa_cmp_w2, gla_gate_w2, gla_gate_b,
              gla_norm, w_out, w_ff1, w_ff2):
    f32 = jnp.float32
    xp, xs = x_prompt, x_sample
    bp = xp.shape[0]
    db = xs.shape[0]
    past_len = page_table.shape[1] * PAGE_SIZE
    conv0 = jnp.zeros((bp, CONV_W - 1, 2 * M_WIDTH), xp.dtype)
    c0 = jnp.zeros((bp, H_M, HEAD_DIM, HEAD_DIM), f32)
    n0 = jnp.zeros((bp, H_M, HEAD_DIM), f32)
    m0 = jnp.zeros((bp, H_M), f32)
    s0 = jnp.zeros((bp, H_G, HEAD_DIM, HEAD_DIM), f32)
    acc_p = [[] for _ in range(7)]
    acc_s = [[] for _ in range(7)]
    for l in range(DEPTH):
        weights = (norms[l], w_in[l], mlstm_conv_w[l], mlstm_gate_b[l], mlstm_norm[l],
                   gla_gate_w2[l], gla_gate_b[l], gla_norm[l], w_out[l], w_ff1[l], w_ff2[l])
        cmp_params = (nsa_cmp_pe[l], nsa_cmp_w1[l], nsa_cmp_w2[l])
        nsa_p = functools.partial(nsa_prompt, cmp_params=cmp_params)
        past = cache_nsa_kv[l, page_table].reshape((db, past_len) + cache_nsa_kv.shape[3:])
        nsa_s = functools.partial(nsa_sample, past_rows=past, win_buf=state_nsa_win[l], cmp_params=cmp_params)
        xp, st_p = trunk_layer(xp, conv0, c0, n0, m0, s0, nsa_p, *weights)
        xs, st_s = trunk_layer(xs, state_mlstm_conv[l], state_mlstm_C[l], state_mlstm_n[l],
                               state_mlstm_m[l], state_gla_S[l], nsa_s, *weights)
        for acc, a in zip(acc_p, st_p):
            acc.append(a.astype(xp.dtype))
        for acc, a in zip(acc_s, st_s):
            acc.append(a.astype(xs.dtype))
    nsa_rows_p, nsa_win_p, mlstm_c_p, mlstm_n_p, mlstm_m_p, mlstm_conv_p, gla_s_p = [jnp.stack(a) for a in acc_p]
    nsa_rows_s, nsa_win_s, mlstm_c_s, mlstm_n_s, mlstm_m_s, mlstm_conv_s, gla_s_s = [jnp.stack(a) for a in acc_s]
    return (xp, xs, nsa_rows_p, nsa_rows_s, nsa_win_p, nsa_win_s, mlstm_c_p, mlstm_c_s,
            mlstm_n_p, mlstm_n_s, mlstm_m_p, mlstm_m_s, mlstm_conv_p, mlstm_conv_s, gla_s_p, gla_s_s)
```

```python
import functools
import math

import jax
import jax.numpy as jnp
import numpy as np
from jax import lax
from jax.experimental import pallas as pl
from jax.experimental.pallas import tpu as pltpu

D_MODEL = 2048
DEPTH = 4
PAGE_SIZE = 128
HEAD_DIM = 128
D_MIX = D_MODEL
M_WIDTH = D_MIX // 4
G_WIDTH = D_MIX // 4
N_WIDTH = D_MIX - M_WIDTH - G_WIDTH
H_M = M_WIDTH // HEAD_DIM
H_N = N_WIDTH // HEAD_DIM
H_G = G_WIDTH // HEAD_DIM
KV_GROUPS = 2
Q_PER_KV = H_N // KV_GROUPS
D_FF = 4 * D_MODEL
CONV_W = 4
CHUNK = 64
CMP_STRIDE = 16
CMP_LEN = 2 * CMP_STRIDE
CMP_HID = 256
SEL_BLK = 64
N_SELECT = 16
WINDOW = 512
QBLK = 128
GATE_RANK = 16
GLA_GATE_TEMP = 16.0
ROPE_THETA = 10000.0
EPS = 1e-6
TINY = 1e-30
FORCE_BONUS = 1e3
NEG_BIG = -1e9
SPLIT_SIZES = (2 * M_WIDTH, M_WIDTH, M_WIDTH, 2 * H_M,
               N_WIDTH, 6 * KV_GROUPS * HEAD_DIM, 3 * H_N,
               G_WIDTH, G_WIDTH, G_WIDTH, G_WIDTH, GATE_RANK)
IN_COLS = sum(SPLIT_SIZES)
SPLIT_OFFSETS = tuple(int(o) for o in np.cumsum(SPLIT_SIZES)[:-1])

F32 = jnp.float32
BF16 = jnp.bfloat16

VMEM_LIMIT_BYTES = 56 * 1024 * 1024
IN_COLS_PAD = 7168
IN_TN = 512


def _rms(x, g):
    return x * lax.rsqrt(jnp.mean(x * x, axis=-1, keepdims=True) + EPS) * g


def _norm_matmul_kernel(x_ref, g_ref, w_ref, o_ref, xn_ref):
    @pl.when(pl.program_id(1) == 0)
    def _():
        xn_ref[...] = _rms(x_ref[...], g_ref[...]).astype(BF16)

    o_ref[...] = jnp.dot(xn_ref[...], w_ref[...], preferred_element_type=F32)


def norm_matmul(x, g, w, tn):
    m, k = x.shape
    n = w.shape[1]
    tm = min(m, 1024)
    return pl.pallas_call(
        _norm_matmul_kernel,
        grid=(m // tm, n // tn),
        in_specs=[pl.BlockSpec((tm, k), lambda i, j: (i, 0)),
                  pl.BlockSpec((1, k), lambda i, j: (0, 0)),
                  pl.BlockSpec((k, tn), lambda i, j: (0, j))],
        out_specs=pl.BlockSpec((tm, tn), lambda i, j: (i, j)),
        out_shape=jax.ShapeDtypeStruct((m, n), F32),
        scratch_shapes=[pltpu.VMEM((tm, k), BF16)],
        compiler_params=pltpu.CompilerParams(
            dimension_semantics=("arbitrary", "arbitrary"), vmem_limit_bytes=VMEM_LIMIT_BYTES),
        name="norm_matmul",
    )(x, g.reshape(1, k), w)


def _matmul_norm_res_kernel(a_ref, w_ref, g_ref, r_ref, o_ref):
    y = jnp.dot(a_ref[...].astype(BF16), w_ref[...], preferred_element_type=F32)
    o_ref[...] = r_ref[...] + _rms(y, g_ref[...])


def matmul_norm_res(a, w, g, r):
    m, k = a.shape
    n = w.shape[1]
    tm = min(m, 512)
    return pl.pallas_call(
        _matmul_norm_res_kernel,
        grid=(m // tm,),
        in_specs=[pl.BlockSpec((tm, k), lambda i: (i, 0)),
                  pl.BlockSpec((k, n), lambda i: (0, 0)),
                  pl.BlockSpec((1, n), lambda i: (0, 0)),
                  pl.BlockSpec((tm, n), lambda i: (i, 0))],
        out_specs=pl.BlockSpec((tm, n), lambda i: (i, 0)),
        out_shape=jax.ShapeDtypeStruct((m, n), F32),
        compiler_params=pltpu.CompilerParams(
            dimension_semantics=("arbitrary",), vmem_limit_bytes=VMEM_LIMIT_BYTES),
        name="matmul_norm_res",
    )(a, w, g.reshape(1, n), r)


def _ffn_kernel(x_ref, g2_ref, w1_ref, w2_ref, g3_ref, o_ref, xn_ref, acc_ref):
    f = pl.program_id(1)

    @pl.when(f == 0)
    def _():
        xn_ref[...] = _rms(x_ref[...], g2_ref[...]).astype(BF16)
        acc_ref[...] = jnp.zeros_like(acc_ref)

    h = jnp.dot(xn_ref[...], w1_ref[...], preferred_element_type=F32)
    a = jnp.square(jnp.maximum(h, 0.0)).astype(BF16)
    acc_ref[...] += jnp.dot(a, w2_ref[...], preferred_element_type=F32)

    @pl.when(f == pl.num_programs(1) - 1)
    def _():
        o_ref[...] = x_ref[...] + _rms(acc_ref[...], g3_ref[...])


def ffn(x, g2, w1, w2, g3):
    m, d = x.shape
    dff = w1.shape[1]
    tm = min(m, 512)
    tf = 512
    return pl.pallas_call(
        _ffn_kernel,
        grid=(m // tm, dff // tf),
        in_specs=[pl.BlockSpec((tm, d), lambda i, f: (i, 0)),
                  pl.BlockSpec((1, d), lambda i, f: (0, 0)),
                  pl.BlockSpec((d, tf), lambda i, f: (0, f)),
                  pl.BlockSpec((tf, d), lambda i, f: (f, 0)),
                  pl.BlockSpec((1, d), lambda i, f: (0, 0))],
        out_specs=pl.BlockSpec((tm, d), lambda i, f: (i, 0)),
        out_shape=jax.ShapeDtypeStruct((m, d), F32),
        scratch_shapes=[pltpu.VMEM((tm, d), BF16), pltpu.VMEM((tm, d), F32)],
        compiler_params=pltpu.CompilerParams(
            dimension_semantics=("arbitrary", "arbitrary"), vmem_limit_bytes=VMEM_LIMIT_BYTES),
        name="ffn",
    )(x, g2.reshape(1, d), w1, w2, g3.reshape(1, d))


def rmsnorm(x, g):
    xf = x.astype(F32)
    y = xf * lax.rsqrt(jnp.mean(xf * xf, axis=-1, keepdims=True) + EPS)
    return (y * g.astype(F32)).astype(x.dtype)


def rope(x, pos):
    half = x.shape[-1] // 2
    inv_freq = jnp.exp(-math.log(ROPE_THETA) * jnp.arange(half, dtype=F32) / half)
    ang = pos.astype(F32)[:, None] * inv_freq[None, :]
    cos = jnp.cos(ang)[:, None, :]
    sin = jnp.sin(ang)[:, None, :]
    xf = x.astype(F32)
    x1, x2 = xf[..., :half], xf[..., half:]
    return jnp.concatenate([x1 * cos - x2 * sin, x2 * cos + x1 * sin], axis=-1).astype(x.dtype)


def masked_softmax(s, mask):
    s = jnp.where(mask, s, -jnp.inf)
    m = jnp.max(s, axis=-1, keepdims=True)
    m = jnp.where(jnp.isfinite(m), m, 0.0)
    p = jnp.exp(s - m)
    return p / jnp.maximum(jnp.sum(p, axis=-1, keepdims=True), TINY)


def chunked_scan(step, carry, xs, chunk):
    b, t = xs[0].shape[:2]
    c = chunk if t % chunk == 0 else t
    n = t // c
    split = lambda a: jnp.moveaxis(a.reshape((b, n, c) + a.shape[2:]), 1, 0)
    carry, ys = lax.scan(step, carry, tuple(split(a) for a in xs))
    return carry, jnp.moveaxis(ys, 0, 1).reshape((b, t) + ys.shape[3:])


def mlstm_chunk(carry, inp):
    c0, n0, m0 = carry
    q, k, v, li, lf = inp
    L = q.shape[1]
    b = jnp.cumsum(lf, axis=1).transpose(0, 2, 1)
    ig = li.transpose(0, 2, 1)
    causal = jnp.tril(jnp.ones((L, L), dtype=bool))
    logw = jnp.where(causal, b[..., :, None] - b[..., None, :] + ig[..., None, :], -jnp.inf)
    g = b + m0[..., None]
    m_row = jnp.maximum(jnp.max(logw, axis=-1), g)
    w = jnp.exp(logw - m_row[..., None]) * jnp.einsum('blhd,bshd->bhls', q, k)
    inter = jnp.exp(g - m_row)
    num = jnp.einsum('bhls,bshd->bhld', w, v) + inter[..., None] * jnp.einsum('bhed,blhd->bhle', c0, q)
    den = jnp.sum(w, axis=-1) + inter * jnp.einsum('bhd,blhd->bhl', n0, q)
    h = num / jnp.maximum(jnp.abs(den), jnp.exp(-m_row))[..., None]
    bl = b[..., -1]
    a = bl[..., None] - b + ig
    m_new = jnp.maximum(jnp.max(a, axis=-1), bl + m0)
    wa = jnp.exp(a - m_new[..., None])
    wc = jnp.exp(bl + m0 - m_new)
    c_new = wc[..., None, None] * c0 + jnp.einsum('bhs,bshe,bshd->bhed', wa, v, k)
    n_new = wc[..., None] * n0 + jnp.einsum('bhs,bshd->bhd', wa, k)
    return (c_new, n_new, m_new), h.transpose(0, 2, 1, 3)


def gla_chunk(s0, inp):
    q, k, v, la = inp
    L = q.shape[1]
    bc = jnp.cumsum(la, axis=1)
    causal = jnp.tril(jnp.ones((L, L), dtype=bool))[None, :, :, None, None]
    decay = jnp.exp(jnp.where(causal, bc[:, :, None] - bc[:, None, :], -jnp.inf))
    a = jnp.einsum('bthk,bshk,btshk->bhts', q, k, decay)
    o = jnp.einsum('bhts,bshv->bthv', a, v) + jnp.einsum('bthk,bhkv->bthv', q * jnp.exp(bc), s0)
    bl = bc[:, -1]
    s_new = jnp.exp(bl)[..., None] * s0 + jnp.einsum('bshk,bshv->bhkv', k * jnp.exp(bl[:, None] - bc), v)
    return s_new, o


def nsa_compress(x, pe, w1, w2):
    b, l = x.shape[:2]
    nb = l // CMP_STRIDE
    xb = x[:, :nb * CMP_STRIDE].reshape(b, nb, CMP_STRIDE, KV_GROUPS, HEAD_DIM)
    first = jnp.einsum('bnjgd,jde->bnge', xb, w1[:CMP_STRIDE])
    second = jnp.einsum('bnjgd,jde->bnge', xb, w1[CMP_STRIDE:])
    hid = first[:, :-1] + second[:, 1:] + jnp.einsum('jd,jde->e', pe, w1)
    return jnp.einsum('bnge,ed->bngd', jax.nn.gelu(hid), w2)


def nsa_context(rows, cmp_params):
    pe, w1, w2 = cmp_params
    b, l = rows.shape[:2]
    kc = nsa_compress(rows[:, :, 0], pe[0], w1[0], w2[0])
    vc = nsa_compress(rows[:, :, 1], pe[1], w1[1], w2[1])
    nc = kc.shape[1]
    ns = -(-l // SEL_BLK)
    sel = jnp.pad(rows[:, :, 2:4], ((0, 0), (0, ns * SEL_BLK - l), (0, 0), (0, 0), (0, 0)))
    sel = sel.reshape(b, ns, SEL_BLK, 2, KV_GROUPS, HEAD_DIM).transpose(3, 0, 4, 1, 2, 5)
    cmp_start = jnp.arange(nc) * CMP_STRIDE
    cmp_end = cmp_start + CMP_LEN - 1
    sel_start = jnp.arange(ns) * SEL_BLK
    cmp_to_sel = ((cmp_start[:, None] < sel_start[None, :] + SEL_BLK)
                  & (cmp_end[:, None] >= sel_start[None, :])).astype(F32)
    return kc, vc, cmp_end, cmp_to_sel, sel[0], sel[1]


def nsa_query_block(q, qpos, gates, kc, vc, cmp_end, cmp_to_sel, ksb, vsb, kw, vw, kwpos):
    b, t = q.shape[:2]
    qg = q.reshape(b, t, KV_GROUPS, Q_PER_KV, HEAD_DIM) * (HEAD_DIM ** -0.5)
    s = jnp.einsum('btgrd,bngd->btgrn', qg, kc).astype(F32)
    p_c = masked_softmax(s, (cmp_end[None, :] <= qpos[:, None])[None, :, None, None, :])
    o_c = jnp.einsum('btgrn,bngd->btgrd', p_c.astype(vc.dtype), vc)
    ns = ksb.shape[2]
    imp = jnp.einsum('btgn,nj->btgj', jnp.sum(p_c, axis=3), cmp_to_sel)
    j = jnp.arange(ns)
    cur = qpos // SEL_BLK
    elig = (j * SEL_BLK)[None, :] <= qpos[:, None]
    forced = (j[None, :] == 0) | (j[None, :] == cur[:, None]) | (j[None, :] == cur[:, None] - 1)
    score = jnp.where(elig[None, :, None, :], imp + jnp.where(forced, FORCE_BONUS, 0.0)[None, :, None, :], NEG_BIG)
    _, idx = lax.top_k(score, min(N_SELECT, ns))
    n = idx.shape[-1]
    idx_g = idx.transpose(0, 2, 1, 3).reshape(b, KV_GROUPS, t * n)
    take = jax.vmap(jax.vmap(lambda blocks, ids: blocks[ids]))
    gk = take(ksb, idx_g).reshape(b, KV_GROUPS, t, n * SEL_BLK, HEAD_DIM)
    gv = take(vsb, idx_g).reshape(b, KV_GROUPS, t, n * SEL_BLK, HEAD_DIM)
    kpos = (idx[..., None] * SEL_BLK + jnp.arange(SEL_BLK)).reshape(b, t, KV_GROUPS, n * SEL_BLK)
    s = jnp.einsum('btgrd,bgtmd->btgrm', qg, gk).astype(F32)
    p_s = masked_softmax(s, (kpos <= qpos[None, :, None, None])[:, :, :, None, :])
    o_s = jnp.einsum('btgrm,bgtmd->btgrd', p_s.astype(gv.dtype), gv)
    s = jnp.einsum('btgrd,bkgd->btgrk', qg, kw).astype(F32)
    d = qpos[:, None] - kwpos[None, :]
    wmask = (kwpos[None, :] >= 0) & (d >= 0) & (d < WINDOW)
    p_w = masked_softmax(s, wmask[None, :, None, None, :])
    o_w = jnp.einsum('btgrk,bkgd->btgrd', p_w.astype(vw.dtype), vw)
    g = jax.nn.sigmoid(gates.astype(F32)).reshape(b, t, KV_GROUPS, Q_PER_KV, 3)
    o = g[..., 0:1] * o_c + g[..., 1:2] * o_s + g[..., 2:3] * o_w
    return o.reshape(b, t, H_N * HEAD_DIM).astype(q.dtype)


def nsa_prep(n_q, n_kv, n_g, pos):
    b, t = n_q.shape[:2]
    q = rope(n_q.reshape(b, t, H_N, HEAD_DIM), pos)
    kv = n_kv.reshape(b, t, 6, KV_GROUPS, HEAD_DIM)
    rows = jnp.stack([rope(kv[:, :, 0], pos), kv[:, :, 1], rope(kv[:, :, 2], pos), kv[:, :, 3]], axis=2)
    win = jnp.stack([rope(kv[:, :, 4], pos), kv[:, :, 5]], axis=2)
    return q, n_g.reshape(b, t, H_N, 3), rows, win


def nsa_prompt(n_q, n_kv, n_g, cmp_params):
    b, s = n_q.shape[:2]
    pos = jnp.arange(s)
    q, gates, rows, win = nsa_prep(n_q, n_kv, n_g, pos)
    kc, vc, cmp_end, cmp_to_sel, ksb, vsb = nsa_context(rows, cmp_params)
    nb = s // QBLK
    kwp = jnp.pad(win, ((0, 0), (WINDOW, 0), (0, 0), (0, 0), (0, 0)))

    def one_block(args):
        qi, gi, bi = args
        start = bi * QBLK
        qpos = start + jnp.arange(QBLK)
        band = lax.dynamic_slice_in_dim(kwp, start, WINDOW + QBLK, axis=1)
        kwpos = start - WINDOW + jnp.arange(WINDOW + QBLK)
        return nsa_query_block(qi, qpos, gi, kc, vc, cmp_end, cmp_to_sel, ksb, vsb,
                               band[:, :, 0], band[:, :, 1], kwpos)

    blocks = lambda a: jnp.moveaxis(a.reshape((b, nb, QBLK) + a.shape[2:]), 1, 0)
    out = lax.map(one_block, (blocks(q), blocks(gates), jnp.arange(nb)))
    out = jnp.moveaxis(out, 0, 1).reshape(b, s, H_N * HEAD_DIM)
    return out, (rows, win[:, -min(WINDOW, s):])


def nsa_sample(n_q, n_kv, n_g, past_rows, win_buf, cmp_params):
    b, t = n_q.shape[:2]
    past_len = past_rows.shape[1]
    pos = past_len + jnp.arange(t)
    q, gates, rows, win = nsa_prep(n_q, n_kv, n_g, pos)
    full = jnp.concatenate([past_rows.astype(rows.dtype), rows], axis=1)
    kc, vc, cmp_end, cmp_to_sel, ksb, vsb = nsa_context(full, cmp_params)
    wl = win_buf.shape[1]
    kw_all = jnp.concatenate([win_buf.astype(win.dtype), win], axis=1)
    kwpos = past_len - wl + jnp.arange(wl + t)
    out = nsa_query_block(q, pos, gates, kc, vc, cmp_end, cmp_to_sel, ksb, vsb,
                          kw_all[:, :, 0], kw_all[:, :, 1], kwpos)
    return out, (rows, kw_all[:, -wl:])


def to_heads(a, n_heads):
    return a.reshape(a.shape[0], a.shape[1], n_heads, -1).astype(F32)


def mixers(z, conv_buf, c0, n0, m0, s0, nsa_fn, conv_w, m_gate_b, m_norm, g_w2, g_b, g_norm):
    b, t, _ = z.shape
    (u_qk, m_v, m_o, m_if, n_q, n_kv, n_g,
     g_q, g_k, g_v, g_r, g_lr) = jnp.split(z[..., :IN_COLS], SPLIT_OFFSETS, axis=-1)
    u = jnp.concatenate([conv_buf.astype(u_qk.dtype), u_qk], axis=1)
    conv = conv_w[0] * u[:, 0:t]
    for j in range(1, CONV_W):
        conv = conv + conv_w[j] * u[:, j:j + t]
    conv = jax.nn.silu(conv)
    new_conv = u[:, t:]
    q_m, k_m = jnp.split(conv, 2, axis=-1)
    ig, fg = jnp.split(m_if.astype(F32) + m_gate_b.reshape(-1).astype(F32), 2, axis=-1)
    (c, n, m), hm = chunked_scan(
        mlstm_chunk, (c0.astype(F32), n0.astype(F32), m0.astype(F32)),
        (to_heads(q_m, H_M), to_heads(k_m, H_M) * (HEAD_DIM ** -0.5), to_heads(m_v, H_M),
         ig, jax.nn.log_sigmoid(fg)), CHUNK)
    hm = (rmsnorm(hm, m_norm.reshape(H_M, HEAD_DIM)) * jax.nn.sigmoid(to_heads(m_o, H_M))).reshape(b, t, M_WIDTH)
    hn, nsa_state = nsa_fn(n_q, n_kv, n_g)
    la = jax.nn.log_sigmoid(g_lr.astype(F32) @ g_w2.astype(F32) + g_b.astype(F32)) / GLA_GATE_TEMP
    s, hg = chunked_scan(
        gla_chunk, s0.astype(F32),
        (to_heads(g_q, H_G) * (HEAD_DIM ** -0.5), to_heads(g_k, H_G), to_heads(g_v, H_G),
         la.reshape(b, t, H_G, HEAD_DIM)), CHUNK)
    hg = rmsnorm(hg, g_norm.reshape(H_G, HEAD_DIM)).reshape(b, t, G_WIDTH) * jax.nn.silu(g_r.astype(F32))
    mix = jnp.concatenate([hm, hn, hg], axis=-1)
    return mix, new_conv, c, n, m, s, nsa_state


def trunk_layer(x, conv_buf, c0, n0, m0, s0, nsa_fn, g_norms, w_in, conv_w, m_gate_b, m_norm,
                g_w2, g_b, g_norm, w_out, w_ff1, w_ff2):
    b, t, d = x.shape
    x2 = x.reshape(b * t, d)
    z = norm_matmul(x2, g_norms[0], w_in, IN_TN).reshape(b, t, -1)
    mix, new_conv, c, n, m, s, nsa_state = mixers(
        z, conv_buf, c0, n0, m0, s0, nsa_fn, conv_w, m_gate_b, m_norm, g_w2, g_b, g_norm)
    x2 = matmul_norm_res(mix.reshape(b * t, -1), w_out, g_norms[1], x2)
    x2 = ffn(x2, g_norms[2], w_ff1, w_ff2, g_norms[3])
    return x2.reshape(b, t, d), (nsa_state[0], nsa_state[1], c, n, m, new_conv, s)


def kernel(x_prompt, x_sample, cache_nsa_kv, state_nsa_win, state_mlstm_C, state_mlstm_n, state_mlstm_m, state_mlstm_conv, state_gla_S, page_table, norms, w_in, mlstm_conv_w, mlstm_gate_b, mlstm_norm, nsa_cmp_pe, nsa_cmp_w1, nsa_cmp_w2, gla_gate_w2, gla_gate_b, gla_norm, w_out, w_ff1, w_ff2):
    xp, xs = x_prompt, x_sample
    bp = xp.shape[0]
    db = xs.shape[0]
    past_len = page_table.shape[1] * PAGE_SIZE
    conv0 = jnp.zeros((bp, CONV_W - 1, 2 * M_WIDTH), xp.dtype)
    c0 = jnp.zeros((bp, H_M, HEAD_DIM, HEAD_DIM), F32)
    n0 = jnp.zeros((bp, H_M, HEAD_DIM), F32)
    m0 = jnp.zeros((bp, H_M), F32)
    s0 = jnp.zeros((bp, H_G, HEAD_DIM, HEAD_DIM), F32)
    w_in_b = jnp.pad(w_in, ((0, 0), (0, 0), (0, IN_COLS_PAD - IN_COLS))).astype(BF16)
    w_out_b = w_out.astype(BF16)
    w_ff1_b = w_ff1.astype(BF16)
    w_ff2_b = w_ff2.astype(BF16)
    acc_p = [[] for _ in range(7)]
    acc_s = [[] for _ in range(7)]
    for l in range(DEPTH):
        weights = (norms[l], w_in_b[l], mlstm_conv_w[l], mlstm_gate_b[l], mlstm_norm[l],
                   gla_gate_w2[l], gla_gate_b[l], gla_norm[l], w_out_b[l], w_ff1_b[l], w_ff2_b[l])
        cmp_params = (nsa_cmp_pe[l], nsa_cmp_w1[l], nsa_cmp_w2[l])
        nsa_p = functools.partial(nsa_prompt, cmp_params=cmp_params)
        past = cache_nsa_kv[l, page_table].reshape((db, past_len) + cache_nsa_kv.shape[3:])
        nsa_s = functools.partial(nsa_sample, past_rows=past, win_buf=state_nsa_win[l], cmp_params=cmp_params)
        xp, st_p = trunk_layer(xp, conv0, c0, n0, m0, s0, nsa_p, *weights)
        xs, st_s = trunk_layer(xs, state_mlstm_conv[l], state_mlstm_C[l], state_mlstm_n[l],
                               state_mlstm_m[l], state_gla_S[l], nsa_s, *weights)
        for acc, a in zip(acc_p, st_p):
            acc.append(a.astype(xp.dtype))
        for acc, a in zip(acc_s, st_s):
            acc.append(a.astype(xs.dtype))
    nsa_rows_p, nsa_win_p, mlstm_c_p, mlstm_n_p, mlstm_m_p, mlstm_conv_p, gla_s_p = [jnp.stack(a) for a in acc_p]
    nsa_rows_s, nsa_win_s, mlstm_c_s, mlstm_n_s, mlstm_m_s, mlstm_conv_s, gla_s_s = [jnp.stack(a) for a in acc_s]
    return (xp, xs, nsa_rows_p, nsa_rows_s, nsa_win_p, nsa_win_s, mlstm_c_p, mlstm_c_s,
            mlstm_n_p, mlstm_n_s, mlstm_m_p, mlstm_m_s, mlstm_conv_p, mlstm_conv_s, gla_s_p, gla_s_s)
```

```python
import functools
import math

import jax
import jax.numpy as jnp
import numpy as np
from jax import lax
from jax.experimental import pallas as pl
from jax.experimental.pallas import tpu as pltpu

D_MODEL = 2048
DEPTH = 4
PAGE_SIZE = 128
HEAD_DIM = 128
D_MIX = D_MODEL
M_WIDTH = D_MIX // 4
G_WIDTH = D_MIX // 4
N_WIDTH = D_MIX - M_WIDTH - G_WIDTH
H_M = M_WIDTH // HEAD_DIM
H_N = N_WIDTH // HEAD_DIM
H_G = G_WIDTH // HEAD_DIM
KV_GROUPS = 2
Q_PER_KV = H_N // KV_GROUPS
D_FF = 4 * D_MODEL
CONV_W = 4
CHUNK = 64
CMP_STRIDE = 16
CMP_LEN = 2 * CMP_STRIDE
CMP_HID = 256
SEL_BLK = 64
N_SELECT = 16
WINDOW = 512
QBLK = 128
GATE_RANK = 16
GLA_GATE_TEMP = 16.0
ROPE_THETA = 10000.0
EPS = 1e-6
TINY = 1e-30
FORCE_BONUS = 1e3
NEG_BIG = -1e9
SPLIT_SIZES = (2 * M_WIDTH, M_WIDTH, M_WIDTH, 2 * H_M,
               N_WIDTH, 6 * KV_GROUPS * HEAD_DIM, 3 * H_N,
               G_WIDTH, G_WIDTH, G_WIDTH, G_WIDTH, GATE_RANK)
IN_COLS = sum(SPLIT_SIZES)
SPLIT_OFFSETS = tuple(int(o) for o in np.cumsum(SPLIT_SIZES)[:-1])

F32 = jnp.float32
BF16 = jnp.bfloat16

VMEM_LIMIT_BYTES = 56 * 1024 * 1024
IN_TN = 512
COL_NQ = 2 * M_WIDTH + 2 * M_WIDTH
COL_NKV = COL_NQ + N_WIDTH
COL_GLA = COL_NKV + 6 * KV_GROUPS * HEAD_DIM
COL_SMALL = COL_GLA + 4 * G_WIDTH
SMALL_MIF = 0
SMALL_NG = 2 * H_M
SMALL_GLR = SMALL_NG + 3 * H_N
IN_COLS_PAD = 7168
NEG = -1e30
NSA_TQ = 128
NSA_TK = 512


def _rms(x, g):
    return x * lax.rsqrt(jnp.mean(x * x, axis=-1, keepdims=True) + EPS) * g


def _norm_matmul_kernel(x_ref, g_ref, w_ref, o_ref, xn_ref):
    @pl.when(pl.program_id(1) == 0)
    def _():
        xn_ref[...] = _rms(x_ref[...], g_ref[...]).astype(BF16)

    o_ref[...] = jnp.dot(xn_ref[...], w_ref[...], preferred_element_type=F32)


def norm_matmul(x, g, w, tn):
    m, k = x.shape
    n = w.shape[1]
    tm = min(m, 1024)
    return pl.pallas_call(
        _norm_matmul_kernel,
        grid=(m // tm, n // tn),
        in_specs=[pl.BlockSpec((tm, k), lambda i, j: (i, 0)),
                  pl.BlockSpec((1, k), lambda i, j: (0, 0)),
                  pl.BlockSpec((k, tn), lambda i, j: (0, j))],
        out_specs=pl.BlockSpec((tm, tn), lambda i, j: (i, j)),
        out_shape=jax.ShapeDtypeStruct((m, n), F32),
        scratch_shapes=[pltpu.VMEM((tm, k), BF16)],
        compiler_params=pltpu.CompilerParams(
            dimension_semantics=("arbitrary", "arbitrary"), vmem_limit_bytes=VMEM_LIMIT_BYTES),
        name="norm_matmul",
    )(x, g.reshape(1, k), w)


def _matmul_norm_res_kernel(a_ref, w_ref, g_ref, r_ref, o_ref):
    y = jnp.dot(a_ref[...].astype(BF16), w_ref[...], preferred_element_type=F32)
    o_ref[...] = r_ref[...] + _rms(y, g_ref[...])


def matmul_norm_res(a, w, g, r):
    m, k = a.shape
    n = w.shape[1]
    tm = min(m, 512)
    return pl.pallas_call(
        _matmul_norm_res_kernel,
        grid=(m // tm,),
        in_specs=[pl.BlockSpec((tm, k), lambda i: (i, 0)),
                  pl.BlockSpec((k, n), lambda i: (0, 0)),
                  pl.BlockSpec((1, n), lambda i: (0, 0)),
                  pl.BlockSpec((tm, n), lambda i: (i, 0))],
        out_specs=pl.BlockSpec((tm, n), lambda i: (i, 0)),
        out_shape=jax.ShapeDtypeStruct((m, n), F32),
        compiler_params=pltpu.CompilerParams(
            dimension_semantics=("arbitrary",), vmem_limit_bytes=VMEM_LIMIT_BYTES),
        name="matmul_norm_res",
    )(a, w, g.reshape(1, n), r)


def _ffn_kernel(x_ref, g2_ref, w1_ref, w2_ref, g3_ref, o_ref, xn_ref, acc_ref):
    f = pl.program_id(1)

    @pl.when(f == 0)
    def _():
        xn_ref[...] = _rms(x_ref[...], g2_ref[...]).astype(BF16)
        acc_ref[...] = jnp.zeros_like(acc_ref)

    h = jnp.dot(xn_ref[...], w1_ref[...], preferred_element_type=F32)
    a = jnp.square(jnp.maximum(h, 0.0)).astype(BF16)
    acc_ref[...] += jnp.dot(a, w2_ref[...], preferred_element_type=F32)

    @pl.when(f == pl.num_programs(1) - 1)
    def _():
        o_ref[...] = x_ref[...] + _rms(acc_ref[...], g3_ref[...])


def ffn(x, g2, w1, w2, g3):
    m, d = x.shape
    dff = w1.shape[1]
    tm = min(m, 512)
    tf = 512
    return pl.pallas_call(
        _ffn_kernel,
        grid=(m // tm, dff // tf),
        in_specs=[pl.BlockSpec((tm, d), lambda i, f: (i, 0)),
                  pl.BlockSpec((1, d), lambda i, f: (0, 0)),
                  pl.BlockSpec((d, tf), lambda i, f: (0, f)),
                  pl.BlockSpec((tf, d), lambda i, f: (f, 0)),
                  pl.BlockSpec((1, d), lambda i, f: (0, 0))],
        out_specs=pl.BlockSpec((tm, d), lambda i, f: (i, 0)),
        out_shape=jax.ShapeDtypeStruct((m, d), F32),
        scratch_shapes=[pltpu.VMEM((tm, d), BF16), pltpu.VMEM((tm, d), F32)],
        compiler_params=pltpu.CompilerParams(
            dimension_semantics=("arbitrary", "arbitrary"), vmem_limit_bytes=VMEM_LIMIT_BYTES),
        name="ffn",
    )(x, g2.reshape(1, d), w1, w2, g3.reshape(1, d))


def rope_tables(pos):
    half = HEAD_DIM // 2
    inv_freq = jnp.exp(-math.log(ROPE_THETA) * jnp.arange(half, dtype=F32) / half)
    ang = pos.astype(F32)[:, None] * inv_freq[None, :]
    cos, sin = jnp.cos(ang), jnp.sin(ang)
    return jnp.concatenate([cos, cos], axis=-1), jnp.concatenate([-sin, sin], axis=-1)


def _rope(x, cos2, sin2):
    return x * cos2 + pltpu.roll(x, HEAD_DIM // 2, 1) * sin2


def _nsa_prep_kernel(nq_ref, nkv_ref, cos_ref, sin_ref, q_ref, rows_ref, win_ref, kvb_ref, cmp_ref):
    cos2 = cos_ref[...]
    sin2 = sin_ref[...]
    for h in range(H_N):
        sl = slice(h * HEAD_DIM, (h + 1) * HEAD_DIM)
        q_ref[:, sl] = (_rope(nq_ref[:, sl], cos2, sin2) * (HEAD_DIM ** -0.5)).astype(BF16)
    for c in range(6 * KV_GROUPS):
        slot = c // KV_GROUPS
        x = nkv_ref[:, c * HEAD_DIM:(c + 1) * HEAD_DIM]
        if slot % 2 == 0:
            x = _rope(x, cos2, sin2)
        xb = x.astype(BF16)
        if slot < 4:
            rows_ref[:, c * HEAD_DIM:(c + 1) * HEAD_DIM] = x
        else:
            win_ref[:, (c - 8) * HEAD_DIM:(c - 7) * HEAD_DIM] = x
        if slot < 2:
            cmp_ref[0, c] = xb
        else:
            kvb_ref[:, (c - 4) * HEAD_DIM:(c - 3) * HEAD_DIM] = xb


def nsa_prep_pallas(z, b, s, cos2, sin2):
    m = b * s
    tm = 512
    nsb = s // tm
    return pl.pallas_call(
        _nsa_prep_kernel,
        grid=(m // tm,),
        in_specs=[pl.BlockSpec((tm, N_WIDTH), lambda i: (i, COL_NQ // N_WIDTH)),
                  pl.BlockSpec((tm, 1536), lambda i: (i, COL_NKV // 1536)),
                  pl.BlockSpec((tm, HEAD_DIM), lambda i: (i % nsb, 0)),
                  pl.BlockSpec((tm, HEAD_DIM), lambda i: (i % nsb, 0))],
        out_specs=[pl.BlockSpec((tm, N_WIDTH), lambda i: (i, 0)),
                   pl.BlockSpec((tm, 1024), lambda i: (i, 0)),
                   pl.BlockSpec((tm, 512), lambda i: (i, 0)),
                   pl.BlockSpec((tm, 1024), lambda i: (i, 0)),
                   pl.BlockSpec((1, 4, tm, HEAD_DIM), lambda i: (i // nsb, 0, i % nsb, 0))],
        out_shape=[jax.ShapeDtypeStruct((m, N_WIDTH), BF16),
                   jax.ShapeDtypeStruct((m, 1024), F32),
                   jax.ShapeDtypeStruct((m, 512), F32),
                   jax.ShapeDtypeStruct((m, 1024), BF16),
                   jax.ShapeDtypeStruct((b, 4, s, HEAD_DIM), BF16)],
        compiler_params=pltpu.CompilerParams(
            dimension_semantics=("arbitrary",), vmem_limit_bytes=VMEM_LIMIT_BYTES),
        name="nsa_prep",
    )(z, z, cos2, sin2)


def _nsa_cmp_kernel(x_ref, pe_ref, w1_ref, w2_ref, o_ref):
    x = x_ref[0, 0]
    nb = x.shape[0]
    half = CMP_STRIDE * HEAD_DIM
    first = jnp.dot(x, w1_ref[0, :half, :], preferred_element_type=F32)
    second = jnp.dot(x, w1_ref[0, half:, :], preferred_element_type=F32)
    pe = jnp.broadcast_to(pe_ref[0], (8, 2 * half)).astype(BF16)
    bias = jnp.dot(pe, w1_ref[0], preferred_element_type=F32)[0:1]
    hid = first + pltpu.roll(second, nb - 1, 0) + bias
    o_ref[0, 0] = jnp.dot(jax.nn.gelu(hid).astype(BF16), w2_ref[0], preferred_element_type=F32).astype(BF16)


def nsa_compress_prompt(cmp_in, pe, w1, w2):
    b, _, nb, kdim = cmp_in.shape
    return pl.pallas_call(
        _nsa_cmp_kernel,
        grid=(b, 4),
        in_specs=[pl.BlockSpec((1, 1, nb, kdim), lambda i, c: (i, c, 0, 0)),
                  pl.BlockSpec((1, 1, 2 * kdim), lambda i, c: (c // 2, 0, 0)),
                  pl.BlockSpec((1, 2 * kdim, CMP_HID), lambda i, c: (c // 2, 0, 0)),
                  pl.BlockSpec((1, CMP_HID, HEAD_DIM), lambda i, c: (c // 2, 0, 0))],
        out_specs=pl.BlockSpec((1, 1, nb, HEAD_DIM), lambda i, c: (i, c, 0, 0)),
        out_shape=jax.ShapeDtypeStruct((b, 4, nb, HEAD_DIM), BF16),
        compiler_params=pltpu.CompilerParams(
            dimension_semantics=("arbitrary", "arbitrary"), vmem_limit_bytes=VMEM_LIMIT_BYTES),
        name="nsa_compress",
    )(cmp_in, pe, w1, w2)


def _softmax_rows(s, mask):
    sm = jnp.where(mask, s, NEG)
    m = jnp.max(sm, axis=-1, keepdims=True)
    p = jnp.where(mask, jnp.exp(sm - m), 0.0)
    return p / jnp.maximum(jnp.sum(p, axis=-1, keepdims=True), TINY)


_NT = (((1,), (1,)), ((), ()))


def _nsa_attn_kernel(q_ref, kc_ref, vc_ref, ks_ref, vs_ref, kw_ref, vw_ref, gate_ref, o_ref):
    tq = NSA_TQ
    r4 = Q_PER_KV
    g = pl.program_id(1)
    i = pl.program_id(2)
    q4 = q_ref[...]
    q = jnp.concatenate([q4[:, r * HEAD_DIM:(r + 1) * HEAD_DIM] for r in range(r4)], axis=0)
    q0 = i * tq
    tpos = q0 + lax.broadcasted_iota(jnp.int32, (tq, 1), 0)

    ncp = kc_ref.shape[2]
    s = lax.dot_general(q, kc_ref[0, 0], _NT, preferred_element_type=F32).reshape(r4, tq, ncp)
    cend = lax.broadcasted_iota(jnp.int32, (tq, ncp), 1) * CMP_STRIDE + (CMP_LEN - 1)
    p_c = _softmax_rows(s, (cend <= tpos)[None])
    o_c = jnp.dot(p_c.reshape(r4 * tq, ncp).astype(BF16), vc_ref[0, 0], preferred_element_type=F32)

    psum = p_c[0] + p_c[1] + p_c[2] + p_c[3]
    n_i = lax.broadcasted_iota(jnp.int32, (ncp, HEAD_DIM), 0)
    j_i = lax.broadcasted_iota(jnp.int32, (ncp, HEAD_DIM), 1)
    ratio = SEL_BLK // CMP_STRIDE
    c2s = jnp.where((n_i >= ratio * j_i - 1) & (n_i <= ratio * j_i + ratio - 1) & (n_i < ncp - 1)
                    & (j_i < ncp // ratio), 1.0, 0.0).astype(BF16)
    p_hi = psum.astype(BF16)
    p_lo = (psum - p_hi.astype(F32)).astype(BF16)
    imp = (jnp.dot(p_hi, c2s, preferred_element_type=F32)
           + jnp.dot(p_lo, c2s, preferred_element_type=F32))

    ns = ncp // ratio
    imp_t = jnp.transpose(imp)[:ns]
    jj = lax.broadcasted_iota(jnp.int32, (ns, tq), 0)
    tt = q0 + lax.broadcasted_iota(jnp.int32, (ns, tq), 1)
    cur = tt // SEL_BLK
    forced = (jj == 0) | (jj == cur) | (jj == cur - 1)
    score = jnp.where(jj * SEL_BLK <= tt, imp_t + jnp.where(forced, FORCE_BONUS, 0.0), NEG_BIG)
    rank = jnp.zeros((ns, tq), F32)
    for jp in range(ns):
        row = score[jp:jp + 1, :]
        rank = rank + jnp.where(row > score, 1.0, jnp.where((row == score) & (jj > jp), 1.0, 0.0))
    sel_t = jnp.where(rank < float(N_SELECT), 1.0, 0.0)
    sel_t = jnp.concatenate([sel_t, jnp.zeros((HEAD_DIM - ns, tq), F32)], axis=0)
    sel = jnp.transpose(sel_t).astype(BF16)

    tk = NSA_TK
    n_tiles = (q0 + tq + tk - 1) // tk

    def sel_tile(kt, carry):
        m_run, l_run, acc = carry
        k0 = pl.multiple_of(kt * tk, tk)
        s = lax.dot_general(q, ks_ref[pl.ds(k0, tk), :], _NT, preferred_element_type=F32).reshape(r4, tq, tk)
        blk = lax.broadcasted_iota(jnp.int32, (HEAD_DIM, tk), 0)
        key = lax.broadcasted_iota(jnp.int32, (HEAD_DIM, tk), 1)
        expand = jnp.where(blk == kt * (tk // SEL_BLK) + key // SEL_BLK, 1.0, 0.0).astype(BF16)
        picked = jnp.dot(sel, expand, preferred_element_type=F32)
        kpos = k0 + lax.broadcasted_iota(jnp.int32, (tq, tk), 1)
        mask = ((picked > 0.5) & (kpos <= tpos))[None]
        sm = jnp.where(mask, s, NEG)
        m_new = jnp.maximum(m_run, jnp.max(sm, axis=-1, keepdims=True))
        alpha = jnp.exp(m_run - m_new)
        p = jnp.where(mask, jnp.exp(sm - m_new), 0.0)
        l_new = alpha * l_run + jnp.sum(p, axis=-1, keepdims=True)
        pv = jnp.dot(p.reshape(r4 * tq, tk).astype(BF16), vs_ref[pl.ds(k0, tk), :], preferred_element_type=F32)
        return m_new, l_new, alpha.reshape(r4 * tq, 1) * acc + pv

    init = (jnp.full((r4, tq, 1), NEG, F32), jnp.zeros((r4, tq, 1), F32), jnp.zeros((r4 * tq, HEAD_DIM), F32))
    _, l_s, acc_s = lax.fori_loop(0, n_tiles, sel_tile, init)
    o_s = acc_s / jnp.maximum(l_s, TINY).reshape(r4 * tq, 1)

    wb = WINDOW + tq
    w0 = pl.multiple_of(jnp.maximum(q0 - WINDOW, 0), tq)
    s = lax.dot_general(q, kw_ref[pl.ds(w0, wb), :], _NT, preferred_element_type=F32).reshape(r4, tq, wb)
    dist = tpos - (w0 + lax.broadcasted_iota(jnp.int32, (tq, wb), 1))
    p_w = _softmax_rows(s, ((dist >= 0) & (dist < WINDOW))[None])
    o_w = jnp.dot(p_w.reshape(r4 * tq, wb).astype(BF16), vw_ref[pl.ds(w0, wb), :], preferred_element_type=F32)

    gates = jax.nn.sigmoid(gate_ref[...])
    for r in range(r4):
        def gate(c):
            lane0 = SMALL_NG + 3 * r + c
            lane1 = lane0 + 3 * r4
            return jnp.where(g == 0, gates[:, lane0:lane0 + 1], gates[:, lane1:lane1 + 1])
        rows = slice(r * tq, (r + 1) * tq)
        o = gate(0) * o_c[rows] + gate(1) * o_s[rows] + gate(2) * o_w[rows]
        o_ref[:, r * HEAD_DIM:(r + 1) * HEAD_DIM] = o.astype(BF16)


def nsa_attention_prompt(q_b, kcvc, kvb, z, b, s):
    tq = NSA_TQ
    nq = s // tq
    ncp = kcvc.shape[2]
    gw = Q_PER_KV * HEAD_DIM
    seq_block = lambda col: pl.BlockSpec((s, HEAD_DIM), lambda bi, g, i: (bi, col + g))
    return pl.pallas_call(
        _nsa_attn_kernel,
        grid=(b, KV_GROUPS, nq),
        in_specs=[pl.BlockSpec((tq, gw), lambda bi, g, i: (bi * nq + i, g)),
                  pl.BlockSpec((1, 1, ncp, HEAD_DIM), lambda bi, g, i: (bi, g, 0, 0)),
                  pl.BlockSpec((1, 1, ncp, HEAD_DIM), lambda bi, g, i: (bi, KV_GROUPS + g, 0, 0)),
                  seq_block(0), seq_block(2), seq_block(4), seq_block(6),
                  pl.BlockSpec((tq, HEAD_DIM), lambda bi, g, i: (bi * nq + i, COL_SMALL // HEAD_DIM))],
        out_specs=pl.BlockSpec((tq, gw), lambda bi, g, i: (bi * nq + i, g)),
        out_shape=jax.ShapeDtypeStruct((b * s, N_WIDTH), BF16),
        compiler_params=pltpu.CompilerParams(
            dimension_semantics=("arbitrary", "arbitrary", "arbitrary"), vmem_limit_bytes=VMEM_LIMIT_BYTES),
        name="nsa_attention",
    )(q_b, kcvc, kcvc, kvb, kvb, kvb, kvb, z)


def nsa_prompt_pallas(z, b, s, cmp_w):
    pe, w1, w2 = cmp_w
    cos2, sin2 = rope_tables(jnp.arange(s))
    q_b, rows_f, win_f, kvb, cmp_in = nsa_prep_pallas(z, b, s, cos2, sin2)
    kcvc = nsa_compress_prompt(cmp_in.reshape(b, 4, s // CMP_STRIDE, CMP_STRIDE * HEAD_DIM), pe, w1, w2)
    hn = nsa_attention_prompt(q_b, kcvc, kvb, z, b, s)
    rows = rows_f.reshape(b, s, 4, KV_GROUPS, HEAD_DIM)
    win = win_f.reshape(b, s, 2, KV_GROUPS, HEAD_DIM)[:, -min(WINDOW, s):]
    return hn, (rows, win)


def rmsnorm(x, g):
    xf = x.astype(F32)
    y = xf * lax.rsqrt(jnp.mean(xf * xf, axis=-1, keepdims=True) + EPS)
    return (y * g.astype(F32)).astype(x.dtype)


def rope(x, pos):
    half = x.shape[-1] // 2
    inv_freq = jnp.exp(-math.log(ROPE_THETA) * jnp.arange(half, dtype=F32) / half)
    ang = pos.astype(F32)[:, None] * inv_freq[None, :]
    cos = jnp.cos(ang)[:, None, :]
    sin = jnp.sin(ang)[:, None, :]
    xf = x.astype(F32)
    x1, x2 = xf[..., :half], xf[..., half:]
    return jnp.concatenate([x1 * cos - x2 * sin, x2 * cos + x1 * sin], axis=-1).astype(x.dtype)


def masked_softmax(s, mask):
    s = jnp.where(mask, s, -jnp.inf)
    m = jnp.max(s, axis=-1, keepdims=True)
    m = jnp.where(jnp.isfinite(m), m, 0.0)
    p = jnp.exp(s - m)
    return p / jnp.maximum(jnp.sum(p, axis=-1, keepdims=True), TINY)


def chunked_scan(step, carry, xs, chunk):
    b, t = xs[0].shape[:2]
    c = chunk if t % chunk == 0 else t
    n = t // c
    split = lambda a: jnp.moveaxis(a.reshape((b, n, c) + a.shape[2:]), 1, 0)
    carry, ys = lax.scan(step, carry, tuple(split(a) for a in xs))
    return carry, jnp.moveaxis(ys, 0, 1).reshape((b, t) + ys.shape[3:])


def mlstm_chunk(carry, inp):
    c0, n0, m0 = carry
    q, k, v, li, lf = inp
    L = q.shape[1]
    b = jnp.cumsum(lf, axis=1).transpose(0, 2, 1)
    ig = li.transpose(0, 2, 1)
    causal = jnp.tril(jnp.ones((L, L), dtype=bool))
    logw = jnp.where(causal, b[..., :, None] - b[..., None, :] + ig[..., None, :], -jnp.inf)
    g = b + m0[..., None]
    m_row = jnp.maximum(jnp.max(logw, axis=-1), g)
    w = jnp.exp(logw - m_row[..., None]) * jnp.einsum('blhd,bshd->bhls', q, k)
    inter = jnp.exp(g - m_row)
    num = jnp.einsum('bhls,bshd->bhld', w, v) + inter[..., None] * jnp.einsum('bhed,blhd->bhle', c0, q)
    den = jnp.sum(w, axis=-1) + inter * jnp.einsum('bhd,blhd->bhl', n0, q)
    h = num / jnp.maximum(jnp.abs(den), jnp.exp(-m_row))[..., None]
    bl = b[..., -1]
    a = bl[..., None] - b + ig
    m_new = jnp.maximum(jnp.max(a, axis=-1), bl + m0)
    wa = jnp.exp(a - m_new[..., None])
    wc = jnp.exp(bl + m0 - m_new)
    c_new = wc[..., None, None] * c0 + jnp.einsum('bhs,bshe,bshd->bhed', wa, v, k)
    n_new = wc[..., None] * n0 + jnp.einsum('bhs,bshd->bhd', wa, k)
    return (c_new, n_new, m_new), h.transpose(0, 2, 1, 3)


def gla_chunk(s0, inp):
    q, k, v, la = inp
    L = q.shape[1]
    bc = jnp.cumsum(la, axis=1)
    causal = jnp.tril(jnp.ones((L, L), dtype=bool))[None, :, :, None, None]
    decay = jnp.exp(jnp.where(causal, bc[:, :, None] - bc[:, None, :], -jnp.inf))
    a = jnp.einsum('bthk,bshk,btshk->bhts', q, k, decay)
    o = jnp.einsum('bhts,bshv->bthv', a, v) + jnp.einsum('bthk,bhkv->bthv', q * jnp.exp(bc), s0)
    bl = bc[:, -1]
    s_new = jnp.exp(bl)[..., None] * s0 + jnp.einsum('bshk,bshv->bhkv', k * jnp.exp(bl[:, None] - bc), v)
    return s_new, o


def nsa_compress(x, pe, w1, w2):
    b, l = x.shape[:2]
    nb = l // CMP_STRIDE
    xb = x[:, :nb * CMP_STRIDE].reshape(b, nb, CMP_STRIDE, KV_GROUPS, HEAD_DIM)
    first = jnp.einsum('bnjgd,jde->bnge', xb, w1[:CMP_STRIDE])
    second = jnp.einsum('bnjgd,jde->bnge', xb, w1[CMP_STRIDE:])
    hid = first[:, :-1] + second[:, 1:] + jnp.einsum('jd,jde->e', pe, w1)
    return jnp.einsum('bnge,ed->bngd', jax.nn.gelu(hid), w2)


def nsa_context(rows, cmp_params):
    pe, w1, w2 = cmp_params
    b, l = rows.shape[:2]
    kc = nsa_compress(rows[:, :, 0], pe[0], w1[0], w2[0])
    vc = nsa_compress(rows[:, :, 1], pe[1], w1[1], w2[1])
    nc = kc.shape[1]
    ns = -(-l // SEL_BLK)
    sel = jnp.pad(rows[:, :, 2:4], ((0, 0), (0, ns * SEL_BLK - l), (0, 0), (0, 0), (0, 0)))
    sel = sel.reshape(b, ns, SEL_BLK, 2, KV_GROUPS, HEAD_DIM).transpose(3, 0, 4, 1, 2, 5)
    cmp_start = jnp.arange(nc) * CMP_STRIDE
    cmp_end = cmp_start + CMP_LEN - 1
    sel_start = jnp.arange(ns) * SEL_BLK
    cmp_to_sel = ((cmp_start[:, None] < sel_start[None, :] + SEL_BLK)
                  & (cmp_end[:, None] >= sel_start[None, :])).astype(F32)
    return kc, vc, cmp_end, cmp_to_sel, sel[0], sel[1]


def nsa_query_block(q, qpos, gates, kc, vc, cmp_end, cmp_to_sel, ksb, vsb, kw, vw, kwpos):
    b, t = q.shape[:2]
    qg = q.reshape(b, t, KV_GROUPS, Q_PER_KV, HEAD_DIM) * (HEAD_DIM ** -0.5)
    s = jnp.einsum('btgrd,bngd->btgrn', qg, kc).astype(F32)
    p_c = masked_softmax(s, (cmp_end[None, :] <= qpos[:, None])[None, :, None, None, :])
    o_c = jnp.einsum('btgrn,bngd->btgrd', p_c.astype(vc.dtype), vc)
    ns = ksb.shape[2]
    imp = jnp.einsum('btgn,nj->btgj', jnp.sum(p_c, axis=3), cmp_to_sel)
    j = jnp.arange(ns)
    cur = qpos // SEL_BLK
    elig = (j * SEL_BLK)[None, :] <= qpos[:, None]
    forced = (j[None, :] == 0) | (j[None, :] == cur[:, None]) | (j[None, :] == cur[:, None] - 1)
    score = jnp.where(elig[None, :, None, :], imp + jnp.where(forced, FORCE_BONUS, 0.0)[None, :, None, :], NEG_BIG)
    _, idx = lax.top_k(score, min(N_SELECT, ns))
    n = idx.shape[-1]
    idx_g = idx.transpose(0, 2, 1, 3).reshape(b, KV_GROUPS, t * n)
    take = jax.vmap(jax.vmap(lambda blocks, ids: blocks[ids]))
    gk = take(ksb, idx_g).reshape(b, KV_GROUPS, t, n * SEL_BLK, HEAD_DIM)
    gv = take(vsb, idx_g).reshape(b, KV_GROUPS, t, n * SEL_BLK, HEAD_DIM)
    kpos = (idx[..., None] * SEL_BLK + jnp.arange(SEL_BLK)).reshape(b, t, KV_GROUPS, n * SEL_BLK)
    s = jnp.einsum('btgrd,bgtmd->btgrm', qg, gk).astype(F32)
    p_s = masked_softmax(s, (kpos <= qpos[None, :, None, None])[:, :, :, None, :])
    o_s = jnp.einsum('btgrm,bgtmd->btgrd', p_s.astype(gv.dtype), gv)
    s = jnp.einsum('btgrd,bkgd->btgrk', qg, kw).astype(F32)
    d = qpos[:, None] - kwpos[None, :]
    wmask = (kwpos[None, :] >= 0) & (d >= 0) & (d < WINDOW)
    p_w = masked_softmax(s, wmask[None, :, None, None, :])
    o_w = jnp.einsum('btgrk,bkgd->btgrd', p_w.astype(vw.dtype), vw)
    g = jax.nn.sigmoid(gates.astype(F32)).reshape(b, t, KV_GROUPS, Q_PER_KV, 3)
    o = g[..., 0:1] * o_c + g[..., 1:2] * o_s + g[..., 2:3] * o_w
    return o.reshape(b, t, H_N * HEAD_DIM).astype(q.dtype)


def nsa_prep(n_q, n_kv, n_g, pos):
    b, t = n_q.shape[:2]
    q = rope(n_q.reshape(b, t, H_N, HEAD_DIM), pos)
    kv = n_kv.reshape(b, t, 6, KV_GROUPS, HEAD_DIM)
    rows = jnp.stack([rope(kv[:, :, 0], pos), kv[:, :, 1], rope(kv[:, :, 2], pos), kv[:, :, 3]], axis=2)
    win = jnp.stack([rope(kv[:, :, 4], pos), kv[:, :, 5]], axis=2)
    return q, n_g.reshape(b, t, H_N, 3), rows, win


def nsa_prompt(n_q, n_kv, n_g, cmp_params):
    b, s = n_q.shape[:2]
    pos = jnp.arange(s)
    q, gates, rows, win = nsa_prep(n_q, n_kv, n_g, pos)
    kc, vc, cmp_end, cmp_to_sel, ksb, vsb = nsa_context(rows, cmp_params)
    nb = s // QBLK
    kwp = jnp.pad(win, ((0, 0), (WINDOW, 0), (0, 0), (0, 0), (0, 0)))

    def one_block(args):
        qi, gi, bi = args
        start = bi * QBLK
        qpos = start + jnp.arange(QBLK)
        band = lax.dynamic_slice_in_dim(kwp, start, WINDOW + QBLK, axis=1)
        kwpos = start - WINDOW + jnp.arange(WINDOW + QBLK)
        return nsa_query_block(qi, qpos, gi, kc, vc, cmp_end, cmp_to_sel, ksb, vsb,
                               band[:, :, 0], band[:, :, 1], kwpos)

    blocks = lambda a: jnp.moveaxis(a.reshape((b, nb, QBLK) + a.shape[2:]), 1, 0)
    out = lax.map(one_block, (blocks(q), blocks(gates), jnp.arange(nb)))
    out = jnp.moveaxis(out, 0, 1).reshape(b, s, H_N * HEAD_DIM)
    return out, (rows, win[:, -min(WINDOW, s):])


def nsa_sample(n_q, n_kv, n_g, past_rows, win_buf, cmp_params):
    b, t = n_q.shape[:2]
    past_len = past_rows.shape[1]
    pos = past_len + jnp.arange(t)
    q, gates, rows, win = nsa_prep(n_q, n_kv, n_g, pos)
    full = jnp.concatenate([past_rows.astype(rows.dtype), rows], axis=1)
    kc, vc, cmp_end, cmp_to_sel, ksb, vsb = nsa_context(full, cmp_params)
    wl = win_buf.shape[1]
    kw_all = jnp.concatenate([win_buf.astype(win.dtype), win], axis=1)
    kwpos = past_len - wl + jnp.arange(wl + t)
    out = nsa_query_block(q, pos, gates, kc, vc, cmp_end, cmp_to_sel, ksb, vsb,
                          kw_all[:, :, 0], kw_all[:, :, 1], kwpos)
    return out, (rows, kw_all[:, -wl:])


def to_heads(a, n_heads):
    return a.reshape(a.shape[0], a.shape[1], n_heads, -1).astype(F32)


def split_cols(z):
    cut = lambda a, n: z[..., a:a + n]
    u_qk, m_v, m_o = cut(0, 2 * M_WIDTH), cut(2 * M_WIDTH, M_WIDTH), cut(3 * M_WIDTH, M_WIDTH)
    n_q, n_kv = cut(COL_NQ, N_WIDTH), cut(COL_NKV, 6 * KV_GROUPS * HEAD_DIM)
    g_q, g_k, g_v, g_r = (cut(COL_GLA + i * G_WIDTH, G_WIDTH) for i in range(4))
    m_if = cut(COL_SMALL + SMALL_MIF, 2 * H_M)
    n_g = cut(COL_SMALL + SMALL_NG, 3 * H_N)
    g_lr = cut(COL_SMALL + SMALL_GLR, GATE_RANK)
    return u_qk, m_v, m_o, m_if, n_q, n_kv, n_g, g_q, g_k, g_v, g_r, g_lr


def regroup_w_in(w_in):
    o = (0,) + SPLIT_OFFSETS + (IN_COLS,)
    piece = lambda i: w_in[..., o[i]:o[i + 1]]
    order = [0, 1, 2, 4, 5, 7, 8, 9, 10, 3, 6, 11]
    pad = jnp.zeros(w_in.shape[:-1] + (IN_COLS_PAD - IN_COLS,), w_in.dtype)
    return jnp.concatenate([piece(i) for i in order] + [pad], axis=-1)


def mixers(z, conv_buf, c0, n0, m0, s0, nsa_fn, conv_w, m_gate_b, m_norm, g_w2, g_b, g_norm):
    b, t, _ = z.shape
    (u_qk, m_v, m_o, m_if, n_q, n_kv, n_g,
     g_q, g_k, g_v, g_r, g_lr) = split_cols(z)
    u = jnp.concatenate([conv_buf.astype(u_qk.dtype), u_qk], axis=1)
    conv = conv_w[0] * u[:, 0:t]
    for j in range(1, CONV_W):
        conv = conv + conv_w[j] * u[:, j:j + t]
    conv = jax.nn.silu(conv)
    new_conv = u[:, t:]
    q_m, k_m = jnp.split(conv, 2, axis=-1)
    ig, fg = jnp.split(m_if.astype(F32) + m_gate_b.reshape(-1).astype(F32), 2, axis=-1)
    (c, n, m), hm = chunked_scan(
        mlstm_chunk, (c0.astype(F32), n0.astype(F32), m0.astype(F32)),
        (to_heads(q_m, H_M), to_heads(k_m, H_M) * (HEAD_DIM ** -0.5), to_heads(m_v, H_M),
         ig, jax.nn.log_sigmoid(fg)), CHUNK)
    hm = (rmsnorm(hm, m_norm.reshape(H_M, HEAD_DIM)) * jax.nn.sigmoid(to_heads(m_o, H_M))).reshape(b, t, M_WIDTH)
    hn, nsa_state = nsa_fn(z, n_q, n_kv, n_g)
    la =jax.nn.log_sigmoid(g_lr.astype(F32) @ g_w2.astype(F32) + g_b.astype(F32)) / GLA_GATE_TEMP
    s, hg = chunked_scan(
        gla_chunk, s0.astype(F32),
        (to_heads(g_q, H_G) * (HEAD_DIM ** -0.5), to_heads(g_k, H_G), to_heads(g_v, H_G),
         la.reshape(b, t, H_G, HEAD_DIM)), CHUNK)
    hg = rmsnorm(hg, g_norm.reshape(H_G, HEAD_DIM)).reshape(b, t, G_WIDTH) * jax.nn.silu(g_r.astype(F32))
    mix = jnp.concatenate([hm.astype(BF16), hn.reshape(b, t, N_WIDTH).astype(BF16), hg.astype(BF16)], axis=-1)
    return mix, new_conv, c, n, m, s, nsa_state


def trunk_layer(x, conv_buf, c0, n0, m0, s0, nsa_fn, g_norms, w_in, conv_w, m_gate_b, m_norm,
                g_w2, g_b, g_norm, w_out, w_ff1, w_ff2):
    b, t, d = x.shape
    x2 = x.reshape(b * t, d)
    z = norm_matmul(x2, g_norms[0], w_in, IN_TN).reshape(b, t, -1)
    mix, new_conv, c, n, m, s, nsa_state = mixers(
        z, conv_buf, c0, n0, m0, s0, nsa_fn, conv_w, m_gate_b, m_norm, g_w2, g_b, g_norm)
    x2 = matmul_norm_res(mix.reshape(b * t, -1), w_out, g_norms[1], x2)
    x2 = ffn(x2, g_norms[2], w_ff1, w_ff2, g_norms[3])
    return x2.reshape(b, t, d), (nsa_state[0], nsa_state[1], c, n, m, new_conv, s)


def kernel(x_prompt, x_sample, cache_nsa_kv, state_nsa_win, state_mlstm_C, state_mlstm_n, state_mlstm_m, state_mlstm_conv, state_gla_S, page_table, norms, w_in, mlstm_conv_w, mlstm_gate_b, mlstm_norm, nsa_cmp_pe, nsa_cmp_w1, nsa_cmp_w2, gla_gate_w2, gla_gate_b, gla_norm, w_out, w_ff1, w_ff2):
    xp, xs = x_prompt, x_sample
    bp = xp.shape[0]
    db = xs.shape[0]
    past_len = page_table.shape[1] * PAGE_SIZE
    conv0 = jnp.zeros((bp, CONV_W - 1, 2 * M_WIDTH), xp.dtype)
    c0 = jnp.zeros((bp, H_M, HEAD_DIM, HEAD_DIM), F32)
    n0 = jnp.zeros((bp, H_M, HEAD_DIM), F32)
    m0 = jnp.zeros((bp, H_M), F32)
    s0 = jnp.zeros((bp, H_G, HEAD_DIM, HEAD_DIM), F32)
    w_in_b = regroup_w_in(w_in).astype(BF16)
    w_out_b = w_out.astype(BF16)
    w_ff1_b = w_ff1.astype(BF16)
    w_ff2_b = w_ff2.astype(BF16)
    cmp_pe = nsa_cmp_pe.reshape(DEPTH, 2, 1, CMP_LEN * HEAD_DIM)
    cmp_w1_b = nsa_cmp_w1.reshape(DEPTH, 2, CMP_LEN * HEAD_DIM, CMP_HID).astype(BF16)
    cmp_w2_b = nsa_cmp_w2.astype(BF16)
    acc_p = [[] for _ in range(7)]
    acc_s = [[] for _ in range(7)]
    for l in range(DEPTH):
        weights = (norms[l], w_in_b[l], mlstm_conv_w[l], mlstm_gate_b[l], mlstm_norm[l],
                   gla_gate_w2[l], gla_gate_b[l], gla_norm[l], w_out_b[l], w_ff1_b[l], w_ff2_b[l])
        cmp_params = (nsa_cmp_pe[l], nsa_cmp_w1[l], nsa_cmp_w2[l])
        cmp_w = (cmp_pe[l], cmp_w1_b[l], cmp_w2_b[l])
        nsa_p = lambda z, n_q, n_kv, n_g, cmp_w=cmp_w: nsa_prompt_pallas(
            z.reshape(-1, IN_COLS_PAD), z.shape[0], z.shape[1], cmp_w)
        past = cache_nsa_kv[l, page_table].reshape((db, past_len) + cache_nsa_kv.shape[3:])
        nsa_s = lambda z, n_q, n_kv, n_g, past=past, l=l, cmp_params=cmp_params: nsa_sample(
            n_q, n_kv, n_g, past_rows=past, win_buf=state_nsa_win[l], cmp_params=cmp_params)
        xp, st_p = trunk_layer(xp, conv0, c0, n0, m0, s0, nsa_p, *weights)
        xs, st_s = trunk_layer(xs, state_mlstm_conv[l], state_mlstm_C[l], state_mlstm_n[l],
                               state_mlstm_m[l], state_gla_S[l], nsa_s, *weights)
        for acc, a in zip(acc_p, st_p):
            acc.append(a.astype(xp.dtype))
        for acc, a in zip(acc_s, st_s):
            acc.append(a.astype(xs.dtype))
    nsa_rows_p, nsa_win_p, mlstm_c_p, mlstm_n_p, mlstm_m_p, mlstm_conv_p, gla_s_p = [jnp.stack(a) for a in acc_p]
    nsa_rows_s, nsa_win_s, mlstm_c_s, mlstm_n_s, mlstm_m_s, mlstm_conv_s, gla_s_s = [jnp.stack(a) for a in acc_s]
    return (xp, xs, nsa_rows_p, nsa_rows_s, nsa_win_p, nsa_win_s, mlstm_c_p, mlstm_c_s,
            mlstm_n_p, mlstm_n_s, mlstm_m_p, mlstm_m_s, mlstm_conv_p, mlstm_conv_s, gla_s_p, gla_s_s)
```

```python
import functools
import math

import jax
import jax.numpy as jnp
import numpy as np
from jax import lax
from jax.experimental import pallas as pl
from jax.experimental.pallas import tpu as pltpu

D_MODEL = 2048
DEPTH = 4
PAGE_SIZE = 128
HEAD_DIM = 128
D_MIX = D_MODEL
M_WIDTH = D_MIX // 4
G_WIDTH = D_MIX // 4
N_WIDTH = D_MIX - M_WIDTH - G_WIDTH
H_M = M_WIDTH // HEAD_DIM
H_N = N_WIDTH // HEAD_DIM
H_G = G_WIDTH // HEAD_DIM
KV_GROUPS = 2
Q_PER_KV = H_N // KV_GROUPS
D_FF = 4 * D_MODEL
CONV_W = 4
CHUNK = 64
CMP_STRIDE = 16
CMP_LEN = 2 * CMP_STRIDE
CMP_HID = 256
SEL_BLK = 64
N_SELECT = 16
WINDOW = 512
QBLK = 128
GATE_RANK = 16
GLA_GATE_TEMP = 16.0
ROPE_THETA = 10000.0
EPS = 1e-6
TINY = 1e-30
FORCE_BONUS = 1e3
NEG_BIG = -1e9
SPLIT_SIZES = (2 * M_WIDTH, M_WIDTH, M_WIDTH, 2 * H_M,
               N_WIDTH, 6 * KV_GROUPS * HEAD_DIM, 3 * H_N,
               G_WIDTH, G_WIDTH, G_WIDTH, G_WIDTH, GATE_RANK)
IN_COLS = sum(SPLIT_SIZES)
SPLIT_OFFSETS = tuple(int(o) for o in np.cumsum(SPLIT_SIZES)[:-1])

F32 = jnp.float32
BF16 = jnp.bfloat16

VMEM_LIMIT_BYTES = 56 * 1024 * 1024
IN_TN = 512
COL_NQ = 2 * M_WIDTH + 2 * M_WIDTH
COL_NKV = COL_NQ + N_WIDTH
COL_GLA = COL_NKV + 6 * KV_GROUPS * HEAD_DIM
COL_SMALL = COL_GLA + 4 * G_WIDTH
SMALL_MIF = 0
SMALL_NG = 2 * H_M
SMALL_GLR = SMALL_NG + 3 * H_N
IN_COLS_PAD = 7168
NEG = -1e30
NSA_TQ = 128
NSA_TK = 512


def _rms(x, g):
    return x * lax.rsqrt(jnp.mean(x * x, axis=-1, keepdims=True) + EPS) * g


def _norm_matmul_kernel(x_ref, g_ref, w_ref, o_ref, xn_ref):
    @pl.when(pl.program_id(1) == 0)
    def _():
        xn_ref[...] = _rms(x_ref[...], g_ref[...]).astype(BF16)

    o_ref[...] = jnp.dot(xn_ref[...], w_ref[...], preferred_element_type=F32)


def norm_matmul(x, g, w, tn):
    m, k = x.shape
    n = w.shape[1]
    tm = min(m, 1024)
    return pl.pallas_call(
        _norm_matmul_kernel,
        grid=(m // tm, n // tn),
        in_specs=[pl.BlockSpec((tm, k), lambda i, j: (i, 0)),
                  pl.BlockSpec((1, k), lambda i, j: (0, 0)),
                  pl.BlockSpec((k, tn), lambda i, j: (0, j))],
        out_specs=pl.BlockSpec((tm, tn), lambda i, j: (i, j)),
        out_shape=jax.ShapeDtypeStruct((m, n), F32),
        scratch_shapes=[pltpu.VMEM((tm, k), BF16)],
        compiler_params=pltpu.CompilerParams(
            dimension_semantics=("arbitrary", "arbitrary"), vmem_limit_bytes=VMEM_LIMIT_BYTES),
        name="norm_matmul",
    )(x, g.reshape(1, k), w)


def _matmul_norm_res_kernel(a0_ref, a1_ref, a2_ref, w_ref, g_ref, r_ref, o_ref):
    k0, k1 = a0_ref.shape[1], a0_ref.shape[1] + a1_ref.shape[1]
    y = (jnp.dot(a0_ref[...], w_ref[:k0, :], preferred_element_type=F32)
         + jnp.dot(a1_ref[...], w_ref[k0:k1, :], preferred_element_type=F32)
         + jnp.dot(a2_ref[...], w_ref[k1:, :], preferred_element_type=F32))
    o_ref[...] = r_ref[...] + _rms(y, g_ref[...])


def matmul_norm_res(a0, a1, a2, w, g, r):
    m = a0.shape[0]
    k, n = w.shape
    tm = min(m, 512)
    rows = lambda a: pl.BlockSpec((tm, a.shape[1]), lambda i: (i, 0))
    return pl.pallas_call(
        _matmul_norm_res_kernel,
        grid=(m // tm,),
        in_specs=[rows(a0), rows(a1), rows(a2),
                  pl.BlockSpec((k, n), lambda i: (0, 0)),
                  pl.BlockSpec((1, n), lambda i: (0, 0)),
                  pl.BlockSpec((tm, n), lambda i: (i, 0))],
        out_specs=pl.BlockSpec((tm, n), lambda i: (i, 0)),
        out_shape=jax.ShapeDtypeStruct((m, n), F32),
        compiler_params=pltpu.CompilerParams(
            dimension_semantics=("arbitrary",), vmem_limit_bytes=VMEM_LIMIT_BYTES),
        name="matmul_norm_res",
    )(a0, a1, a2, w, g.reshape(1, n), r)


def _ffn_kernel(x_ref, g2_ref, w1_ref, w2_ref, g3_ref, o_ref, xn_ref, acc_ref):
    f = pl.program_id(1)

    @pl.when(f == 0)
    def _():
        xn_ref[...] = _rms(x_ref[...], g2_ref[...]).astype(BF16)
        acc_ref[...] = jnp.zeros_like(acc_ref)

    h = jnp.dot(xn_ref[...], w1_ref[...], preferred_element_type=F32)
    a = jnp.square(jnp.maximum(h, 0.0)).astype(BF16)
    acc_ref[...] += jnp.dot(a, w2_ref[...], preferred_element_type=F32)

    @pl.when(f == pl.num_programs(1) - 1)
    def _():
        o_ref[...] = x_ref[...] + _rms(acc_ref[...], g3_ref[...])


def ffn(x, g2, w1, w2, g3):
    m, d = x.shape
    dff = w1.shape[1]
    tm = min(m, 512)
    tf = 512
    return pl.pallas_call(
        _ffn_kernel,
        grid=(m // tm, dff // tf),
        in_specs=[pl.BlockSpec((tm, d), lambda i, f: (i, 0)),
                  pl.BlockSpec((1, d), lambda i, f: (0, 0)),
                  pl.BlockSpec((d, tf), lambda i, f: (0, f)),
                  pl.BlockSpec((tf, d), lambda i, f: (f, 0)),
                  pl.BlockSpec((1, d), lambda i, f: (0, 0))],
        out_specs=pl.BlockSpec((tm, d), lambda i, f: (i, 0)),
        out_shape=jax.ShapeDtypeStruct((m, d), F32),
        scratch_shapes=[pltpu.VMEM((tm, d), BF16), pltpu.VMEM((tm, d), F32)],
        compiler_params=pltpu.CompilerParams(
            dimension_semantics=("arbitrary", "arbitrary"), vmem_limit_bytes=VMEM_LIMIT_BYTES),
        name="ffn",
    )(x, g2.reshape(1, d), w1, w2, g3.reshape(1, d))


def rope_tables(pos):
    half = HEAD_DIM // 2
    inv_freq = jnp.exp(-math.log(ROPE_THETA) * jnp.arange(half, dtype=F32) / half)
    ang = pos.astype(F32)[:, None] * inv_freq[None, :]
    cos, sin = jnp.cos(ang), jnp.sin(ang)
    return jnp.concatenate([cos, cos], axis=-1), jnp.concatenate([-sin, sin], axis=-1)


def _rope(x, cos2, sin2):
    return x * cos2 + pltpu.roll(x, HEAD_DIM // 2, 1) * sin2


def _nsa_prep_kernel(nq_ref, nkv_ref, cos_ref, sin_ref, q_ref, rows_ref, win_ref, kvb_ref, cmp_ref):
    cos2 = cos_ref[...]
    sin2 = sin_ref[...]
    for h in range(H_N):
        sl = slice(h * HEAD_DIM, (h + 1) * HEAD_DIM)
        q_ref[:, sl] = (_rope(nq_ref[:, sl], cos2, sin2) * (HEAD_DIM ** -0.5)).astype(BF16)
    for c in range(6 * KV_GROUPS):
        slot = c // KV_GROUPS
        x = nkv_ref[:, c * HEAD_DIM:(c + 1) * HEAD_DIM]
        if slot % 2 == 0:
            x = _rope(x, cos2, sin2)
        xb = x.astype(BF16)
        if slot < 4:
            rows_ref[:, c * HEAD_DIM:(c + 1) * HEAD_DIM] = x
        else:
            win_ref[:, (c - 8) * HEAD_DIM:(c - 7) * HEAD_DIM] = x
        if slot < 2:
            cmp_ref[0, c] = xb
        else:
            kvb_ref[:, (c - 4) * HEAD_DIM:(c - 3) * HEAD_DIM] = xb


def nsa_prep_pallas(z, b, s, cos2, sin2):
    m = b * s
    tm = 512
    nsb = s // tm
    return pl.pallas_call(
        _nsa_prep_kernel,
        grid=(m // tm,),
        in_specs=[pl.BlockSpec((tm, N_WIDTH), lambda i: (i, COL_NQ // N_WIDTH)),
                  pl.BlockSpec((tm, 1536), lambda i: (i, COL_NKV // 1536)),
                  pl.BlockSpec((tm, HEAD_DIM), lambda i: (i % nsb, 0)),
                  pl.BlockSpec((tm, HEAD_DIM), lambda i: (i % nsb, 0))],
        out_specs=[pl.BlockSpec((tm, N_WIDTH), lambda i: (i, 0)),
                   pl.BlockSpec((tm, 1024), lambda i: (i, 0)),
                   pl.BlockSpec((tm, 512), lambda i: (i, 0)),
                   pl.BlockSpec((tm, 1024), lambda i: (i, 0)),
                   pl.BlockSpec((1, 4, tm, HEAD_DIM), lambda i: (i // nsb, 0, i % nsb, 0))],
        out_shape=[jax.ShapeDtypeStruct((m, N_WIDTH), BF16),
                   jax.ShapeDtypeStruct((m, 1024), F32),
                   jax.ShapeDtypeStruct((m, 512), F32),
                   jax.ShapeDtypeStruct((m, 1024), BF16),
                   jax.ShapeDtypeStruct((b, 4, s, HEAD_DIM), BF16)],
        compiler_params=pltpu.CompilerParams(
            dimension_semantics=("arbitrary",), vmem_limit_bytes=VMEM_LIMIT_BYTES),
        name="nsa_prep",
    )(z, z, cos2, sin2)


def _nsa_cmp_kernel(x_ref, pe_ref, w1_ref, w2_ref, o_ref):
    x = x_ref[0, 0]
    nb = x.shape[0]
    half = CMP_STRIDE * HEAD_DIM
    first = jnp.dot(x, w1_ref[0, :half, :], preferred_element_type=F32)
    second = jnp.dot(x, w1_ref[0, half:, :], preferred_element_type=F32)
    pe = jnp.broadcast_to(pe_ref[0], (8, 2 * half)).astype(BF16)
    bias = jnp.dot(pe, w1_ref[0], preferred_element_type=F32)[0:1]
    hid = first + pltpu.roll(second, nb - 1, 0) + bias
    o_ref[0, 0] = jnp.dot(jax.nn.gelu(hid).astype(BF16), w2_ref[0], preferred_element_type=F32).astype(BF16)


def nsa_compress_prompt(cmp_in, pe, w1, w2):
    b, _, nb, kdim = cmp_in.shape
    return pl.pallas_call(
        _nsa_cmp_kernel,
        grid=(b, 4),
        in_specs=[pl.BlockSpec((1, 1, nb, kdim), lambda i, c: (i, c, 0, 0)),
                  pl.BlockSpec((1, 1, 2 * kdim), lambda i, c: (c // 2, 0, 0)),
                  pl.BlockSpec((1, 2 * kdim, CMP_HID), lambda i, c: (c // 2, 0, 0)),
                  pl.BlockSpec((1, CMP_HID, HEAD_DIM), lambda i, c: (c // 2, 0, 0))],
        out_specs=pl.BlockSpec((1, 1, nb, HEAD_DIM), lambda i, c: (i, c, 0, 0)),
        out_shape=jax.ShapeDtypeStruct((b, 4, nb, HEAD_DIM), BF16),
        compiler_params=pltpu.CompilerParams(
            dimension_semantics=("arbitrary", "arbitrary"), vmem_limit_bytes=VMEM_LIMIT_BYTES),
        name="nsa_compress",
    )(cmp_in, pe, w1, w2)


def _softmax_rows(s, mask):
    sm = jnp.where(mask, s, NEG)
    m = jnp.max(sm, axis=-1, keepdims=True)
    p = jnp.where(mask, jnp.exp(sm - m), 0.0)
    return p / jnp.maximum(jnp.sum(p, axis=-1, keepdims=True), TINY)


_NT = (((1,), (1,)), ((), ()))


def _nsa_attn_kernel(q_ref, kc_ref, vc_ref, ks_ref, vs_ref, kw_ref, vw_ref, gate_ref, o_ref):
    tq = NSA_TQ
    r4 = Q_PER_KV
    g = pl.program_id(1)
    i = pl.program_id(2)
    q4 = q_ref[...]
    q = jnp.concatenate([q4[:, r * HEAD_DIM:(r + 1) * HEAD_DIM] for r in range(r4)], axis=0)
    q0 = i * tq
    tpos = q0 + lax.broadcasted_iota(jnp.int32, (tq, 1), 0)

    ncp = kc_ref.shape[2]
    s = lax.dot_general(q, kc_ref[0, 0], _NT, preferred_element_type=F32).reshape(r4, tq, ncp)
    cend = lax.broadcasted_iota(jnp.int32, (tq, ncp), 1) * CMP_STRIDE + (CMP_LEN - 1)
    p_c = _softmax_rows(s, (cend <= tpos)[None])
    o_c = jnp.dot(p_c.reshape(r4 * tq, ncp).astype(BF16), vc_ref[0, 0], preferred_element_type=F32)

    psum = p_c[0] + p_c[1] + p_c[2] + p_c[3]
    n_i = lax.broadcasted_iota(jnp.int32, (ncp, HEAD_DIM), 0)
    j_i = lax.broadcasted_iota(jnp.int32, (ncp, HEAD_DIM), 1)
    ratio = SEL_BLK // CMP_STRIDE
    c2s = jnp.where((n_i >= ratio * j_i - 1) & (n_i <= ratio * j_i + ratio - 1) & (n_i < ncp - 1)
                    & (j_i < ncp // ratio), 1.0, 0.0).astype(BF16)
    p_hi = psum.astype(BF16)
    p_lo = (psum - p_hi.astype(F32)).astype(BF16)
    imp = (jnp.dot(p_hi, c2s, preferred_element_type=F32)
           + jnp.dot(p_lo, c2s, preferred_element_type=F32))

    ns = ncp // ratio
    imp_t = jnp.transpose(imp)[:ns]
    jj = lax.broadcasted_iota(jnp.int32, (ns, tq), 0)
    tt = q0 + lax.broadcasted_iota(jnp.int32, (ns, tq), 1)
    cur = tt // SEL_BLK
    forced = (jj == 0) | (jj == cur) | (jj == cur - 1)
    score = jnp.where(jj * SEL_BLK <= tt, imp_t + jnp.where(forced, FORCE_BONUS, 0.0), NEG_BIG)
    rank = jnp.zeros((ns, tq), F32)
    for jp in range(ns):
        row = score[jp:jp + 1, :]
        rank = rank + jnp.where(row > score, 1.0, jnp.where((row == score) & (jj > jp), 1.0, 0.0))
    sel_t = jnp.where(rank < float(N_SELECT), 1.0, 0.0)
    sel_t = jnp.concatenate([sel_t, jnp.zeros((HEAD_DIM - ns, tq), F32)], axis=0)
    sel = jnp.transpose(sel_t).astype(BF16)

    tk = NSA_TK
    n_tiles = (q0 + tq + tk - 1) // tk

    def sel_tile(kt, carry):
        m_run, l_run, acc = carry
        k0 = pl.multiple_of(kt * tk, tk)
        s = lax.dot_general(q, ks_ref[pl.ds(k0, tk), :], _NT, preferred_element_type=F32).reshape(r4, tq, tk)
        blk = lax.broadcasted_iota(jnp.int32, (HEAD_DIM, tk), 0)
        key = lax.broadcasted_iota(jnp.int32, (HEAD_DIM, tk), 1)
        expand = jnp.where(blk == kt * (tk // SEL_BLK) + key // SEL_BLK, 1.0, 0.0).astype(BF16)
        picked = jnp.dot(sel, expand, preferred_element_type=F32)
        kpos = k0 + lax.broadcasted_iota(jnp.int32, (tq, tk), 1)
        mask = ((picked > 0.5) & (kpos <= tpos))[None]
        sm = jnp.where(mask, s, NEG)
        m_new = jnp.maximum(m_run, jnp.max(sm, axis=-1, keepdims=True))
        alpha = jnp.exp(m_run - m_new)
        p = jnp.where(mask, jnp.exp(sm - m_new), 0.0)
        l_new = alpha * l_run + jnp.sum(p, axis=-1, keepdims=True)
        pv = jnp.dot(p.reshape(r4 * tq, tk).astype(BF16), vs_ref[pl.ds(k0, tk), :], preferred_element_type=F32)
        return m_new, l_new, alpha.reshape(r4 * tq, 1) * acc + pv

    init = (jnp.full((r4, tq, 1), NEG, F32), jnp.zeros((r4, tq, 1), F32), jnp.zeros((r4 * tq, HEAD_DIM), F32))
    _, l_s, acc_s = lax.fori_loop(0, n_tiles, sel_tile, init)
    o_s = acc_s / jnp.maximum(l_s, TINY).reshape(r4 * tq, 1)

    wb = WINDOW + tq
    w0 = pl.multiple_of(jnp.maximum(q0 - WINDOW, 0), tq)
    s = lax.dot_general(q, kw_ref[pl.ds(w0, wb), :], _NT, preferred_element_type=F32).reshape(r4, tq, wb)
    dist = tpos - (w0 + lax.broadcasted_iota(jnp.int32, (tq, wb), 1))
    p_w = _softmax_rows(s, ((dist >= 0) & (dist < WINDOW))[None])
    o_w = jnp.dot(p_w.reshape(r4 * tq, wb).astype(BF16), vw_ref[pl.ds(w0, wb), :], preferred_element_type=F32)

    gates = jax.nn.sigmoid(gate_ref[...])
    for r in range(r4):
        def gate(c):
            lane0 = SMALL_NG + 3 * r + c
            lane1 = lane0 + 3 * r4
            return jnp.where(g == 0, gates[:, lane0:lane0 + 1], gates[:, lane1:lane1 + 1])
        rows = slice(r * tq, (r + 1) * tq)
        o = gate(0) * o_c[rows] + gate(1) * o_s[rows] + gate(2) * o_w[rows]
        o_ref[:, r * HEAD_DIM:(r + 1) * HEAD_DIM] = o.astype(BF16)


def nsa_attention_prompt(q_b, kcvc, kvb, z, b, s):
    tq = NSA_TQ
    nq = s // tq
    ncp = kcvc.shape[2]
    gw = Q_PER_KV * HEAD_DIM
    seq_block = lambda col: pl.BlockSpec((s, HEAD_DIM), lambda bi, g, i: (bi, col + g))
    return pl.pallas_call(
        _nsa_attn_kernel,
        grid=(b, KV_GROUPS, nq),
        in_specs=[pl.BlockSpec((tq, gw), lambda bi, g, i: (bi * nq + i, g)),
                  pl.BlockSpec((1, 1, ncp, HEAD_DIM), lambda bi, g, i: (bi, g, 0, 0)),
                  pl.BlockSpec((1, 1, ncp, HEAD_DIM), lambda bi, g, i: (bi, KV_GROUPS + g, 0, 0)),
                  seq_block(0), seq_block(2), seq_block(4), seq_block(6),
                  pl.BlockSpec((tq, HEAD_DIM), lambda bi, g, i: (bi * nq + i, COL_SMALL // HEAD_DIM))],
        out_specs=pl.BlockSpec((tq, gw), lambda bi, g, i: (bi * nq + i, g)),
        out_shape=jax.ShapeDtypeStruct((b * s, N_WIDTH), BF16),
        compiler_params=pltpu.CompilerParams(
            dimension_semantics=("arbitrary", "arbitrary", "arbitrary"), vmem_limit_bytes=VMEM_LIMIT_BYTES),
        name="nsa_attention",
    )(q_b, kcvc, kcvc, kvb, kvb, kvb, kvb, z)


def nsa_prompt_pallas(z, b, s, cmp_w):
    pe, w1, w2 = cmp_w
    cos2, sin2 = rope_tables(jnp.arange(s))
    q_b, rows_f, win_f, kvb, cmp_in = nsa_prep_pallas(z, b, s, cos2, sin2)
    kcvc = nsa_compress_prompt(cmp_in.reshape(b, 4, s // CMP_STRIDE, CMP_STRIDE * HEAD_DIM), pe, w1, w2)
    hn = nsa_attention_prompt(q_b, kcvc, kvb, z, b, s)
    rows = rows_f.reshape(b, s, 4, KV_GROUPS, HEAD_DIM)
    win = win_f.reshape(b, s, 2, KV_GROUPS, HEAD_DIM)[:, -min(WINDOW, s):]
    return hn, (rows, win)


MIX_L = 128
GLA_SUB = 16


def _logsig(x):
    return jnp.minimum(x, 0.0) - jnp.log1p(jnp.exp(-jnp.abs(x)))


def _prefix_sum(x, axis):
    n = x.shape[axis]
    idx = lax.broadcasted_iota(jnp.int32, x.shape, axis)
    step = 1
    while step < n:
        x = x + jnp.where(idx >= step, pltpu.roll(x, step, axis), 0.0)
        step *= 2
    return x


def _mlstm_kernel(uqk_ref, v_ref, og_ref, small_ref, convw_ref, bias_ref, norm_ref, conv0_ref, c0_ref, n0_ref,
                  m0_ref, h_ref, c_ref, n_ref, m_ref, prev_sc, c_sc, n_sc, m_sc, *, t_valid):
    t = pl.program_id(1)
    L = MIX_L
    d = HEAD_DIM

    @pl.when(t == 0)
    def _():
        prev_sc[...] = jnp.zeros_like(prev_sc)
        prev_sc[L - 8:, :] = conv0_ref[0]
        c_sc[...] = c0_ref[0]
        n_sc[...] = n0_ref[0]
        m_sc[...] = m0_ref[0]

    x = uqk_ref[...]
    prev = prev_sc[...]
    row = lax.broadcasted_iota(jnp.int32, x.shape, 0)
    w = convw_ref[...]
    conv = w[CONV_W - 1:CONV_W] * x
    for k in range(1, CONV_W):
        shifted = jnp.where(row >= k, pltpu.roll(x, k, 0), pltpu.roll(prev, k, 0))
        conv = conv + w[CONV_W - 1 - k:CONV_W - k] * shifted
    prev_sc[...] = x
    act = conv * jax.nn.sigmoid(conv)

    pre = small_ref[...] + bias_ref[...]
    pos_c = t * L + lax.broadcasted_iota(jnp.int32, (L, HEAD_DIM), 0)
    ig_c = jnp.where(pos_c < t_valid, pre, NEG)
    b_c = _prefix_sum(jnp.where(pos_c < t_valid, _logsig(pre), 0.0), 0)
    pre_r = jnp.transpose(pre)[0:8]
    pos_r = t * L + lax.broadcasted_iota(jnp.int32, (8, L), 1)
    ig_r = jnp.where(pos_r < t_valid, pre_r, NEG)
    b_r = _prefix_sum(jnp.where(pos_r < t_valid, _logsig(pre_r), 0.0), 1)

    li = lax.broadcasted_iota(jnp.int32, (L, L), 0)
    si = lax.broadcasted_iota(jnp.int32, (L, L), 1)
    for h in range(H_M):
        sl = slice(h * d, (h + 1) * d)
        bcol, igcol = b_c[:, H_M + h:H_M + h + 1], ig_c[:, h:h + 1]
        brow, igrow = b_r[H_M + h:H_M + h + 1, :], ig_r[h:h + 1, :]
        m0 = m_sc[h:h + 1, 0:1]
        c0 = c_sc[h]
        n0 = n_sc[h:h + 1, :]
        qf = act[:, sl]
        kf = act[:, M_WIDTH + h * d:M_WIDTH + (h + 1) * d] * (d ** -0.5)
        vf = v_ref[:, sl]
        qb, kb, vb = qf.astype(BF16), kf.astype(BF16), vf.astype(BF16)
        logw = jnp.where(si <= li, bcol - brow + igrow, NEG)
        gsum = bcol + m0
        m_row = jnp.maximum(jnp.max(logw, axis=1, keepdims=True), gsum)
        wgt = jnp.exp(logw - m_row) * lax.dot_general(qb, kb, _NT, preferred_element_type=F32)
        inter = jnp.exp(gsum - m_row)
        num = (jnp.dot(wgt.astype(BF16), vb, preferred_element_type=F32)
               + inter * lax.dot_general(qb, c0.astype(BF16), _NT, preferred_element_type=F32))
        den = jnp.sum(wgt, axis=1, keepdims=True) + inter * jnp.sum(qf * n0, axis=1, keepdims=True)
        hh = num / jnp.maximum(jnp.abs(den), jnp.exp(-m_row))
        bl = bcol[L - 1:L, :]
        m_new = jnp.maximum(jnp.max(bl - brow + igrow, axis=1, keepdims=True), bl + m0)
        wa = jnp.exp(bl - bcol + igcol - m_new)
        wc = jnp.exp(bl + m0 - m_new)
        c_sc[h] = wc * c0 + jnp.dot(jnp.transpose(vf * wa).astype(BF16), kb, preferred_element_type=F32)
        n_sc[h:h + 1, :] = wc * n0 + jnp.sum(wa * kf, axis=0, keepdims=True)
        m_sc[h:h + 1, :] = jnp.broadcast_to(m_new, (1, d))
        out = _rms(hh, norm_ref[:, sl]) * jax.nn.sigmoid(og_ref[:, sl])
        h_ref[:, sl] = out.astype(BF16)

    @pl.when(t == pl.num_programs(1) - 1)
    def _():
        c_ref[0] = c_sc[...]
        n_ref[0] = n_sc[...]
        m_ref[0] = m_sc[...]


def mlstm_pallas(z, b, t_pad, t_valid, conv_w, gate_b, norm, conv0, c0, n0, m0):
    L = MIX_L
    nt = t_pad // L
    rowblk = lambda col: (lambda bi, t: (bi * nt + t, col))
    const2 = lambda bi, t: (0, 0)
    per_b3 = lambda bi, t: (bi, 0, 0)
    per_b4 = lambda bi, t: (bi, 0, 0, 0)
    return pl.pallas_call(
        functools.partial(_mlstm_kernel, t_valid=t_valid),
        grid=(b, nt),
        in_specs=[pl.BlockSpec((L, 2 * M_WIDTH), rowblk(0)),
                  pl.BlockSpec((L, M_WIDTH), rowblk(2)),
                  pl.BlockSpec((L, M_WIDTH), rowblk(3)),
                  pl.BlockSpec((L, HEAD_DIM), rowblk(COL_SMALL // HEAD_DIM)),
                  pl.BlockSpec((CONV_W, 2 * M_WIDTH), const2),
                  pl.BlockSpec((1, HEAD_DIM), const2),
                  pl.BlockSpec((1, M_WIDTH), const2),
                  pl.BlockSpec((1, 8, 2 * M_WIDTH), per_b3),
                  pl.BlockSpec((1, H_M, HEAD_DIM, HEAD_DIM), per_b4),
                  pl.BlockSpec((1, 8, HEAD_DIM), per_b3),
                  pl.BlockSpec((1, 8, HEAD_DIM), per_b3)],
        out_specs=[pl.BlockSpec((L, M_WIDTH), rowblk(0)),
                   pl.BlockSpec((1, H_M, HEAD_DIM, HEAD_DIM), per_b4),
                   pl.BlockSpec((1, 8, HEAD_DIM), per_b3),
                   pl.BlockSpec((1, 8, HEAD_DIM), per_b3)],
        out_shape=[jax.ShapeDtypeStruct((b * t_pad, M_WIDTH), BF16),
                   jax.ShapeDtypeStruct((b, H_M, HEAD_DIM, HEAD_DIM), F32),
                   jax.ShapeDtypeStruct((b, 8, HEAD_DIM), F32),
                   jax.ShapeDtypeStruct((b, 8, HEAD_DIM), F32)],
        scratch_shapes=[pltpu.VMEM((L, 2 * M_WIDTH), F32),
                        pltpu.VMEM((H_M, HEAD_DIM, HEAD_DIM), F32),
                        pltpu.VMEM((8, HEAD_DIM), F32),
                        pltpu.VMEM((8, HEAD_DIM), F32)],
        compiler_params=pltpu.CompilerParams(
            dimension_semantics=("arbitrary", "arbitrary"), vmem_limit_bytes=VMEM_LIMIT_BYTES),
        name="mlstm",
    )(z, z, z, z, conv_w, gate_b, norm, conv0, c0, n0, m0)


def _gla_kernel(q_ref, k_ref, v_ref, r_ref, small_ref, w2_ref, gb_ref, norm_ref, s0_ref, h_ref, s_ref, st_sc,
                *, t_valid):
    t = pl.program_id(1)
    L = MIX_L
    d = HEAD_DIM

    @pl.when(t == 0)
    def _():
        for h in range(H_G):
            st_sc[h] = jnp.transpose(s0_ref[0, h])

    pre = jnp.dot(small_ref[...].astype(BF16), w2_ref[...], preferred_element_type=F32) + gb_ref[...]
    pos = t * L + lax.broadcasted_iota(jnp.int32, (L, G_WIDTH), 0)
    la = jnp.where(pos < t_valid, _logsig(pre) / GLA_GATE_TEMP, 0.0)
    bc_all = _prefix_sum(la, 0)
    rowi = lax.broadcasted_iota(jnp.int32, (L, d), 0)
    li = lax.broadcasted_iota(jnp.int32, (L, L), 0)
    si = lax.broadcasted_iota(jnp.int32, (L, L), 1)
    for h in range(H_G):
        sl = slice(h * d, (h + 1) * d)
        bc = bc_all[:, sl]
        qf = q_ref[:, sl] * (d ** -0.5)
        kf = jnp.where(t * L + rowi < t_valid, k_ref[:, sl], 0.0)
        vf = v_ref[:, sl]
        qparts, kparts = [], []
        for j in range(L // GLA_SUB):
            lo, hi = j * GLA_SUB, (j + 1) * GLA_SUB
            e = bc[hi - 1:hi, :]
            qparts.append((qf * jnp.exp(jnp.where(rowi >= lo, bc - e, NEG))).astype(BF16))
            kparts.append((kf * jnp.exp(jnp.where((rowi >= lo) & (rowi < hi), e - bc, NEG))).astype(BF16))
        a = lax.dot_general(jnp.concatenate(qparts, axis=1), jnp.concatenate(kparts, axis=1), _NT,
                            preferred_element_type=F32)
        a = jnp.where(si <= li, a, 0.0)
        st = st_sc[h]
        o = (jnp.dot(a.astype(BF16), vf.astype(BF16), preferred_element_type=F32)
             + lax.dot_general((qf * jnp.exp(bc)).astype(BF16), st.astype(BF16), _NT, preferred_element_type=F32))
        bl = bc[L - 1:L, :]
        kd = (kf * jnp.exp(bl - bc)).astype(BF16)
        st_sc[h] = st * jnp.exp(bl) + jnp.dot(jnp.transpose(vf).astype(BF16), kd, preferred_element_type=F32)
        gate = r_ref[:, sl]
        h_ref[:, sl] = (_rms(o, norm_ref[:, sl]) * (gate * jax.nn.sigmoid(gate))).astype(BF16)

    @pl.when(t == pl.num_programs(1) - 1)
    def _():
        for h in range(H_G):
            s_ref[0, h] = jnp.transpose(st_sc[h])


def gla_pallas(z, b, t_pad, t_valid, w2_pad, gate_b, norm, s0):
    L = MIX_L
    nt = t_pad // L
    rowblk = lambda col: (lambda bi, t: (bi * nt + t, col))
    const2 = lambda bi, t: (0, 0)
    per_b4 = lambda bi, t: (bi, 0, 0, 0)
    gcol = COL_GLA // G_WIDTH
    return pl.pallas_call(
        functools.partial(_gla_kernel, t_valid=t_valid),
        grid=(b, nt),
        in_specs=[pl.BlockSpec((L, G_WIDTH), rowblk(gcol)),
                  pl.BlockSpec((L, G_WIDTH), rowblk(gcol + 1)),
                  pl.BlockSpec((L, G_WIDTH), rowblk(gcol + 2)),
                  pl.BlockSpec((L, G_WIDTH), rowblk(gcol + 3)),
                  pl.BlockSpec((L, HEAD_DIM), rowblk(COL_SMALL // HEAD_DIM)),
                  pl.BlockSpec((HEAD_DIM, G_WIDTH), const2),
                  pl.BlockSpec((1, G_WIDTH), const2),
                  pl.BlockSpec((1, G_WIDTH), const2),
                  pl.BlockSpec((1, H_G, HEAD_DIM, HEAD_DIM), per_b4)],
        out_specs=[pl.BlockSpec((L, G_WIDTH), rowblk(0)),
                   pl.BlockSpec((1, H_G, HEAD_DIM, HEAD_DIM), per_b4)],
        out_shape=[jax.ShapeDtypeStruct((b * t_pad, G_WIDTH), BF16),
                   jax.ShapeDtypeStruct((b, H_G, HEAD_DIM, HEAD_DIM), F32)],
        scratch_shapes=[pltpu.VMEM((H_G, HEAD_DIM, HEAD_DIM), F32)],
        compiler_params=pltpu.CompilerParams(
            dimension_semantics=("arbitrary", "arbitrary"), vmem_limit_bytes=VMEM_LIMIT_BYTES),
        name="gla",
    )(z, z, z, z, z, w2_pad, gate_b, norm, s0)


def mixers_pallas(z, b, t, conv_buf, c0, n0, m0, s0, conv_w, m_gate_b, m_norm, g_w2, g_b, g_norm):
    t_pad = z.shape[0] // b
    lanes = lambda a: jnp.pad(a.reshape(1, -1), ((0, 0), (0, HEAD_DIM - a.size)))
    conv0 = jnp.pad(conv_buf, ((0, 0), (8 - (CONV_W - 1), 0), (0, 0)))
    n0p = jnp.pad(n0, ((0, 0), (0, 8 - H_M), (0, 0)))
    m0p = jnp.broadcast_to(jnp.pad(m0, ((0, 0), (0, 8 - H_M)))[:, :, None], (b, 8, HEAD_DIM))
    hm, c, n, m = mlstm_pallas(z, b, t_pad, t, conv_w, lanes(m_gate_b), m_norm.reshape(1, -1), conv0, c0, n0p, m0p)
    w2_pad = jnp.pad(g_w2, ((SMALL_GLR, HEAD_DIM - SMALL_GLR - GATE_RANK), (0, 0))).astype(BF16)
    hg, s = gla_pallas(z, b, t_pad, t, w2_pad, g_b.reshape(1, -1), g_norm.reshape(1, -1), s0)
    valid = lambda a: a.reshape(b, t_pad, -1)[:, :t]
    return valid(hm), valid(hg), c, n[:, :H_M], m[:, :H_M, 0], s


def rmsnorm(x, g):
    xf = x.astype(F32)
    y = xf * lax.rsqrt(jnp.mean(xf * xf, axis=-1, keepdims=True) + EPS)
    return (y * g.astype(F32)).astype(x.dtype)


def rope(x, pos):
    half = x.shape[-1] // 2
    inv_freq = jnp.exp(-math.log(ROPE_THETA) * jnp.arange(half, dtype=F32) / half)
    ang = pos.astype(F32)[:, None] * inv_freq[None, :]
    cos = jnp.cos(ang)[:, None, :]
    sin = jnp.sin(ang)[:, None, :]
    xf = x.astype(F32)
    x1, x2 = xf[..., :half], xf[..., half:]
    return jnp.concatenate([x1 * cos - x2 * sin, x2 * cos + x1 * sin], axis=-1).astype(x.dtype)


def masked_softmax(s, mask):
    s = jnp.where(mask, s, -jnp.inf)
    m = jnp.max(s, axis=-1, keepdims=True)
    m = jnp.where(jnp.isfinite(m), m, 0.0)
    p = jnp.exp(s - m)
    return p / jnp.maximum(jnp.sum(p, axis=-1, keepdims=True), TINY)


def chunked_scan(step, carry, xs, chunk):
    b, t = xs[0].shape[:2]
    c = chunk if t % chunk == 0 else t
    n = t // c
    split = lambda a: jnp.moveaxis(a.reshape((b, n, c) + a.shape[2:]), 1, 0)
    carry, ys = lax.scan(step, carry, tuple(split(a) for a in xs))
    return carry, jnp.moveaxis(ys, 0, 1).reshape((b, t) + ys.shape[3:])


def mlstm_chunk(carry, inp):
    c0, n0, m0 = carry
    q, k, v, li, lf = inp
    L = q.shape[1]
    b = jnp.cumsum(lf, axis=1).transpose(0, 2, 1)
    ig = li.transpose(0, 2, 1)
    causal = jnp.tril(jnp.ones((L, L), dtype=bool))
    logw = jnp.where(causal, b[..., :, None] - b[..., None, :] + ig[..., None, :], -jnp.inf)
    g = b + m0[..., None]
    m_row = jnp.maximum(jnp.max(logw, axis=-1), g)
    w = jnp.exp(logw - m_row[..., None]) * jnp.einsum('blhd,bshd->bhls', q, k)
    inter = jnp.exp(g - m_row)
    num = jnp.einsum('bhls,bshd->bhld', w, v) + inter[..., None] * jnp.einsum('bhed,blhd->bhle', c0, q)
    den = jnp.sum(w, axis=-1) + inter * jnp.einsum('bhd,blhd->bhl', n0, q)
    h = num / jnp.maximum(jnp.abs(den), jnp.exp(-m_row))[..., None]
    bl = b[..., -1]
    a = bl[..., None] - b + ig
    m_new = jnp.maximum(jnp.max(a, axis=-1), bl + m0)
    wa = jnp.exp(a - m_new[..., None])
    wc = jnp.exp(bl + m0 - m_new)
    c_new = wc[..., None, None] * c0 + jnp.einsum('bhs,bshe,bshd->bhed', wa, v, k)
    n_new = wc[..., None] * n0 + jnp.einsum('bhs,bshd->bhd', wa, k)
    return (c_new, n_new, m_new), h.transpose(0, 2, 1, 3)


def gla_chunk(s0, inp):
    q, k, v, la = inp
    L = q.shape[1]
    bc = jnp.cumsum(la, axis=1)
    causal = jnp.tril(jnp.ones((L, L), dtype=bool))[None, :, :, None, None]
    decay = jnp.exp(jnp.where(causal, bc[:, :, None] - bc[:, None, :], -jnp.inf))
    a = jnp.einsum('bthk,bshk,btshk->bhts', q, k, decay)
    o = jnp.einsum('bhts,bshv->bthv', a, v) + jnp.einsum('bthk,bhkv->bthv', q * jnp.exp(bc), s0)
    bl = bc[:, -1]
    s_new = jnp.exp(bl)[..., None] * s0 + jnp.einsum('bshk,bshv->bhkv', k * jnp.exp(bl[:, None] - bc), v)
    return s_new, o


def nsa_compress(x, pe, w1, w2):
    b, l = x.shape[:2]
    nb = l // CMP_STRIDE
    xb = x[:, :nb * CMP_STRIDE].reshape(b, nb, CMP_STRIDE, KV_GROUPS, HEAD_DIM)
    first = jnp.einsum('bnjgd,jde->bnge', xb, w1[:CMP_STRIDE])
    second = jnp.einsum('bnjgd,jde->bnge', xb, w1[CMP_STRIDE:])
    hid = first[:, :-1] + second[:, 1:] + jnp.einsum('jd,jde->e', pe, w1)
    return jnp.einsum('bnge,ed->bngd', jax.nn.gelu(hid), w2)


def nsa_context(rows, cmp_params):
    pe, w1, w2 = cmp_params
    b, l = rows.shape[:2]
    kc = nsa_compress(rows[:, :, 0], pe[0], w1[0], w2[0])
    vc = nsa_compress(rows[:, :, 1], pe[1], w1[1], w2[1])
    nc = kc.shape[1]
    ns = -(-l // SEL_BLK)
    sel = jnp.pad(rows[:, :, 2:4], ((0, 0), (0, ns * SEL_BLK - l), (0, 0), (0, 0), (0, 0)))
    sel = sel.reshape(b, ns, SEL_BLK, 2, KV_GROUPS, HEAD_DIM).transpose(3, 0, 4, 1, 2, 5)
    cmp_start = jnp.arange(nc) * CMP_STRIDE
    cmp_end = cmp_start + CMP_LEN - 1
    sel_start = jnp.arange(ns) * SEL_BLK
    cmp_to_sel = ((cmp_start[:, None] < sel_start[None, :] + SEL_BLK)
                  & (cmp_end[:, None] >= sel_start[None, :])).astype(F32)
    return kc, vc, cmp_end, cmp_to_sel, sel[0], sel[1]


def nsa_query_block(q, qpos, gates, kc, vc, cmp_end, cmp_to_sel, ksb, vsb, kw, vw, kwpos):
    b, t = q.shape[:2]
    qg = q.reshape(b, t, KV_GROUPS, Q_PER_KV, HEAD_DIM) * (HEAD_DIM ** -0.5)
    s = jnp.einsum('btgrd,bngd->btgrn', qg, kc).astype(F32)
    p_c = masked_softmax(s, (cmp_end[None, :] <= qpos[:, None])[None, :, None, None, :])
    o_c = jnp.einsum('btgrn,bngd->btgrd', p_c.astype(vc.dtype), vc)
    ns = ksb.shape[2]
    imp = jnp.einsum('btgn,nj->btgj', jnp.sum(p_c, axis=3), cmp_to_sel)
    j = jnp.arange(ns)
    cur = qpos // SEL_BLK
    elig = (j * SEL_BLK)[None, :] <= qpos[:, None]
    forced = (j[None, :] == 0) | (j[None, :] == cur[:, None]) | (j[None, :] == cur[:, None] - 1)
    score = jnp.where(elig[None, :, None, :], imp + jnp.where(forced, FORCE_BONUS, 0.0)[None, :, None, :], NEG_BIG)
    _, idx = lax.top_k(score, min(N_SELECT, ns))
    n = idx.shape[-1]
    idx_g = idx.transpose(0, 2, 1, 3).reshape(b, KV_GROUPS, t * n)
    take = jax.vmap(jax.vmap(lambda blocks, ids: blocks[ids]))
    gk = take(ksb, idx_g).reshape(b, KV_GROUPS, t, n * SEL_BLK, HEAD_DIM)
    gv = take(vsb, idx_g).reshape(b, KV_GROUPS, t, n * SEL_BLK, HEAD_DIM)
    kpos = (idx[..., None] * SEL_BLK + jnp.arange(SEL_BLK)).reshape(b, t, KV_GROUPS, n * SEL_BLK)
    s = jnp.einsum('btgrd,bgtmd->btgrm', qg, gk).astype(F32)
    p_s = masked_softmax(s, (kpos <= qpos[None, :, None, None])[:, :, :, None, :])
    o_s = jnp.einsum('btgrm,bgtmd->btgrd', p_s.astype(gv.dtype), gv)
    s = jnp.einsum('btgrd,bkgd->btgrk', qg, kw).astype(F32)
    d = qpos[:, None] - kwpos[None, :]
    wmask = (kwpos[None, :] >= 0) & (d >= 0) & (d < WINDOW)
    p_w = masked_softmax(s, wmask[None, :, None, None, :])
    o_w = jnp.einsum('btgrk,bkgd->btgrd', p_w.astype(vw.dtype), vw)
    g = jax.nn.sigmoid(gates.astype(F32)).reshape(b, t, KV_GROUPS, Q_PER_KV, 3)
    o = g[..., 0:1] * o_c + g[..., 1:2] * o_s + g[..., 2:3] * o_w
    return o.reshape(b, t, H_N * HEAD_DIM).astype(q.dtype)


def nsa_prep(n_q, n_kv, n_g, pos):
    b, t = n_q.shape[:2]
    q = rope(n_q.reshape(b, t, H_N, HEAD_DIM), pos)
    kv = n_kv.reshape(b, t, 6, KV_GROUPS, HEAD_DIM)
    rows = jnp.stack([rope(kv[:, :, 0], pos), kv[:, :, 1], rope(kv[:, :, 2], pos), kv[:, :, 3]], axis=2)
    win = jnp.stack([rope(kv[:, :, 4], pos), kv[:, :, 5]], axis=2)
    return q, n_g.reshape(b, t, H_N, 3), rows, win


def nsa_prompt(n_q, n_kv, n_g, cmp_params):
    b, s = n_q.shape[:2]
    pos = jnp.arange(s)
    q, gates, rows, win = nsa_prep(n_q, n_kv, n_g, pos)
    kc, vc, cmp_end, cmp_to_sel, ksb, vsb = nsa_context(rows, cmp_params)
    nb = s // QBLK
    kwp = jnp.pad(win, ((0, 0), (WINDOW, 0), (0, 0), (0, 0), (0, 0)))

    def one_block(args):
        qi, gi, bi = args
        start = bi * QBLK
        qpos = start + jnp.arange(QBLK)
        band = lax.dynamic_slice_in_dim(kwp, start, WINDOW + QBLK, axis=1)
        kwpos = start - WINDOW + jnp.arange(WINDOW + QBLK)
        return nsa_query_block(qi, qpos, gi, kc, vc, cmp_end, cmp_to_sel, ksb, vsb,
                               band[:, :, 0], band[:, :, 1], kwpos)

    blocks = lambda a: jnp.moveaxis(a.reshape((b, nb, QBLK) + a.shape[2:]), 1, 0)
    out = lax.map(one_block, (blocks(q), blocks(gates), jnp.arange(nb)))
    out = jnp.moveaxis(out, 0, 1).reshape(b, s, H_N * HEAD_DIM)
    return out, (rows, win[:, -min(WINDOW, s):])


def nsa_sample(n_q, n_kv, n_g, past_rows, win_buf, cmp_params):
    b, t = n_q.shape[:2]
    past_len = past_rows.shape[1]
    pos = past_len + jnp.arange(t)
    q, gates, rows, win = nsa_prep(n_q, n_kv, n_g, pos)
    full = jnp.concatenate([past_rows.astype(rows.dtype), rows], axis=1)
    kc, vc, cmp_end, cmp_to_sel, ksb, vsb = nsa_context(full, cmp_params)
    wl = win_buf.shape[1]
    kw_all = jnp.concatenate([win_buf.astype(win.dtype), win], axis=1)
    kwpos = past_len - wl + jnp.arange(wl + t)
    out = nsa_query_block(q, pos, gates, kc, vc, cmp_end, cmp_to_sel, ksb, vsb,
                          kw_all[:, :, 0], kw_all[:, :, 1], kwpos)
    return out, (rows, kw_all[:, -wl:])


def to_heads(a, n_heads):
    return a.reshape(a.shape[0], a.shape[1], n_heads, -1).astype(F32)


def split_cols(z):
    cut = lambda a, n: z[..., a:a + n]
    u_qk, m_v, m_o = cut(0, 2 * M_WIDTH), cut(2 * M_WIDTH, M_WIDTH), cut(3 * M_WIDTH, M_WIDTH)
    n_q, n_kv = cut(COL_NQ, N_WIDTH), cut(COL_NKV, 6 * KV_GROUPS * HEAD_DIM)
    g_q, g_k, g_v, g_r = (cut(COL_GLA + i * G_WIDTH, G_WIDTH) for i in range(4))
    m_if = cut(COL_SMALL + SMALL_MIF, 2 * H_M)
    n_g = cut(COL_SMALL + SMALL_NG, 3 * H_N)
    g_lr = cut(COL_SMALL + SMALL_GLR, GATE_RANK)
    return u_qk, m_v, m_o, m_if, n_q, n_kv, n_g, g_q, g_k, g_v, g_r, g_lr


def regroup_w_in(w_in):
    o = (0,) + SPLIT_OFFSETS + (IN_COLS,)
    piece = lambda i: w_in[..., o[i]:o[i + 1]]
    order = [0, 1, 2, 4, 5, 7, 8, 9, 10, 3, 6, 11]
    pad = jnp.zeros(w_in.shape[:-1] + (IN_COLS_PAD - IN_COLS,), w_in.dtype)
    return jnp.concatenate([piece(i) for i in order] + [pad], axis=-1)


def trunk_layer(x, conv_buf, c0, n0, m0, s0, nsa_fn, g_norms, w_in, conv_w, m_gate_b, m_norm,
                g_w2, g_b, g_norm, w_out, w_ff1, w_ff2):
    b, t, d = x.shape
    x2 = x.reshape(b * t, d)
    z = norm_matmul(x2, g_norms[0], w_in, IN_TN)
    z3 = z.reshape(b, t, -1)
    t_pad = -(-t // MIX_L) * MIX_L
    zp = z if t_pad == t else jnp.pad(z3, ((0, 0), (0, t_pad - t), (0, 0))).reshape(b * t_pad, -1)
    hm, hg, c, n, m, s = mixers_pallas(zp, b, t, conv_buf, c0, n0, m0, s0,
                                       conv_w, m_gate_b, m_norm, g_w2, g_b, g_norm)
    new_conv = z3[:, t - (CONV_W - 1):, :2 * M_WIDTH]
    hn, nsa_state = nsa_fn(z3)
    x2 = matmul_norm_res(hm.reshape(b * t, -1), hn.reshape(b * t, -1), hg.reshape(b * t, -1),
                         w_out, g_norms[1], x2)
    x2 = ffn(x2, g_norms[2], w_ff1, w_ff2, g_norms[3])
    return x2.reshape(b, t, d), (nsa_state[0], nsa_state[1], c, n, m, new_conv, s)


def kernel(x_prompt, x_sample, cache_nsa_kv, state_nsa_win, state_mlstm_C, state_mlstm_n, state_mlstm_m, state_mlstm_conv, state_gla_S, page_table, norms, w_in, mlstm_conv_w, mlstm_gate_b, mlstm_norm, nsa_cmp_pe, nsa_cmp_w1, nsa_cmp_w2, gla_gate_w2, gla_gate_b, gla_norm, w_out, w_ff1, w_ff2):
    xp, xs = x_prompt, x_sample
    bp = xp.shape[0]
    db = xs.shape[0]
    past_len = page_table.shape[1] * PAGE_SIZE
    conv0 = jnp.zeros((bp, CONV_W - 1, 2 * M_WIDTH), xp.dtype)
    c0 = jnp.zeros((bp, H_M, HEAD_DIM, HEAD_DIM), F32)
    n0 = jnp.zeros((bp, H_M, HEAD_DIM), F32)
    m0 = jnp.zeros((bp, H_M), F32)
    s0 = jnp.zeros((bp, H_G, HEAD_DIM, HEAD_DIM), F32)
    w_in_b = regroup_w_in(w_in).astype(BF16)
    w_out_b = w_out.astype(BF16)
    w_ff1_b = w_ff1.astype(BF16)
    w_ff2_b = w_ff2.astype(BF16)
    cmp_pe = nsa_cmp_pe.reshape(DEPTH, 2, 1, CMP_LEN * HEAD_DIM)
    cmp_w1_b = nsa_cmp_w1.reshape(DEPTH, 2, CMP_LEN * HEAD_DIM, CMP_HID).astype(BF16)
    cmp_w2_b = nsa_cmp_w2.astype(BF16)
    acc_p = [[] for _ in range(7)]
    acc_s = [[] for _ in range(7)]
    for l in range(DEPTH):
        weights = (norms[l], w_in_b[l], mlstm_conv_w[l], mlstm_gate_b[l], mlstm_norm[l],
                   gla_gate_w2[l], gla_gate_b[l], gla_norm[l], w_out_b[l], w_ff1_b[l], w_ff2_b[l])
        cmp_params = (nsa_cmp_pe[l], nsa_cmp_w1[l], nsa_cmp_w2[l])
        cmp_w = (cmp_pe[l], cmp_w1_b[l], cmp_w2_b[l])
        nsa_p = lambda z, cmp_w=cmp_w: nsa_prompt_pallas(
            z.reshape(-1, IN_COLS_PAD), z.shape[0], z.shape[1], cmp_w)
        past = cache_nsa_kv[l, page_table].reshape((db, past_len) + cache_nsa_kv.shape[3:])

        def nsa_s(z, past=past, l=l, cmp_params=cmp_params):
            cols = split_cols(z)
            hn, st = nsa_sample(cols[4], cols[5], cols[6], past_rows=past, win_buf=state_nsa_win[l],
                                cmp_params=cmp_params)
            return hn.astype(BF16), st
        xp, st_p = trunk_layer(xp, conv0, c0, n0, m0, s0, nsa_p, *weights)
        xs, st_s = trunk_layer(xs, state_mlstm_conv[l], state_mlstm_C[l], state_mlstm_n[l],
                               state_mlstm_m[l], state_gla_S[l], nsa_s, *weights)
        for acc, a in zip(acc_p, st_p):
            acc.append(a.astype(xp.dtype))
        for acc, a in zip(acc_s, st_s):
            acc.append(a.astype(xs.dtype))
    nsa_rows_p, nsa_win_p, mlstm_c_p, mlstm_n_p, mlstm_m_p, mlstm_conv_p, gla_s_p = [jnp.stack(a) for a in acc_p]
    nsa_rows_s, nsa_win_s, mlstm_c_s, mlstm_n_s, mlstm_m_s, mlstm_conv_s, gla_s_s = [jnp.stack(a) for a in acc_s]
    return (xp, xs, nsa_rows_p, nsa_rows_s, nsa_win_p, nsa_win_s, mlstm_c_p, mlstm_c_s,
            mlstm_n_p, mlstm_n_s, mlstm_m_p, mlstm_m_s, mlstm_conv_p, mlstm_conv_s, gla_s_p, gla_s_s)
```

```python
import functools
import math

import jax
import jax.numpy as jnp
import numpy as np
from jax import lax
from jax.experimental import pallas as pl
from jax.experimental.pallas import tpu as pltpu

D_MODEL = 2048
DEPTH = 4
PAGE_SIZE = 128
HEAD_DIM = 128
D_MIX = D_MODEL
M_WIDTH = D_MIX // 4
G_WIDTH = D_MIX // 4
N_WIDTH = D_MIX - M_WIDTH - G_WIDTH
H_M = M_WIDTH // HEAD_DIM
H_N = N_WIDTH // HEAD_DIM
H_G = G_WIDTH // HEAD_DIM
KV_GROUPS = 2
Q_PER_KV = H_N // KV_GROUPS
D_FF = 4 * D_MODEL
CONV_W = 4
CHUNK = 64
CMP_STRIDE = 16
CMP_LEN = 2 * CMP_STRIDE
CMP_HID = 256
SEL_BLK = 64
N_SELECT = 16
WINDOW = 512
QBLK = 128
GATE_RANK = 16
GLA_GATE_TEMP = 16.0
ROPE_THETA = 10000.0
EPS = 1e-6
TINY = 1e-30
FORCE_BONUS = 1e3
NEG_BIG = -1e9
SPLIT_SIZES = (2 * M_WIDTH, M_WIDTH, M_WIDTH, 2 * H_M,
               N_WIDTH, 6 * KV_GROUPS * HEAD_DIM, 3 * H_N,
               G_WIDTH, G_WIDTH, G_WIDTH, G_WIDTH, GATE_RANK)
IN_COLS = sum(SPLIT_SIZES)
SPLIT_OFFSETS = tuple(int(o) for o in np.cumsum(SPLIT_SIZES)[:-1])

F32 = jnp.float32
BF16 = jnp.bfloat16

VMEM_LIMIT_BYTES = 56 * 1024 * 1024
IN_TN = 512
COL_NQ = 2 * M_WIDTH + 2 * M_WIDTH
COL_NKV = COL_NQ + N_WIDTH
COL_GLA = COL_NKV + 6 * KV_GROUPS * HEAD_DIM
COL_SMALL = COL_GLA + 4 * G_WIDTH
SMALL_MIF = 0
SMALL_NG = 2 * H_M
SMALL_GLR = SMALL_NG + 3 * H_N
IN_COLS_PAD = 7168
NEG = -1e30
NSA_TQ = 128
NSA_TK = 512


def _rms(x, g):
    return x * lax.rsqrt(jnp.mean(x * x, axis=-1, keepdims=True) + EPS) * g


def _norm_matmul_kernel(x_ref, g_ref, w_ref, o_ref, xn_ref):
    @pl.when(pl.program_id(1) == 0)
    def _():
        xn_ref[...] = _rms(x_ref[...], g_ref[...]).astype(BF16)

    o_ref[...] = jnp.dot(xn_ref[...], w_ref[...], preferred_element_type=F32)


def norm_matmul(x, g, w, tn):
    m, k = x.shape
    n = w.shape[1]
    tm = min(m, 1024)
    return pl.pallas_call(
        _norm_matmul_kernel,
        grid=(m // tm, n // tn),
        in_specs=[pl.BlockSpec((tm, k), lambda i, j: (i, 0)),
                  pl.BlockSpec((1, k), lambda i, j: (0, 0)),
                  pl.BlockSpec((k, tn), lambda i, j: (0, j))],
        out_specs=pl.BlockSpec((tm, tn), lambda i, j: (i, j)),
        out_shape=jax.ShapeDtypeStruct((m, n), F32),
        scratch_shapes=[pltpu.VMEM((tm, k), BF16)],
        compiler_params=pltpu.CompilerParams(
            dimension_semantics=("arbitrary", "arbitrary"), vmem_limit_bytes=VMEM_LIMIT_BYTES),
        name="norm_matmul",
    )(x, g.reshape(1, k), w)


def _matmul_norm_res_kernel(a0_ref, a1_ref, a2_ref, w_ref, g_ref, r_ref, o_ref):
    k0, k1 = a0_ref.shape[1], a0_ref.shape[1] + a1_ref.shape[1]
    y = (jnp.dot(a0_ref[...], w_ref[:k0, :], preferred_element_type=F32)
         + jnp.dot(a1_ref[...], w_ref[k0:k1, :], preferred_element_type=F32)
         + jnp.dot(a2_ref[...], w_ref[k1:, :], preferred_element_type=F32))
    o_ref[...] = r_ref[...] + _rms(y, g_ref[...])


def matmul_norm_res(a0, a1, a2, w, g, r):
    m = a0.shape[0]
    k, n = w.shape
    tm = min(m, 512)
    rows = lambda a: pl.BlockSpec((tm, a.shape[1]), lambda i: (i, 0))
    return pl.pallas_call(
        _matmul_norm_res_kernel,
        grid=(m // tm,),
        in_specs=[rows(a0), rows(a1), rows(a2),
                  pl.BlockSpec((k, n), lambda i: (0, 0)),
                  pl.BlockSpec((1, n), lambda i: (0, 0)),
                  pl.BlockSpec((tm, n), lambda i: (i, 0))],
        out_specs=pl.BlockSpec((tm, n), lambda i: (i, 0)),
        out_shape=jax.ShapeDtypeStruct((m, n), F32),
        compiler_params=pltpu.CompilerParams(
            dimension_semantics=("arbitrary",), vmem_limit_bytes=VMEM_LIMIT_BYTES),
        name="matmul_norm_res",
    )(a0, a1, a2, w, g.reshape(1, n), r)


def _ffn_kernel(x_ref, g2_ref, w1_ref, w2_ref, g3_ref, o_ref, xn_ref, acc_ref):
    f = pl.program_id(1)

    @pl.when(f == 0)
    def _():
        xn_ref[...] = _rms(x_ref[...], g2_ref[...]).astype(BF16)
        acc_ref[...] = jnp.zeros_like(acc_ref)

    h = jnp.dot(xn_ref[...], w1_ref[...], preferred_element_type=F32)
    a = jnp.square(jnp.maximum(h, 0.0)).astype(BF16)
    acc_ref[...] += jnp.dot(a, w2_ref[...], preferred_element_type=F32)

    @pl.when(f == pl.num_programs(1) - 1)
    def _():
        o_ref[...] = x_ref[...] + _rms(acc_ref[...], g3_ref[...])


def ffn(x, g2, w1, w2, g3):
    m, d = x.shape
    dff = w1.shape[1]
    tm = min(m, 512)
    tf = 512
    return pl.pallas_call(
        _ffn_kernel,
        grid=(m // tm, dff // tf),
        in_specs=[pl.BlockSpec((tm, d), lambda i, f: (i, 0)),
                  pl.BlockSpec((1, d), lambda i, f: (0, 0)),
                  pl.BlockSpec((d, tf), lambda i, f: (0, f)),
                  pl.BlockSpec((tf, d), lambda i, f: (f, 0)),
                  pl.BlockSpec((1, d), lambda i, f: (0, 0))],
        out_specs=pl.BlockSpec((tm, d), lambda i, f: (i, 0)),
        out_shape=jax.ShapeDtypeStruct((m, d), F32),
        scratch_shapes=[pltpu.VMEM((tm, d), BF16), pltpu.VMEM((tm, d), F32)],
        compiler_params=pltpu.CompilerParams(
            dimension_semantics=("arbitrary", "arbitrary"), vmem_limit_bytes=VMEM_LIMIT_BYTES),
        name="ffn",
    )(x, g2.reshape(1, d), w1, w2, g3.reshape(1, d))


def rope_tables(pos):
    half = HEAD_DIM // 2
    inv_freq = jnp.exp(-math.log(ROPE_THETA) * jnp.arange(half, dtype=F32) / half)
    ang = pos.astype(F32)[:, None] * inv_freq[None, :]
    cos, sin = jnp.cos(ang), jnp.sin(ang)
    return jnp.concatenate([cos, cos], axis=-1), jnp.concatenate([-sin, sin], axis=-1)


def _rope(x, cos2, sin2):
    return x * cos2 + pltpu.roll(x, HEAD_DIM // 2, 1) * sin2


def _nsa_prep_kernel(nq_ref, nkv_ref, cos_ref, sin_ref, q_ref, rows_ref, win_ref, kvb_ref, cmp_ref):
    cos2 = cos_ref[...]
    sin2 = sin_ref[...]
    for h in range(H_N):
        sl = slice(h * HEAD_DIM, (h + 1) * HEAD_DIM)
        q_ref[:, sl] = (_rope(nq_ref[:, sl], cos2, sin2) * (HEAD_DIM ** -0.5)).astype(BF16)
    for c in range(6 * KV_GROUPS):
        slot = c // KV_GROUPS
        x = nkv_ref[:, c * HEAD_DIM:(c + 1) * HEAD_DIM]
        if slot % 2 == 0:
            x = _rope(x, cos2, sin2)
        xb = x.astype(BF16)
        if slot < 4:
            rows_ref[:, c * HEAD_DIM:(c + 1) * HEAD_DIM] = x
        else:
            win_ref[:, (c - 8) * HEAD_DIM:(c - 7) * HEAD_DIM] = x
        if slot < 2:
            cmp_ref[0, c] = xb
        else:
            kvb_ref[:, (c - 4) * HEAD_DIM:(c - 3) * HEAD_DIM] = xb


def nsa_prep_pallas(z, b, s, cos2, sin2, tm=512):
    m = b * s
    nsb = s // tm
    return pl.pallas_call(
        _nsa_prep_kernel,
        grid=(m // tm,),
        in_specs=[pl.BlockSpec((tm, N_WIDTH), lambda i: (i, COL_NQ // N_WIDTH)),
                  pl.BlockSpec((tm, 1536), lambda i: (i, COL_NKV // 1536)),
                  pl.BlockSpec((tm, HEAD_DIM), lambda i: (i % nsb, 0)),
                  pl.BlockSpec((tm, HEAD_DIM), lambda i: (i % nsb, 0))],
        out_specs=[pl.BlockSpec((tm, N_WIDTH), lambda i: (i, 0)),
                   pl.BlockSpec((tm, 1024), lambda i: (i, 0)),
                   pl.BlockSpec((tm, 512), lambda i: (i, 0)),
                   pl.BlockSpec((tm, 1024), lambda i: (i, 0)),
                   pl.BlockSpec((1, 4, tm, HEAD_DIM), lambda i: (i // nsb, 0, i % nsb, 0))],
        out_shape=[jax.ShapeDtypeStruct((m, N_WIDTH), BF16),
                   jax.ShapeDtypeStruct((m, 1024), F32),
                   jax.ShapeDtypeStruct((m, 512), F32),
                   jax.ShapeDtypeStruct((m, 1024), BF16),
                   jax.ShapeDtypeStruct((b, 4, s, HEAD_DIM), BF16)],
        compiler_params=pltpu.CompilerParams(
            dimension_semantics=("arbitrary",), vmem_limit_bytes=VMEM_LIMIT_BYTES),
        name="nsa_prep",
    )(z, z, cos2, sin2)


def _nsa_cmp_kernel(x_ref, pe_ref, w1_ref, w2_ref, o_ref):
    x = x_ref[0, 0]
    nb = x.shape[0]
    half = CMP_STRIDE * HEAD_DIM
    first = jnp.dot(x, w1_ref[0, :half, :], preferred_element_type=F32)
    second = jnp.dot(x, w1_ref[0, half:, :], preferred_element_type=F32)
    pe = jnp.broadcast_to(pe_ref[0], (8, 2 * half)).astype(BF16)
    bias = jnp.dot(pe, w1_ref[0], preferred_element_type=F32)[0:1]
    hid = first + pltpu.roll(second, nb - 1, 0) + bias
    o_ref[0, 0] = jnp.dot(jax.nn.gelu(hid).astype(BF16), w2_ref[0], preferred_element_type=F32).astype(BF16)


def nsa_compress_prompt(cmp_in, pe, w1, w2):
    b, _, nb, kdim = cmp_in.shape
    return pl.pallas_call(
        _nsa_cmp_kernel,
        grid=(b, 4),
        in_specs=[pl.BlockSpec((1, 1, nb, kdim), lambda i, c: (i, c, 0, 0)),
                  pl.BlockSpec((1, 1, 2 * kdim), lambda i, c: (c // 2, 0, 0)),
                  pl.BlockSpec((1, 2 * kdim, CMP_HID), lambda i, c: (c // 2, 0, 0)),
                  pl.BlockSpec((1, CMP_HID, HEAD_DIM), lambda i, c: (c // 2, 0, 0))],
        out_specs=pl.BlockSpec((1, 1, nb, HEAD_DIM), lambda i, c: (i, c, 0, 0)),
        out_shape=jax.ShapeDtypeStruct((b, 4, nb, HEAD_DIM), BF16),
        compiler_params=pltpu.CompilerParams(
            dimension_semantics=("arbitrary", "arbitrary"), vmem_limit_bytes=VMEM_LIMIT_BYTES),
        name="nsa_compress",
    )(cmp_in, pe, w1, w2)


def _softmax_rows(s, mask):
    sm = jnp.where(mask, s, NEG)
    m = jnp.max(sm, axis=-1, keepdims=True)
    p = jnp.where(mask, jnp.exp(sm - m), 0.0)
    return p / jnp.maximum(jnp.sum(p, axis=-1, keepdims=True), TINY)


_NT = (((1,), (1,)), ((), ()))


def _nsa_attn_kernel(q_ref, kc_ref, vc_ref, ks_ref, vs_ref, kw_ref, vw_ref, gate_ref, o_ref):
    tq = NSA_TQ
    r4 = Q_PER_KV
    g = pl.program_id(1)
    i = pl.program_id(2)
    q4 = q_ref[...]
    q = jnp.concatenate([q4[:, r * HEAD_DIM:(r + 1) * HEAD_DIM] for r in range(r4)], axis=0)
    q0 = i * tq
    tpos = q0 + lax.broadcasted_iota(jnp.int32, (tq, 1), 0)

    ncp = kc_ref.shape[2]
    s = lax.dot_general(q, kc_ref[0, 0], _NT, preferred_element_type=F32).reshape(r4, tq, ncp)
    cend = lax.broadcasted_iota(jnp.int32, (tq, ncp), 1) * CMP_STRIDE + (CMP_LEN - 1)
    p_c = _softmax_rows(s, (cend <= tpos)[None])
    o_c = jnp.dot(p_c.reshape(r4 * tq, ncp).astype(BF16), vc_ref[0, 0], preferred_element_type=F32)

    psum = p_c[0] + p_c[1] + p_c[2] + p_c[3]
    n_i = lax.broadcasted_iota(jnp.int32, (ncp, HEAD_DIM), 0)
    j_i = lax.broadcasted_iota(jnp.int32, (ncp, HEAD_DIM), 1)
    ratio = SEL_BLK // CMP_STRIDE
    c2s = jnp.where((n_i >= ratio * j_i - 1) & (n_i <= ratio * j_i + ratio - 1) & (n_i < ncp - 1)
                    & (j_i < ncp // ratio), 1.0, 0.0).astype(BF16)
    p_hi = psum.astype(BF16)
    p_lo = (psum - p_hi.astype(F32)).astype(BF16)
    imp = (jnp.dot(p_hi, c2s, preferred_element_type=F32)
           + jnp.dot(p_lo, c2s, preferred_element_type=F32))

    ns = ncp // ratio
    imp_t = jnp.transpose(imp)[:ns]
    jj = lax.broadcasted_iota(jnp.int32, (ns, tq), 0)
    tt = q0 + lax.broadcasted_iota(jnp.int32, (ns, tq), 1)
    cur = tt // SEL_BLK
    forced = (jj == 0) | (jj == cur) | (jj == cur - 1)
    score = jnp.where(jj * SEL_BLK <= tt, imp_t + jnp.where(forced, FORCE_BONUS, 0.0), NEG_BIG)
    rank = jnp.zeros((ns, tq), F32)
    for jp in range(ns):
        row = score[jp:jp + 1, :]
        rank = rank + jnp.where(row > score, 1.0, jnp.where((row == score) & (jj > jp), 1.0, 0.0))
    sel_t = jnp.where(rank < float(N_SELECT), 1.0, 0.0)
    sel_t = jnp.concatenate([sel_t, jnp.zeros((HEAD_DIM - ns, tq), F32)], axis=0)
    sel = jnp.transpose(sel_t).astype(BF16)

    tk = NSA_TK
    n_tiles = (q0 + tq + tk - 1) // tk

    def sel_tile(kt, carry):
        m_run, l_run, acc = carry
        k0 = pl.multiple_of(kt * tk, tk)
        s = lax.dot_general(q, ks_ref[pl.ds(k0, tk), :], _NT, preferred_element_type=F32).reshape(r4, tq, tk)
        blk = lax.broadcasted_iota(jnp.int32, (HEAD_DIM, tk), 0)
        key = lax.broadcasted_iota(jnp.int32, (HEAD_DIM, tk), 1)
        expand = jnp.where(blk == kt * (tk // SEL_BLK) + key // SEL_BLK, 1.0, 0.0).astype(BF16)
        picked = jnp.dot(sel, expand, preferred_element_type=F32)
        kpos = k0 + lax.broadcasted_iota(jnp.int32, (tq, tk), 1)
        mask = ((picked > 0.5) & (kpos <= tpos))[None]
        sm = jnp.where(mask, s, NEG)
        m_new = jnp.maximum(m_run, jnp.max(sm, axis=-1, keepdims=True))
        alpha = jnp.exp(m_run - m_new)
        p = jnp.where(mask, jnp.exp(sm - m_new), 0.0)
        l_new = alpha * l_run + jnp.sum(p, axis=-1, keepdims=True)
        pv = jnp.dot(p.reshape(r4 * tq, tk).astype(BF16), vs_ref[pl.ds(k0, tk), :], preferred_element_type=F32)
        return m_new, l_new, alpha.reshape(r4 * tq, 1) * acc + pv

    init = (jnp.full((r4, tq, 1), NEG, F32), jnp.zeros((r4, tq, 1), F32), jnp.zeros((r4 * tq, HEAD_DIM), F32))
    _, l_s, acc_s = lax.fori_loop(0, n_tiles, sel_tile, init)
    o_s = acc_s / jnp.maximum(l_s, TINY).reshape(r4 * tq, 1)

    wb = WINDOW + tq
    w0 = pl.multiple_of(jnp.maximum(q0 - WINDOW, 0), tq)
    s = lax.dot_general(q, kw_ref[pl.ds(w0, wb), :], _NT, preferred_element_type=F32).reshape(r4, tq, wb)
    dist = tpos - (w0 + lax.broadcasted_iota(jnp.int32, (tq, wb), 1))
    p_w = _softmax_rows(s, ((dist >= 0) & (dist < WINDOW))[None])
    o_w = jnp.dot(p_w.reshape(r4 * tq, wb).astype(BF16), vw_ref[pl.ds(w0, wb), :], preferred_element_type=F32)

    gates = jax.nn.sigmoid(gate_ref[...])
    for r in range(r4):
        def gate(c):
            lane0 = SMALL_NG + 3 * r + c
            lane1 = lane0 + 3 * r4
            return jnp.where(g == 0, gates[:, lane0:lane0 + 1], gates[:, lane1:lane1 + 1])
        rows = slice(r * tq, (r + 1) * tq)
        o = gate(0) * o_c[rows] + gate(1) * o_s[rows] + gate(2) * o_w[rows]
        o_ref[:, r * HEAD_DIM:(r + 1) * HEAD_DIM] = o.astype(BF16)


def nsa_attention_prompt(q_b, kcvc, kvb, z, b, s):
    tq = NSA_TQ
    nq = s // tq
    ncp = kcvc.shape[2]
    gw = Q_PER_KV * HEAD_DIM
    seq_block = lambda col: pl.BlockSpec((s, HEAD_DIM), lambda bi, g, i: (bi, col + g))
    return pl.pallas_call(
        _nsa_attn_kernel,
        grid=(b, KV_GROUPS, nq),
        in_specs=[pl.BlockSpec((tq, gw), lambda bi, g, i: (bi * nq + i, g)),
                  pl.BlockSpec((1, 1, ncp, HEAD_DIM), lambda bi, g, i: (bi, g, 0, 0)),
                  pl.BlockSpec((1, 1, ncp, HEAD_DIM), lambda bi, g, i: (bi, KV_GROUPS + g, 0, 0)),
                  seq_block(0), seq_block(2), seq_block(4), seq_block(6),
                  pl.BlockSpec((tq, HEAD_DIM), lambda bi, g, i: (bi * nq + i, COL_SMALL // HEAD_DIM))],
        out_specs=pl.BlockSpec((tq, gw), lambda bi, g, i: (bi * nq + i, g)),
        out_shape=jax.ShapeDtypeStruct((b * s, N_WIDTH), BF16),
        compiler_params=pltpu.CompilerParams(
            dimension_semantics=("arbitrary", "arbitrary", "arbitrary"), vmem_limit_bytes=VMEM_LIMIT_BYTES),
        name="nsa_attention",
    )(q_b, kcvc, kcvc, kvb, kvb, kvb, kvb, z)


def nsa_prompt_pallas(z, b, s, cmp_w):
    pe, w1, w2 = cmp_w
    cos2, sin2 = rope_tables(jnp.arange(s))
    q_b, rows_f, win_f, kvb, cmp_in = nsa_prep_pallas(z, b, s, cos2, sin2)
    kcvc = nsa_compress_prompt(cmp_in.reshape(b, 4, s // CMP_STRIDE, CMP_STRIDE * HEAD_DIM), pe, w1, w2)
    hn = nsa_attention_prompt(q_b, kcvc, kvb, z, b, s)
    rows = rows_f.reshape(b, s, 4, KV_GROUPS, HEAD_DIM)
    win = win_f.reshape(b, s, 2, KV_GROUPS, HEAD_DIM)[:, -min(WINDOW, s):]
    return hn, (rows, win)


SAMPLE_PAGES = 16
SAMPLE_TS = 16
ROW_COLS = 4 * KV_GROUPS * HEAD_DIM
NEG_DEAD = -3e38


def _page_specs(half):
    def spec(k):
        return pl.BlockSpec((PAGE_SIZE, ROW_COLS // 2), lambda bi, i, pt: (pt[bi, i * SAMPLE_PAGES + k], half))
    return [spec(k) for k in range(SAMPLE_PAGES)]


def _nsa_cmp_sample_kernel(pt_ref, *refs):
    del pt_ref
    pages = refs[:SAMPLE_PAGES]
    pe_ref, w1_ref, w2_ref, o_ref, carry_sc, bias_sc = refs[SAMPLE_PAGES:]
    i = pl.program_id(1)
    half = CMP_STRIDE * HEAD_DIM
    nbp = PAGE_SIZE // CMP_STRIDE
    nb = SAMPLE_PAGES * nbp

    @pl.when(i == 0)
    def _():
        carry_sc[...] = jnp.zeros_like(carry_sc)
        for kv in range(2):
            pe = jnp.broadcast_to(pe_ref[kv], (8, 2 * half)).astype(BF16)
            bias_sc[kv] = jnp.dot(pe, w1_ref[kv], preferred_element_type=F32)

    r_i = lax.broadcasted_iota(jnp.int32, (PAGE_SIZE, PAGE_SIZE), 0)
    s_i = lax.broadcasted_iota(jnp.int32, (PAGE_SIZE, PAGE_SIZE), 1)
    perm = jnp.where(s_i == CMP_STRIDE * (r_i % nbp) + r_i // nbp, 1.0, 0.0).astype(BF16)
    regrouped = [jnp.dot(perm, pg[...].astype(BF16), preferred_element_type=F32) for pg in pages]
    row = lax.broadcasted_iota(jnp.int32, (nb, CMP_HID), 0)
    for kv in range(2):
        def flat(c):
            lanes = slice(c * HEAD_DIM, (c + 1) * HEAD_DIM)
            return jnp.concatenate(
                [jnp.concatenate([rg[j * nbp:(j + 1) * nbp, lanes] for j in range(CMP_STRIDE)], axis=1)
                 for rg in regrouped], axis=0)
        x = jnp.concatenate([flat(kv * KV_GROUPS + g) for g in range(KV_GROUPS)], axis=0).astype(BF16)
        first_all = jnp.dot(x, w1_ref[kv, :half, :], preferred_element_type=F32)
        second_all = jnp.dot(x, w1_ref[kv, half:, :], preferred_element_type=F32)
        for g in range(KV_GROUPS):
            c = kv * KV_GROUPS + g
            first = first_all[g * nb:(g + 1) * nb]
            second = second_all[g * nb:(g + 1) * nb]
            shifted = jnp.where(row == 0, carry_sc[c, 7:8, :], pltpu.roll(first, 1, 0))
            hid = shifted + second + bias_sc[kv, 0:1, :]
            o_ref[0, c] = jnp.dot(jax.nn.gelu(hid).astype(BF16), w2_ref[kv],
                                  preferred_element_type=F32).astype(BF16)
            carry_sc[c] = first[nb - 8:, :]


def nsa_compress_sample(pt, cache2d, pe, w1, w2, db):
    n_pages = pt.shape[1]
    nb_all = n_pages * (PAGE_SIZE // CMP_STRIDE)
    nb = SAMPLE_PAGES * (PAGE_SIZE // CMP_STRIDE)
    const3 = lambda bi, i, pt: (0, 0, 0)
    grid_spec = pltpu.PrefetchScalarGridSpec(
        num_scalar_prefetch=1,
        grid=(db, n_pages // SAMPLE_PAGES),
        in_specs=_page_specs(0) + [pl.BlockSpec(pe.shape, const3), pl.BlockSpec(w1.shape, const3),
                                   pl.BlockSpec(w2.shape, const3)],
        out_specs=pl.BlockSpec((1, 2 * KV_GROUPS, nb, HEAD_DIM), lambda bi, i, pt: (bi, 0, i, 0)),
        scratch_shapes=[pltpu.VMEM((2 * KV_GROUPS, 8, CMP_HID), F32), pltpu.VMEM((2, 8, CMP_HID), F32)])
    return pl.pallas_call(
        _nsa_cmp_sample_kernel,
        grid_spec=grid_spec,
        out_shape=jax.ShapeDtypeStruct((db, 2 * KV_GROUPS, nb_all, HEAD_DIM), BF16),
        compiler_params=pltpu.CompilerParams(
            dimension_semantics=("arbitrary", "arbitrary"), vmem_limit_bytes=VMEM_LIMIT_BYTES),
        name="nsa_compress_sample",
    )(pt, *([cache2d] * SAMPLE_PAGES), pe, w1, w2)


def _nsa_attn_sample_kernel(pt_ref, *refs, past_len, t_valid, n_steps):
    del pt_ref
    pages = refs[:SAMPLE_PAGES]
    q_ref, new_ref, kc_ref, win_ref, gate_ref, o_ref, sel_sc, m_sc, l_sc, acc_sc, oc_sc, ow_sc = refs[SAMPLE_PAGES:]
    i = pl.program_id(1)
    ts = SAMPLE_TS
    r4 = Q_PER_KV
    rows = r4 * ts
    d = HEAD_DIM
    blocks_per_step = SAMPLE_PAGES * PAGE_SIZE // SEL_BLK
    ns = past_len // SEL_BLK + 1
    trow = lax.broadcasted_iota(jnp.int32, (rows, 1), 0) % ts
    qpos = past_len + trow

    def q_of(g):
        return jnp.concatenate([q_ref[:, (g * r4 + r) * d:(g * r4 + r + 1) * d] for r in range(r4)], axis=0)

    @pl.when(i == 0)
    def _():
        ncp = kc_ref.shape[2]
        nsl = (n_steps + 1) * d
        lane = lax.broadcasted_iota(jnp.int32, (ncp, nsl), 1)
        tok = lax.broadcasted_iota(jnp.int32, (ncp, nsl), 0) - 1
        blk = (lane // d) * blocks_per_step + lane % d
        ratio = SEL_BLK // CMP_STRIDE
        c2s = jnp.where((lane % d < blocks_per_step) & (blk < ns) & (tok >= 0)
                        & (tok >= ratio * blk - 1) & (tok <= ratio * blk + ratio - 1), 1.0, 0.0).astype(BF16)
        lane_r = lax.broadcasted_iota(jnp.int32, (rows, nsl), 1)
        blk_r = (lane_r // d) * blocks_per_step + lane_r % d
        live = (lane_r % d < blocks_per_step) & (blk_r < ns)
        blk_f = blk_r.astype(F32)
        cur = qpos // SEL_BLK
        forced = (blk_r == 0) | (blk_r == cur) | (blk_r == cur - 1)
        midx = lax.broadcasted_iota(jnp.int32, (rows, ncp), 1)
        cmask = (midx >= 1) & ((midx - 1) * CMP_STRIDE + CMP_LEN - 1 <= qpos)
        sidx = lax.broadcasted_iota(jnp.int32, (rows, ts), 1)
        new_ok = (sidx <= trow) & (sidx < t_valid)
        wl = win_ref.shape[0]
        dist_buf = trow + wl - lax.broadcasted_iota(jnp.int32, (rows, wl), 1)
        buf_ok = (dist_buf >= 0) & (dist_buf < WINDOW)
        dist_new = trow - sidx
        wnew_ok = (dist_new >= 0) & (dist_new < WINDOW) & (sidx < t_valid)
        for g in range(KV_GROUPS):
            qg = q_of(g)
            s = lax.dot_general(qg, kc_ref[0, g], _NT, preferred_element_type=F32)
            p_c = _softmax_rows(s, cmask)
            oc_sc[g] = jnp.dot(p_c.astype(BF16), kc_ref[0, KV_GROUPS + g], preferred_element_type=F32)
            psum_t = p_c[0:ts] + p_c[ts:2 * ts] + p_c[2 * ts:3 * ts] + p_c[3 * ts:4 * ts]
            psum = jnp.concatenate([psum_t] * r4, axis=0)
            p_hi = psum.astype(BF16)
            p_lo = (psum - p_hi.astype(F32)).astype(BF16)
            imp = (jnp.dot(p_hi, c2s, preferred_element_type=F32) + jnp.dot(p_lo, c2s, preferred_element_type=F32))
            score = jnp.where(live, jnp.where(blk_r * SEL_BLK <= qpos, imp + jnp.where(forced, FORCE_BONUS, 0.0),
                                              NEG_BIG), NEG_DEAD)
            sel = jnp.zeros((rows, nsl), F32)
            for _ in range(N_SELECT):
                top = jnp.max(score, axis=1, keepdims=True)
                first = jnp.min(jnp.where(score == top, blk_f, 1e9), axis=1, keepdims=True)
                hit = live & (blk_f == first)
                sel = jnp.where(hit, 1.0, sel)
                score = jnp.where(hit, NEG_DEAD, score)
            for step in range(n_steps):
                sel_sc[g, step] = sel[:, step * d:(step + 1) * d]
            knew = new_ref[:, g * d:(g + 1) * d]
            vnew = new_ref[:, (KV_GROUPS + g) * d:(KV_GROUPS + g + 1) * d]
            sm = jnp.where(new_ok, lax.dot_general(qg, knew, _NT, preferred_element_type=F32), NEG)
            m0 = jnp.max(sm, axis=1, keepdims=True)
            p = jnp.where(new_ok, jnp.exp(sm - m0), 0.0)
            m_sc[g] = m0
            l_sc[g] = jnp.sum(p, axis=1, keepdims=True)
            acc_sc[g] = jnp.dot(p.astype(BF16), vnew, preferred_element_type=F32)
            kwb = win_ref[:, g * d:(g + 1) * d].astype(BF16)
            vwb = win_ref[:, (KV_GROUPS + g) * d:(KV_GROUPS + g + 1) * d].astype(BF16)
            kwn = new_ref[:, (2 * KV_GROUPS + g) * d:(2 * KV_GROUPS + g + 1) * d]
            vwn = new_ref[:, (3 * KV_GROUPS + g) * d:(3 * KV_GROUPS + g + 1) * d]
            s1 = jnp.where(buf_ok, lax.dot_general(qg, kwb, _NT, preferred_element_type=F32), NEG)
            s2 = jnp.where(wnew_ok, lax.dot_general(qg, kwn, _NT, preferred_element_type=F32), NEG)
            mw = jnp.maximum(jnp.max(s1, axis=1, keepdims=True), jnp.max(s2, axis=1, keepdims=True))
            p1 = jnp.where(buf_ok, jnp.exp(s1 - mw), 0.0)
            p2 = jnp.where(wnew_ok, jnp.exp(s2 - mw), 0.0)
            den = jnp.maximum(jnp.sum(p1, axis=1, keepdims=True) + jnp.sum(p2, axis=1, keepdims=True), TINY)
            ow_sc[g] = (jnp.dot((p1 / den).astype(BF16), vwb, preferred_element_type=F32)
                        + jnp.dot((p2 / den).astype(BF16), vwn, preferred_element_type=F32))

    half_lane = lax.broadcasted_iota(jnp.int32, (rows, PAGE_SIZE), 1) < SEL_BLK
    for g in range(KV_GROUPS):
        qg = q_of(g)
        kt = jnp.concatenate([pg[:, g * d:(g + 1) * d] for pg in pages], axis=0).astype(BF16)
        vt = jnp.concatenate([pg[:, (KV_GROUPS + g) * d:(KV_GROUPS + g + 1) * d] for pg in pages],
                             axis=0).astype(BF16)
        s = lax.dot_general(qg, kt, _NT, preferred_element_type=F32)
        selg = sel_sc[g, i]
        picked = jnp.concatenate(
            [jnp.where(half_lane, selg[:, 2 * k:2 * k + 1], selg[:, 2 * k + 1:2 * k + 2])
             for k in range(SAMPLE_PAGES)], axis=1)
        mask = picked > 0.5
        sm = jnp.where(mask, s, NEG)
        m_old = m_sc[g]
        m_new = jnp.maximum(m_old, jnp.max(sm, axis=1, keepdims=True))
        alpha = jnp.exp(m_old - m_new)
        p = jnp.where(mask, jnp.exp(sm - m_new), 0.0)
        m_sc[g] = m_new
        l_sc[g] = alpha * l_sc[g] + jnp.sum(p, axis=1, keepdims=True)
        acc_sc[g] = alpha * acc_sc[g] + jnp.dot(p.astype(BF16), vt, preferred_element_type=F32)

    @pl.when(i == n_steps - 1)
    def _():
        gates = jax.nn.sigmoid(gate_ref[...])
        for g in range(KV_GROUPS):
            o_s = acc_sc[g] / jnp.maximum(l_sc[g], TINY)
            o_c = oc_sc[g]
            o_w = ow_sc[g]
            for r in range(r4):
                h = g * r4 + r
                gate = lambda c: gates[:, SMALL_NG + 3 * h + c:SMALL_NG + 3 * h + c + 1]
                rs = slice(r * ts, (r + 1) * ts)
                o = gate(0) * o_c[rs] + gate(1) * o_s[rs] + gate(2) * o_w[rs]
                o_ref[0, :, h * d:(h + 1) * d] = o.astype(BF16)


def nsa_attention_sample(pt, cache2d, q_b, kvb, kcvc, win2d, zp, layer, db, t_pad, t_valid, past_len):
    ts = SAMPLE_TS
    n_steps = pt.shape[1] // SAMPLE_PAGES
    rows = Q_PER_KV * ts
    wl = win2d.shape[0] // (DEPTH * db)
    seq_rows = lambda width, col: pl.BlockSpec((ts, width), lambda bi, i, pt: (bi * (t_pad // ts), col))
    grid_spec = pltpu.PrefetchScalarGridSpec(
        num_scalar_prefetch=1,
        grid=(db, n_steps),
        in_specs=_page_specs(1) + [
            seq_rows(N_WIDTH, 0),
            seq_rows(ROW_COLS, 0),
            pl.BlockSpec((1,) + kcvc.shape[1:], lambda bi, i, pt: (bi, 0, 0, 0)),
            pl.BlockSpec((wl, 2 * KV_GROUPS * HEAD_DIM), lambda bi, i, pt: (layer * db + bi, 0)),
            seq_rows(HEAD_DIM, COL_SMALL // HEAD_DIM)],
        out_specs=pl.BlockSpec((1, ts, N_WIDTH), lambda bi, i, pt: (bi, 0, 0)),
        scratch_shapes=[pltpu.VMEM((KV_GROUPS, n_steps, rows, HEAD_DIM), F32),
                        pltpu.VMEM((KV_GROUPS, rows, 1), F32),
                        pltpu.VMEM((KV_GROUPS, rows, 1), F32),
                        pltpu.VMEM((KV_GROUPS, rows, HEAD_DIM), F32),
                        pltpu.VMEM((KV_GROUPS, rows, HEAD_DIM), F32),
                        pltpu.VMEM((KV_GROUPS, rows, HEAD_DIM), F32)])
    return pl.pallas_call(
        functools.partial(_nsa_attn_sample_kernel, past_len=past_len, t_valid=t_valid, n_steps=n_steps),
        grid_spec=grid_spec,
        out_shape=jax.ShapeDtypeStruct((db, ts, N_WIDTH), BF16),
        compiler_params=pltpu.CompilerParams(
            dimension_semantics=("arbitrary", "arbitrary"), vmem_limit_bytes=VMEM_LIMIT_BYTES),
        name="nsa_attention_sample",
    )(pt, *([cache2d] * SAMPLE_PAGES), q_b, kvb, kcvc, win2d, zp)


def nsa_sample_pallas(zp, db, t, t_pad, layer, cache2d, pt, win_state, cmp_w):
    pe, w1, w2 = cmp_w
    past_len = pt.shape[1] * PAGE_SIZE
    cos2, sin2 = rope_tables(past_len + jnp.arange(t_pad))
    q_b, rows_f, win_f, kvb, _ = nsa_prep_pallas(zp, db, t_pad, cos2, sin2, tm=t_pad)
    kcvc = nsa_compress_sample(pt, cache2d, pe, w1, w2, db)
    win2d = win_state.reshape(-1, 2 * KV_GROUPS * HEAD_DIM)
    hn = nsa_attention_sample(pt, cache2d, q_b, kvb, kcvc, win2d, zp, layer, db, t_pad, t, past_len)
    rows = rows_f.reshape(db, t_pad, 4, KV_GROUPS, HEAD_DIM)[:, :t]
    win_new = win_f.reshape(db, t_pad, 2, KV_GROUPS, HEAD_DIM)[:, :t]
    wl = win_state.shape[2]
    win = jnp.concatenate([win_state[layer], win_new], axis=1)[:, -wl:]
    return hn[:, :t], (rows, win)


MIX_L = 128
GLA_SUB = 16


def _logsig(x):
    return jnp.minimum(x, 0.0) - jnp.log1p(jnp.exp(-jnp.abs(x)))


def _prefix_sum(x, axis):
    n = x.shape[axis]
    idx = lax.broadcasted_iota(jnp.int32, x.shape, axis)
    step = 1
    while step < n:
        x = x + jnp.where(idx >= step, pltpu.roll(x, step, axis), 0.0)
        step *= 2
    return x


def _mlstm_kernel(uqk_ref, v_ref, og_ref, small_ref, convw_ref, bias_ref, norm_ref, conv0_ref, c0_ref, n0_ref,
                  m0_ref, h_ref, c_ref, n_ref, m_ref, prev_sc, c_sc, n_sc, m_sc, *, t_valid):
    t = pl.program_id(1)
    L = MIX_L
    d = HEAD_DIM

    @pl.when(t == 0)
    def _():
        prev_sc[...] = jnp.zeros_like(prev_sc)
        prev_sc[L - 8:, :] = conv0_ref[0]
        c_sc[...] = c0_ref[0]
        n_sc[...] = n0_ref[0]
        m_sc[...] = m0_ref[0]

    x = uqk_ref[...]
    prev = prev_sc[...]
    row = lax.broadcasted_iota(jnp.int32, x.shape, 0)
    w = convw_ref[...]
    conv = w[CONV_W - 1:CONV_W] * x
    for k in range(1, CONV_W):
        shifted = jnp.where(row >= k, pltpu.roll(x, k, 0), pltpu.roll(prev, k, 0))
        conv = conv + w[CONV_W - 1 - k:CONV_W - k] * shifted
    prev_sc[...] = x
    act = conv * jax.nn.sigmoid(conv)

    pre = small_ref[...] + bias_ref[...]
    pos_c = t * L + lax.broadcasted_iota(jnp.int32, (L, HEAD_DIM), 0)
    ig_c = jnp.where(pos_c < t_valid, pre, NEG)
    b_c = _prefix_sum(jnp.where(pos_c < t_valid, _logsig(pre), 0.0), 0)
    pre_r = jnp.transpose(pre)[0:8]
    pos_r = t * L + lax.broadcasted_iota(jnp.int32, (8, L), 1)
    ig_r = jnp.where(pos_r < t_valid, pre_r, NEG)
    b_r = _prefix_sum(jnp.where(pos_r < t_valid, _logsig(pre_r), 0.0), 1)

    li = lax.broadcasted_iota(jnp.int32, (L, L), 0)
    si = lax.broadcasted_iota(jnp.int32, (L, L), 1)
    for h in range(H_M):
        sl = slice(h * d, (h + 1) * d)
        bcol, igcol = b_c[:, H_M + h:H_M + h + 1], ig_c[:, h:h + 1]
        brow, igrow = b_r[H_M + h:H_M + h + 1, :], ig_r[h:h + 1, :]
        m0 = m_sc[h:h + 1, 0:1]
        c0 = c_sc[h]
        n0 = n_sc[h:h + 1, :]
        qf = act[:, sl]
        kf = act[:, M_WIDTH + h * d:M_WIDTH + (h + 1) * d] * (d ** -0.5)
        vf = v_ref[:, sl]
        qb, kb, vb = qf.astype(BF16), kf.astype(BF16), vf.astype(BF16)
        logw = jnp.where(si <= li, bcol - brow + igrow, NEG)
        gsum = bcol + m0
        m_row = jnp.maximum(jnp.max(logw, axis=1, keepdims=True), gsum)
        wgt = jnp.exp(logw - m_row) * lax.dot_general(qb, kb, _NT, preferred_element_type=F32)
        inter = jnp.exp(gsum - m_row)
        num = (jnp.dot(wgt.astype(BF16), vb, preferred_element_type=F32)
               + inter * lax.dot_general(qb, c0.astype(BF16), _NT, preferred_element_type=F32))
        den = jnp.sum(wgt, axis=1, keepdims=True) + inter * jnp.sum(qf * n0, axis=1, keepdims=True)
        hh = num / jnp.maximum(jnp.abs(den), jnp.exp(-m_row))
        bl = bcol[L - 1:L, :]
        m_new = jnp.maximum(jnp.max(bl - brow + igrow, axis=1, keepdims=True), bl + m0)
        wa = jnp.exp(bl - bcol + igcol - m_new)
        wc = jnp.exp(bl + m0 - m_new)
        c_sc[h] = wc * c0 + jnp.dot(jnp.transpose(vf * wa).astype(BF16), kb, preferred_element_type=F32)
        n_sc[h:h + 1, :] = wc * n0 + jnp.sum(wa * kf, axis=0, keepdims=True)
        m_sc[h:h + 1, :] = jnp.broadcast_to(m_new, (1, d))
        out = _rms(hh, norm_ref[:, sl]) * jax.nn.sigmoid(og_ref[:, sl])
        h_ref[:, sl] = out.astype(BF16)

    @pl.when(t == pl.num_programs(1) - 1)
    def _():
        c_ref[0] = c_sc[...]
        n_ref[0] = n_sc[...]
        m_ref[0] = m_sc[...]


def mlstm_pallas(z, b, t_pad, t_valid, conv_w, gate_b, norm, conv0, c0, n0, m0):
    L = MIX_L
    nt = t_pad // L
    rowblk = lambda col: (lambda bi, t: (bi * nt + t, col))
    const2 = lambda bi, t: (0, 0)
    per_b3 = lambda bi, t: (bi, 0, 0)
    per_b4 = lambda bi, t: (bi, 0, 0, 0)
    return pl.pallas_call(
        functools.partial(_mlstm_kernel, t_valid=t_valid),
        grid=(b, nt),
        in_specs=[pl.BlockSpec((L, 2 * M_WIDTH), rowblk(0)),
                  pl.BlockSpec((L, M_WIDTH), rowblk(2)),
                  pl.BlockSpec((L, M_WIDTH), rowblk(3)),
                  pl.BlockSpec((L, HEAD_DIM), rowblk(COL_SMALL // HEAD_DIM)),
                  pl.BlockSpec((CONV_W, 2 * M_WIDTH), const2),
                  pl.BlockSpec((1, HEAD_DIM), const2),
                  pl.BlockSpec((1, M_WIDTH), const2),
                  pl.BlockSpec((1, 8, 2 * M_WIDTH), per_b3),
                  pl.BlockSpec((1, H_M, HEAD_DIM, HEAD_DIM), per_b4),
                  pl.BlockSpec((1, 8, HEAD_DIM), per_b3),
                  pl.BlockSpec((1, 8, HEAD_DIM), per_b3)],
        out_specs=[pl.BlockSpec((L, M_WIDTH), rowblk(0)),
                   pl.BlockSpec((1, H_M, HEAD_DIM, HEAD_DIM), per_b4),
                   pl.BlockSpec((1, 8, HEAD_DIM), per_b3),
                   pl.BlockSpec((1, 8, HEAD_DIM), per_b3)],
        out_shape=[jax.ShapeDtypeStruct((b * t_pad, M_WIDTH), BF16),
                   jax.ShapeDtypeStruct((b, H_M, HEAD_DIM, HEAD_DIM), F32),
                   jax.ShapeDtypeStruct((b, 8, HEAD_DIM), F32),
                   jax.ShapeDtypeStruct((b, 8, HEAD_DIM), F32)],
        scratch_shapes=[pltpu.VMEM((L, 2 * M_WIDTH), F32),
                        pltpu.VMEM((H_M, HEAD_DIM, HEAD_DIM), F32),
                        pltpu.VMEM((8, HEAD_DIM), F32),
                        pltpu.VMEM((8, HEAD_DIM), F32)],
        compiler_params=pltpu.CompilerParams(
            dimension_semantics=("arbitrary", "arbitrary"), vmem_limit_bytes=VMEM_LIMIT_BYTES),
        name="mlstm",
    )(z, z, z, z, conv_w, gate_b, norm, conv0, c0, n0, m0)


def _gla_kernel(q_ref, k_ref, v_ref, r_ref, small_ref, w2_ref, gb_ref, norm_ref, s0_ref, h_ref, s_ref, st_sc,
                *, t_valid):
    t = pl.program_id(1)
    L = MIX_L
    d = HEAD_DIM

    @pl.when(t == 0)
    def _():
        for h in range(H_G):
            st_sc[h] = jnp.transpose(s0_ref[0, h])

    pre = jnp.dot(small_ref[...].astype(BF16), w2_ref[...], preferred_element_type=F32) + gb_ref[...]
    pos = t * L + lax.broadcasted_iota(jnp.int32, (L, G_WIDTH), 0)
    la = jnp.where(pos < t_valid, _logsig(pre) / GLA_GATE_TEMP, 0.0)
    bc_all = _prefix_sum(la, 0)
    rowi = lax.broadcasted_iota(jnp.int32, (L, d), 0)
    li = lax.broadcasted_iota(jnp.int32, (L, L), 0)
    si = lax.broadcasted_iota(jnp.int32, (L, L), 1)
    for h in range(H_G):
        sl = slice(h * d, (h + 1) * d)
        bc = bc_all[:, sl]
        qf = q_ref[:, sl] * (d ** -0.5)
        kf = jnp.where(t * L + rowi < t_valid, k_ref[:, sl], 0.0)
        vf = v_ref[:, sl]
        qparts, kparts = [], []
        for j in range(L // GLA_SUB):
            lo, hi = j * GLA_SUB, (j + 1) * GLA_SUB
            e = bc[hi - 1:hi, :]
            qparts.append((qf * jnp.exp(jnp.where(rowi >= lo, bc - e, NEG))).astype(BF16))
            kparts.append((kf * jnp.exp(jnp.where((rowi >= lo) & (rowi < hi), e - bc, NEG))).astype(BF16))
        a = lax.dot_general(jnp.concatenate(qparts, axis=1), jnp.concatenate(kparts, axis=1), _NT,
                            preferred_element_type=F32)
        a = jnp.where(si <= li, a, 0.0)
        st = st_sc[h]
        o = (jnp.dot(a.astype(BF16), vf.astype(BF16), preferred_element_type=F32)
             + lax.dot_general((qf * jnp.exp(bc)).astype(BF16), st.astype(BF16), _NT, preferred_element_type=F32))
        bl = bc[L - 1:L, :]
        kd = (kf * jnp.exp(bl - bc)).astype(BF16)
        st_sc[h] = st * jnp.exp(bl) + jnp.dot(jnp.transpose(vf).astype(BF16), kd, preferred_element_type=F32)
        gate = r_ref[:, sl]
        h_ref[:, sl] = (_rms(o, norm_ref[:, sl]) * (gate * jax.nn.sigmoid(gate))).astype(BF16)

    @pl.when(t == pl.num_programs(1) - 1)
    def _():
        for h in range(H_G):
            s_ref[0, h] = jnp.transpose(st_sc[h])


def gla_pallas(z, b, t_pad, t_valid, w2_pad, gate_b, norm, s0):
    L = MIX_L
    nt = t_pad // L
    rowblk = lambda col: (lambda bi, t: (bi * nt + t, col))
    const2 = lambda bi, t: (0, 0)
    per_b4 = lambda bi, t: (bi, 0, 0, 0)
    gcol = COL_GLA // G_WIDTH
    return pl.pallas_call(
        functools.partial(_gla_kernel, t_valid=t_valid),
        grid=(b, nt),
        in_specs=[pl.BlockSpec((L, G_WIDTH), rowblk(gcol)),
                  pl.BlockSpec((L, G_WIDTH), rowblk(gcol + 1)),
                  pl.BlockSpec((L, G_WIDTH), rowblk(gcol + 2)),
                  pl.BlockSpec((L, G_WIDTH), rowblk(gcol + 3)),
                  pl.BlockSpec((L, HEAD_DIM), rowblk(COL_SMALL // HEAD_DIM)),
                  pl.BlockSpec((HEAD_DIM, G_WIDTH), const2),
                  pl.BlockSpec((1, G_WIDTH), const2),
                  pl.BlockSpec((1, G_WIDTH), const2),
                  pl.BlockSpec((1, H_G, HEAD_DIM, HEAD_DIM), per_b4)],
        out_specs=[pl.BlockSpec((L, G_WIDTH), rowblk(0)),
                   pl.BlockSpec((1, H_G, HEAD_DIM, HEAD_DIM), per_b4)],
        out_shape=[jax.ShapeDtypeStruct((b * t_pad, G_WIDTH), BF16),
                   jax.ShapeDtypeStruct((b, H_G, HEAD_DIM, HEAD_DIM), F32)],
        scratch_shapes=[pltpu.VMEM((H_G, HEAD_DIM, HEAD_DIM), F32)],
        compiler_params=pltpu.CompilerParams(
            dimension_semantics=("arbitrary", "arbitrary"), vmem_limit_bytes=VMEM_LIMIT_BYTES),
        name="gla",
    )(z, z, z, z, z, w2_pad, gate_b, norm, s0)


def mixers_pallas(z, b, t, conv_buf, c0, n0, m0, s0, conv_w, m_gate_b, m_norm, g_w2, g_b, g_norm):
    t_pad = z.shape[0] // b
    lanes = lambda a: jnp.pad(a.reshape(1, -1), ((0, 0), (0, HEAD_DIM - a.size)))
    conv0 = jnp.pad(conv_buf, ((0, 0), (8 - (CONV_W - 1), 0), (0, 0)))
    n0p = jnp.pad(n0, ((0, 0), (0, 8 - H_M), (0, 0)))
    m0p = jnp.broadcast_to(jnp.pad(m0, ((0, 0), (0, 8 - H_M)))[:, :, None], (b, 8, HEAD_DIM))
    hm, c, n, m = mlstm_pallas(z, b, t_pad, t, conv_w, lanes(m_gate_b), m_norm.reshape(1, -1), conv0, c0, n0p, m0p)
    w2_pad = jnp.pad(g_w2, ((SMALL_GLR, HEAD_DIM - SMALL_GLR - GATE_RANK), (0, 0))).astype(BF16)
    hg, s = gla_pallas(z, b, t_pad, t, w2_pad, g_b.reshape(1, -1), g_norm.reshape(1, -1), s0)
    valid = lambda a: a.reshape(b, t_pad, -1)[:, :t]
    return valid(hm), valid(hg), c, n[:, :H_M], m[:, :H_M, 0], s


def rmsnorm(x, g):
    xf = x.astype(F32)
    y = xf * lax.rsqrt(jnp.mean(xf * xf, axis=-1, keepdims=True) + EPS)
    return (y * g.astype(F32)).astype(x.dtype)


def rope(x, pos):
    half = x.shape[-1] // 2
    inv_freq = jnp.exp(-math.log(ROPE_THETA) * jnp.arange(half, dtype=F32) / half)
    ang = pos.astype(F32)[:, None] * inv_freq[None, :]
    cos = jnp.cos(ang)[:, None, :]
    sin = jnp.sin(ang)[:, None, :]
    xf = x.astype(F32)
    x1, x2 = xf[..., :half], xf[..., half:]
    return jnp.concatenate([x1 * cos - x2 * sin, x2 * cos + x1 * sin], axis=-1).astype(x.dtype)


def masked_softmax(s, mask):
    s = jnp.where(mask, s, -jnp.inf)
    m = jnp.max(s, axis=-1, keepdims=True)
    m = jnp.where(jnp.isfinite(m), m, 0.0)
    p = jnp.exp(s - m)
    return p / jnp.maximum(jnp.sum(p, axis=-1, keepdims=True), TINY)


def chunked_scan(step, carry, xs, chunk):
    b, t = xs[0].shape[:2]
    c = chunk if t % chunk == 0 else t
    n = t // c
    split = lambda a: jnp.moveaxis(a.reshape((b, n, c) + a.shape[2:]), 1, 0)
    carry, ys = lax.scan(step, carry, tuple(split(a) for a in xs))
    return carry, jnp.moveaxis(ys, 0, 1).reshape((b, t) + ys.shape[3:])


def mlstm_chunk(carry, inp):
    c0, n0, m0 = carry
    q, k, v, li, lf = inp
    L = q.shape[1]
    b = jnp.cumsum(lf, axis=1).transpose(0, 2, 1)
    ig = li.transpose(0, 2, 1)
    causal = jnp.tril(jnp.ones((L, L), dtype=bool))
    logw = jnp.where(causal, b[..., :, None] - b[..., None, :] + ig[..., None, :], -jnp.inf)
    g = b + m0[..., None]
    m_row = jnp.maximum(jnp.max(logw, axis=-1), g)
    w = jnp.exp(logw - m_row[..., None]) * jnp.einsum('blhd,bshd->bhls', q, k)
    inter = jnp.exp(g - m_row)
    num = jnp.einsum('bhls,bshd->bhld', w, v) + inter[..., None] * jnp.einsum('bhed,blhd->bhle', c0, q)
    den = jnp.sum(w, axis=-1) + inter * jnp.einsum('bhd,blhd->bhl', n0, q)
    h = num / jnp.maximum(jnp.abs(den), jnp.exp(-m_row))[..., None]
    bl = b[..., -1]
    a = bl[..., None] - b + ig
    m_new = jnp.maximum(jnp.max(a, axis=-1), bl + m0)
    wa = jnp.exp(a - m_new[..., None])
    wc = jnp.exp(bl + m0 - m_new)
    c_new = wc[..., None, None] * c0 + jnp.einsum('bhs,bshe,bshd->bhed', wa, v, k)
    n_new = wc[..., None] * n0 + jnp.einsum('bhs,bshd->bhd', wa, k)
    return (c_new, n_new, m_new), h.transpose(0, 2, 1, 3)


def gla_chunk(s0, inp):
    q, k, v, la = inp
    L = q.shape[1]
    bc = jnp.cumsum(la, axis=1)
    causal = jnp.tril(jnp.ones((L, L), dtype=bool))[None, :, :, None, None]
    decay = jnp.exp(jnp.where(causal, bc[:, :, None] - bc[:, None, :], -jnp.inf))
    a = jnp.einsum('bthk,bshk,btshk->bhts', q, k, decay)
    o = jnp.einsum('bhts,bshv->bthv', a, v) + jnp.einsum('bthk,bhkv->bthv', q * jnp.exp(bc), s0)
    bl = bc[:, -1]
    s_new = jnp.exp(bl)[..., None] * s0 + jnp.einsum('bshk,bshv->bhkv', k * jnp.exp(bl[:, None] - bc), v)
    return s_new, o


def nsa_compress(x, pe, w1, w2):
    b, l = x.shape[:2]
    nb = l // CMP_STRIDE
    xb = x[:, :nb * CMP_STRIDE].reshape(b, nb, CMP_STRIDE, KV_GROUPS, HEAD_DIM)
    first = jnp.einsum('bnjgd,jde->bnge', xb, w1[:CMP_STRIDE])
    second = jnp.einsum('bnjgd,jde->bnge', xb, w1[CMP_STRIDE:])
    hid = first[:, :-1] + second[:, 1:] + jnp.einsum('jd,jde->e', pe, w1)
    return jnp.einsum('bnge,ed->bngd', jax.nn.gelu(hid), w2)


def nsa_context(rows, cmp_params):
    pe, w1, w2 = cmp_params
    b, l = rows.shape[:2]
    kc = nsa_compress(rows[:, :, 0], pe[0], w1[0], w2[0])
    vc = nsa_compress(rows[:, :, 1], pe[1], w1[1], w2[1])
    nc = kc.shape[1]
    ns = -(-l // SEL_BLK)
    sel = jnp.pad(rows[:, :, 2:4], ((0, 0), (0, ns * SEL_BLK - l), (0, 0), (0, 0), (0, 0)))
    sel = sel.reshape(b, ns, SEL_BLK, 2, KV_GROUPS, HEAD_DIM).transpose(3, 0, 4, 1, 2, 5)
    cmp_start = jnp.arange(nc) * CMP_STRIDE
    cmp_end = cmp_start + CMP_LEN - 1
    sel_start = jnp.arange(ns) * SEL_BLK
    cmp_to_sel = ((cmp_start[:, None] < sel_start[None, :] + SEL_BLK)
                  & (cmp_end[:, None] >= sel_start[None, :])).astype(F32)
    return kc, vc, cmp_end, cmp_to_sel, sel[0], sel[1]


def nsa_query_block(q, qpos, gates, kc, vc, cmp_end, cmp_to_sel, ksb, vsb, kw, vw, kwpos):
    b, t = q.shape[:2]
    qg = q.reshape(b, t, KV_GROUPS, Q_PER_KV, HEAD_DIM) * (HEAD_DIM ** -0.5)
    s = jnp.einsum('btgrd,bngd->btgrn', qg, kc).astype(F32)
    p_c = masked_softmax(s, (cmp_end[None, :] <= qpos[:, None])[None, :, None, None, :])
    o_c = jnp.einsum('btgrn,bngd->btgrd', p_c.astype(vc.dtype), vc)
    ns = ksb.shape[2]
    imp = jnp.einsum('btgn,nj->btgj', jnp.sum(p_c, axis=3), cmp_to_sel)
    j = jnp.arange(ns)
    cur = qpos // SEL_BLK
    elig = (j * SEL_BLK)[None, :] <= qpos[:, None]
    forced = (j[None, :] == 0) | (j[None, :] == cur[:, None]) | (j[None, :] == cur[:, None] - 1)
    score = jnp.where(elig[None, :, None, :], imp + jnp.where(forced, FORCE_BONUS, 0.0)[None, :, None, :], NEG_BIG)
    _, idx = lax.top_k(score, min(N_SELECT, ns))
    n = idx.shape[-1]
    idx_g = idx.transpose(0, 2, 1, 3).reshape(b, KV_GROUPS, t * n)
    take = jax.vmap(jax.vmap(lambda blocks, ids: blocks[ids]))
    gk = take(ksb, idx_g).reshape(b, KV_GROUPS, t, n * SEL_BLK, HEAD_DIM)
    gv = take(vsb, idx_g).reshape(b, KV_GROUPS, t, n * SEL_BLK, HEAD_DIM)
    kpos = (idx[..., None] * SEL_BLK + jnp.arange(SEL_BLK)).reshape(b, t, KV_GROUPS, n * SEL_BLK)
    s = jnp.einsum('btgrd,bgtmd->btgrm', qg, gk).astype(F32)
    p_s = masked_softmax(s, (kpos <= qpos[None, :, None, None])[:, :, :, None, :])
    o_s = jnp.einsum('btgrm,bgtmd->btgrd', p_s.astype(gv.dtype), gv)
    s = jnp.einsum('btgrd,bkgd->btgrk', qg, kw).astype(F32)
    d = qpos[:, None] - kwpos[None, :]
    wmask = (kwpos[None, :] >= 0) & (d >= 0) & (d < WINDOW)
    p_w = masked_softmax(s, wmask[None, :, None, None, :])
    o_w = jnp.einsum('btgrk,bkgd->btgrd', p_w.astype(vw.dtype), vw)
    g = jax.nn.sigmoid(gates.astype(F32)).reshape(b, t, KV_GROUPS, Q_PER_KV, 3)
    o = g[..., 0:1] * o_c + g[..., 1:2] * o_s + g[..., 2:3] * o_w
    return o.reshape(b, t, H_N * HEAD_DIM).astype(q.dtype)


def nsa_prep(n_q, n_kv, n_g, pos):
    b, t = n_q.shape[:2]
    q = rope(n_q.reshape(b, t, H_N, HEAD_DIM), pos)
    kv = n_kv.reshape(b, t, 6, KV_GROUPS, HEAD_DIM)
    rows = jnp.stack([rope(kv[:, :, 0], pos), kv[:, :, 1], rope(kv[:, :, 2], pos), kv[:, :, 3]], axis=2)
    win = jnp.stack([rope(kv[:, :, 4], pos), kv[:, :, 5]], axis=2)
    return q, n_g.reshape(b, t, H_N, 3), rows, win


def nsa_prompt(n_q, n_kv, n_g, cmp_params):
    b, s = n_q.shape[:2]
    pos = jnp.arange(s)
    q, gates, rows, win = nsa_prep(n_q, n_kv, n_g, pos)
    kc, vc, cmp_end, cmp_to_sel, ksb, vsb = nsa_context(rows, cmp_params)
    nb = s // QBLK
    kwp = jnp.pad(win, ((0, 0), (WINDOW, 0), (0, 0), (0, 0), (0, 0)))

    def one_block(args):
        qi, gi, bi = args
        start = bi * QBLK
        qpos = start + jnp.arange(QBLK)
        band = lax.dynamic_slice_in_dim(kwp, start, WINDOW + QBLK, axis=1)
        kwpos = start - WINDOW + jnp.arange(WINDOW + QBLK)
        return nsa_query_block(qi, qpos, gi, kc, vc, cmp_end, cmp_to_sel, ksb, vsb,
                               band[:, :, 0], band[:, :, 1], kwpos)

    blocks = lambda a: jnp.moveaxis(a.reshape((b, nb, QBLK) + a.shape[2:]), 1, 0)
    out = lax.map(one_block, (blocks(q), blocks(gates), jnp.arange(nb)))
    out = jnp.moveaxis(out, 0, 1).reshape(b, s, H_N * HEAD_DIM)
    return out, (rows, win[:, -min(WINDOW, s):])


def nsa_sample(n_q, n_kv, n_g, past_rows, win_buf, cmp_params):
    b, t = n_q.shape[:2]
    past_len = past_rows.shape[1]
    pos = past_len + jnp.arange(t)
    q, gates, rows, win = nsa_prep(n_q, n_kv, n_g, pos)
    full = jnp.concatenate([past_rows.astype(rows.dtype), rows], axis=1)
    kc, vc, cmp_end, cmp_to_sel, ksb, vsb = nsa_context(full, cmp_params)
    wl = win_buf.shape[1]
    kw_all = jnp.concatenate([win_buf.astype(win.dtype), win], axis=1)
    kwpos = past_len - wl + jnp.arange(wl + t)
    out = nsa_query_block(q, pos, gates, kc, vc, cmp_end, cmp_to_sel, ksb, vsb,
                          kw_all[:, :, 0], kw_all[:, :, 1], kwpos)
    return out, (rows, kw_all[:, -wl:])


def to_heads(a, n_heads):
    return a.reshape(a.shape[0], a.shape[1], n_heads, -1).astype(F32)


def split_cols(z):
    cut = lambda a, n: z[..., a:a + n]
    u_qk, m_v, m_o = cut(0, 2 * M_WIDTH), cut(2 * M_WIDTH, M_WIDTH), cut(3 * M_WIDTH, M_WIDTH)
    n_q, n_kv = cut(COL_NQ, N_WIDTH), cut(COL_NKV, 6 * KV_GROUPS * HEAD_DIM)
    g_q, g_k, g_v, g_r = (cut(COL_GLA + i * G_WIDTH, G_WIDTH) for i in range(4))
    m_if = cut(COL_SMALL + SMALL_MIF, 2 * H_M)
    n_g = cut(COL_SMALL + SMALL_NG, 3 * H_N)
    g_lr = cut(COL_SMALL + SMALL_GLR, GATE_RANK)
    return u_qk, m_v, m_o, m_if, n_q, n_kv, n_g, g_q, g_k, g_v, g_r, g_lr


def regroup_w_in(w_in):
    o = (0,) + SPLIT_OFFSETS + (IN_COLS,)
    piece = lambda i: w_in[..., o[i]:o[i + 1]]
    order = [0, 1, 2, 4, 5, 7, 8, 9, 10, 3, 6, 11]
    pad = jnp.zeros(w_in.shape[:-1] + (IN_COLS_PAD - IN_COLS,), w_in.dtype)
    return jnp.concatenate([piece(i) for i in order] + [pad], axis=-1)


def trunk_layer(x, conv_buf, c0, n0, m0, s0, nsa_fn, g_norms, w_in, conv_w, m_gate_b, m_norm,
                g_w2, g_b, g_norm, w_out, w_ff1, w_ff2):
    b, t, d = x.shape
    x2 = x.reshape(b * t, d)
    z = norm_matmul(x2, g_norms[0], w_in, IN_TN)
    z3 = z.reshape(b, t, -1)
    t_pad = -(-t // MIX_L) * MIX_L
    zp = z if t_pad == t else jnp.pad(z3, ((0, 0), (0, t_pad - t), (0, 0))).reshape(b * t_pad, -1)
    hm, hg, c, n, m, s = mixers_pallas(zp, b, t, conv_buf, c0, n0, m0, s0,
                                       conv_w, m_gate_b, m_norm, g_w2, g_b, g_norm)
    new_conv = z3[:, t - (CONV_W - 1):, :2 * M_WIDTH]
    hn, nsa_state = nsa_fn(zp, b, t, t_pad)
    x2 = matmul_norm_res(hm.reshape(b * t, -1), hn.reshape(b * t, -1), hg.reshape(b * t, -1),
                         w_out, g_norms[1], x2)
    x2 = ffn(x2, g_norms[2], w_ff1, w_ff2, g_norms[3])
    return x2.reshape(b, t, d), (nsa_state[0], nsa_state[1], c, n, m, new_conv, s)


def kernel(x_prompt, x_sample, cache_nsa_kv, state_nsa_win, state_mlstm_C, state_mlstm_n, state_mlstm_m, state_mlstm_conv, state_gla_S, page_table, norms, w_in, mlstm_conv_w, mlstm_gate_b, mlstm_norm, nsa_cmp_pe, nsa_cmp_w1, nsa_cmp_w2, gla_gate_w2, gla_gate_b, gla_norm, w_out, w_ff1, w_ff2):
    xp, xs = x_prompt, x_sample
    bp = xp.shape[0]
    db = xs.shape[0]
    past_len = page_table.shape[1] * PAGE_SIZE
    conv0 = jnp.zeros((bp, CONV_W - 1, 2 * M_WIDTH), xp.dtype)
    c0 = jnp.zeros((bp, H_M, HEAD_DIM, HEAD_DIM), F32)
    n0 = jnp.zeros((bp, H_M, HEAD_DIM), F32)
    m0 = jnp.zeros((bp, H_M), F32)
    s0 = jnp.zeros((bp, H_G, HEAD_DIM, HEAD_DIM), F32)
    w_in_b = regroup_w_in(w_in).astype(BF16)
    w_out_b = w_out.astype(BF16)
    w_ff1_b = w_ff1.astype(BF16)
    w_ff2_b = w_ff2.astype(BF16)
    cmp_pe = nsa_cmp_pe.reshape(DEPTH, 2, 1, CMP_LEN * HEAD_DIM)
    cmp_w1_b = nsa_cmp_w1.reshape(DEPTH, 2, CMP_LEN * HEAD_DIM, CMP_HID).astype(BF16)
    cmp_w2_b = nsa_cmp_w2.astype(BF16)
    n_pool = cache_nsa_kv.shape[1]
    cache2d = cache_nsa_kv.reshape(DEPTH * n_pool * PAGE_SIZE, ROW_COLS)
    acc_p = [[] for _ in range(7)]
    acc_s = [[] for _ in range(7)]
    for l in range(DEPTH):
        weights = (norms[l], w_in_b[l], mlstm_conv_w[l], mlstm_gate_b[l], mlstm_norm[l],
                   gla_gate_w2[l], gla_gate_b[l], gla_norm[l], w_out_b[l], w_ff1_b[l], w_ff2_b[l])
        cmp_params = (nsa_cmp_pe[l], nsa_cmp_w1[l], nsa_cmp_w2[l])
        cmp_w = (cmp_pe[l], cmp_w1_b[l], cmp_w2_b[l])
        nsa_p = lambda zp, b, t, t_pad, cmp_w=cmp_w: nsa_prompt_pallas(zp, b, t, cmp_w)
        nsa_s = lambda zp, b, t, t_pad, l=l, cmp_w=cmp_w: nsa_sample_pallas(
            zp, b, t, t_pad, l, cache2d, page_table + l * n_pool, state_nsa_win, cmp_w)
        xp, st_p = trunk_layer(xp, conv0, c0, n0, m0, s0, nsa_p, *weights)
        xs, st_s = trunk_layer(xs, state_mlstm_conv[l], state_mlstm_C[l], state_mlstm_n[l],
                               state_mlstm_m[l], state_gla_S[l], nsa_s, *weights)
        for acc, a in zip(acc_p, st_p):
            acc.append(a.astype(xp.dtype))
        for acc, a in zip(acc_s, st_s):
            acc.append(a.astype(xs.dtype))
    nsa_rows_p, nsa_win_p, mlstm_c_p, mlstm_n_p, mlstm_m_p, mlstm_conv_p, gla_s_p = [jnp.stack(a) for a in acc_p]
    nsa_rows_s, nsa_win_s, mlstm_c_s, mlstm_n_s, mlstm_m_s, mlstm_conv_s, gla_s_s = [jnp.stack(a) for a in acc_s]
    return (xp, xs, nsa_rows_p, nsa_rows_s, nsa_win_p, nsa_win_s, mlstm_c_p, mlstm_c_s,
            mlstm_n_p, mlstm_n_s, mlstm_m_p, mlstm_m_s, mlstm_conv_p, mlstm_conv_s, gla_s_p, gla_s_s)
```

```python
import functools
import math

import jax
import jax.numpy as jnp
import numpy as np
from jax import lax
from jax.experimental import pallas as pl
from jax.experimental.pallas import tpu as pltpu

D_MODEL = 2048
DEPTH = 4
PAGE_SIZE = 128
HEAD_DIM = 128
D_MIX = D_MODEL
M_WIDTH = D_MIX // 4
G_WIDTH = D_MIX // 4
N_WIDTH = D_MIX - M_WIDTH - G_WIDTH
H_M = M_WIDTH // HEAD_DIM
H_N = N_WIDTH // HEAD_DIM
H_G = G_WIDTH // HEAD_DIM
KV_GROUPS = 2
Q_PER_KV = H_N // KV_GROUPS
D_FF = 4 * D_MODEL
CONV_W = 4
CHUNK = 64
CMP_STRIDE = 16
CMP_LEN = 2 * CMP_STRIDE
CMP_HID = 256
SEL_BLK = 64
N_SELECT = 16
WINDOW = 512
QBLK = 128
GATE_RANK = 16
GLA_GATE_TEMP = 16.0
ROPE_THETA = 10000.0
EPS = 1e-6
TINY = 1e-30
FORCE_BONUS = 1e3
NEG_BIG = -1e9
SPLIT_SIZES = (2 * M_WIDTH, M_WIDTH, M_WIDTH, 2 * H_M,
               N_WIDTH, 6 * KV_GROUPS * HEAD_DIM, 3 * H_N,
               G_WIDTH, G_WIDTH, G_WIDTH, G_WIDTH, GATE_RANK)
IN_COLS = sum(SPLIT_SIZES)
SPLIT_OFFSETS = tuple(int(o) for o in np.cumsum(SPLIT_SIZES)[:-1])

F32 = jnp.float32
BF16 = jnp.bfloat16

VMEM_LIMIT_BYTES = 56 * 1024 * 1024
IN_TN = 512
COL_NQ = 2 * M_WIDTH + 2 * M_WIDTH
COL_NKV = COL_NQ + N_WIDTH
COL_GLA = COL_NKV + 6 * KV_GROUPS * HEAD_DIM
COL_SMALL = COL_GLA + 4 * G_WIDTH
SMALL_MIF = 0
SMALL_NG = 2 * H_M
SMALL_GLR = SMALL_NG + 3 * H_N
IN_COLS_PAD = 7168
NEG = -1e30
NSA_TQ = 128
NSA_TK = 512


def _rms(x, g):
    return x * lax.rsqrt(jnp.mean(x * x, axis=-1, keepdims=True) + EPS) * g


def _norm_matmul_kernel(x_ref, g_ref, w_ref, o_ref, xn_ref):
    @pl.when(pl.program_id(1) == 0)
    def _():
        xn_ref[...] = _rms(x_ref[...], g_ref[...]).astype(BF16)

    o_ref[...] = jnp.dot(xn_ref[...], w_ref[...], preferred_element_type=F32)


def norm_matmul(x, g, w, tn):
    m, k = x.shape
    n = w.shape[1]
    tm = min(m, 1024)
    return pl.pallas_call(
        _norm_matmul_kernel,
        grid=(m // tm, n // tn),
        in_specs=[pl.BlockSpec((tm, k), lambda i, j: (i, 0)),
                  pl.BlockSpec((1, k), lambda i, j: (0, 0)),
                  pl.BlockSpec((k, tn), lambda i, j: (0, j))],
        out_specs=pl.BlockSpec((tm, tn), lambda i, j: (i, j)),
        out_shape=jax.ShapeDtypeStruct((m, n), F32),
        scratch_shapes=[pltpu.VMEM((tm, k), BF16)],
        compiler_params=pltpu.CompilerParams(
            dimension_semantics=("arbitrary", "arbitrary"), vmem_limit_bytes=VMEM_LIMIT_BYTES),
        name="norm_matmul",
    )(x, g.reshape(1, k), w)


def _matmul_norm_res_kernel(a0_ref, a1_ref, a2_ref, w_ref, g_ref, r_ref, o_ref):
    k0, k1 = a0_ref.shape[1], a0_ref.shape[1] + a1_ref.shape[1]
    y = (jnp.dot(a0_ref[...], w_ref[:k0, :], preferred_element_type=F32)
         + jnp.dot(a1_ref[...], w_ref[k0:k1, :], preferred_element_type=F32)
         + jnp.dot(a2_ref[...], w_ref[k1:, :], preferred_element_type=F32))
    o_ref[...] = r_ref[...] + _rms(y, g_ref[...])


def matmul_norm_res(a0, a1, a2, w, g, r):
    m = a0.shape[0]
    k, n = w.shape
    tm = min(m, 512)
    rows = lambda a: pl.BlockSpec((tm, a.shape[1]), lambda i: (i, 0))
    return pl.pallas_call(
        _matmul_norm_res_kernel,
        grid=(m // tm,),
        in_specs=[rows(a0), rows(a1), rows(a2),
                  pl.BlockSpec((k, n), lambda i: (0, 0)),
                  pl.BlockSpec((1, n), lambda i: (0, 0)),
                  pl.BlockSpec((tm, n), lambda i: (i, 0))],
        out_specs=pl.BlockSpec((tm, n), lambda i: (i, 0)),
        out_shape=jax.ShapeDtypeStruct((m, n), F32),
        compiler_params=pltpu.CompilerParams(
            dimension_semantics=("arbitrary",), vmem_limit_bytes=VMEM_LIMIT_BYTES),
        name="matmul_norm_res",
    )(a0, a1, a2, w, g.reshape(1, n), r)


def _ffn_kernel(x_ref, g2_ref, w1_ref, w2_ref, g3_ref, o_ref, xn_ref, acc_ref):
    f = pl.program_id(1)

    @pl.when(f == 0)
    def _():
        xn_ref[...] = _rms(x_ref[...], g2_ref[...]).astype(BF16)
        acc_ref[...] = jnp.zeros_like(acc_ref)

    h = jnp.dot(xn_ref[...], w1_ref[...], preferred_element_type=F32)
    a = jnp.square(jnp.maximum(h, 0.0)).astype(BF16)
    acc_ref[...] += jnp.dot(a, w2_ref[...], preferred_element_type=F32)

    @pl.when(f == pl.num_programs(1) - 1)
    def _():
        o_ref[...] = x_ref[...] + _rms(acc_ref[...], g3_ref[...])


def ffn(x, g2, w1, w2, g3):
    m, d = x.shape
    dff = w1.shape[1]
    tm = min(m, 512)
    tf = 512
    return pl.pallas_call(
        _ffn_kernel,
        grid=(m // tm, dff // tf),
        in_specs=[pl.BlockSpec((tm, d), lambda i, f: (i, 0)),
                  pl.BlockSpec((1, d), lambda i, f: (0, 0)),
                  pl.BlockSpec((d, tf), lambda i, f: (0, f)),
                  pl.BlockSpec((tf, d), lambda i, f: (f, 0)),
                  pl.BlockSpec((1, d), lambda i, f: (0, 0))],
        out_specs=pl.BlockSpec((tm, d), lambda i, f: (i, 0)),
        out_shape=jax.ShapeDtypeStruct((m, d), F32),
        scratch_shapes=[pltpu.VMEM((tm, d), BF16), pltpu.VMEM((tm, d), F32)],
        compiler_params=pltpu.CompilerParams(
            dimension_semantics=("arbitrary", "arbitrary"), vmem_limit_bytes=VMEM_LIMIT_BYTES),
        name="ffn",
    )(x, g2.reshape(1, d), w1, w2, g3.reshape(1, d))


def rope_tables(pos):
    half = HEAD_DIM // 2
    inv_freq = jnp.exp(-math.log(ROPE_THETA) * jnp.arange(half, dtype=F32) / half)
    ang = pos.astype(F32)[:, None] * inv_freq[None, :]
    cos, sin = jnp.cos(ang), jnp.sin(ang)
    return jnp.concatenate([cos, cos], axis=-1), jnp.concatenate([-sin, sin], axis=-1)


def _rope(x, cos2, sin2):
    return x * cos2 + pltpu.roll(x, HEAD_DIM // 2, 1) * sin2


def _nsa_prep_kernel(nq_ref, nkv_ref, cos_ref, sin_ref, q_ref, rows_ref, win_ref, kvb_ref, cmp_ref):
    cos2 = cos_ref[...]
    sin2 = sin_ref[...]
    for h in range(H_N):
        sl = slice(h * HEAD_DIM, (h + 1) * HEAD_DIM)
        q_ref[:, sl] = (_rope(nq_ref[:, sl], cos2, sin2) * (HEAD_DIM ** -0.5)).astype(BF16)
    for c in range(6 * KV_GROUPS):
        slot = c // KV_GROUPS
        x = nkv_ref[:, c * HEAD_DIM:(c + 1) * HEAD_DIM]
        if slot % 2 == 0:
            x = _rope(x, cos2, sin2)
        xb = x.astype(BF16)
        if slot < 4:
            rows_ref[:, c * HEAD_DIM:(c + 1) * HEAD_DIM] = x
        else:
            win_ref[:, (c - 8) * HEAD_DIM:(c - 7) * HEAD_DIM] = x
        if slot < 2:
            cmp_ref[0, c] = xb
        else:
            kvb_ref[:, (c - 4) * HEAD_DIM:(c - 3) * HEAD_DIM] = xb


def nsa_prep_pallas(z, b, s, cos2, sin2, tm=512):
    m = b * s
    nsb = s // tm
    return pl.pallas_call(
        _nsa_prep_kernel,
        grid=(m // tm,),
        in_specs=[pl.BlockSpec((tm, N_WIDTH), lambda i: (i, COL_NQ // N_WIDTH)),
                  pl.BlockSpec((tm, 1536), lambda i: (i, COL_NKV // 1536)),
                  pl.BlockSpec((tm, HEAD_DIM), lambda i: (i % nsb, 0)),
                  pl.BlockSpec((tm, HEAD_DIM), lambda i: (i % nsb, 0))],
        out_specs=[pl.BlockSpec((tm, N_WIDTH), lambda i: (i, 0)),
                   pl.BlockSpec((tm, 1024), lambda i: (i, 0)),
                   pl.BlockSpec((tm, 512), lambda i: (i, 0)),
                   pl.BlockSpec((tm, 1024), lambda i: (i, 0)),
                   pl.BlockSpec((1, 4, tm, HEAD_DIM), lambda i: (i // nsb, 0, i % nsb, 0))],
        out_shape=[jax.ShapeDtypeStruct((m, N_WIDTH), BF16),
                   jax.ShapeDtypeStruct((m, 1024), F32),
                   jax.ShapeDtypeStruct((m, 512), F32),
                   jax.ShapeDtypeStruct((m, 1024), BF16),
                   jax.ShapeDtypeStruct((b, 4, s, HEAD_DIM), BF16)],
        compiler_params=pltpu.CompilerParams(
            dimension_semantics=("arbitrary",), vmem_limit_bytes=VMEM_LIMIT_BYTES),
        name="nsa_prep",
    )(z, z, cos2, sin2)


def _nsa_cmp_kernel(x_ref, pe_ref, w1_ref, w2_ref, o_ref):
    x = x_ref[0, 0]
    nb = x.shape[0]
    half = CMP_STRIDE * HEAD_DIM
    first = jnp.dot(x, w1_ref[0, :half, :], preferred_element_type=F32)
    second = jnp.dot(x, w1_ref[0, half:, :], preferred_element_type=F32)
    pe = jnp.broadcast_to(pe_ref[0], (8, 2 * half)).astype(BF16)
    bias = jnp.dot(pe, w1_ref[0], preferred_element_type=F32)[0:1]
    hid = first + pltpu.roll(second, nb - 1, 0) + bias
    o_ref[0, 0] = jnp.dot(jax.nn.gelu(hid).astype(BF16), w2_ref[0], preferred_element_type=F32).astype(BF16)


def nsa_compress_prompt(cmp_in, pe, w1, w2):
    b, _, nb, kdim = cmp_in.shape
    return pl.pallas_call(
        _nsa_cmp_kernel,
        grid=(b, 4),
        in_specs=[pl.BlockSpec((1, 1, nb, kdim), lambda i, c: (i, c, 0, 0)),
                  pl.BlockSpec((1, 1, 2 * kdim), lambda i, c: (c // 2, 0, 0)),
                  pl.BlockSpec((1, 2 * kdim, CMP_HID), lambda i, c: (c // 2, 0, 0)),
                  pl.BlockSpec((1, CMP_HID, HEAD_DIM), lambda i, c: (c // 2, 0, 0))],
        out_specs=pl.BlockSpec((1, 1, nb, HEAD_DIM), lambda i, c: (i, c, 0, 0)),
        out_shape=jax.ShapeDtypeStruct((b, 4, nb, HEAD_DIM), BF16),
        compiler_params=pltpu.CompilerParams(
            dimension_semantics=("arbitrary", "arbitrary"), vmem_limit_bytes=VMEM_LIMIT_BYTES),
        name="nsa_compress",
    )(cmp_in, pe, w1, w2)


def _softmax_rows(s, mask):
    sm = jnp.where(mask, s, NEG)
    m = jnp.max(sm, axis=-1, keepdims=True)
    p = jnp.where(mask, jnp.exp(sm - m), 0.0)
    return p / jnp.maximum(jnp.sum(p, axis=-1, keepdims=True), TINY)


_NT = (((1,), (1,)), ((), ()))


def _nsa_attn_kernel(q_ref, kc_ref, vc_ref, ks_ref, vs_ref, kw_ref, vw_ref, gate_ref, o_ref):
    tq = NSA_TQ
    r4 = Q_PER_KV
    g = pl.program_id(1)
    i = pl.program_id(2)
    q4 = q_ref[...]
    q = jnp.concatenate([q4[:, r * HEAD_DIM:(r + 1) * HEAD_DIM] for r in range(r4)], axis=0)
    q0 = i * tq
    tpos = q0 + lax.broadcasted_iota(jnp.int32, (tq, 1), 0)

    ncp = kc_ref.shape[2]
    s = lax.dot_general(q, kc_ref[0, 0], _NT, preferred_element_type=F32).reshape(r4, tq, ncp)
    cend = lax.broadcasted_iota(jnp.int32, (tq, ncp), 1) * CMP_STRIDE + (CMP_LEN - 1)
    p_c = _softmax_rows(s, (cend <= tpos)[None])
    o_c = jnp.dot(p_c.reshape(r4 * tq, ncp).astype(BF16), vc_ref[0, 0], preferred_element_type=F32)

    psum = p_c[0] + p_c[1] + p_c[2] + p_c[3]
    n_i = lax.broadcasted_iota(jnp.int32, (ncp, HEAD_DIM), 0)
    j_i = lax.broadcasted_iota(jnp.int32, (ncp, HEAD_DIM), 1)
    ratio = SEL_BLK // CMP_STRIDE
    c2s = jnp.where((n_i >= ratio * j_i - 1) & (n_i <= ratio * j_i + ratio - 1) & (n_i < ncp - 1)
                    & (j_i < ncp // ratio), 1.0, 0.0).astype(BF16)
    p_hi = psum.astype(BF16)
    p_lo = (psum - p_hi.astype(F32)).astype(BF16)
    imp = (jnp.dot(p_hi, c2s, preferred_element_type=F32)
           + jnp.dot(p_lo, c2s, preferred_element_type=F32))

    ns = ncp // ratio
    imp_t = jnp.transpose(imp)[:ns]
    jj = lax.broadcasted_iota(jnp.int32, (ns, tq), 0)
    tt = q0 + lax.broadcasted_iota(jnp.int32, (ns, tq), 1)
    cur = tt // SEL_BLK
    forced = (jj == 0) | (jj == cur) | (jj == cur - 1)
    score = jnp.where(jj * SEL_BLK <= tt, imp_t + jnp.where(forced, FORCE_BONUS, 0.0), NEG_BIG)
    rank = jnp.zeros((ns, tq), F32)
    for jp in range(ns):
        row = score[jp:jp + 1, :]
        rank = rank + jnp.where(row > score, 1.0, jnp.where((row == score) & (jj > jp), 1.0, 0.0))
    sel_t = jnp.where(rank < float(N_SELECT), 1.0, 0.0)
    sel_t = jnp.concatenate([sel_t, jnp.zeros((HEAD_DIM - ns, tq), F32)], axis=0)
    sel = jnp.transpose(sel_t)

    tk = NSA_TK
    n_tiles = (q0 + tq + tk - 1) // tk
    unpicked = jnp.where(sel > 0.5, 0.0, NEG).astype(BF16)
    q_aug = jnp.concatenate([q, jnp.concatenate([unpicked] * r4, axis=0)], axis=1)
    blk_lane = lax.broadcasted_iota(jnp.int32, (tk, HEAD_DIM), 1)
    blk_of_key = lax.broadcasted_iota(jnp.int32, (tk, HEAD_DIM), 0) // SEL_BLK

    def tile_scores(kt):
        k0 = pl.multiple_of(kt * tk, tk)
        onehot = jnp.where(blk_lane == kt * (tk // SEL_BLK) + blk_of_key, 1.0, 0.0).astype(BF16)
        k_aug = jnp.concatenate([ks_ref[pl.ds(k0, tk), :], onehot], axis=1)
        return lax.dot_general(q_aug, k_aug, _NT, preferred_element_type=F32).reshape(r4, tq, tk), k0

    def accumulate(carry, s, k0):
        m_run, l_run, acc = carry
        m_new = jnp.maximum(m_run, jnp.max(s, axis=-1, keepdims=True))
        alpha = jnp.exp(m_run - m_new)
        p = jnp.exp(s - m_new)
        l_new = alpha * l_run + jnp.sum(p, axis=-1, keepdims=True)
        pv = jnp.dot(p.reshape(r4 * tq, tk).astype(BF16), vs_ref[pl.ds(k0, tk), :], preferred_element_type=F32)
        return m_new, l_new, alpha.reshape(r4 * tq, 1) * acc + pv

    def full_tile(kt, carry):
        s, k0 = tile_scores(kt)
        return accumulate(carry, s, k0)

    init = (jnp.full((r4, tq, 1), NEG, F32), jnp.zeros((r4, tq, 1), F32), jnp.zeros((r4 * tq, HEAD_DIM), F32))
    carry = lax.fori_loop(0, n_tiles - 1, full_tile, init)
    s, k0 = tile_scores(n_tiles - 1)
    kpos = k0 + lax.broadcasted_iota(jnp.int32, (tq, tk), 1)
    _, l_s, acc_s = accumulate(carry, jnp.where((kpos <= tpos)[None], s, NEG), k0)
    o_s = acc_s / jnp.maximum(l_s, TINY).reshape(r4 * tq, 1)

    wb = WINDOW + tq
    w0 = pl.multiple_of(jnp.maximum(q0 - WINDOW, 0), tq)
    s = lax.dot_general(q, kw_ref[pl.ds(w0, wb), :], _NT, preferred_element_type=F32).reshape(r4, tq, wb)
    dist = tpos - (w0 + lax.broadcasted_iota(jnp.int32, (tq, wb), 1))
    p_w = _softmax_rows(s, ((dist >= 0) & (dist < WINDOW))[None])
    o_w = jnp.dot(p_w.reshape(r4 * tq, wb).astype(BF16), vw_ref[pl.ds(w0, wb), :], preferred_element_type=F32)

    gates = jax.nn.sigmoid(gate_ref[...])
    for r in range(r4):
        def gate(c):
            lane0 = SMALL_NG + 3 * r + c
            lane1 = lane0 + 3 * r4
            return jnp.where(g == 0, gates[:, lane0:lane0 + 1], gates[:, lane1:lane1 + 1])
        rows = slice(r * tq, (r + 1) * tq)
        o = gate(0) * o_c[rows] + gate(1) * o_s[rows] + gate(2) * o_w[rows]
        o_ref[:, r * HEAD_DIM:(r + 1) * HEAD_DIM] = o.astype(BF16)


def nsa_attention_prompt(q_b, kcvc, kvb, z, b, s):
    tq = NSA_TQ
    nq = s // tq
    ncp = kcvc.shape[2]
    gw = Q_PER_KV * HEAD_DIM
    seq_block = lambda col: pl.BlockSpec((s, HEAD_DIM), lambda bi, g, i: (bi, col + g))
    return pl.pallas_call(
        _nsa_attn_kernel,
        grid=(b, KV_GROUPS, nq),
        in_specs=[pl.BlockSpec((tq, gw), lambda bi, g, i: (bi * nq + i, g)),
                  pl.BlockSpec((1, 1, ncp, HEAD_DIM), lambda bi, g, i: (bi, g, 0, 0)),
                  pl.BlockSpec((1, 1, ncp, HEAD_DIM), lambda bi, g, i: (bi, KV_GROUPS + g, 0, 0)),
                  seq_block(0), seq_block(2), seq_block(4), seq_block(6),
                  pl.BlockSpec((tq, HEAD_DIM), lambda bi, g, i: (bi * nq + i, COL_SMALL // HEAD_DIM))],
        out_specs=pl.BlockSpec((tq, gw), lambda bi, g, i: (bi * nq + i, g)),
        out_shape=jax.ShapeDtypeStruct((b * s, N_WIDTH), BF16),
        compiler_params=pltpu.CompilerParams(
            dimension_semantics=("arbitrary", "arbitrary", "arbitrary"), vmem_limit_bytes=VMEM_LIMIT_BYTES),
        name="nsa_attention",
    )(q_b, kcvc, kcvc, kvb, kvb, kvb, kvb, z)


def nsa_prompt_pallas(z, b, s, cmp_w):
    pe, w1, w2 = cmp_w
    cos2, sin2 = rope_tables(jnp.arange(s))
    q_b, rows_f, win_f, kvb, cmp_in = nsa_prep_pallas(z, b, s, cos2, sin2)
    kcvc = nsa_compress_prompt(cmp_in.reshape(b, 4, s // CMP_STRIDE, CMP_STRIDE * HEAD_DIM), pe, w1, w2)
    hn = nsa_attention_prompt(q_b, kcvc, kvb, z, b, s)
    rows = rows_f.reshape(b, s, 4, KV_GROUPS, HEAD_DIM)
    win = win_f.reshape(b, s, 2, KV_GROUPS, HEAD_DIM)[:, -min(WINDOW, s):]
    return hn, (rows, win)


SAMPLE_PAGES = 16
SAMPLE_TS = 16
ROW_COLS = 4 * KV_GROUPS * HEAD_DIM
ROW_VECS = 4 * KV_GROUPS
WIN_VECS = 2 * KV_GROUPS
NEG_DEAD = -3e38


def _page_specs():
    def spec(k):
        return pl.BlockSpec((PAGE_SIZE * ROW_VECS, HEAD_DIM), lambda bi, i, pt: (pt[bi, i * SAMPLE_PAGES + k], 0))
    return [spec(k) for k in range(SAMPLE_PAGES)]


def _page_vecs(pg, vec):
    return pg[pl.ds(vec, PAGE_SIZE, stride=ROW_VECS), :]


def _nsa_cmp_sample_kernel(pt_ref, *refs):
    del pt_ref
    pages = refs[:SAMPLE_PAGES]
    pe_ref, w1_ref, w2_ref, o_ref, carry_sc, bias_sc = refs[SAMPLE_PAGES:]
    i = pl.program_id(1)
    half = CMP_STRIDE * HEAD_DIM
    nbp = PAGE_SIZE // CMP_STRIDE
    nb = SAMPLE_PAGES * nbp

    @pl.when(i == 0)
    def _():
        carry_sc[...] = jnp.zeros_like(carry_sc)
        for kv in range(2):
            pe = jnp.broadcast_to(pe_ref[kv], (8, 2 * half)).astype(BF16)
            bias_sc[kv] = jnp.dot(pe, w1_ref[kv], preferred_element_type=F32)

    row = lax.broadcasted_iota(jnp.int32, (nb, CMP_HID), 0)
    for kv in range(2):
        def flat(c):
            piece = lambda pg, j: pg[pl.ds(j * ROW_VECS + c, nbp, stride=CMP_STRIDE * ROW_VECS), :]
            return jnp.concatenate(
                [jnp.concatenate([piece(pg, j) for j in range(CMP_STRIDE)], axis=1) for pg in pages],
                axis=0)
        x = jnp.concatenate([flat(kv * KV_GROUPS + g) for g in range(KV_GROUPS)], axis=0).astype(BF16)
        first_all = jnp.dot(x, w1_ref[kv, :half, :], preferred_element_type=F32)
        second_all = jnp.dot(x, w1_ref[kv, half:, :], preferred_element_type=F32)
        for g in range(KV_GROUPS):
            c = kv * KV_GROUPS + g
            first = first_all[g * nb:(g + 1) * nb]
            second = second_all[g * nb:(g + 1) * nb]
            shifted = jnp.where(row == 0, carry_sc[c, 7:8, :], pltpu.roll(first, 1, 0))
            hid = shifted + second + bias_sc[kv, 0:1, :]
            o_ref[0, c] = jnp.dot(jax.nn.gelu(hid).astype(BF16), w2_ref[kv],
                                  preferred_element_type=F32).astype(BF16)
            carry_sc[c] = first[nb - 8:, :]


def nsa_compress_sample(pt, cache2d, pe, w1, w2, db):
    n_pages = pt.shape[1]
    nb_all = n_pages * (PAGE_SIZE // CMP_STRIDE)
    nb = SAMPLE_PAGES * (PAGE_SIZE // CMP_STRIDE)
    const3 = lambda bi, i, pt: (0, 0, 0)
    grid_spec = pltpu.PrefetchScalarGridSpec(
        num_scalar_prefetch=1,
        grid=(db, n_pages // SAMPLE_PAGES),
        in_specs=_page_specs() + [pl.BlockSpec(pe.shape, const3), pl.BlockSpec(w1.shape, const3),
                                   pl.BlockSpec(w2.shape, const3)],
        out_specs=pl.BlockSpec((1, 2 * KV_GROUPS, nb, HEAD_DIM), lambda bi, i, pt: (bi, 0, i, 0)),
        scratch_shapes=[pltpu.VMEM((2 * KV_GROUPS, 8, CMP_HID), F32), pltpu.VMEM((2, 8, CMP_HID), F32)])
    return pl.pallas_call(
        _nsa_cmp_sample_kernel,
        grid_spec=grid_spec,
        out_shape=jax.ShapeDtypeStruct((db, 2 * KV_GROUPS, nb_all, HEAD_DIM), BF16),
        compiler_params=pltpu.CompilerParams(
            dimension_semantics=("arbitrary", "arbitrary"), vmem_limit_bytes=VMEM_LIMIT_BYTES),
        name="nsa_compress_sample",
    )(pt, *([cache2d] * SAMPLE_PAGES), pe, w1, w2)


def _nsa_attn_sample_kernel(pt_ref, *refs, past_len, t_valid, n_steps):
    del pt_ref
    pages = refs[:SAMPLE_PAGES]
    q_ref, new_ref, kc_ref, win_ref, gate_ref, o_ref, sel_sc, m_sc, l_sc, acc_sc, oc_sc, ow_sc = refs[SAMPLE_PAGES:]
    i = pl.program_id(1)
    ts = SAMPLE_TS
    r4 = Q_PER_KV
    rows = r4 * ts
    d = HEAD_DIM
    blocks_per_step = SAMPLE_PAGES * PAGE_SIZE // SEL_BLK
    ns = past_len // SEL_BLK + 1
    trow = lax.broadcasted_iota(jnp.int32, (rows, 1), 0) % ts
    qpos = past_len + trow

    def q_of(g):
        return jnp.concatenate([q_ref[:, (g * r4 + r) * d:(g * r4 + r + 1) * d] for r in range(r4)], axis=0)

    @pl.when(i == 0)
    def _():
        ncp = kc_ref.shape[2]
        nsl = (n_steps + 1) * d
        lane = lax.broadcasted_iota(jnp.int32, (ncp, nsl), 1)
        tok = lax.broadcasted_iota(jnp.int32, (ncp, nsl), 0) - 1
        blk = (lane // d) * blocks_per_step + lane % d
        ratio = SEL_BLK // CMP_STRIDE
        c2s = jnp.where((lane % d < blocks_per_step) & (blk < ns) & (tok >= 0)
                        & (tok >= ratio * blk - 1) & (tok <= ratio * blk + ratio - 1), 1.0, 0.0).astype(BF16)
        lane_r = lax.broadcasted_iota(jnp.int32, (rows, nsl), 1)
        blk_r = (lane_r // d) * blocks_per_step + lane_r % d
        live = (lane_r % d < blocks_per_step) & (blk_r < ns)
        blk_f = blk_r.astype(F32)
        cur = qpos // SEL_BLK
        forced = (blk_r == 0) | (blk_r == cur) | (blk_r == cur - 1)
        midx = lax.broadcasted_iota(jnp.int32, (rows, ncp), 1)
        cmask = (midx >= 1) & ((midx - 1) * CMP_STRIDE + CMP_LEN - 1 <= qpos)
        sidx = lax.broadcasted_iota(jnp.int32, (rows, ts), 1)
        new_ok = (sidx <= trow) & (sidx < t_valid)
        wl = win_ref.shape[0] // WIN_VECS
        dist_buf = trow + wl - lax.broadcasted_iota(jnp.int32, (rows, wl), 1)
        buf_ok = (dist_buf >= 0) & (dist_buf < WINDOW)
        dist_new = trow - sidx
        wnew_ok = (dist_new >= 0) & (dist_new < WINDOW) & (sidx < t_valid)
        for g in range(KV_GROUPS):
            qg = q_of(g)
            s = lax.dot_general(qg, kc_ref[0, g], _NT, preferred_element_type=F32)
            p_c = _softmax_rows(s, cmask)
            oc_sc[g] = jnp.dot(p_c.astype(BF16), kc_ref[0, KV_GROUPS + g], preferred_element_type=F32)
            psum_t = p_c[0:ts] + p_c[ts:2 * ts] + p_c[2 * ts:3 * ts] + p_c[3 * ts:4 * ts]
            psum = jnp.concatenate([psum_t] * r4, axis=0)
            p_hi = psum.astype(BF16)
            p_lo = (psum - p_hi.astype(F32)).astype(BF16)
            imp = (jnp.dot(p_hi, c2s, preferred_element_type=F32) + jnp.dot(p_lo, c2s, preferred_element_type=F32))
            score = jnp.where(live, jnp.where(blk_r * SEL_BLK <= qpos, imp + jnp.where(forced, FORCE_BONUS, 0.0),
                                              NEG_BIG), NEG_DEAD)
            sel = jnp.zeros((rows, nsl), F32)
            for _ in range(N_SELECT):
                top = jnp.max(score, axis=1, keepdims=True)
                first = jnp.min(jnp.where(score == top, blk_f, 1e9), axis=1, keepdims=True)
                hit = live & (blk_f == first)
                sel = jnp.where(hit, 1.0, sel)
                score = jnp.where(hit, NEG_DEAD, score)
            for step in range(n_steps):
                sel_sc[g, step] = sel[:, step * d:(step + 1) * d]
            knew = new_ref[:, g * d:(g + 1) * d]
            vnew = new_ref[:, (KV_GROUPS + g) * d:(KV_GROUPS + g + 1) * d]
            sm = jnp.where(new_ok, lax.dot_general(qg, knew, _NT, preferred_element_type=F32), NEG)
            m0 = jnp.max(sm, axis=1, keepdims=True)
            p = jnp.where(new_ok, jnp.exp(sm - m0), 0.0)
            m_sc[g] = m0
            l_sc[g] = jnp.sum(p, axis=1, keepdims=True)
            acc_sc[g] = jnp.dot(p.astype(BF16), vnew, preferred_element_type=F32)
            kwb = win_ref[pl.ds(g, wl, stride=WIN_VECS), :].astype(BF16)
            vwb = win_ref[pl.ds(KV_GROUPS + g, wl, stride=WIN_VECS), :].astype(BF16)
            kwn = new_ref[:, (2 * KV_GROUPS + g) * d:(2 * KV_GROUPS + g + 1) * d]
            vwn = new_ref[:, (3 * KV_GROUPS + g) * d:(3 * KV_GROUPS + g + 1) * d]
            s1 = jnp.where(buf_ok, lax.dot_general(qg, kwb, _NT, preferred_element_type=F32), NEG)
            s2 = jnp.where(wnew_ok, lax.dot_general(qg, kwn, _NT, preferred_element_type=F32), NEG)
            mw = jnp.maximum(jnp.max(s1, axis=1, keepdims=True), jnp.max(s2, axis=1, keepdims=True))
            p1 = jnp.where(buf_ok, jnp.exp(s1 - mw), 0.0)
            p2 = jnp.where(wnew_ok, jnp.exp(s2 - mw), 0.0)
            den = jnp.maximum(jnp.sum(p1, axis=1, keepdims=True) + jnp.sum(p2, axis=1, keepdims=True), TINY)
            ow_sc[g] = (jnp.dot((p1 / den).astype(BF16), vwb, preferred_element_type=F32)
                        + jnp.dot((p2 / den).astype(BF16), vwn, preferred_element_type=F32))

    half_lane = lax.broadcasted_iota(jnp.int32, (rows, PAGE_SIZE), 1) < SEL_BLK
    for g in range(KV_GROUPS):
        qg = q_of(g)
        kt = jnp.concatenate([_page_vecs(pg, 2 * KV_GROUPS + g) for pg in pages], axis=0).astype(BF16)
        vt = jnp.concatenate([_page_vecs(pg, 3 * KV_GROUPS + g) for pg in pages], axis=0).astype(BF16)
        s = lax.dot_general(qg, kt, _NT, preferred_element_type=F32)
        selg = sel_sc[g, i]
        picked = jnp.concatenate(
            [jnp.where(half_lane, selg[:, 2 * k:2 * k + 1], selg[:, 2 * k + 1:2 * k + 2])
             for k in range(SAMPLE_PAGES)], axis=1)
        mask = picked > 0.5
        sm = jnp.where(mask, s, NEG)
        m_old = m_sc[g]
        m_new = jnp.maximum(m_old, jnp.max(sm, axis=1, keepdims=True))
        alpha = jnp.exp(m_old - m_new)
        p = jnp.where(mask, jnp.exp(sm - m_new), 0.0)
        m_sc[g] = m_new
        l_sc[g] = alpha * l_sc[g] + jnp.sum(p, axis=1, keepdims=True)
        acc_sc[g] = alpha * acc_sc[g] + jnp.dot(p.astype(BF16), vt, preferred_element_type=F32)

    @pl.when(i == n_steps - 1)
    def _():
        gates = jax.nn.sigmoid(gate_ref[...])
        for g in range(KV_GROUPS):
            o_s = acc_sc[g] / jnp.maximum(l_sc[g], TINY)
            o_c = oc_sc[g]
            o_w = ow_sc[g]
            for r in range(r4):
                h = g * r4 + r
                gate = lambda c: gates[:, SMALL_NG + 3 * h + c:SMALL_NG + 3 * h + c + 1]
                rs = slice(r * ts, (r + 1) * ts)
                o = gate(0) * o_c[rs] + gate(1) * o_s[rs] + gate(2) * o_w[rs]
                o_ref[0, :, h * d:(h + 1) * d] = o.astype(BF16)


def nsa_attention_sample(pt, cache2d, q_b, kvb, kcvc, win2d, zp, layer, db, t_pad, t_valid, past_len):
    ts = SAMPLE_TS
    n_steps = pt.shape[1] // SAMPLE_PAGES
    rows = Q_PER_KV * ts
    wl_rows = win2d.shape[0] // (DEPTH * db)
    seq_rows = lambda width, col: pl.BlockSpec((ts, width), lambda bi, i, pt: (bi * (t_pad // ts), col))
    grid_spec = pltpu.PrefetchScalarGridSpec(
        num_scalar_prefetch=1,
        grid=(db, n_steps),
        in_specs=_page_specs() + [
            seq_rows(N_WIDTH, 0),
            seq_rows(ROW_COLS, 0),
            pl.BlockSpec((1,) + kcvc.shape[1:], lambda bi, i, pt: (bi, 0, 0, 0)),
            pl.BlockSpec((wl_rows, HEAD_DIM), lambda bi, i, pt: (layer * db + bi, 0)),
            seq_rows(HEAD_DIM, COL_SMALL // HEAD_DIM)],
        out_specs=pl.BlockSpec((1, ts, N_WIDTH), lambda bi, i, pt: (bi, 0, 0)),
        scratch_shapes=[pltpu.VMEM((KV_GROUPS, n_steps, rows, HEAD_DIM), F32),
                        pltpu.VMEM((KV_GROUPS, rows, 1), F32),
                        pltpu.VMEM((KV_GROUPS, rows, 1), F32),
                        pltpu.VMEM((KV_GROUPS, rows, HEAD_DIM), F32),
                        pltpu.VMEM((KV_GROUPS, rows, HEAD_DIM), F32),
                        pltpu.VMEM((KV_GROUPS, rows, HEAD_DIM), F32)])
    return pl.pallas_call(
        functools.partial(_nsa_attn_sample_kernel, past_len=past_len, t_valid=t_valid, n_steps=n_steps),
        grid_spec=grid_spec,
        out_shape=jax.ShapeDtypeStruct((db, ts, N_WIDTH), BF16),
        compiler_params=pltpu.CompilerParams(
            dimension_semantics=("arbitrary", "arbitrary"), vmem_limit_bytes=VMEM_LIMIT_BYTES),
        name="nsa_attention_sample",
    )(pt, *([cache2d] * SAMPLE_PAGES), q_b, kvb, kcvc, win2d, zp)


def nsa_sample_pallas(zp, db, t, t_pad, layer, cache2d, pt, win_state, cmp_w):
    pe, w1, w2 = cmp_w
    past_len = pt.shape[1] * PAGE_SIZE
    cos2, sin2 = rope_tables(past_len + jnp.arange(t_pad))
    q_b, rows_f, win_f, kvb, _ = nsa_prep_pallas(zp, db, t_pad, cos2, sin2, tm=t_pad)
    kcvc = nsa_compress_sample(pt, cache2d, pe, w1, w2, db)
    win2d = win_state.reshape(-1, HEAD_DIM)
    hn = nsa_attention_sample(pt, cache2d, q_b, kvb, kcvc, win2d, zp, layer, db, t_pad, t, past_len)
    rows = rows_f.reshape(db, t_pad, 4, KV_GROUPS, HEAD_DIM)[:, :t]
    win_new = win_f.reshape(db, t_pad, 2, KV_GROUPS, HEAD_DIM)[:, :t]
    wl = win_state.shape[2]
    win = jnp.concatenate([win_state[layer], win_new], axis=1)[:, -wl:]
    return hn[:, :t], (rows, win)


MIX_L = 128
GLA_SUB = 16


def _logsig(x):
    return jnp.minimum(x, 0.0) - jnp.log1p(jnp.exp(-jnp.abs(x)))


def _prefix_sum(x, axis):
    n = x.shape[axis]
    idx = lax.broadcasted_iota(jnp.int32, x.shape, axis)
    step = 1
    while step < n:
        x = x + jnp.where(idx >= step, pltpu.roll(x, step, axis), 0.0)
        step *= 2
    return x


def _mlstm_kernel(uqk_ref, v_ref, og_ref, small_ref, convw_ref, bias_ref, norm_ref, conv0_ref, c0_ref, n0_ref,
                  m0_ref, h_ref, c_ref, n_ref, m_ref, prev_sc, c_sc, n_sc, m_sc, *, t_valid):
    t = pl.program_id(1)
    L = MIX_L
    d = HEAD_DIM

    @pl.when(t == 0)
    def _():
        prev_sc[...] = jnp.zeros_like(prev_sc)
        prev_sc[L - 8:, :] = conv0_ref[0]
        c_sc[...] = c0_ref[0]
        n_sc[...] = n0_ref[0]
        m_sc[...] = m0_ref[0]

    x = uqk_ref[...]
    prev = prev_sc[...]
    row = lax.broadcasted_iota(jnp.int32, x.shape, 0)
    w = convw_ref[...]
    conv = w[CONV_W - 1:CONV_W] * x
    for k in range(1, CONV_W):
        shifted = jnp.where(row >= k, pltpu.roll(x, k, 0), pltpu.roll(prev, k, 0))
        conv = conv + w[CONV_W - 1 - k:CONV_W - k] * shifted
    prev_sc[...] = x
    act = conv * jax.nn.sigmoid(conv)

    pre = small_ref[...] + bias_ref[...]
    pos_c = t * L + lax.broadcasted_iota(jnp.int32, (L, HEAD_DIM), 0)
    ig_c = jnp.where(pos_c < t_valid, pre, NEG)
    b_c = _prefix_sum(jnp.where(pos_c < t_valid, _logsig(pre), 0.0), 0)
    pre_r = jnp.transpose(pre)[0:8]
    pos_r = t * L + lax.broadcasted_iota(jnp.int32, (8, L), 1)
    ig_r = jnp.where(pos_r < t_valid, pre_r, NEG)
    b_r = _prefix_sum(jnp.where(pos_r < t_valid, _logsig(pre_r), 0.0), 1)

    li = lax.broadcasted_iota(jnp.int32, (L, L), 0)
    si = lax.broadcasted_iota(jnp.int32, (L, L), 1)
    for h in range(H_M):
        sl = slice(h * d, (h + 1) * d)
        bcol, igcol = b_c[:, H_M + h:H_M + h + 1], ig_c[:, h:h + 1]
        brow, igrow = b_r[H_M + h:H_M + h + 1, :], ig_r[h:h + 1, :]
        m0 = m_sc[h:h + 1, 0:1]
        c0 = c_sc[h]
        n0 = n_sc[h:h + 1, :]
        qf = act[:, sl]
        kf = act[:, M_WIDTH + h * d:M_WIDTH + (h + 1) * d] * (d ** -0.5)
        vf = v_ref[:, sl]
        qb, kb, vb = qf.astype(BF16), kf.astype(BF16), vf.astype(BF16)
        logw = jnp.where(si <= li, bcol - brow + igrow, NEG)
        gsum = bcol + m0
        m_row = jnp.maximum(jnp.max(logw, axis=1, keepdims=True), gsum)
        wgt = jnp.exp(logw - m_row) * lax.dot_general(qb, kb, _NT, preferred_element_type=F32)
        inter = jnp.exp(gsum - m_row)
        num = (jnp.dot(wgt.astype(BF16), vb, preferred_element_type=F32)
               + inter * lax.dot_general(qb, c0.astype(BF16), _NT, preferred_element_type=F32))
        den = jnp.sum(wgt, axis=1, keepdims=True) + inter * jnp.sum(qf * n0, axis=1, keepdims=True)
        hh = num / jnp.maximum(jnp.abs(den), jnp.exp(-m_row))
        bl = bcol[L - 1:L, :]
        m_new = jnp.maximum(jnp.max(bl - brow + igrow, axis=1, keepdims=True), bl + m0)
        wa = jnp.exp(bl - bcol + igcol - m_new)
        wc = jnp.exp(bl + m0 - m_new)
        c_sc[h] = wc * c0 + jnp.dot(jnp.transpose(vf * wa).astype(BF16), kb, preferred_element_type=F32)
        n_sc[h:h + 1, :] = wc * n0 + jnp.sum(wa * kf, axis=0, keepdims=True)
        m_sc[h:h + 1, :] = jnp.broadcast_to(m_new, (1, d))
        out = _rms(hh, norm_ref[:, sl]) * jax.nn.sigmoid(og_ref[:, sl])
        h_ref[:, sl] = out.astype(BF16)

    @pl.when(t == pl.num_programs(1) - 1)
    def _():
        c_ref[0] = c_sc[...]
        n_ref[0] = n_sc[...]
        m_ref[0] = m_sc[...]


def mlstm_pallas(z, b, t_pad, t_valid, conv_w, gate_b, norm, conv0, c0, n0, m0):
    L = MIX_L
    nt = t_pad // L
    rowblk = lambda col: (lambda bi, t: (bi * nt + t, col))
    const2 = lambda bi, t: (0, 0)
    per_b3 = lambda bi, t: (bi, 0, 0)
    per_b4 = lambda bi, t: (bi, 0, 0, 0)
    return pl.pallas_call(
        functools.partial(_mlstm_kernel, t_valid=t_valid),
        grid=(b, nt),
        in_specs=[pl.BlockSpec((L, 2 * M_WIDTH), rowblk(0)),
                  pl.BlockSpec((L, M_WIDTH), rowblk(2)),
                  pl.BlockSpec((L, M_WIDTH), rowblk(3)),
                  pl.BlockSpec((L, HEAD_DIM), rowblk(COL_SMALL // HEAD_DIM)),
                  pl.BlockSpec((CONV_W, 2 * M_WIDTH), const2),
                  pl.BlockSpec((1, HEAD_DIM), const2),
                  pl.BlockSpec((1, M_WIDTH), const2),
                  pl.BlockSpec((1, 8, 2 * M_WIDTH), per_b3),
                  pl.BlockSpec((1, H_M, HEAD_DIM, HEAD_DIM), per_b4),
                  pl.BlockSpec((1, 8, HEAD_DIM), per_b3),
                  pl.BlockSpec((1, 8, HEAD_DIM), per_b3)],
        out_specs=[pl.BlockSpec((L, M_WIDTH), rowblk(0)),
                   pl.BlockSpec((1, H_M, HEAD_DIM, HEAD_DIM), per_b4),
                   pl.BlockSpec((1, 8, HEAD_DIM), per_b3),
                   pl.BlockSpec((1, 8, HEAD_DIM), per_b3)],
        out_shape=[jax.ShapeDtypeStruct((b * t_pad, M_WIDTH), BF16),
                   jax.ShapeDtypeStruct((b, H_M, HEAD_DIM, HEAD_DIM), F32),
                   jax.ShapeDtypeStruct((b, 8, HEAD_DIM), F32),
                   jax.ShapeDtypeStruct((b, 8, HEAD_DIM), F32)],
        scratch_shapes=[pltpu.VMEM((L, 2 * M_WIDTH), F32),
                        pltpu.VMEM((H_M, HEAD_DIM, HEAD_DIM), F32),
                        pltpu.VMEM((8, HEAD_DIM), F32),
                        pltpu.VMEM((8, HEAD_DIM), F32)],
        compiler_params=pltpu.CompilerParams(
            dimension_semantics=("arbitrary", "arbitrary"), vmem_limit_bytes=VMEM_LIMIT_BYTES),
        name="mlstm",
    )(z, z, z, z, conv_w, gate_b, norm, conv0, c0, n0, m0)


def _gla_kernel(q_ref, k_ref, v_ref, r_ref, small_ref, w2_ref, gb_ref, norm_ref, s0_ref, h_ref, s_ref, st_sc,
                *, t_valid):
    t = pl.program_id(1)
    L = MIX_L
    d = HEAD_DIM

    @pl.when(t == 0)
    def _():
        for h in range(H_G):
            st_sc[h] = jnp.transpose(s0_ref[0, h])

    pre = jnp.dot(small_ref[...].astype(BF16), w2_ref[...], preferred_element_type=F32) + gb_ref[...]
    pos = t * L + lax.broadcasted_iota(jnp.int32, (L, G_WIDTH), 0)
    la = jnp.where(pos < t_valid, _logsig(pre) / GLA_GATE_TEMP, 0.0)
    bc_all = _prefix_sum(la, 0)
    rowi = lax.broadcasted_iota(jnp.int32, (L, d), 0)
    li = lax.broadcasted_iota(jnp.int32, (L, L), 0)
    si = lax.broadcasted_iota(jnp.int32, (L, L), 1)
    for h in range(H_G):
        sl = slice(h * d, (h + 1) * d)
        bc = bc_all[:, sl]
        qf = q_ref[:, sl] * (d ** -0.5)
        kf = jnp.where(t * L + rowi < t_valid, k_ref[:, sl], 0.0)
        vf = v_ref[:, sl]
        qparts, kparts = [], []
        for j in range(L // GLA_SUB):
            lo, hi = j * GLA_SUB, (j + 1) * GLA_SUB
            e = bc[hi - 1:hi, :]
            qparts.append((qf * jnp.exp(jnp.where(rowi >= lo, bc - e, NEG))).astype(BF16))
            kparts.append((kf * jnp.exp(jnp.where((rowi >= lo) & (rowi < hi), e - bc, NEG))).astype(BF16))
        a = lax.dot_general(jnp.concatenate(qparts, axis=1), jnp.concatenate(kparts, axis=1), _NT,
                            preferred_element_type=F32)
        a = jnp.where(si <= li, a, 0.0)
        st = st_sc[h]
        o = (jnp.dot(a.astype(BF16), vf.astype(BF16), preferred_element_type=F32)
             + lax.dot_general((qf * jnp.exp(bc)).astype(BF16), st.astype(BF16), _NT, preferred_element_type=F32))
        bl = bc[L - 1:L, :]
        kd = (kf * jnp.exp(bl - bc)).astype(BF16)
        st_sc[h] = st * jnp.exp(bl) + jnp.dot(jnp.transpose(vf).astype(BF16), kd, preferred_element_type=F32)
        gate = r_ref[:, sl]
        h_ref[:, sl] = (_rms(o, norm_ref[:, sl]) * (gate * jax.nn.sigmoid(gate))).astype(BF16)

    @pl.when(t == pl.num_programs(1) - 1)
    def _():
        for h in range(H_G):
            s_ref[0, h] = jnp.transpose(st_sc[h])


def gla_pallas(z, b, t_pad, t_valid, w2_pad, gate_b, norm, s0):
    L = MIX_L
    nt = t_pad // L
    rowblk = lambda col: (lambda bi, t: (bi * nt + t, col))
    const2 = lambda bi, t: (0, 0)
    per_b4 = lambda bi, t: (bi, 0, 0, 0)
    gcol = COL_GLA // G_WIDTH
    return pl.pallas_call(
        functools.partial(_gla_kernel, t_valid=t_valid),
        grid=(b, nt),
        in_specs=[pl.BlockSpec((L, G_WIDTH), rowblk(gcol)),
                  pl.BlockSpec((L, G_WIDTH), rowblk(gcol + 1)),
                  pl.BlockSpec((L, G_WIDTH), rowblk(gcol + 2)),
                  pl.BlockSpec((L, G_WIDTH), rowblk(gcol + 3)),
                  pl.BlockSpec((L, HEAD_DIM), rowblk(COL_SMALL // HEAD_DIM)),
                  pl.BlockSpec((HEAD_DIM, G_WIDTH), const2),
                  pl.BlockSpec((1, G_WIDTH), const2),
                  pl.BlockSpec((1, G_WIDTH), const2),
                  pl.BlockSpec((1, H_G, HEAD_DIM, HEAD_DIM), per_b4)],
        out_specs=[pl.BlockSpec((L, G_WIDTH), rowblk(0)),
                   pl.BlockSpec((1, H_G, HEAD_DIM, HEAD_DIM), per_b4)],
        out_shape=[jax.ShapeDtypeStruct((b * t_pad, G_WIDTH), BF16),
                   jax.ShapeDtypeStruct((b, H_G, HEAD_DIM, HEAD_DIM), F32)],
        scratch_shapes=[pltpu.VMEM((H_G, HEAD_DIM, HEAD_DIM), F32)],
        compiler_params=pltpu.CompilerParams(
            dimension_semantics=("arbitrary", "arbitrary"), vmem_limit_bytes=VMEM_LIMIT_BYTES),
        name="gla",
    )(z, z, z, z, z, w2_pad, gate_b, norm, s0)


def mixers_pallas(z, b, t, conv_buf, c0, n0, m0, s0, conv_w, m_gate_b, m_norm, g_w2, g_b, g_norm):
    t_pad = z.shape[0] // b
    lanes = lambda a: jnp.pad(a.reshape(1, -1), ((0, 0), (0, HEAD_DIM - a.size)))
    conv0 = jnp.pad(conv_buf, ((0, 0), (8 - (CONV_W - 1), 0), (0, 0)))
    n0p = jnp.pad(n0, ((0, 0), (0, 8 - H_M), (0, 0)))
    m0p = jnp.broadcast_to(jnp.pad(m0, ((0, 0), (0, 8 - H_M)))[:, :, None], (b, 8, HEAD_DIM))
    hm, c, n, m = mlstm_pallas(z, b, t_pad, t, conv_w, lanes(m_gate_b), m_norm.reshape(1, -1), conv0, c0, n0p, m0p)
    w2_pad = jnp.pad(g_w2, ((SMALL_GLR, HEAD_DIM - SMALL_GLR - GATE_RANK), (0, 0))).astype(BF16)
    hg, s = gla_pallas(z, b, t_pad, t, w2_pad, g_b.reshape(1, -1), g_norm.reshape(1, -1), s0)
    valid = lambda a: a.reshape(b, t_pad, -1)[:, :t]
    return valid(hm), valid(hg), c, n[:, :H_M], m[:, :H_M, 0], s


def rmsnorm(x, g):
    xf = x.astype(F32)
    y = xf * lax.rsqrt(jnp.mean(xf * xf, axis=-1, keepdims=True) + EPS)
    return (y * g.astype(F32)).astype(x.dtype)


def rope(x, pos):
    half = x.shape[-1] // 2
    inv_freq = jnp.exp(-math.log(ROPE_THETA) * jnp.arange(half, dtype=F32) / half)
    ang = pos.astype(F32)[:, None] * inv_freq[None, :]
    cos = jnp.cos(ang)[:, None, :]
    sin = jnp.sin(ang)[:, None, :]
    xf = x.astype(F32)
    x1, x2 = xf[..., :half], xf[..., half:]
    return jnp.concatenate([x1 * cos - x2 * sin, x2 * cos + x1 * sin], axis=-1).astype(x.dtype)


def masked_softmax(s, mask):
    s = jnp.where(mask, s, -jnp.inf)
    m = jnp.max(s, axis=-1, keepdims=True)
    m = jnp.where(jnp.isfinite(m), m, 0.0)
    p = jnp.exp(s - m)
    return p / jnp.maximum(jnp.sum(p, axis=-1, keepdims=True), TINY)


def chunked_scan(step, carry, xs, chunk):
    b, t = xs[0].shape[:2]
    c = chunk if t % chunk == 0 else t
    n = t // c
    split = lambda a: jnp.moveaxis(a.reshape((b, n, c) + a.shape[2:]), 1, 0)
    carry, ys = lax.scan(step, carry, tuple(split(a) for a in xs))
    return carry, jnp.moveaxis(ys, 0, 1).reshape((b, t) + ys.shape[3:])


def mlstm_chunk(carry, inp):
    c0, n0, m0 = carry
    q, k, v, li, lf = inp
    L = q.shape[1]
    b = jnp.cumsum(lf, axis=1).transpose(0, 2, 1)
    ig = li.transpose(0, 2, 1)
    causal = jnp.tril(jnp.ones((L, L), dtype=bool))
    logw = jnp.where(causal, b[..., :, None] - b[..., None, :] + ig[..., None, :], -jnp.inf)
    g = b + m0[..., None]
    m_row = jnp.maximum(jnp.max(logw, axis=-1), g)
    w = jnp.exp(logw - m_row[..., None]) * jnp.einsum('blhd,bshd->bhls', q, k)
    inter = jnp.exp(g - m_row)
    num = jnp.einsum('bhls,bshd->bhld', w, v) + inter[..., None] * jnp.einsum('bhed,blhd->bhle', c0, q)
    den = jnp.sum(w, axis=-1) + inter * jnp.einsum('bhd,blhd->bhl', n0, q)
    h = num / jnp.maximum(jnp.abs(den), jnp.exp(-m_row))[..., None]
    bl = b[..., -1]
    a = bl[..., None] - b + ig
    m_new = jnp.maximum(jnp.max(a, axis=-1), bl + m0)
    wa = jnp.exp(a - m_new[..., None])
    wc = jnp.exp(bl + m0 - m_new)
    c_new = wc[..., None, None] * c0 + jnp.einsum('bhs,bshe,bshd->bhed', wa, v, k)
    n_new = wc[..., None] * n0 + jnp.einsum('bhs,bshd->bhd', wa, k)
    return (c_new, n_new, m_new), h.transpose(0, 2, 1, 3)


def gla_chunk(s0, inp):
    q, k, v, la = inp
    L = q.shape[1]
    bc = jnp.cumsum(la, axis=1)
    causal = jnp.tril(jnp.ones((L, L), dtype=bool))[None, :, :, None, None]
    decay = jnp.exp(jnp.where(causal, bc[:, :, None] - bc[:, None, :], -jnp.inf))
    a = jnp.einsum('bthk,bshk,btshk->bhts', q, k, decay)
    o = jnp.einsum('bhts,bshv->bthv', a, v) + jnp.einsum('bthk,bhkv->bthv', q * jnp.exp(bc), s0)
    bl = bc[:, -1]
    s_new = jnp.exp(bl)[..., None] * s0 + jnp.einsum('bshk,bshv->bhkv', k * jnp.exp(bl[:, None] - bc), v)
    return s_new, o


def nsa_compress(x, pe, w1, w2):
    b, l = x.shape[:2]
    nb = l // CMP_STRIDE
    xb = x[:, :nb * CMP_STRIDE].reshape(b, nb, CMP_STRIDE, KV_GROUPS, HEAD_DIM)
    first = jnp.einsum('bnjgd,jde->bnge', xb, w1[:CMP_STRIDE])
    second = jnp.einsum('bnjgd,jde->bnge', xb, w1[CMP_STRIDE:])
    hid = first[:, :-1] + second[:, 1:] + jnp.einsum('jd,jde->e', pe, w1)
    return jnp.einsum('bnge,ed->bngd', jax.nn.gelu(hid), w2)


def nsa_context(rows, cmp_params):
    pe, w1, w2 = cmp_params
    b, l = rows.shape[:2]
    kc = nsa_compress(rows[:, :, 0], pe[0], w1[0], w2[0])
    vc = nsa_compress(rows[:, :, 1], pe[1], w1[1], w2[1])
    nc = kc.shape[1]
    ns = -(-l // SEL_BLK)
    sel = jnp.pad(rows[:, :, 2:4], ((0, 0), (0, ns * SEL_BLK - l), (0, 0), (0, 0), (0, 0)))
    sel = sel.reshape(b, ns, SEL_BLK, 2, KV_GROUPS, HEAD_DIM).transpose(3, 0, 4, 1, 2, 5)
    cmp_start = jnp.arange(nc) * CMP_STRIDE
    cmp_end = cmp_start + CMP_LEN - 1
    sel_start = jnp.arange(ns) * SEL_BLK
    cmp_to_sel = ((cmp_start[:, None] < sel_start[None, :] + SEL_BLK)
                  & (cmp_end[:, None] >= sel_start[None, :])).astype(F32)
    return kc, vc, cmp_end, cmp_to_sel, sel[0], sel[1]


def nsa_query_block(q, qpos, gates, kc, vc, cmp_end, cmp_to_sel, ksb, vsb, kw, vw, kwpos):
    b, t = q.shape[:2]
    qg = q.reshape(b, t, KV_GROUPS, Q_PER_KV, HEAD_DIM) * (HEAD_DIM ** -0.5)
    s = jnp.einsum('btgrd,bngd->btgrn', qg, kc).astype(F32)
    p_c = masked_softmax(s, (cmp_end[None, :] <= qpos[:, None])[None, :, None, None, :])
    o_c = jnp.einsum('btgrn,bngd->btgrd', p_c.astype(vc.dtype), vc)
    ns = ksb.shape[2]
    imp = jnp.einsum('btgn,nj->btgj', jnp.sum(p_c, axis=3), cmp_to_sel)
    j = jnp.arange(ns)
    cur = qpos // SEL_BLK
    elig = (j * SEL_BLK)[None, :] <= qpos[:, None]
    forced = (j[None, :] == 0) | (j[None, :] == cur[:, None]) | (j[None, :] == cur[:, None] - 1)
    score = jnp.where(elig[None, :, None, :], imp + jnp.where(forced, FORCE_BONUS, 0.0)[None, :, None, :], NEG_BIG)
    _, idx = lax.top_k(score, min(N_SELECT, ns))
    n = idx.shape[-1]
    idx_g = idx.transpose(0, 2, 1, 3).reshape(b, KV_GROUPS, t * n)
    take = jax.vmap(jax.vmap(lambda blocks, ids: blocks[ids]))
    gk = take(ksb, idx_g).reshape(b, KV_GROUPS, t, n * SEL_BLK, HEAD_DIM)
    gv = take(vsb, idx_g).reshape(b, KV_GROUPS, t, n * SEL_BLK, HEAD_DIM)
    kpos = (idx[..., None] * SEL_BLK + jnp.arange(SEL_BLK)).reshape(b, t, KV_GROUPS, n * SEL_BLK)
    s = jnp.einsum('btgrd,bgtmd->btgrm', qg, gk).astype(F32)
    p_s = masked_softmax(s, (kpos <= qpos[None, :, None, None])[:, :, :, None, :])
    o_s = jnp.einsum('btgrm,bgtmd->btgrd', p_s.astype(gv.dtype), gv)
    s = jnp.einsum('btgrd,bkgd->btgrk', qg, kw).astype(F32)
    d = qpos[:, None] - kwpos[None, :]
    wmask = (kwpos[None, :] >= 0) & (d >= 0) & (d < WINDOW)
    p_w = masked_softmax(s, wmask[None, :, None, None, :])
    o_w = jnp.einsum('btgrk,bkgd->btgrd', p_w.astype(vw.dtype), vw)
    g = jax.nn.sigmoid(gates.astype(F32)).reshape(b, t, KV_GROUPS, Q_PER_KV, 3)
    o = g[..., 0:1] * o_c + g[..., 1:2] * o_s + g[..., 2:3] * o_w
    return o.reshape(b, t, H_N * HEAD_DIM).astype(q.dtype)


def nsa_prep(n_q, n_kv, n_g, pos):
    b, t = n_q.shape[:2]
    q = rope(n_q.reshape(b, t, H_N, HEAD_DIM), pos)
    kv = n_kv.reshape(b, t, 6, KV_GROUPS, HEAD_DIM)
    rows = jnp.stack([rope(kv[:, :, 0], pos), kv[:, :, 1], rope(kv[:, :, 2], pos), kv[:, :, 3]], axis=2)
    win = jnp.stack([rope(kv[:, :, 4], pos), kv[:, :, 5]], axis=2)
    return q, n_g.reshape(b, t, H_N, 3), rows, win


def nsa_prompt(n_q, n_kv, n_g, cmp_params):
    b, s = n_q.shape[:2]
    pos = jnp.arange(s)
    q, gates, rows, win = nsa_prep(n_q, n_kv, n_g, pos)
    kc, vc, cmp_end, cmp_to_sel, ksb, vsb = nsa_context(rows, cmp_params)
    nb = s // QBLK
    kwp = jnp.pad(win, ((0, 0), (WINDOW, 0), (0, 0), (0, 0), (0, 0)))

    def one_block(args):
        qi, gi, bi = args
        start = bi * QBLK
        qpos = start + jnp.arange(QBLK)
        band = lax.dynamic_slice_in_dim(kwp, start, WINDOW + QBLK, axis=1)
        kwpos = start - WINDOW + jnp.arange(WINDOW + QBLK)
        return nsa_query_block(qi, qpos, gi, kc, vc, cmp_end, cmp_to_sel, ksb, vsb,
                               band[:, :, 0], band[:, :, 1], kwpos)

    blocks = lambda a: jnp.moveaxis(a.reshape((b, nb, QBLK) + a.shape[2:]), 1, 0)
    out = lax.map(one_block, (blocks(q), blocks(gates), jnp.arange(nb)))
    out = jnp.moveaxis(out, 0, 1).reshape(b, s, H_N * HEAD_DIM)
    return out, (rows, win[:, -min(WINDOW, s):])


def nsa_sample(n_q, n_kv, n_g, past_rows, win_buf, cmp_params):
    b, t = n_q.shape[:2]
    past_len = past_rows.shape[1]
    pos = past_len + jnp.arange(t)
    q, gates, rows, win = nsa_prep(n_q, n_kv, n_g, pos)
    full = jnp.concatenate([past_rows.astype(rows.dtype), rows], axis=1)
    kc, vc, cmp_end, cmp_to_sel, ksb, vsb = nsa_context(full, cmp_params)
    wl = win_buf.shape[1]
    kw_all = jnp.concatenate([win_buf.astype(win.dtype), win], axis=1)
    kwpos = past_len - wl + jnp.arange(wl + t)
    out = nsa_query_block(q, pos, gates, kc, vc, cmp_end, cmp_to_sel, ksb, vsb,
                          kw_all[:, :, 0], kw_all[:, :, 1], kwpos)
    return out, (rows, kw_all[:, -wl:])


def to_heads(a, n_heads):
    return a.reshape(a.shape[0], a.shape[1], n_heads, -1).astype(F32)


def split_cols(z):
    cut = lambda a, n: z[..., a:a + n]
    u_qk, m_v, m_o = cut(0, 2 * M_WIDTH), cut(2 * M_WIDTH, M_WIDTH), cut(3 * M_WIDTH, M_WIDTH)
    n_q, n_kv = cut(COL_NQ, N_WIDTH), cut(COL_NKV, 6 * KV_GROUPS * HEAD_DIM)
    g_q, g_k, g_v, g_r = (cut(COL_GLA + i * G_WIDTH, G_WIDTH) for i in range(4))
    m_if = cut(COL_SMALL + SMALL_MIF, 2 * H_M)
    n_g = cut(COL_SMALL + SMALL_NG, 3 * H_N)
    g_lr = cut(COL_SMALL + SMALL_GLR, GATE_RANK)
    return u_qk, m_v, m_o, m_if, n_q, n_kv, n_g, g_q, g_k, g_v, g_r, g_lr


def regroup_w_in(w_in):
    o = (0,) + SPLIT_OFFSETS + (IN_COLS,)
    piece = lambda i: w_in[..., o[i]:o[i + 1]]
    order = [0, 1, 2, 4, 5, 7, 8, 9, 10, 3, 6, 11]
    pad = jnp.zeros(w_in.shape[:-1] + (IN_COLS_PAD - IN_COLS,), w_in.dtype)
    return jnp.concatenate([piece(i) for i in order] + [pad], axis=-1)


def trunk_layer(x, conv_buf, c0, n0, m0, s0, nsa_fn, g_norms, w_in, conv_w, m_gate_b, m_norm,
                g_w2, g_b, g_norm, w_out, w_ff1, w_ff2):
    b, t, d = x.shape
    x2 = x.reshape(b * t, d)
    z = norm_matmul(x2, g_norms[0], w_in, IN_TN)
    z3 = z.reshape(b, t, -1)
    t_pad = -(-t // MIX_L) * MIX_L
    zp = z if t_pad == t else jnp.pad(z3, ((0, 0), (0, t_pad - t), (0, 0))).reshape(b * t_pad, -1)
    hm, hg, c, n, m, s = mixers_pallas(zp, b, t, conv_buf, c0, n0, m0, s0,
                                       conv_w, m_gate_b, m_norm, g_w2, g_b, g_norm)
    new_conv = z3[:, t - (CONV_W - 1):, :2 * M_WIDTH]
    hn, nsa_state = nsa_fn(zp, b, t, t_pad)
    x2 = matmul_norm_res(hm.reshape(b * t, -1), hn.reshape(b * t, -1), hg.reshape(b * t, -1),
                         w_out, g_norms[1], x2)
    x2 = ffn(x2, g_norms[2], w_ff1, w_ff2, g_norms[3])
    return x2.reshape(b, t, d), (nsa_state[0], nsa_state[1], c, n, m, new_conv, s)


def kernel(x_prompt, x_sample, cache_nsa_kv, state_nsa_win, state_mlstm_C, state_mlstm_n, state_mlstm_m, state_mlstm_conv, state_gla_S, page_table, norms, w_in, mlstm_conv_w, mlstm_gate_b, mlstm_norm, nsa_cmp_pe, nsa_cmp_w1, nsa_cmp_w2, gla_gate_w2, gla_gate_b, gla_norm, w_out, w_ff1, w_ff2):
    xp, xs = x_prompt, x_sample
    bp = xp.shape[0]
    db = xs.shape[0]
    past_len = page_table.shape[1] * PAGE_SIZE
    conv0 = jnp.zeros((bp, CONV_W - 1, 2 * M_WIDTH), xp.dtype)
    c0 = jnp.zeros((bp, H_M, HEAD_DIM, HEAD_DIM), F32)
    n0 = jnp.zeros((bp, H_M, HEAD_DIM), F32)
    m0 = jnp.zeros((bp, H_M), F32)
    s0 = jnp.zeros((bp, H_G, HEAD_DIM, HEAD_DIM), F32)
    w_in_b = regroup_w_in(w_in).astype(BF16)
    w_out_b = w_out.astype(BF16)
    w_ff1_b = w_ff1.astype(BF16)
    w_ff2_b = w_ff2.astype(BF16)
    cmp_pe = nsa_cmp_pe.reshape(DEPTH, 2, 1, CMP_LEN * HEAD_DIM)
    cmp_w1_b = nsa_cmp_w1.reshape(DEPTH, 2, CMP_LEN * HEAD_DIM, CMP_HID).astype(BF16)
    cmp_w2_b = nsa_cmp_w2.astype(BF16)
    n_pool = cache_nsa_kv.shape[1]
    cache2d = cache_nsa_kv.reshape(-1, HEAD_DIM)
    acc_p = [[] for _ in range(7)]
    acc_s = [[] for _ in range(7)]
    for l in range(DEPTH):
        weights = (norms[l], w_in_b[l], mlstm_conv_w[l], mlstm_gate_b[l], mlstm_norm[l],
                   gla_gate_w2[l], gla_gate_b[l], gla_norm[l], w_out_b[l], w_ff1_b[l], w_ff2_b[l])
        cmp_params = (nsa_cmp_pe[l], nsa_cmp_w1[l], nsa_cmp_w2[l])
        cmp_w = (cmp_pe[l], cmp_w1_b[l], cmp_w2_b[l])
        nsa_p = lambda zp, b, t, t_pad, cmp_w=cmp_w: nsa_prompt_pallas(zp, b, t, cmp_w)
        nsa_s = lambda zp, b, t, t_pad, l=l, cmp_w=cmp_w: nsa_sample_pallas(
            zp, b, t, t_pad, l, cache2d, page_table + l * n_pool, state_nsa_win, cmp_w)
        xp, st_p = trunk_layer(xp, conv0, c0, n0, m0, s0, nsa_p, *weights)
        xs, st_s = trunk_layer(xs, state_mlstm_conv[l], state_mlstm_C[l], state_mlstm_n[l],
                               state_mlstm_m[l], state_gla_S[l], nsa_s, *weights)
        for acc, a in zip(acc_p, st_p):
            acc.append(a.astype(xp.dtype))
        for acc, a in zip(acc_s, st_s):
            acc.append(a.astype(xs.dtype))
    nsa_rows_p, nsa_win_p, mlstm_c_p, mlstm_n_p, mlstm_m_p, mlstm_conv_p, gla_s_p = [jnp.stack(a) for a in acc_p]
    nsa_rows_s, nsa_win_s, mlstm_c_s, mlstm_n_s, mlstm_m_s, mlstm_conv_s, gla_s_s = [jnp.stack(a) for a in acc_s]
    return (xp, xs, nsa_rows_p, nsa_rows_s, nsa_win_p, nsa_win_s, mlstm_c_p, mlstm_c_s,
            mlstm_n_p, mlstm_n_s, mlstm_m_p, mlstm_m_s, mlstm_conv_p, mlstm_conv_s, gla_s_p, gla_s_s)
```

```python
import functools
import math

import jax
import jax.numpy as jnp
import numpy as np
from jax import lax
from jax.experimental import pallas as pl
from jax.experimental.pallas import tpu as pltpu

D_MODEL = 2048
DEPTH = 4
PAGE_SIZE = 128
HEAD_DIM = 128
D_MIX = D_MODEL
M_WIDTH = D_MIX // 4
G_WIDTH = D_MIX // 4
N_WIDTH = D_MIX - M_WIDTH - G_WIDTH
H_M = M_WIDTH // HEAD_DIM
H_N = N_WIDTH // HEAD_DIM
H_G = G_WIDTH // HEAD_DIM
KV_GROUPS = 2
Q_PER_KV = H_N // KV_GROUPS
D_FF = 4 * D_MODEL
CONV_W = 4
CHUNK = 64
CMP_STRIDE = 16
CMP_LEN = 2 * CMP_STRIDE
CMP_HID = 256
SEL_BLK = 64
N_SELECT = 16
WINDOW = 512
QBLK = 128
GATE_RANK = 16
GLA_GATE_TEMP = 16.0
ROPE_THETA = 10000.0
EPS = 1e-6
TINY = 1e-30
FORCE_BONUS = 1e3
NEG_BIG = -1e9
SPLIT_SIZES = (2 * M_WIDTH, M_WIDTH, M_WIDTH, 2 * H_M,
               N_WIDTH, 6 * KV_GROUPS * HEAD_DIM, 3 * H_N,
               G_WIDTH, G_WIDTH, G_WIDTH, G_WIDTH, GATE_RANK)
IN_COLS = sum(SPLIT_SIZES)
SPLIT_OFFSETS = tuple(int(o) for o in np.cumsum(SPLIT_SIZES)[:-1])

F32 = jnp.float32
BF16 = jnp.bfloat16

VMEM_LIMIT_BYTES = 56 * 1024 * 1024
IN_TN = 512
COL_NQ = 2 * M_WIDTH + 2 * M_WIDTH
COL_NKV = COL_NQ + N_WIDTH
COL_GLA = COL_NKV + 6 * KV_GROUPS * HEAD_DIM
COL_SMALL = COL_GLA + 4 * G_WIDTH
SMALL_MIF = 0
SMALL_NG = 2 * H_M
SMALL_GLR = SMALL_NG + 3 * H_N
IN_COLS_PAD = 7168
NEG = -1e30
NSA_TQ = 128
NSA_TK = 512


def _rms(x, g):
    return x * lax.rsqrt(jnp.mean(x * x, axis=-1, keepdims=True) + EPS) * g


def _norm_matmul_kernel(x_ref, g_ref, w_ref, o_ref, xn_ref):
    @pl.when(pl.program_id(1) == 0)
    def _():
        xn_ref[...] = _rms(x_ref[...], g_ref[...]).astype(BF16)

    o_ref[...] = jnp.dot(xn_ref[...], w_ref[...], preferred_element_type=F32)


def norm_matmul(x, g, w, layer, tn):
    m, k = x.shape
    n = w.shape[2]
    tm = min(m, 1024)
    return pl.pallas_call(
        _norm_matmul_kernel,
        grid=(m // tm, n // tn),
        in_specs=[pl.BlockSpec((tm, k), lambda i, j: (i, 0)),
                  pl.BlockSpec((1, k), lambda i, j: (0, 0)),
                  pl.BlockSpec((None, k, tn), lambda i, j: (layer, 0, j))],
        out_specs=pl.BlockSpec((tm, tn), lambda i, j: (i, j)),
        out_shape=jax.ShapeDtypeStruct((m, n), F32),
        scratch_shapes=[pltpu.VMEM((tm, k), BF16)],
        compiler_params=pltpu.CompilerParams(
            dimension_semantics=("arbitrary", "arbitrary"), vmem_limit_bytes=VMEM_LIMIT_BYTES),
        name="norm_matmul",
    )(x, g.reshape(1, k), w)


def _matmul_norm_res_kernel(a0_ref, a1_ref, a2_ref, w_ref, g_ref, r_ref, o_ref):
    k0, k1 = a0_ref.shape[1], a0_ref.shape[1] + a1_ref.shape[1]
    y = (jnp.dot(a0_ref[...], w_ref[:k0, :], preferred_element_type=F32)
         + jnp.dot(a1_ref[...], w_ref[k0:k1, :], preferred_element_type=F32)
         + jnp.dot(a2_ref[...], w_ref[k1:, :], preferred_element_type=F32))
    o_ref[...] = r_ref[...] + _rms(y, g_ref[...])


def matmul_norm_res(a0, a1, a2, w, layer, g, r):
    m = a0.shape[0]
    _, k, n = w.shape
    tm = min(m, 512)
    rows = lambda a: pl.BlockSpec((tm, a.shape[1]), lambda i: (i, 0))
    return pl.pallas_call(
        _matmul_norm_res_kernel,
        grid=(m // tm,),
        in_specs=[rows(a0), rows(a1), rows(a2),
                  pl.BlockSpec((None, k, n), lambda i: (layer, 0, 0)),
                  pl.BlockSpec((1, n), lambda i: (0, 0)),
                  pl.BlockSpec((tm, n), lambda i: (i, 0))],
        out_specs=pl.BlockSpec((tm, n), lambda i: (i, 0)),
        out_shape=jax.ShapeDtypeStruct((m, n), F32),
        compiler_params=pltpu.CompilerParams(
            dimension_semantics=("arbitrary",), vmem_limit_bytes=VMEM_LIMIT_BYTES),
        name="matmul_norm_res",
    )(a0, a1, a2, w, g.reshape(1, n), r)


def _ffn_kernel(x_ref, g2_ref, w1_ref, w2_ref, g3_ref, o_ref, xn_ref, acc_ref):
    f = pl.program_id(1)

    @pl.when(f == 0)
    def _():
        xn_ref[...] = _rms(x_ref[...], g2_ref[...]).astype(BF16)
        acc_ref[...] = jnp.zeros_like(acc_ref)

    h = jnp.dot(xn_ref[...], w1_ref[...], preferred_element_type=F32)
    a = jnp.square(jnp.maximum(h, 0.0)).astype(BF16)
    acc_ref[...] += jnp.dot(a, w2_ref[...], preferred_element_type=F32)

    @pl.when(f == pl.num_programs(1) - 1)
    def _():
        o_ref[...] = x_ref[...] + _rms(acc_ref[...], g3_ref[...])


def ffn(x, g2, w1, w2, layer, g3):
    m, d = x.shape
    dff = w1.shape[2]
    tm = min(m, 512)
    tf = 1024
    return pl.pallas_call(
        _ffn_kernel,
        grid=(m // tm, dff // tf),
        in_specs=[pl.BlockSpec((tm, d), lambda i, f: (i, 0)),
                  pl.BlockSpec((1, d), lambda i, f: (0, 0)),
                  pl.BlockSpec((None, d, tf), lambda i, f: (layer, 0, f)),
                  pl.BlockSpec((None, tf, d), lambda i, f: (layer, f, 0)),
                  pl.BlockSpec((1, d), lambda i, f: (0, 0))],
        out_specs=pl.BlockSpec((tm, d), lambda i, f: (i, 0)),
        out_shape=jax.ShapeDtypeStruct((m, d), F32),
        scratch_shapes=[pltpu.VMEM((tm, d), BF16), pltpu.VMEM((tm, d), F32)],
        compiler_params=pltpu.CompilerParams(
            dimension_semantics=("arbitrary", "arbitrary"), vmem_limit_bytes=VMEM_LIMIT_BYTES),
        name="ffn",
    )(x, g2.reshape(1, d), w1, w2, g3.reshape(1, d))


def rope_tables(pos):
    half = HEAD_DIM // 2
    inv_freq = jnp.exp(-math.log(ROPE_THETA) * jnp.arange(half, dtype=F32) / half)
    ang = pos.astype(F32)[:, None] * inv_freq[None, :]
    cos, sin = jnp.cos(ang), jnp.sin(ang)
    return jnp.concatenate([cos, cos], axis=-1), jnp.concatenate([-sin, sin], axis=-1)


def _rope(x, cos2, sin2):
    return x * cos2 + pltpu.roll(x, HEAD_DIM // 2, 1) * sin2


def _nsa_prep_kernel(nq_ref, nkv_ref, cos_ref, sin_ref, q_ref, rows_ref, win_ref, kvb_ref, cmp_ref):
    cos2 = cos_ref[...]
    sin2 = sin_ref[...]
    for h in range(H_N):
        sl = slice(h * HEAD_DIM, (h + 1) * HEAD_DIM)
        q_ref[:, sl] = (_rope(nq_ref[:, sl], cos2, sin2) * (HEAD_DIM ** -0.5)).astype(BF16)
    for c in range(6 * KV_GROUPS):
        slot = c // KV_GROUPS
        x = nkv_ref[:, c * HEAD_DIM:(c + 1) * HEAD_DIM]
        if slot % 2 == 0:
            x = _rope(x, cos2, sin2)
        xb = x.astype(BF16)
        tm = x.shape[0]
        if slot < 4:
            rows_ref[pl.ds(c, tm, stride=ROW_VECS), :] = x
        else:
            win_ref[pl.ds(c - ROW_VECS, tm, stride=WIN_VECS), :] = x
        if slot < 2:
            cmp_ref[0, c] = xb
        else:
            kvb_ref[:, (c - 4) * HEAD_DIM:(c - 3) * HEAD_DIM] = xb


def nsa_prep_pallas(z, b, s, cos2, sin2, tm=512):
    m = b * s
    nsb = s // tm
    return pl.pallas_call(
        _nsa_prep_kernel,
        grid=(m // tm,),
        in_specs=[pl.BlockSpec((tm, N_WIDTH), lambda i: (i, COL_NQ // N_WIDTH)),
                  pl.BlockSpec((tm, 1536), lambda i: (i, COL_NKV // 1536)),
                  pl.BlockSpec((tm, HEAD_DIM), lambda i: (i % nsb, 0)),
                  pl.BlockSpec((tm, HEAD_DIM), lambda i: (i % nsb, 0))],
        out_specs=[pl.BlockSpec((tm, N_WIDTH), lambda i: (i, 0)),
                   pl.BlockSpec((tm * 8, HEAD_DIM), lambda i: (i, 0)),
                   pl.BlockSpec((tm * 4, HEAD_DIM), lambda i: (i, 0)),
                   pl.BlockSpec((tm, 1024), lambda i: (i, 0)),
                   pl.BlockSpec((1, 4, tm, HEAD_DIM), lambda i: (i // nsb, 0, i % nsb, 0))],
        out_shape=[jax.ShapeDtypeStruct((m, N_WIDTH), BF16),
                   jax.ShapeDtypeStruct((m * 8, HEAD_DIM), F32),
                   jax.ShapeDtypeStruct((m * 4, HEAD_DIM), F32),
                   jax.ShapeDtypeStruct((m, 1024), BF16),
                   jax.ShapeDtypeStruct((b, 4, s, HEAD_DIM), BF16)],
        compiler_params=pltpu.CompilerParams(
            dimension_semantics=("arbitrary",), vmem_limit_bytes=VMEM_LIMIT_BYTES),
        name="nsa_prep",
    )(z, z, cos2, sin2)


def _nsa_cmp_kernel(x_ref, pe_ref, w1_ref, w2_ref, o_ref):
    x = x_ref[0, 0]
    nb = x.shape[0]
    half = CMP_STRIDE * HEAD_DIM
    first = jnp.dot(x, w1_ref[0, :half, :], preferred_element_type=F32)
    second = jnp.dot(x, w1_ref[0, half:, :], preferred_element_type=F32)
    pe = jnp.broadcast_to(pe_ref[0], (8, 2 * half)).astype(BF16)
    bias = jnp.dot(pe, w1_ref[0], preferred_element_type=F32)[0:1]
    hid = first + pltpu.roll(second, nb - 1, 0) + bias
    o_ref[0, 0] = jnp.dot(jax.nn.gelu(hid).astype(BF16), w2_ref[0], preferred_element_type=F32).astype(BF16)


def nsa_compress_prompt(cmp_in, pe, w1, w2):
    b, _, nb, kdim = cmp_in.shape
    return pl.pallas_call(
        _nsa_cmp_kernel,
        grid=(b, 4),
        in_specs=[pl.BlockSpec((1, 1, nb, kdim), lambda i, c: (i, c, 0, 0)),
                  pl.BlockSpec((1, 1, 2 * kdim), lambda i, c: (c // 2, 0, 0)),
                  pl.BlockSpec((1, 2 * kdim, CMP_HID), lambda i, c: (c // 2, 0, 0)),
                  pl.BlockSpec((1, CMP_HID, HEAD_DIM), lambda i, c: (c // 2, 0, 0))],
        out_specs=pl.BlockSpec((1, 1, nb, HEAD_DIM), lambda i, c: (i, c, 0, 0)),
        out_shape=jax.ShapeDtypeStruct((b, 4, nb, HEAD_DIM), BF16),
        compiler_params=pltpu.CompilerParams(
            dimension_semantics=("arbitrary", "arbitrary"), vmem_limit_bytes=VMEM_LIMIT_BYTES),
        name="nsa_compress",
    )(cmp_in, pe, w1, w2)


def _softmax_rows(s, mask):
    sm = jnp.where(mask, s, NEG)
    m = jnp.max(sm, axis=-1, keepdims=True)
    p = jnp.where(mask, jnp.exp(sm - m), 0.0)
    return p / jnp.maximum(jnp.sum(p, axis=-1, keepdims=True), TINY)


_NT = (((1,), (1,)), ((), ()))


def _nsa_attn_kernel(q_ref, kc_ref, vc_ref, ks_ref, vs_ref, kw_ref, vw_ref, gate_ref, o_ref):
    tq = NSA_TQ
    r4 = Q_PER_KV
    g = pl.program_id(1)
    i = pl.program_id(2)
    q4 = q_ref[...]
    q = jnp.concatenate([q4[:, r * HEAD_DIM:(r + 1) * HEAD_DIM] for r in range(r4)], axis=0)
    q0 = i * tq
    tpos = q0 + lax.broadcasted_iota(jnp.int32, (tq, 1), 0)

    ncp = kc_ref.shape[2]
    s = lax.dot_general(q, kc_ref[0, 0], _NT, preferred_element_type=F32).reshape(r4, tq, ncp)
    cend = lax.broadcasted_iota(jnp.int32, (tq, ncp), 1) * CMP_STRIDE + (CMP_LEN - 1)
    p_c = _softmax_rows(s, (cend <= tpos)[None])
    o_c = jnp.dot(p_c.reshape(r4 * tq, ncp).astype(BF16), vc_ref[0, 0], preferred_element_type=F32)

    psum = p_c[0] + p_c[1] + p_c[2] + p_c[3]
    n_i = lax.broadcasted_iota(jnp.int32, (ncp, HEAD_DIM), 0)
    j_i = lax.broadcasted_iota(jnp.int32, (ncp, HEAD_DIM), 1)
    ratio = SEL_BLK // CMP_STRIDE
    c2s = jnp.where((n_i >= ratio * j_i - 1) & (n_i <= ratio * j_i + ratio - 1) & (n_i < ncp - 1)
                    & (j_i < ncp // ratio), 1.0, 0.0).astype(BF16)
    p_hi = psum.astype(BF16)
    p_lo = (psum - p_hi.astype(F32)).astype(BF16)
    imp = (jnp.dot(p_hi, c2s, preferred_element_type=F32)
           + jnp.dot(p_lo, c2s, preferred_element_type=F32))

    ns = ncp // ratio
    imp_t = jnp.transpose(imp)[:ns]
    jj = lax.broadcasted_iota(jnp.int32, (ns, tq), 0)
    tt = q0 + lax.broadcasted_iota(jnp.int32, (ns, tq), 1)
    cur = tt // SEL_BLK
    forced = (jj == 0) | (jj == cur) | (jj == cur - 1)
    score = jnp.where(jj * SEL_BLK <= tt, imp_t + jnp.where(forced, FORCE_BONUS, 0.0), NEG_BIG)
    rank = jnp.zeros((ns, tq), F32)
    for jp in range(ns):
        row = score[jp:jp + 1, :]
        rank = rank + jnp.where(row > score, 1.0, jnp.where((row == score) & (jj > jp), 1.0, 0.0))
    sel_t = jnp.where(rank < float(N_SELECT), 1.0, 0.0)
    sel_t = jnp.concatenate([sel_t, jnp.zeros((HEAD_DIM - ns, tq), F32)], axis=0)
    sel = jnp.transpose(sel_t)

    tk = NSA_TK
    n_tiles = (q0 + tq + tk - 1) // tk
    unpicked = jnp.where(sel > 0.5, 0.0, NEG).astype(BF16)
    q_aug = jnp.concatenate([q, jnp.concatenate([unpicked] * r4, axis=0)], axis=1)
    blk_lane = lax.broadcasted_iota(jnp.int32, (tk, HEAD_DIM), 1)
    blk_of_key = lax.broadcasted_iota(jnp.int32, (tk, HEAD_DIM), 0) // SEL_BLK

    def tile_scores(kt):
        k0 = pl.multiple_of(kt * tk, tk)
        onehot = jnp.where(blk_lane == kt * (tk // SEL_BLK) + blk_of_key, 1.0, 0.0).astype(BF16)
        k_aug = jnp.concatenate([ks_ref[pl.ds(k0, tk), :], onehot], axis=1)
        return lax.dot_general(q_aug, k_aug, _NT, preferred_element_type=F32).reshape(r4, tq, tk), k0

    def accumulate(carry, s, k0):
        m_run, l_run, acc = carry
        m_new = jnp.maximum(m_run, jnp.max(s, axis=-1, keepdims=True))
        alpha = jnp.exp(m_run - m_new)
        p = jnp.exp(s - m_new)
        l_new = alpha * l_run + jnp.sum(p, axis=-1, keepdims=True)
        pv = jnp.dot(p.reshape(r4 * tq, tk).astype(BF16), vs_ref[pl.ds(k0, tk), :], preferred_element_type=F32)
        return m_new, l_new, alpha.reshape(r4 * tq, 1) * acc + pv

    def full_tile(kt, carry):
        s, k0 = tile_scores(kt)
        return accumulate(carry, s, k0)

    init = (jnp.full((r4, tq, 1), NEG, F32), jnp.zeros((r4, tq, 1), F32), jnp.zeros((r4 * tq, HEAD_DIM), F32))
    carry = lax.fori_loop(0, n_tiles - 1, full_tile, init)
    s, k0 = tile_scores(n_tiles - 1)
    kpos = k0 + lax.broadcasted_iota(jnp.int32, (tq, tk), 1)
    _, l_s, acc_s = accumulate(carry, jnp.where((kpos <= tpos)[None], s, NEG), k0)
    o_s = acc_s / jnp.maximum(l_s, TINY).reshape(r4 * tq, 1)

    wb = WINDOW + tq
    w0 = pl.multiple_of(jnp.maximum(q0 - WINDOW, 0), tq)
    s = lax.dot_general(q, kw_ref[pl.ds(w0, wb), :], _NT, preferred_element_type=F32).reshape(r4, tq, wb)
    dist = tpos - (w0 + lax.broadcasted_iota(jnp.int32, (tq, wb), 1))
    wmask = ((dist >= 0) & (dist < WINDOW))[None]
    sm = jnp.where(wmask, s, NEG)
    p_w = jnp.where(wmask, jnp.exp(sm - jnp.max(sm, axis=-1, keepdims=True)), 0.0)
    den_w = jnp.maximum(jnp.sum(p_w, axis=-1, keepdims=True), TINY).reshape(r4 * tq, 1)
    o_w = jnp.dot(p_w.reshape(r4 * tq, wb).astype(BF16), vw_ref[pl.ds(w0, wb), :],
                  preferred_element_type=F32) / den_w

    gates = jax.nn.sigmoid(gate_ref[...])
    for r in range(r4):
        def gate(c):
            lane0 = SMALL_NG + 3 * r + c
            lane1 = lane0 + 3 * r4
            return jnp.where(g == 0, gates[:, lane0:lane0 + 1], gates[:, lane1:lane1 + 1])
        rows = slice(r * tq, (r + 1) * tq)
        o = gate(0) * o_c[rows] + gate(1) * o_s[rows] + gate(2) * o_w[rows]
        o_ref[:, r * HEAD_DIM:(r + 1) * HEAD_DIM] = o.astype(BF16)


def nsa_attention_prompt(q_b, kcvc, kvb, z, b, s):
    tq = NSA_TQ
    nq = s // tq
    ncp = kcvc.shape[2]
    gw = Q_PER_KV * HEAD_DIM
    seq_block = lambda col: pl.BlockSpec((s, HEAD_DIM), lambda bi, g, i: (bi, col + g))
    return pl.pallas_call(
        _nsa_attn_kernel,
        grid=(b, KV_GROUPS, nq),
        in_specs=[pl.BlockSpec((tq, gw), lambda bi, g, i: (bi * nq + i, g)),
                  pl.BlockSpec((1, 1, ncp, HEAD_DIM), lambda bi, g, i: (bi, g, 0, 0)),
                  pl.BlockSpec((1, 1, ncp, HEAD_DIM), lambda bi, g, i: (bi, KV_GROUPS + g, 0, 0)),
                  seq_block(0), seq_block(2), seq_block(4), seq_block(6),
                  pl.BlockSpec((tq, HEAD_DIM), lambda bi, g, i: (bi * nq + i, COL_SMALL // HEAD_DIM))],
        out_specs=pl.BlockSpec((tq, gw), lambda bi, g, i: (bi * nq + i, g)),
        out_shape=jax.ShapeDtypeStruct((b * s, N_WIDTH), BF16),
        compiler_params=pltpu.CompilerParams(
            dimension_semantics=("arbitrary", "arbitrary", "arbitrary"), vmem_limit_bytes=VMEM_LIMIT_BYTES),
        name="nsa_attention",
    )(q_b, kcvc, kcvc, kvb, kvb, kvb, kvb, z)


def nsa_prompt_pallas(z, b, s, cmp_w):
    pe, w1, w2 = cmp_w
    cos2, sin2 = rope_tables(jnp.arange(s))
    q_b, rows_f, win_f, kvb, cmp_in = nsa_prep_pallas(z, b, s, cos2, sin2)
    kcvc = nsa_compress_prompt(cmp_in.reshape(b, 4, s // CMP_STRIDE, CMP_STRIDE * HEAD_DIM), pe, w1, w2)
    hn = nsa_attention_prompt(q_b, kcvc, kvb, z, b, s)
    rows = rows_f.reshape(b, s, 4, KV_GROUPS, HEAD_DIM)
    win = win_f.reshape(b, s, 2, KV_GROUPS, HEAD_DIM)[:, -min(WINDOW, s):]
    return hn, (rows, win)


SAMPLE_PAGES = 16
SAMPLE_TS = 16
ROW_COLS = 4 * KV_GROUPS * HEAD_DIM
ROW_VECS = 4 * KV_GROUPS
WIN_VECS = 2 * KV_GROUPS
NEG_DEAD = -3e38


def _page_specs():
    def spec(k):
        return pl.BlockSpec((PAGE_SIZE * ROW_VECS, HEAD_DIM), lambda bi, i, pt: (pt[bi, i * SAMPLE_PAGES + k], 0))
    return [spec(k) for k in range(SAMPLE_PAGES)]


def _page_vecs(pg, vec):
    return pg[pl.ds(vec, PAGE_SIZE, stride=ROW_VECS), :]


def _nsa_cmp_sample_kernel(pt_ref, *refs):
    del pt_ref
    pages = refs[:SAMPLE_PAGES]
    pe_ref, w1_ref, w2_ref, o_ref, carry_sc, bias_sc = refs[SAMPLE_PAGES:]
    i = pl.program_id(1)
    half = CMP_STRIDE * HEAD_DIM
    nbp = PAGE_SIZE // CMP_STRIDE
    nb = SAMPLE_PAGES * nbp

    @pl.when(i == 0)
    def _():
        carry_sc[...] = jnp.zeros_like(carry_sc)
        for kv in range(2):
            pe = jnp.broadcast_to(pe_ref[kv], (8, 2 * half)).astype(BF16)
            bias_sc[kv] = jnp.dot(pe, w1_ref[kv], preferred_element_type=F32)

    row = lax.broadcasted_iota(jnp.int32, (nb, CMP_HID), 0)
    for kv in range(2):
        def flat(c):
            piece = lambda pg, j: pg[pl.ds(j * ROW_VECS + c, nbp, stride=CMP_STRIDE * ROW_VECS), :]
            return jnp.concatenate(
                [jnp.concatenate([piece(pg, j) for j in range(CMP_STRIDE)], axis=1) for pg in pages],
                axis=0)
        x = jnp.concatenate([flat(kv * KV_GROUPS + g) for g in range(KV_GROUPS)], axis=0).astype(BF16)
        first_all = jnp.dot(x, w1_ref[kv, :half, :], preferred_element_type=F32)
        second_all = jnp.dot(x, w1_ref[kv, half:, :], preferred_element_type=F32)
        for g in range(KV_GROUPS):
            c = kv * KV_GROUPS + g
            first = first_all[g * nb:(g + 1) * nb]
            second = second_all[g * nb:(g + 1) * nb]
            shifted = jnp.where(row == 0, carry_sc[c, 7:8, :], pltpu.roll(first, 1, 0))
            hid = shifted + second + bias_sc[kv, 0:1, :]
            o_ref[0, c] = jnp.dot(jax.nn.gelu(hid).astype(BF16), w2_ref[kv],
                                  preferred_element_type=F32).astype(BF16)
            carry_sc[c] = first[nb - 8:, :]


def nsa_compress_sample(pt, cache2d, pe, w1, w2, db):
    n_pages = pt.shape[1]
    nb_all = n_pages * (PAGE_SIZE // CMP_STRIDE)
    nb = SAMPLE_PAGES * (PAGE_SIZE // CMP_STRIDE)
    const3 = lambda bi, i, pt: (0, 0, 0)
    grid_spec = pltpu.PrefetchScalarGridSpec(
        num_scalar_prefetch=1,
        grid=(db, n_pages // SAMPLE_PAGES),
        in_specs=_page_specs() + [pl.BlockSpec(pe.shape, const3), pl.BlockSpec(w1.shape, const3),
                                   pl.BlockSpec(w2.shape, const3)],
        out_specs=pl.BlockSpec((1, 2 * KV_GROUPS, nb, HEAD_DIM), lambda bi, i, pt: (bi, 0, i, 0)),
        scratch_shapes=[pltpu.VMEM((2 * KV_GROUPS, 8, CMP_HID), F32), pltpu.VMEM((2, 8, CMP_HID), F32)])
    return pl.pallas_call(
        _nsa_cmp_sample_kernel,
        grid_spec=grid_spec,
        out_shape=jax.ShapeDtypeStruct((db, 2 * KV_GROUPS, nb_all, HEAD_DIM), BF16),
        compiler_params=pltpu.CompilerParams(
            dimension_semantics=("arbitrary", "arbitrary"), vmem_limit_bytes=VMEM_LIMIT_BYTES),
        name="nsa_compress_sample",
    )(pt, *([cache2d] * SAMPLE_PAGES), pe, w1, w2)


def _nsa_attn_sample_kernel(pt_ref, *refs, past_len, t_valid, n_steps):
    del pt_ref
    pages = refs[:SAMPLE_PAGES]
    q_ref, new_ref, kc_ref, win_ref, gate_ref, o_ref, sel_sc, m_sc, l_sc, acc_sc, oc_sc, ow_sc = refs[SAMPLE_PAGES:]
    i = pl.program_id(1)
    ts = SAMPLE_TS
    r4 = Q_PER_KV
    rows = r4 * ts
    d = HEAD_DIM
    blocks_per_step = SAMPLE_PAGES * PAGE_SIZE // SEL_BLK
    ns = past_len // SEL_BLK + 1
    trow = lax.broadcasted_iota(jnp.int32, (rows, 1), 0) % ts
    qpos = past_len + trow

    def q_of(g):
        return jnp.concatenate([q_ref[:, (g * r4 + r) * d:(g * r4 + r + 1) * d] for r in range(r4)], axis=0)

    @pl.when(i == 0)
    def _():
        ncp = kc_ref.shape[2]
        nsl = (n_steps + 1) * d
        lane = lax.broadcasted_iota(jnp.int32, (ncp, nsl), 1)
        tok = lax.broadcasted_iota(jnp.int32, (ncp, nsl), 0) - 1
        blk = (lane // d) * blocks_per_step + lane % d
        ratio = SEL_BLK // CMP_STRIDE
        c2s = jnp.where((lane % d < blocks_per_step) & (blk < ns) & (tok >= 0)
                        & (tok >= ratio * blk - 1) & (tok <= ratio * blk + ratio - 1), 1.0, 0.0).astype(BF16)
        lane_r = lax.broadcasted_iota(jnp.int32, (rows, nsl), 1)
        blk_r = (lane_r // d) * blocks_per_step + lane_r % d
        live = (lane_r % d < blocks_per_step) & (blk_r < ns)
        blk_f = blk_r.astype(F32)
        cur = qpos // SEL_BLK
        forced = (blk_r == 0) | (blk_r == cur) | (blk_r == cur - 1)
        midx = lax.broadcasted_iota(jnp.int32, (rows, ncp), 1)
        cmask = (midx >= 1) & ((midx - 1) * CMP_STRIDE + CMP_LEN - 1 <= qpos)
        sidx = lax.broadcasted_iota(jnp.int32, (rows, ts), 1)
        new_ok = (sidx <= trow) & (sidx < t_valid)
        wl = win_ref.shape[0] // WIN_VECS
        dist_buf = trow + wl - lax.broadcasted_iota(jnp.int32, (rows, wl), 1)
        buf_ok = (dist_buf >= 0) & (dist_buf < WINDOW)
        dist_new = trow - sidx
        wnew_ok = (dist_new >= 0) & (dist_new < WINDOW) & (sidx < t_valid)
        for g in range(KV_GROUPS):
            qg = q_of(g)
            s = lax.dot_general(qg, kc_ref[0, g], _NT, preferred_element_type=F32)
            p_c = _softmax_rows(s, cmask)
            oc_sc[g] = jnp.dot(p_c.astype(BF16), kc_ref[0, KV_GROUPS + g], preferred_element_type=F32)
            psum_t = p_c[0:ts] + p_c[ts:2 * ts] + p_c[2 * ts:3 * ts] + p_c[3 * ts:4 * ts]
            psum = jnp.concatenate([psum_t] * r4, axis=0)
            p_hi = psum.astype(BF16)
            p_lo = (psum - p_hi.astype(F32)).astype(BF16)
            imp = (jnp.dot(p_hi, c2s, preferred_element_type=F32) + jnp.dot(p_lo, c2s, preferred_element_type=F32))
            score = jnp.where(live, jnp.where(blk_r * SEL_BLK <= qpos, imp + jnp.where(forced, FORCE_BONUS, 0.0),
                                              NEG_BIG), NEG_DEAD)
            sel = jnp.zeros((rows, nsl), F32)
            for _ in range(N_SELECT):
                top = jnp.max(score, axis=1, keepdims=True)
                first = jnp.min(jnp.where(score == top, blk_f, 1e9), axis=1, keepdims=True)
                hit = live & (blk_f == first)
                sel = jnp.where(hit, 1.0, sel)
                score = jnp.where(hit, NEG_DEAD, score)
            for step in range(n_steps):
                sel_sc[g, step] = sel[:, step * d:(step + 1) * d]
            knew = new_ref[:, g * d:(g + 1) * d]
            vnew = new_ref[:, (KV_GROUPS + g) * d:(KV_GROUPS + g + 1) * d]
            sm = jnp.where(new_ok, lax.dot_general(qg, knew, _NT, preferred_element_type=F32), NEG)
            m0 = jnp.max(sm, axis=1, keepdims=True)
            p = jnp.where(new_ok, jnp.exp(sm - m0), 0.0)
            m_sc[g] = m0
            l_sc[g] = jnp.sum(p, axis=1, keepdims=True)
            acc_sc[g] = jnp.dot(p.astype(BF16), vnew, preferred_element_type=F32)
            kwb = win_ref[pl.ds(g, wl, stride=WIN_VECS), :].astype(BF16)
            vwb = win_ref[pl.ds(KV_GROUPS + g, wl, stride=WIN_VECS), :].astype(BF16)
            kwn = new_ref[:, (2 * KV_GROUPS + g) * d:(2 * KV_GROUPS + g + 1) * d]
            vwn = new_ref[:, (3 * KV_GROUPS + g) * d:(3 * KV_GROUPS + g + 1) * d]
            s1 = jnp.where(buf_ok, lax.dot_general(qg, kwb, _NT, preferred_element_type=F32), NEG)
            s2 = jnp.where(wnew_ok, lax.dot_general(qg, kwn, _NT, preferred_element_type=F32), NEG)
            mw = jnp.maximum(jnp.max(s1, axis=1, keepdims=True), jnp.max(s2, axis=1, keepdims=True))
            p1 = jnp.where(buf_ok, jnp.exp(s1 - mw), 0.0)
            p2 = jnp.where(wnew_ok, jnp.exp(s2 - mw), 0.0)
            den = jnp.maximum(jnp.sum(p1, axis=1, keepdims=True) + jnp.sum(p2, axis=1, keepdims=True), TINY)
            ow_sc[g] = (jnp.dot((p1 / den).astype(BF16), vwb, preferred_element_type=F32)
                        + jnp.dot((p2 / den).astype(BF16), vwn, preferred_element_type=F32))

    half_lane = lax.broadcasted_iota(jnp.int32, (rows, PAGE_SIZE), 1) < SEL_BLK
    for g in range(KV_GROUPS):
        qg = q_of(g)
        kt = jnp.concatenate([_page_vecs(pg, 2 * KV_GROUPS + g) for pg in pages], axis=0).astype(BF16)
        vt = jnp.concatenate([_page_vecs(pg, 3 * KV_GROUPS + g) for pg in pages], axis=0).astype(BF16)
        s = lax.dot_general(qg, kt, _NT, preferred_element_type=F32)
        selg = sel_sc[g, i]
        picked = jnp.concatenate(
            [jnp.where(half_lane, selg[:, 2 * k:2 * k + 1], selg[:, 2 * k + 1:2 * k + 2])
             for k in range(SAMPLE_PAGES)], axis=1)
        mask = picked > 0.5
        sm = jnp.where(mask, s, NEG)
        m_old = m_sc[g]
        m_new = jnp.maximum(m_old, jnp.max(sm, axis=1, keepdims=True))
        alpha = jnp.exp(m_old - m_new)
        p = jnp.where(mask, jnp.exp(sm - m_new), 0.0)
        m_sc[g] = m_new
        l_sc[g] = alpha * l_sc[g] + jnp.sum(p, axis=1, keepdims=True)
        acc_sc[g] = alpha * acc_sc[g] + jnp.dot(p.astype(BF16), vt, preferred_element_type=F32)

    @pl.when(i == n_steps - 1)
    def _():
        gates = jax.nn.sigmoid(gate_ref[...])
        for g in range(KV_GROUPS):
            o_s = acc_sc[g] / jnp.maximum(l_sc[g], TINY)
            o_c = oc_sc[g]
            o_w = ow_sc[g]
            for r in range(r4):
                h = g * r4 + r
                gate = lambda c: gates[:, SMALL_NG + 3 * h + c:SMALL_NG + 3 * h + c + 1]
                rs = slice(r * ts, (r + 1) * ts)
                o = gate(0) * o_c[rs] + gate(1) * o_s[rs] + gate(2) * o_w[rs]
                o_ref[0, :, h * d:(h + 1) * d] = o.astype(BF16)


def nsa_attention_sample(pt, cache2d, q_b, kvb, kcvc, win2d, zp, layer, db, t_pad, t_valid, past_len):
    ts = SAMPLE_TS
    n_steps = pt.shape[1] // SAMPLE_PAGES
    rows = Q_PER_KV * ts
    wl_rows = win2d.shape[0] // (DEPTH * db)
    seq_rows = lambda width, col: pl.BlockSpec((ts, width), lambda bi, i, pt: (bi * (t_pad // ts), col))
    grid_spec = pltpu.PrefetchScalarGridSpec(
        num_scalar_prefetch=1,
        grid=(db, n_steps),
        in_specs=_page_specs() + [
            seq_rows(N_WIDTH, 0),
            seq_rows(ROW_COLS, 0),
            pl.BlockSpec((1,) + kcvc.shape[1:], lambda bi, i, pt: (bi, 0, 0, 0)),
            pl.BlockSpec((wl_rows, HEAD_DIM), lambda bi, i, pt: (layer * db + bi, 0)),
            seq_rows(HEAD_DIM, COL_SMALL // HEAD_DIM)],
        out_specs=pl.BlockSpec((1, ts, N_WIDTH), lambda bi, i, pt: (bi, 0, 0)),
        scratch_shapes=[pltpu.VMEM((KV_GROUPS, n_steps, rows, HEAD_DIM), F32),
                        pltpu.VMEM((KV_GROUPS, rows, 1), F32),
                        pltpu.VMEM((KV_GROUPS, rows, 1), F32),
                        pltpu.VMEM((KV_GROUPS, rows, HEAD_DIM), F32),
                        pltpu.VMEM((KV_GROUPS, rows, HEAD_DIM), F32),
                        pltpu.VMEM((KV_GROUPS, rows, HEAD_DIM), F32)])
    return pl.pallas_call(
        functools.partial(_nsa_attn_sample_kernel, past_len=past_len, t_valid=t_valid, n_steps=n_steps),
        grid_spec=grid_spec,
        out_shape=jax.ShapeDtypeStruct((db, ts, N_WIDTH), BF16),
        compiler_params=pltpu.CompilerParams(
            dimension_semantics=("arbitrary", "arbitrary"), vmem_limit_bytes=VMEM_LIMIT_BYTES),
        name="nsa_attention_sample",
    )(pt, *([cache2d] * SAMPLE_PAGES), q_b, kvb, kcvc, win2d, zp)


def nsa_sample_pallas(zp, db, t, t_pad, layer, cache2d, pt, win_state, cmp_w):
    pe, w1, w2 = cmp_w
    past_len = pt.shape[1] * PAGE_SIZE
    cos2, sin2 = rope_tables(past_len + jnp.arange(t_pad))
    q_b, rows_f, win_f, kvb, _ = nsa_prep_pallas(zp, db, t_pad, cos2, sin2, tm=t_pad)
    kcvc = nsa_compress_sample(pt, cache2d, pe, w1, w2, db)
    win2d = win_state.reshape(-1, HEAD_DIM)
    hn = nsa_attention_sample(pt, cache2d, q_b, kvb, kcvc, win2d, zp, layer, db, t_pad, t, past_len)
    rows = rows_f.reshape(db, t_pad, 4, KV_GROUPS, HEAD_DIM)[:, :t]
    win_new = win_f.reshape(db, t_pad, 2, KV_GROUPS, HEAD_DIM)[:, :t]
    wl = win_state.shape[2]
    win = jnp.concatenate([win_state[layer], win_new], axis=1)[:, -wl:]
    return hn[:, :t], (rows, win)


MIX_L = 128
GLA_SUB = 16


def _logsig(x):
    return jnp.minimum(x, 0.0) - jnp.log1p(jnp.exp(-jnp.abs(x)))


def _prefix_sum(x, axis):
    n = x.shape[axis]
    idx = lax.broadcasted_iota(jnp.int32, x.shape, axis)
    step = 1
    while step < n:
        x = x + jnp.where(idx >= step, pltpu.roll(x, step, axis), 0.0)
        step *= 2
    return x


def _mlstm_kernel(uqk_ref, v_ref, og_ref, small_ref, convw_ref, bias_ref, norm_ref, conv0_ref, c0_ref, n0_ref,
                  m0_ref, h_ref, c_ref, n_ref, m_ref, prev_sc, c_sc, n_sc, m_sc, *, t_valid):
    t = pl.program_id(1)
    L = MIX_L
    d = HEAD_DIM

    @pl.when(t == 0)
    def _():
        prev_sc[...] = jnp.zeros_like(prev_sc)
        prev_sc[L - 8:, :] = conv0_ref[0]
        c_sc[...] = c0_ref[0]
        n_sc[...] = n0_ref[0]
        m_sc[...] = m0_ref[0]

    x = uqk_ref[...]
    prev = prev_sc[...]
    row = lax.broadcasted_iota(jnp.int32, x.shape, 0)
    w = convw_ref[...]
    conv = w[CONV_W - 1:CONV_W] * x
    for k in range(1, CONV_W):
        shifted = jnp.where(row >= k, pltpu.roll(x, k, 0), pltpu.roll(prev, k, 0))
        conv = conv + w[CONV_W - 1 - k:CONV_W - k] * shifted
    prev_sc[...] = x
    act = conv * jax.nn.sigmoid(conv)

    pre = small_ref[...] + bias_ref[...]
    pos_c = t * L + lax.broadcasted_iota(jnp.int32, (L, HEAD_DIM), 0)
    ig_c = jnp.where(pos_c < t_valid, pre, NEG)
    b_c = _prefix_sum(jnp.where(pos_c < t_valid, _logsig(pre), 0.0), 0)
    pre_r = jnp.transpose(pre)[0:8]
    pos_r = t * L + lax.broadcasted_iota(jnp.int32, (8, L), 1)
    ig_r = jnp.where(pos_r < t_valid, pre_r, NEG)
    b_r = _prefix_sum(jnp.where(pos_r < t_valid, _logsig(pre_r), 0.0), 1)

    li = lax.broadcasted_iota(jnp.int32, (L, L), 0)
    si = lax.broadcasted_iota(jnp.int32, (L, L), 1)
    for h in range(H_M):
        sl = slice(h * d, (h + 1) * d)
        bcol, igcol = b_c[:, H_M + h:H_M + h + 1], ig_c[:, h:h + 1]
        brow, igrow = b_r[H_M + h:H_M + h + 1, :], ig_r[h:h + 1, :]
        m0 = m_sc[h:h + 1, 0:1]
        c0 = c_sc[h]
        n0 = n_sc[h:h + 1, :]
        qf = act[:, sl]
        kf = act[:, M_WIDTH + h * d:M_WIDTH + (h + 1) * d] * (d ** -0.5)
        vf = v_ref[:, sl]
        qb, kb, vb = qf.astype(BF16), kf.astype(BF16), vf.astype(BF16)
        logw = jnp.where(si <= li, bcol - brow + igrow, NEG)
        gsum = bcol + m0
        m_row = jnp.maximum(jnp.max(logw, axis=1, keepdims=True), gsum)
        wgt = jnp.exp(logw - m_row) * lax.dot_general(qb, kb, _NT, preferred_element_type=F32)
        inter = jnp.exp(gsum - m_row)
        num = (jnp.dot(wgt.astype(BF16), vb, preferred_element_type=F32)
               + inter * lax.dot_general(qb, c0.astype(BF16), _NT, preferred_element_type=F32))
        den = jnp.sum(wgt, axis=1, keepdims=True) + inter * jnp.sum(qf * n0, axis=1, keepdims=True)
        hh = num / jnp.maximum(jnp.abs(den), jnp.exp(-m_row))
        bl = bcol[L - 1:L, :]
        m_new = jnp.maximum(jnp.max(bl - brow + igrow, axis=1, keepdims=True), bl + m0)
        wa = jnp.exp(bl - bcol + igcol - m_new)
        wc = jnp.exp(bl + m0 - m_new)
        c_sc[h] = wc * c0 + jnp.dot(jnp.transpose(vf * wa).astype(BF16), kb, preferred_element_type=F32)
        n_sc[h:h + 1, :] = wc * n0 + jnp.sum(wa * kf, axis=0, keepdims=True)
        m_sc[h:h + 1, :] = jnp.broadcast_to(m_new, (1, d))
        out = _rms(hh, norm_ref[:, sl]) * jax.nn.sigmoid(og_ref[:, sl])
        h_ref[:, sl] = out.astype(BF16)

    @pl.when(t == pl.num_programs(1) - 1)
    def _():
        c_ref[0] = c_sc[...]
        n_ref[0] = n_sc[...]
        m_ref[0] = m_sc[...]


def mlstm_pallas(z, b, t_pad, t_valid, conv_w, gate_b, norm, conv0, c0, n0, m0):
    L = MIX_L
    nt = t_pad // L
    rowblk = lambda col: (lambda bi, t: (bi * nt + t, col))
    const2 = lambda bi, t: (0, 0)
    per_b3 = lambda bi, t: (bi, 0, 0)
    per_b4 = lambda bi, t: (bi, 0, 0, 0)
    return pl.pallas_call(
        functools.partial(_mlstm_kernel, t_valid=t_valid),
        grid=(b, nt),
        in_specs=[pl.BlockSpec((L, 2 * M_WIDTH), rowblk(0)),
                  pl.BlockSpec((L, M_WIDTH), rowblk(2)),
                  pl.BlockSpec((L, M_WIDTH), rowblk(3)),
                  pl.BlockSpec((L, HEAD_DIM), rowblk(COL_SMALL // HEAD_DIM)),
                  pl.BlockSpec((CONV_W, 2 * M_WIDTH), const2),
                  pl.BlockSpec((1, HEAD_DIM), const2),
                  pl.BlockSpec((1, M_WIDTH), const2),
                  pl.BlockSpec((1, 8, 2 * M_WIDTH), per_b3),
                  pl.BlockSpec((1, H_M, HEAD_DIM, HEAD_DIM), per_b4),
                  pl.BlockSpec((1, 8, HEAD_DIM), per_b3),
                  pl.BlockSpec((1, 8, HEAD_DIM), per_b3)],
        out_specs=[pl.BlockSpec((L, M_WIDTH), rowblk(0)),
                   pl.BlockSpec((1, H_M, HEAD_DIM, HEAD_DIM), per_b4),
                   pl.BlockSpec((1, 8, HEAD_DIM), per_b3),
                   pl.BlockSpec((1, 8, HEAD_DIM), per_b3)],
        out_shape=[jax.ShapeDtypeStruct((b * t_pad, M_WIDTH), BF16),
                   jax.ShapeDtypeStruct((b, H_M, HEAD_DIM, HEAD_DIM), F32),
                   jax.ShapeDtypeStruct((b, 8, HEAD_DIM), F32),
                   jax.ShapeDtypeStruct((b, 8, HEAD_DIM), F32)],
        scratch_shapes=[pltpu.VMEM((L, 2 * M_WIDTH), F32),
                        pltpu.VMEM((H_M, HEAD_DIM, HEAD_DIM), F32),
                        pltpu.VMEM((8, HEAD_DIM), F32),
                        pltpu.VMEM((8, HEAD_DIM), F32)],
        compiler_params=pltpu.CompilerParams(
            dimension_semantics=("arbitrary", "arbitrary"), vmem_limit_bytes=VMEM_LIMIT_BYTES),
        name="mlstm",
    )(z, z, z, z, conv_w, gate_b, norm, conv0, c0, n0, m0)


def _gla_kernel(q_ref, k_ref, v_ref, r_ref, small_ref, w2_ref, gb_ref, norm_ref, s0_ref, h_ref, s_ref, st_sc,
                *, t_valid):
    t = pl.program_id(1)
    L = MIX_L
    d = HEAD_DIM

    @pl.when(t == 0)
    def _():
        for h in range(H_G):
            st_sc[h] = jnp.transpose(s0_ref[0, h])

    pre = jnp.dot(small_ref[...].astype(BF16), w2_ref[...], preferred_element_type=F32) + gb_ref[...]
    pos = t * L + lax.broadcasted_iota(jnp.int32, (L, G_WIDTH), 0)
    la = jnp.where(pos < t_valid, _logsig(pre) / GLA_GATE_TEMP, 0.0)
    bc_all = _prefix_sum(la, 0)
    rowi = lax.broadcasted_iota(jnp.int32, (L, d), 0)
    li = lax.broadcasted_iota(jnp.int32, (L, L), 0)
    si = lax.broadcasted_iota(jnp.int32, (L, L), 1)
    for h in range(H_G):
        sl = slice(h * d, (h + 1) * d)
        bc = bc_all[:, sl]
        qf = q_ref[:, sl] * (d ** -0.5)
        kf = jnp.where(t * L + rowi < t_valid, k_ref[:, sl], 0.0)
        vf = v_ref[:, sl]
        qparts, kparts = [], []
        for j in range(L // GLA_SUB):
            lo, hi = j * GLA_SUB, (j + 1) * GLA_SUB
            e = bc[hi - 1:hi, :]
            qparts.append((qf * jnp.exp(jnp.where(rowi >= lo, bc - e, NEG))).astype(BF16))
            kparts.append((kf * jnp.exp(jnp.where((rowi >= lo) & (rowi < hi), e - bc, NEG))).astype(BF16))
        a = lax.dot_general(jnp.concatenate(qparts, axis=1), jnp.concatenate(kparts, axis=1), _NT,
                            preferred_element_type=F32)
        a = jnp.where(si <= li, a, 0.0)
        st = st_sc[h]
        o = (jnp.dot(a.astype(BF16), vf.astype(BF16), preferred_element_type=F32)
             + lax.dot_general((qf * jnp.exp(bc)).astype(BF16), st.astype(BF16), _NT, preferred_element_type=F32))
        bl = bc[L - 1:L, :]
        kd = (kf * jnp.exp(bl - bc)).astype(BF16)
        st_sc[h] = st * jnp.exp(bl) + jnp.dot(jnp.transpose(vf).astype(BF16), kd, preferred_element_type=F32)
        gate = r_ref[:, sl]
        h_ref[:, sl] = (_rms(o, norm_ref[:, sl]) * (gate * jax.nn.sigmoid(gate))).astype(BF16)

    @pl.when(t == pl.num_programs(1) - 1)
    def _():
        for h in range(H_G):
            s_ref[0, h] = jnp.transpose(st_sc[h])


def gla_pallas(z, b, t_pad, t_valid, w2_pad, gate_b, norm, s0):
    L = MIX_L
    nt = t_pad // L
    rowblk = lambda col: (lambda bi, t: (bi * nt + t, col))
    const2 = lambda bi, t: (0, 0)
    per_b4 = lambda bi, t: (bi, 0, 0, 0)
    gcol = COL_GLA // G_WIDTH
    return pl.pallas_call(
        functools.partial(_gla_kernel, t_valid=t_valid),
        grid=(b, nt),
        in_specs=[pl.BlockSpec((L, G_WIDTH), rowblk(gcol)),
                  pl.BlockSpec((L, G_WIDTH), rowblk(gcol + 1)),
                  pl.BlockSpec((L, G_WIDTH), rowblk(gcol + 2)),
                  pl.BlockSpec((L, G_WIDTH), rowblk(gcol + 3)),
                  pl.BlockSpec((L, HEAD_DIM), rowblk(COL_SMALL // HEAD_DIM)),
                  pl.BlockSpec((HEAD_DIM, G_WIDTH), const2),
                  pl.BlockSpec((1, G_WIDTH), const2),
                  pl.BlockSpec((1, G_WIDTH), const2),
                  pl.BlockSpec((1, H_G, HEAD_DIM, HEAD_DIM), per_b4)],
        out_specs=[pl.BlockSpec((L, G_WIDTH), rowblk(0)),
                   pl.BlockSpec((1, H_G, HEAD_DIM, HEAD_DIM), per_b4)],
        out_shape=[jax.ShapeDtypeStruct((b * t_pad, G_WIDTH), BF16),
                   jax.ShapeDtypeStruct((b, H_G, HEAD_DIM, HEAD_DIM), F32)],
        scratch_shapes=[pltpu.VMEM((H_G, HEAD_DIM, HEAD_DIM), F32)],
        compiler_params=pltpu.CompilerParams(
            dimension_semantics=("arbitrary", "arbitrary"), vmem_limit_bytes=VMEM_LIMIT_BYTES),
        name="gla",
    )(z, z, z, z, z, w2_pad, gate_b, norm, s0)


def mixers_pallas(z, b, t, conv_buf, c0, n0, m0, s0, conv_w, m_gate_b, m_norm, g_w2, g_b, g_norm):
    t_pad = z.shape[0] // b
    lanes = lambda a: jnp.pad(a.reshape(1, -1), ((0, 0), (0, HEAD_DIM - a.size)))
    conv0 = jnp.pad(conv_buf, ((0, 0), (8 - (CONV_W - 1), 0), (0, 0)))
    n0p = jnp.pad(n0, ((0, 0), (0, 8 - H_M), (0, 0)))
    m0p = jnp.broadcast_to(jnp.pad(m0, ((0, 0), (0, 8 - H_M)))[:, :, None], (b, 8, HEAD_DIM))
    hm, c, n, m = mlstm_pallas(z, b, t_pad, t, conv_w, lanes(m_gate_b), m_norm.reshape(1, -1), conv0, c0, n0p, m0p)
    w2_pad = jnp.pad(g_w2, ((SMALL_GLR, HEAD_DIM - SMALL_GLR - GATE_RANK), (0, 0))).astype(BF16)
    hg, s = gla_pallas(z, b, t_pad, t, w2_pad, g_b.reshape(1, -1), g_norm.reshape(1, -1), s0)
    valid = lambda a: a.reshape(b, t_pad, -1)[:, :t]
    return valid(hm), valid(hg), c, n[:, :H_M], m[:, :H_M, 0], s


def rmsnorm(x, g):
    xf = x.astype(F32)
    y = xf * lax.rsqrt(jnp.mean(xf * xf, axis=-1, keepdims=True) + EPS)
    return (y * g.astype(F32)).astype(x.dtype)


def rope(x, pos):
    half = x.shape[-1] // 2
    inv_freq = jnp.exp(-math.log(ROPE_THETA) * jnp.arange(half, dtype=F32) / half)
    ang = pos.astype(F32)[:, None] * inv_freq[None, :]
    cos = jnp.cos(ang)[:, None, :]
    sin = jnp.sin(ang)[:, None, :]
    xf = x.astype(F32)
    x1, x2 = xf[..., :half], xf[..., half:]
    return jnp.concatenate([x1 * cos - x2 * sin, x2 * cos + x1 * sin], axis=-1).astype(x.dtype)


def masked_softmax(s, mask):
    s = jnp.where(mask, s, -jnp.inf)
    m = jnp.max(s, axis=-1, keepdims=True)
    m = jnp.where(jnp.isfinite(m), m, 0.0)
    p = jnp.exp(s - m)
    return p / jnp.maximum(jnp.sum(p, axis=-1, keepdims=True), TINY)


def chunked_scan(step, carry, xs, chunk):
    b, t = xs[0].shape[:2]
    c = chunk if t % chunk == 0 else t
    n = t // c
    split = lambda a: jnp.moveaxis(a.reshape((b, n, c) + a.shape[2:]), 1, 0)
    carry, ys = lax.scan(step, carry, tuple(split(a) for a in xs))
    return carry, jnp.moveaxis(ys, 0, 1).reshape((b, t) + ys.shape[3:])


def mlstm_chunk(carry, inp):
    c0, n0, m0 = carry
    q, k, v, li, lf = inp
    L = q.shape[1]
    b = jnp.cumsum(lf, axis=1).transpose(0, 2, 1)
    ig = li.transpose(0, 2, 1)
    causal = jnp.tril(jnp.ones((L, L), dtype=bool))
    logw = jnp.where(causal, b[..., :, None] - b[..., None, :] + ig[..., None, :], -jnp.inf)
    g = b + m0[..., None]
    m_row = jnp.maximum(jnp.max(logw, axis=-1), g)
    w = jnp.exp(logw - m_row[..., None]) * jnp.einsum('blhd,bshd->bhls', q, k)
    inter = jnp.exp(g - m_row)
    num = jnp.einsum('bhls,bshd->bhld', w, v) + inter[..., None] * jnp.einsum('bhed,blhd->bhle', c0, q)
    den = jnp.sum(w, axis=-1) + inter * jnp.einsum('bhd,blhd->bhl', n0, q)
    h = num / jnp.maximum(jnp.abs(den), jnp.exp(-m_row))[..., None]
    bl = b[..., -1]
    a = bl[..., None] - b + ig
    m_new = jnp.maximum(jnp.max(a, axis=-1), bl + m0)
    wa = jnp.exp(a - m_new[..., None])
    wc = jnp.exp(bl + m0 - m_new)
    c_new = wc[..., None, None] * c0 + jnp.einsum('bhs,bshe,bshd->bhed', wa, v, k)
    n_new = wc[..., None] * n0 + jnp.einsum('bhs,bshd->bhd', wa, k)
    return (c_new, n_new, m_new), h.transpose(0, 2, 1, 3)


def gla_chunk(s0, inp):
    q, k, v, la = inp
    L = q.shape[1]
    bc = jnp.cumsum(la, axis=1)
    causal = jnp.tril(jnp.ones((L, L), dtype=bool))[None, :, :, None, None]
    decay = jnp.exp(jnp.where(causal, bc[:, :, None] - bc[:, None, :], -jnp.inf))
    a = jnp.einsum('bthk,bshk,btshk->bhts', q, k, decay)
    o = jnp.einsum('bhts,bshv->bthv', a, v) + jnp.einsum('bthk,bhkv->bthv', q * jnp.exp(bc), s0)
    bl = bc[:, -1]
    s_new = jnp.exp(bl)[..., None] * s0 + jnp.einsum('bshk,bshv->bhkv', k * jnp.exp(bl[:, None] - bc), v)
    return s_new, o


def nsa_compress(x, pe, w1, w2):
    b, l = x.shape[:2]
    nb = l // CMP_STRIDE
    xb = x[:, :nb * CMP_STRIDE].reshape(b, nb, CMP_STRIDE, KV_GROUPS, HEAD_DIM)
    first = jnp.einsum('bnjgd,jde->bnge', xb, w1[:CMP_STRIDE])
    second = jnp.einsum('bnjgd,jde->bnge', xb, w1[CMP_STRIDE:])
    hid = first[:, :-1] + second[:, 1:] + jnp.einsum('jd,jde->e', pe, w1)
    return jnp.einsum('bnge,ed->bngd', jax.nn.gelu(hid), w2)


def nsa_context(rows, cmp_params):
    pe, w1, w2 = cmp_params
    b, l = rows.shape[:2]
    kc = nsa_compress(rows[:, :, 0], pe[0], w1[0], w2[0])
    vc = nsa_compress(rows[:, :, 1], pe[1], w1[1], w2[1])
    nc = kc.shape[1]
    ns = -(-l // SEL_BLK)
    sel = jnp.pad(rows[:, :, 2:4], ((0, 0), (0, ns * SEL_BLK - l), (0, 0), (0, 0), (0, 0)))
    sel = sel.reshape(b, ns, SEL_BLK, 2, KV_GROUPS, HEAD_DIM).transpose(3, 0, 4, 1, 2, 5)
    cmp_start = jnp.arange(nc) * CMP_STRIDE
    cmp_end = cmp_start + CMP_LEN - 1
    sel_start = jnp.arange(ns) * SEL_BLK
    cmp_to_sel = ((cmp_start[:, None] < sel_start[None, :] + SEL_BLK)
                  & (cmp_end[:, None] >= sel_start[None, :])).astype(F32)
    return kc, vc, cmp_end, cmp_to_sel, sel[0], sel[1]


def nsa_query_block(q, qpos, gates, kc, vc, cmp_end, cmp_to_sel, ksb, vsb, kw, vw, kwpos):
    b, t = q.shape[:2]
    qg = q.reshape(b, t, KV_GROUPS, Q_PER_KV, HEAD_DIM) * (HEAD_DIM ** -0.5)
    s = jnp.einsum('btgrd,bngd->btgrn', qg, kc).astype(F32)
    p_c = masked_softmax(s, (cmp_end[None, :] <= qpos[:, None])[None, :, None, None, :])
    o_c = jnp.einsum('btgrn,bngd->btgrd', p_c.astype(vc.dtype), vc)
    ns = ksb.shape[2]
    imp = jnp.einsum('btgn,nj->btgj', jnp.sum(p_c, axis=3), cmp_to_sel)
    j = jnp.arange(ns)
    cur = qpos // SEL_BLK
    elig = (j * SEL_BLK)[None, :] <= qpos[:, None]
    forced = (j[None, :] == 0) | (j[None, :] == cur[:, None]) | (j[None, :] == cur[:, None] - 1)
    score = jnp.where(elig[None, :, None, :], imp + jnp.where(forced, FORCE_BONUS, 0.0)[None, :, None, :], NEG_BIG)
    _, idx = lax.top_k(score, min(N_SELECT, ns))
    n = idx.shape[-1]
    idx_g = idx.transpose(0, 2, 1, 3).reshape(b, KV_GROUPS, t * n)
    take = jax.vmap(jax.vmap(lambda blocks, ids: blocks[ids]))
    gk = take(ksb, idx_g).reshape(b, KV_GROUPS, t, n * SEL_BLK, HEAD_DIM)
    gv = take(vsb, idx_g).reshape(b, KV_GROUPS, t, n * SEL_BLK, HEAD_DIM)
    kpos = (idx[..., None] * SEL_BLK + jnp.arange(SEL_BLK)).reshape(b, t, KV_GROUPS, n * SEL_BLK)
    s = jnp.einsum('btgrd,bgtmd->btgrm', qg, gk).astype(F32)
    p_s = masked_softmax(s, (kpos <= qpos[None, :, None, None])[:, :, :, None, :])
    o_s = jnp.einsum('btgrm,bgtmd->btgrd', p_s.astype(gv.dtype), gv)
    s = jnp.einsum('btgrd,bkgd->btgrk', qg, kw).astype(F32)
    d = qpos[:, None] - kwpos[None, :]
    wmask = (kwpos[None, :] >= 0) & (d >= 0) & (d < WINDOW)
    p_w = masked_softmax(s, wmask[None, :, None, None, :])
    o_w = jnp.einsum('btgrk,bkgd->btgrd', p_w.astype(vw.dtype), vw)
    g = jax.nn.sigmoid(gates.astype(F32)).reshape(b, t, KV_GROUPS, Q_PER_KV, 3)
    o = g[..., 0:1] * o_c + g[..., 1:2] * o_s + g[..., 2:3] * o_w
    return o.reshape(b, t, H_N * HEAD_DIM).astype(q.dtype)


def nsa_prep(n_q, n_kv, n_g, pos):
    b, t = n_q.shape[:2]
    q = rope(n_q.reshape(b, t, H_N, HEAD_DIM), pos)
    kv = n_kv.reshape(b, t, 6, KV_GROUPS, HEAD_DIM)
    rows = jnp.stack([rope(kv[:, :, 0], pos), kv[:, :, 1], rope(kv[:, :, 2], pos), kv[:, :, 3]], axis=2)
    win = jnp.stack([rope(kv[:, :, 4], pos), kv[:, :, 5]], axis=2)
    return q, n_g.reshape(b, t, H_N, 3), rows, win


def nsa_prompt(n_q, n_kv, n_g, cmp_params):
    b, s = n_q.shape[:2]
    pos = jnp.arange(s)
    q, gates, rows, win = nsa_prep(n_q, n_kv, n_g, pos)
    kc, vc, cmp_end, cmp_to_sel, ksb, vsb = nsa_context(rows, cmp_params)
    nb = s // QBLK
    kwp = jnp.pad(win, ((0, 0), (WINDOW, 0), (0, 0), (0, 0), (0, 0)))

    def one_block(args):
        qi, gi, bi = args
        start = bi * QBLK
        qpos = start + jnp.arange(QBLK)
        band = lax.dynamic_slice_in_dim(kwp, start, WINDOW + QBLK, axis=1)
        kwpos = start - WINDOW + jnp.arange(WINDOW + QBLK)
        return nsa_query_block(qi, qpos, gi, kc, vc, cmp_end, cmp_to_sel, ksb, vsb,
                               band[:, :, 0], band[:, :, 1], kwpos)

    blocks = lambda a: jnp.moveaxis(a.reshape((b, nb, QBLK) + a.shape[2:]), 1, 0)
    out = lax.map(one_block, (blocks(q), blocks(gates), jnp.arange(nb)))
    out = jnp.moveaxis(out, 0, 1).reshape(b, s, H_N * HEAD_DIM)
    return out, (rows, win[:, -min(WINDOW, s):])


def nsa_sample(n_q, n_kv, n_g, past_rows, win_buf, cmp_params):
    b, t = n_q.shape[:2]
    past_len = past_rows.shape[1]
    pos = past_len + jnp.arange(t)
    q, gates, rows, win = nsa_prep(n_q, n_kv, n_g, pos)
    full = jnp.concatenate([past_rows.astype(rows.dtype), rows], axis=1)
    kc, vc, cmp_end, cmp_to_sel, ksb, vsb = nsa_context(full, cmp_params)
    wl = win_buf.shape[1]
    kw_all = jnp.concatenate([win_buf.astype(win.dtype), win], axis=1)
    kwpos = past_len - wl + jnp.arange(wl + t)
    out = nsa_query_block(q, pos, gates, kc, vc, cmp_end, cmp_to_sel, ksb, vsb,
                          kw_all[:, :, 0], kw_all[:, :, 1], kwpos)
    return out, (rows, kw_all[:, -wl:])


def to_heads(a, n_heads):
    return a.reshape(a.shape[0], a.shape[1], n_heads, -1).astype(F32)


def split_cols(z):
    cut = lambda a, n: z[..., a:a + n]
    u_qk, m_v, m_o = cut(0, 2 * M_WIDTH), cut(2 * M_WIDTH, M_WIDTH), cut(3 * M_WIDTH, M_WIDTH)
    n_q, n_kv = cut(COL_NQ, N_WIDTH), cut(COL_NKV, 6 * KV_GROUPS * HEAD_DIM)
    g_q, g_k, g_v, g_r = (cut(COL_GLA + i * G_WIDTH, G_WIDTH) for i in range(4))
    m_if = cut(COL_SMALL + SMALL_MIF, 2 * H_M)
    n_g = cut(COL_SMALL + SMALL_NG, 3 * H_N)
    g_lr = cut(COL_SMALL + SMALL_GLR, GATE_RANK)
    return u_qk, m_v, m_o, m_if, n_q, n_kv, n_g, g_q, g_k, g_v, g_r, g_lr


def regroup_w_in(w_in):
    o = (0,) + SPLIT_OFFSETS + (IN_COLS,)
    piece = lambda i: w_in[..., o[i]:o[i + 1]]
    order = [0, 1, 2, 4, 5, 7, 8, 9, 10, 3, 6, 11]
    pad = jnp.zeros(w_in.shape[:-1] + (IN_COLS_PAD - IN_COLS,), w_in.dtype)
    return jnp.concatenate([piece(i) for i in order] + [pad], axis=-1)


def trunk_layer(x, conv_buf, c0, n0, m0, s0, nsa_fn, layer, g_norms, w_in, conv_w, m_gate_b, m_norm,
                g_w2, g_b, g_norm, w_out, w_ff1, w_ff2):
    b, t, d = x.shape
    x2 = x.reshape(b * t, d)
    z = norm_matmul(x2, g_norms[0], w_in, layer, IN_TN)
    z3 = z.reshape(b, t, -1)
    t_pad = -(-t // MIX_L) * MIX_L
    zp = z if t_pad == t else jnp.pad(z3, ((0, 0), (0, t_pad - t), (0, 0))).reshape(b * t_pad, -1)
    hm, hg, c, n, m, s = mixers_pallas(zp, b, t, conv_buf, c0, n0, m0, s0,
                                       conv_w, m_gate_b, m_norm, g_w2, g_b, g_norm)
    new_conv = z3[:, t - (CONV_W - 1):, :2 * M_WIDTH]
    hn, nsa_state = nsa_fn(zp, b, t, t_pad)
    x2 = matmul_norm_res(hm.reshape(b * t, -1), hn.reshape(b * t, -1), hg.reshape(b * t, -1),
                         w_out, layer, g_norms[1], x2)
    x2 = ffn(x2, g_norms[2], w_ff1, w_ff2, layer, g_norms[3])
    return x2.reshape(b, t, d), (nsa_state[0], nsa_state[1], c, n, m, new_conv, s)


def kernel(x_prompt, x_sample, cache_nsa_kv, state_nsa_win, state_mlstm_C, state_mlstm_n, state_mlstm_m, state_mlstm_conv, state_gla_S, page_table, norms, w_in, mlstm_conv_w, mlstm_gate_b, mlstm_norm, nsa_cmp_pe, nsa_cmp_w1, nsa_cmp_w2, gla_gate_w2, gla_gate_b, gla_norm, w_out, w_ff1, w_ff2):
    xp, xs = x_prompt, x_sample
    bp = xp.shape[0]
    db = xs.shape[0]
    past_len = page_table.shape[1] * PAGE_SIZE
    conv0 = jnp.zeros((bp, CONV_W - 1, 2 * M_WIDTH), xp.dtype)
    c0 = jnp.zeros((bp, H_M, HEAD_DIM, HEAD_DIM), F32)
    n0 = jnp.zeros((bp, H_M, HEAD_DIM), F32)
    m0 = jnp.zeros((bp, H_M), F32)
    s0 = jnp.zeros((bp, H_G, HEAD_DIM, HEAD_DIM), F32)
    w_in_b = regroup_w_in(w_in).astype(BF16)
    w_out_b = w_out.astype(BF16)
    w_ff1_b = w_ff1.astype(BF16)
    w_ff2_b = w_ff2.astype(BF16)
    cmp_pe = nsa_cmp_pe.reshape(DEPTH, 2, 1, CMP_LEN * HEAD_DIM)
    cmp_w1_b = nsa_cmp_w1.reshape(DEPTH, 2, CMP_LEN * HEAD_DIM, CMP_HID).astype(BF16)
    cmp_w2_b = nsa_cmp_w2.astype(BF16)
    n_pool = cache_nsa_kv.shape[1]
    cache2d = cache_nsa_kv.reshape(-1, HEAD_DIM)
    acc_p = [[] for _ in range(7)]
    acc_s = [[] for _ in range(7)]
    for l in range(DEPTH):
        weights = (l, norms[l], w_in_b, mlstm_conv_w[l], mlstm_gate_b[l], mlstm_norm[l],
                   gla_gate_w2[l], gla_gate_b[l], gla_norm[l], w_out_b, w_ff1_b, w_ff2_b)
        cmp_w = (cmp_pe[l], cmp_w1_b[l], cmp_w2_b[l])
        nsa_p = lambda zp, b, t, t_pad, cmp_w=cmp_w: nsa_prompt_pallas(zp, b, t, cmp_w)
        nsa_s = lambda zp, b, t, t_pad, l=l, cmp_w=cmp_w: nsa_sample_pallas(
            zp, b, t, t_pad, l, cache2d, page_table + l * n_pool, state_nsa_win, cmp_w)
        xp, st_p = trunk_layer(xp, conv0, c0, n0, m0, s0, nsa_p, *weights)
        xs, st_s = trunk_layer(xs, state_mlstm_conv[l], state_mlstm_C[l], state_mlstm_n[l],
                               state_mlstm_m[l], state_gla_S[l], nsa_s, *weights)
        for acc, a in zip(acc_p, st_p):
            acc.append(a.astype(xp.dtype))
        for acc, a in zip(acc_s, st_s):
            acc.append(a.astype(xs.dtype))
    nsa_rows_p, nsa_win_p, mlstm_c_p, mlstm_n_p, mlstm_m_p, mlstm_conv_p, gla_s_p = [jnp.stack(a) for a in acc_p]
    nsa_rows_s, nsa_win_s, mlstm_c_s, mlstm_n_s, mlstm_m_s, mlstm_conv_s, gla_s_s = [jnp.stack(a) for a in acc_s]
    return (xp, xs, nsa_rows_p, nsa_rows_s, nsa_win_p, nsa_win_s, mlstm_c_p, mlstm_c_s,
            mlstm_n_p, mlstm_n_s, mlstm_m_p, mlstm_m_s, mlstm_conv_p, mlstm_conv_s, gla_s_p, gla_s_s)
```

```python
import functools
import math

import jax
import jax.numpy as jnp
import numpy as np
from jax import lax
from jax.experimental import pallas as pl
from jax.experimental.pallas import tpu as pltpu

D_MODEL = 2048
DEPTH = 4
PAGE_SIZE = 128
HEAD_DIM = 128
D_MIX = D_MODEL
M_WIDTH = D_MIX // 4
G_WIDTH = D_MIX // 4
N_WIDTH = D_MIX - M_WIDTH - G_WIDTH
H_M = M_WIDTH // HEAD_DIM
H_N = N_WIDTH // HEAD_DIM
H_G = G_WIDTH // HEAD_DIM
KV_GROUPS = 2
Q_PER_KV = H_N // KV_GROUPS
D_FF = 4 * D_MODEL
CONV_W = 4
CHUNK = 64
CMP_STRIDE = 16
CMP_LEN = 2 * CMP_STRIDE
CMP_HID = 256
SEL_BLK = 64
N_SELECT = 16
WINDOW = 512
QBLK = 128
GATE_RANK = 16
GLA_GATE_TEMP = 16.0
ROPE_THETA = 10000.0
EPS = 1e-6
TINY = 1e-30
FORCE_BONUS = 1e3
NEG_BIG = -1e9
SPLIT_SIZES = (2 * M_WIDTH, M_WIDTH, M_WIDTH, 2 * H_M,
               N_WIDTH, 6 * KV_GROUPS * HEAD_DIM, 3 * H_N,
               G_WIDTH, G_WIDTH, G_WIDTH, G_WIDTH, GATE_RANK)
IN_COLS = sum(SPLIT_SIZES)
SPLIT_OFFSETS = tuple(int(o) for o in np.cumsum(SPLIT_SIZES)[:-1])

F32 = jnp.float32
BF16 = jnp.bfloat16

VMEM_LIMIT_BYTES = 56 * 1024 * 1024
IN_TN = 1024
COL_NQ = 2 * M_WIDTH + 2 * M_WIDTH
COL_NKV = COL_NQ + N_WIDTH
COL_GLA = COL_NKV + 6 * KV_GROUPS * HEAD_DIM
COL_SMALL = COL_GLA + 4 * G_WIDTH
SMALL_MIF = 0
SMALL_NG = 2 * H_M
SMALL_GLR = SMALL_NG + 3 * H_N
IN_COLS_PAD = 7168
NEG = -1e30
NSA_TQ = 128
NSA_TK = 512


def _rms(x, g):
    return x * lax.rsqrt(jnp.mean(x * x, axis=-1, keepdims=True) + EPS) * g


def _norm_matmul_kernel(x_ref, g_ref, w_ref, o_ref, xn_ref):
    @pl.when(pl.program_id(1) == 0)
    def _():
        xn_ref[...] = _rms(x_ref[...], g_ref[...]).astype(BF16)

    o_ref[...] = jnp.dot(xn_ref[...], w_ref[...], preferred_element_type=F32)


def norm_matmul(x, g, w, layer, tn):
    m, k = x.shape
    n = w.shape[2]
    tm = min(m, 1024)
    return pl.pallas_call(
        _norm_matmul_kernel,
        grid=(m // tm, n // tn),
        in_specs=[pl.BlockSpec((tm, k), lambda i, j: (i, 0)),
                  pl.BlockSpec((1, k), lambda i, j: (0, 0)),
                  pl.BlockSpec((None, k, tn), lambda i, j: (layer, 0, j))],
        out_specs=pl.BlockSpec((tm, tn), lambda i, j: (i, j)),
        out_shape=jax.ShapeDtypeStruct((m, n), F32),
        scratch_shapes=[pltpu.VMEM((tm, k), BF16)],
        compiler_params=pltpu.CompilerParams(
            dimension_semantics=("arbitrary", "arbitrary"), vmem_limit_bytes=VMEM_LIMIT_BYTES),
        name="norm_matmul",
    )(x, g.reshape(1, k), w)


def _matmul_norm_res_kernel(a0_ref, a1_ref, a2_ref, w_ref, g_ref, r_ref, o_ref):
    k0, k1 = a0_ref.shape[1], a0_ref.shape[1] + a1_ref.shape[1]
    y = (jnp.dot(a0_ref[...], w_ref[:k0, :], preferred_element_type=F32)
         + jnp.dot(a1_ref[...], w_ref[k0:k1, :], preferred_element_type=F32)
         + jnp.dot(a2_ref[...], w_ref[k1:, :], preferred_element_type=F32))
    o_ref[...] = r_ref[...] + _rms(y, g_ref[...])


def matmul_norm_res(a0, a1, a2, w, layer, g, r):
    m = a0.shape[0]
    _, k, n = w.shape
    tm = min(m, 512)
    rows = lambda a: pl.BlockSpec((tm, a.shape[1]), lambda i: (i, 0))
    return pl.pallas_call(
        _matmul_norm_res_kernel,
        grid=(m // tm,),
        in_specs=[rows(a0), rows(a1), rows(a2),
                  pl.BlockSpec((None, k, n), lambda i: (layer, 0, 0)),
                  pl.BlockSpec((1, n), lambda i: (0, 0)),
                  pl.BlockSpec((tm, n), lambda i: (i, 0))],
        out_specs=pl.BlockSpec((tm, n), lambda i: (i, 0)),
        out_shape=jax.ShapeDtypeStruct((m, n), F32),
        compiler_params=pltpu.CompilerParams(
            dimension_semantics=("arbitrary",), vmem_limit_bytes=VMEM_LIMIT_BYTES),
        name="matmul_norm_res",
    )(a0, a1, a2, w, g.reshape(1, n), r)


def _ffn_kernel(x_ref, g2_ref, w1_ref, w2_ref, g3_ref, o_ref, xn_ref, acc_ref):
    f = pl.program_id(1)

    @pl.when(f == 0)
    def _():
        xn_ref[...] = _rms(x_ref[...], g2_ref[...]).astype(BF16)
        acc_ref[...] = jnp.zeros_like(acc_ref)

    h = jnp.dot(xn_ref[...], w1_ref[...], preferred_element_type=F32)
    a = jnp.square(jnp.maximum(h, 0.0)).astype(BF16)
    acc_ref[...] += jnp.dot(a, w2_ref[...], preferred_element_type=F32)

    @pl.when(f == pl.num_programs(1) - 1)
    def _():
        o_ref[...] = x_ref[...] + _rms(acc_ref[...], g3_ref[...])


def ffn(x, g2, w1, w2, layer, g3):
    m, d = x.shape
    dff = w1.shape[2]
    tm = min(m, 512)
    tf = 1024
    return pl.pallas_call(
        _ffn_kernel,
        grid=(m // tm, dff // tf),
        in_specs=[pl.BlockSpec((tm, d), lambda i, f: (i, 0)),
                  pl.BlockSpec((1, d), lambda i, f: (0, 0)),
                  pl.BlockSpec((None, d, tf), lambda i, f: (layer, 0, f)),
                  pl.BlockSpec((None, tf, d), lambda i, f: (layer, f, 0)),
                  pl.BlockSpec((1, d), lambda i, f: (0, 0))],
        out_specs=pl.BlockSpec((tm, d), lambda i, f: (i, 0)),
        out_shape=jax.ShapeDtypeStruct((m, d), F32),
        scratch_shapes=[pltpu.VMEM((tm, d), BF16), pltpu.VMEM((tm, d), F32)],
        compiler_params=pltpu.CompilerParams(
            dimension_semantics=("arbitrary", "arbitrary"), vmem_limit_bytes=VMEM_LIMIT_BYTES),
        name="ffn",
    )(x, g2.reshape(1, d), w1, w2, g3.reshape(1, d))


def rope_tables(pos):
    half = HEAD_DIM // 2
    inv_freq = jnp.exp(-math.log(ROPE_THETA) * jnp.arange(half, dtype=F32) / half)
    ang = pos.astype(F32)[:, None] * inv_freq[None, :]
    cos, sin = jnp.cos(ang), jnp.sin(ang)
    return jnp.concatenate([cos, cos], axis=-1), jnp.concatenate([-sin, sin], axis=-1)


def _rope(x, cos2, sin2):
    return x * cos2 + pltpu.roll(x, HEAD_DIM // 2, 1) * sin2


def _nsa_prep_kernel(nq_ref, nkv_ref, cos_ref, sin_ref, q_ref, rows_ref, win_ref, kvb_ref, cmp_ref):
    cos2 = cos_ref[...]
    sin2 = sin_ref[...]
    for h in range(H_N):
        sl = slice(h * HEAD_DIM, (h + 1) * HEAD_DIM)
        q_ref[:, sl] = (_rope(nq_ref[:, sl], cos2, sin2) * (HEAD_DIM ** -0.5)).astype(BF16)
    for c in range(6 * KV_GROUPS):
        slot = c // KV_GROUPS
        x = nkv_ref[:, c * HEAD_DIM:(c + 1) * HEAD_DIM]
        if slot % 2 == 0:
            x = _rope(x, cos2, sin2)
        xb = x.astype(BF16)
        tm = x.shape[0]
        if slot < 4:
            rows_ref[pl.ds(c, tm, stride=ROW_VECS), :] = x
        else:
            win_ref[pl.ds(c - ROW_VECS, tm, stride=WIN_VECS), :] = x
        if slot < 2:
            cmp_ref[0, c] = xb
        else:
            kvb_ref[:, (c - 4) * HEAD_DIM:(c - 3) * HEAD_DIM] = xb


def nsa_prep_pallas(z, b, s, cos2, sin2, tm=512):
    m = b * s
    nsb = s // tm
    return pl.pallas_call(
        _nsa_prep_kernel,
        grid=(m // tm,),
        in_specs=[pl.BlockSpec((tm, N_WIDTH), lambda i: (i, COL_NQ // N_WIDTH)),
                  pl.BlockSpec((tm, 1536), lambda i: (i, COL_NKV // 1536)),
                  pl.BlockSpec((tm, HEAD_DIM), lambda i: (i % nsb, 0)),
                  pl.BlockSpec((tm, HEAD_DIM), lambda i: (i % nsb, 0))],
        out_specs=[pl.BlockSpec((tm, N_WIDTH), lambda i: (i, 0)),
                   pl.BlockSpec((tm * 8, HEAD_DIM), lambda i: (i, 0)),
                   pl.BlockSpec((tm * 4, HEAD_DIM), lambda i: (i, 0)),
                   pl.BlockSpec((tm, 1024), lambda i: (i, 0)),
                   pl.BlockSpec((1, 4, tm, HEAD_DIM), lambda i: (i // nsb, 0, i % nsb, 0))],
        out_shape=[jax.ShapeDtypeStruct((m, N_WIDTH), BF16),
                   jax.ShapeDtypeStruct((m * 8, HEAD_DIM), F32),
                   jax.ShapeDtypeStruct((m * 4, HEAD_DIM), F32),
                   jax.ShapeDtypeStruct((m, 1024), BF16),
                   jax.ShapeDtypeStruct((b, 4, s, HEAD_DIM), BF16)],
        compiler_params=pltpu.CompilerParams(
            dimension_semantics=("arbitrary",), vmem_limit_bytes=VMEM_LIMIT_BYTES),
        name="nsa_prep",
    )(z, z, cos2, sin2)


def _nsa_cmp_kernel(x_ref, pe_ref, w1_ref, w2_ref, o_ref):
    x = x_ref[0, 0]
    nb = x.shape[0]
    half = CMP_STRIDE * HEAD_DIM
    first = jnp.dot(x, w1_ref[0, :half, :], preferred_element_type=F32)
    second = jnp.dot(x, w1_ref[0, half:, :], preferred_element_type=F32)
    pe = jnp.broadcast_to(pe_ref[0], (8, 2 * half)).astype(BF16)
    bias = jnp.dot(pe, w1_ref[0], preferred_element_type=F32)[0:1]
    hid = first + pltpu.roll(second, nb - 1, 0) + bias
    o_ref[0, 0] = jnp.dot(jax.nn.gelu(hid).astype(BF16), w2_ref[0], preferred_element_type=F32).astype(BF16)


def nsa_compress_prompt(cmp_in, pe, w1, w2):
    b, _, nb, kdim = cmp_in.shape
    return pl.pallas_call(
        _nsa_cmp_kernel,
        grid=(b, 4),
        in_specs=[pl.BlockSpec((1, 1, nb, kdim), lambda i, c: (i, c, 0, 0)),
                  pl.BlockSpec((1, 1, 2 * kdim), lambda i, c: (c // 2, 0, 0)),
                  pl.BlockSpec((1, 2 * kdim, CMP_HID), lambda i, c: (c // 2, 0, 0)),
                  pl.BlockSpec((1, CMP_HID, HEAD_DIM), lambda i, c: (c // 2, 0, 0))],
        out_specs=pl.BlockSpec((1, 1, nb, HEAD_DIM), lambda i, c: (i, c, 0, 0)),
        out_shape=jax.ShapeDtypeStruct((b, 4, nb, HEAD_DIM), BF16),
        compiler_params=pltpu.CompilerParams(
            dimension_semantics=("arbitrary", "arbitrary"), vmem_limit_bytes=VMEM_LIMIT_BYTES),
        name="nsa_compress",
    )(cmp_in, pe, w1, w2)


def _softmax_rows(s, mask):
    sm = jnp.where(mask, s, NEG)
    m = jnp.max(sm, axis=-1, keepdims=True)
    p = jnp.where(mask, jnp.exp(sm - m), 0.0)
    return p / jnp.maximum(jnp.sum(p, axis=-1, keepdims=True), TINY)


_NT = (((1,), (1,)), ((), ()))


def _nsa_attn_kernel(q_ref, kc_ref, vc_ref, ks_ref, vs_ref, kw_ref, vw_ref, gate_ref, o_ref):
    tq = NSA_TQ
    r4 = Q_PER_KV
    g = pl.program_id(1)
    i = pl.program_id(2)
    q4 = q_ref[...]
    q = jnp.concatenate([q4[:, r * HEAD_DIM:(r + 1) * HEAD_DIM] for r in range(r4)], axis=0)
    q0 = i * tq
    tpos = q0 + lax.broadcasted_iota(jnp.int32, (tq, 1), 0)

    ncp = kc_ref.shape[2]
    s = lax.dot_general(q, kc_ref[0, 0], _NT, preferred_element_type=F32).reshape(r4, tq, ncp)
    cend = lax.broadcasted_iota(jnp.int32, (tq, ncp), 1) * CMP_STRIDE + (CMP_LEN - 1)
    p_c = _softmax_rows(s, (cend <= tpos)[None])
    o_c = jnp.dot(p_c.reshape(r4 * tq, ncp).astype(BF16), vc_ref[0, 0], preferred_element_type=F32)

    psum = p_c[0] + p_c[1] + p_c[2] + p_c[3]
    n_i = lax.broadcasted_iota(jnp.int32, (ncp, HEAD_DIM), 0)
    j_i = lax.broadcasted_iota(jnp.int32, (ncp, HEAD_DIM), 1)
    ratio = SEL_BLK // CMP_STRIDE
    c2s = jnp.where((n_i >= ratio * j_i - 1) & (n_i <= ratio * j_i + ratio - 1) & (n_i < ncp - 1)
                    & (j_i < ncp // ratio), 1.0, 0.0).astype(BF16)
    p_hi = psum.astype(BF16)
    p_lo = (psum - p_hi.astype(F32)).astype(BF16)
    imp = (jnp.dot(p_hi, c2s, preferred_element_type=F32)
           + jnp.dot(p_lo, c2s, preferred_element_type=F32))

    ns = ncp // ratio
    imp_t = jnp.transpose(imp)[:ns]
    jj = lax.broadcasted_iota(jnp.int32, (ns, tq), 0)
    tt = q0 + lax.broadcasted_iota(jnp.int32, (ns, tq), 1)
    cur = tt // SEL_BLK
    forced = (jj == 0) | (jj == cur) | (jj == cur - 1)
    score = jnp.where(jj * SEL_BLK <= tt, imp_t + jnp.where(forced, FORCE_BONUS, 0.0), NEG_BIG)
    rank = jnp.zeros((ns, tq), F32)
    for jp in range(ns):
        row = score[jp:jp + 1, :]
        rank = rank + jnp.where(row > score, 1.0, jnp.where((row == score) & (jj > jp), 1.0, 0.0))
    sel_t = jnp.where(rank < float(N_SELECT), 1.0, 0.0)
    sel_t = jnp.concatenate([sel_t, jnp.zeros((HEAD_DIM - ns, tq), F32)], axis=0)
    sel = jnp.transpose(sel_t)

    tk = NSA_TK
    n_tiles = (q0 + tq + tk - 1) // tk
    unpicked = jnp.where(sel > 0.5, 0.0, NEG).astype(BF16)
    q_aug = jnp.concatenate([q, jnp.concatenate([unpicked] * r4, axis=0)], axis=1)
    blk_lane = lax.broadcasted_iota(jnp.int32, (tk, HEAD_DIM), 1)
    blk_of_key = lax.broadcasted_iota(jnp.int32, (tk, HEAD_DIM), 0) // SEL_BLK

    hc = r4
    chains = r4 // hc
    q_chain = [q_aug[c * hc * tq:(c + 1) * hc * tq] for c in range(chains)]

    def tile_update(states, kt, causal):
        k0 = pl.multiple_of(kt * tk, tk)
        onehot = jnp.where(blk_lane == kt * (tk // SEL_BLK) + blk_of_key, 1.0, 0.0).astype(BF16)
        k_aug = jnp.concatenate([ks_ref[pl.ds(k0, tk), :], onehot], axis=1)
        v_tile = vs_ref[pl.ds(k0, tk), :]
        out = []
        for c in range(chains):
            m_run, l_run, acc = states[c]
            s = lax.dot_general(q_chain[c], k_aug, _NT, preferred_element_type=F32).reshape(hc, tq, tk)
            if causal is not None:
                s = jnp.where(causal, s, NEG)
            m_new = jnp.maximum(m_run, jnp.max(s, axis=-1, keepdims=True))
            alpha = jnp.exp(m_run - m_new)
            p = jnp.exp(s - m_new)
            l_new = alpha * l_run + jnp.sum(p, axis=-1, keepdims=True)
            pv = jnp.dot(p.reshape(hc * tq, tk).astype(BF16), v_tile, preferred_element_type=F32)
            out.append((m_new, l_new, alpha.reshape(hc * tq, 1) * acc + pv))
        return tuple(out)

    init = tuple((jnp.full((hc, tq, 1), NEG, F32), jnp.zeros((hc, tq, 1), F32),
                  jnp.zeros((hc * tq, HEAD_DIM), F32)) for _ in range(chains))
    states = lax.fori_loop(0, n_tiles - 1, lambda kt, st: tile_update(st, kt, None), init)
    kpos = (n_tiles - 1) * tk + lax.broadcasted_iota(jnp.int32, (tq, tk), 1)
    states = tile_update(states, n_tiles - 1, (kpos <= tpos)[None])
    o_s = jnp.concatenate([acc / jnp.maximum(l_run, TINY).reshape(hc * tq, 1) for _, l_run, acc in states], axis=0)

    wb = WINDOW + tq
    w0 = pl.multiple_of(jnp.maximum(q0 - WINDOW, 0), tq)
    s = lax.dot_general(q, kw_ref[pl.ds(w0, wb), :], _NT, preferred_element_type=F32).reshape(r4, tq, wb)
    dist = tpos - (w0 + lax.broadcasted_iota(jnp.int32, (tq, wb), 1))
    wmask = ((dist >= 0) & (dist < WINDOW))[None]
    sm = jnp.where(wmask, s, NEG)
    p_w = jnp.where(wmask, jnp.exp(sm - jnp.max(sm, axis=-1, keepdims=True)), 0.0)
    den_w = jnp.maximum(jnp.sum(p_w, axis=-1, keepdims=True), TINY).reshape(r4 * tq, 1)
    o_w = jnp.dot(p_w.reshape(r4 * tq, wb).astype(BF16), vw_ref[pl.ds(w0, wb), :],
                  preferred_element_type=F32) / den_w

    gates = jax.nn.sigmoid(gate_ref[...])
    for r in range(r4):
        def gate(c):
            lane0 = SMALL_NG + 3 * r + c
            lane1 = lane0 + 3 * r4
            return jnp.where(g == 0, gates[:, lane0:lane0 + 1], gates[:, lane1:lane1 + 1])
        rows = slice(r * tq, (r + 1) * tq)
        o = gate(0) * o_c[rows] + gate(1) * o_s[rows] + gate(2) * o_w[rows]
        o_ref[:, r * HEAD_DIM:(r + 1) * HEAD_DIM] = o.astype(BF16)


def nsa_attention_prompt(q_b, kcvc, kvb, z, b, s):
    tq = NSA_TQ
    nq = s // tq
    ncp = kcvc.shape[2]
    gw = Q_PER_KV * HEAD_DIM
    seq_block = lambda col: pl.BlockSpec((s, HEAD_DIM), lambda bi, g, i: (bi, col + g))
    return pl.pallas_call(
        _nsa_attn_kernel,
        grid=(b, KV_GROUPS, nq),
        in_specs=[pl.BlockSpec((tq, gw), lambda bi, g, i: (bi * nq + i, g)),
                  pl.BlockSpec((1, 1, ncp, HEAD_DIM), lambda bi, g, i: (bi, g, 0, 0)),
                  pl.BlockSpec((1, 1, ncp, HEAD_DIM), lambda bi, g, i: (bi, KV_GROUPS + g, 0, 0)),
                  seq_block(0), seq_block(2), seq_block(4), seq_block(6),
                  pl.BlockSpec((tq, HEAD_DIM), lambda bi, g, i: (bi * nq + i, COL_SMALL // HEAD_DIM))],
        out_specs=pl.BlockSpec((tq, gw), lambda bi, g, i: (bi * nq + i, g)),
        out_shape=jax.ShapeDtypeStruct((b * s, N_WIDTH), BF16),
        compiler_params=pltpu.CompilerParams(
            dimension_semantics=("arbitrary", "arbitrary", "arbitrary"), vmem_limit_bytes=VMEM_LIMIT_BYTES),
        name="nsa_attention",
    )(q_b, kcvc, kcvc, kvb, kvb, kvb, kvb, z)


def nsa_prompt_pallas(z, b, s, cmp_w):
    pe, w1, w2 = cmp_w
    cos2, sin2 = rope_tables(jnp.arange(s))
    q_b, rows_f, win_f, kvb, cmp_in = nsa_prep_pallas(z, b, s, cos2, sin2)
    kcvc = nsa_compress_prompt(cmp_in.reshape(b, 4, s // CMP_STRIDE, CMP_STRIDE * HEAD_DIM), pe, w1, w2)
    hn = nsa_attention_prompt(q_b, kcvc, kvb, z, b, s)
    rows = rows_f.reshape(b, s, 4, KV_GROUPS, HEAD_DIM)
    win = win_f.reshape(b, s, 2, KV_GROUPS, HEAD_DIM)[:, -min(WINDOW, s):]
    return hn, (rows, win)


SAMPLE_PAGES = 16
SAMPLE_TS = 16
ROW_COLS = 4 * KV_GROUPS * HEAD_DIM
ROW_VECS = 4 * KV_GROUPS
WIN_VECS = 2 * KV_GROUPS
NEG_DEAD = -3e38


def _page_specs():
    def spec(k):
        return pl.BlockSpec((PAGE_SIZE * ROW_VECS, HEAD_DIM), lambda bi, i, pt: (pt[bi, i * SAMPLE_PAGES + k], 0))
    return [spec(k) for k in range(SAMPLE_PAGES)]


def _page_vecs(pg, vec):
    return pg[pl.ds(vec, PAGE_SIZE, stride=ROW_VECS), :]


def _nsa_cmp_sample_kernel(pt_ref, *refs):
    del pt_ref
    pages = refs[:SAMPLE_PAGES]
    pe_ref, w1_ref, w2_ref, o_ref, carry_sc, bias_sc = refs[SAMPLE_PAGES:]
    i = pl.program_id(1)
    half = CMP_STRIDE * HEAD_DIM
    nbp = PAGE_SIZE // CMP_STRIDE
    nb = SAMPLE_PAGES * nbp

    @pl.when(i == 0)
    def _():
        carry_sc[...] = jnp.zeros_like(carry_sc)
        for kv in range(2):
            pe = jnp.broadcast_to(pe_ref[kv], (8, 2 * half)).astype(BF16)
            bias_sc[kv] = jnp.dot(pe, w1_ref[kv], preferred_element_type=F32)

    row = lax.broadcasted_iota(jnp.int32, (nb, CMP_HID), 0)
    for kv in range(2):
        def flat(c):
            piece = lambda pg, j: pg[pl.ds(j * ROW_VECS + c, nbp, stride=CMP_STRIDE * ROW_VECS), :]
            return jnp.concatenate(
                [jnp.concatenate([piece(pg, j) for j in range(CMP_STRIDE)], axis=1) for pg in pages],
                axis=0)
        x = jnp.concatenate([flat(kv * KV_GROUPS + g) for g in range(KV_GROUPS)], axis=0).astype(BF16)
        first_all = jnp.dot(x, w1_ref[kv, :half, :], preferred_element_type=F32)
        second_all = jnp.dot(x, w1_ref[kv, half:, :], preferred_element_type=F32)
        for g in range(KV_GROUPS):
            c = kv * KV_GROUPS + g
            first = first_all[g * nb:(g + 1) * nb]
            second = second_all[g * nb:(g + 1) * nb]
            shifted = jnp.where(row == 0, carry_sc[c, 7:8, :], pltpu.roll(first, 1, 0))
            hid = shifted + second + bias_sc[kv, 0:1, :]
            o_ref[0, c] = jnp.dot(jax.nn.gelu(hid).astype(BF16), w2_ref[kv],
                                  preferred_element_type=F32).astype(BF16)
            carry_sc[c] = first[nb - 8:, :]


def nsa_compress_sample(pt, cache2d, pe, w1, w2, db):
    n_pages = pt.shape[1]
    nb_all = n_pages * (PAGE_SIZE // CMP_STRIDE)
    nb = SAMPLE_PAGES * (PAGE_SIZE // CMP_STRIDE)
    const3 = lambda bi, i, pt: (0, 0, 0)
    grid_spec = pltpu.PrefetchScalarGridSpec(
        num_scalar_prefetch=1,
        grid=(db, n_pages // SAMPLE_PAGES),
        in_specs=_page_specs() + [pl.BlockSpec(pe.shape, const3), pl.BlockSpec(w1.shape, const3),
                                   pl.BlockSpec(w2.shape, const3)],
        out_specs=pl.BlockSpec((1, 2 * KV_GROUPS, nb, HEAD_DIM), lambda bi, i, pt: (bi, 0, i, 0)),
        scratch_shapes=[pltpu.VMEM((2 * KV_GROUPS, 8, CMP_HID), F32), pltpu.VMEM((2, 8, CMP_HID), F32)])
    return pl.pallas_call(
        _nsa_cmp_sample_kernel,
        grid_spec=grid_spec,
        out_shape=jax.ShapeDtypeStruct((db, 2 * KV_GROUPS, nb_all, HEAD_DIM), BF16),
        compiler_params=pltpu.CompilerParams(
            dimension_semantics=("arbitrary", "arbitrary"), vmem_limit_bytes=VMEM_LIMIT_BYTES),
        name="nsa_compress_sample",
    )(pt, *([cache2d] * SAMPLE_PAGES), pe, w1, w2)


def _nsa_attn_sample_kernel(pt_ref, *refs, past_len, t_valid, n_steps):
    del pt_ref
    pages = refs[:SAMPLE_PAGES]
    (q_ref, new_ref, kc_ref, win_ref, gate_ref, c2s_ref, o_ref,
     sel_sc, m_sc, l_sc, acc_sc, oc_sc, ow_sc) = refs[SAMPLE_PAGES:]
    i = pl.program_id(1)
    ts = SAMPLE_TS
    r4 = Q_PER_KV
    rows = r4 * ts
    d = HEAD_DIM
    blocks_per_step = SAMPLE_PAGES * PAGE_SIZE // SEL_BLK
    ns = past_len // SEL_BLK + 1
    trow = lax.broadcasted_iota(jnp.int32, (rows, 1), 0) % ts
    qpos = past_len + trow

    def q_of(g):
        return jnp.concatenate([q_ref[:, (g * r4 + r) * d:(g * r4 + r + 1) * d] for r in range(r4)], axis=0)

    @pl.when(i == 0)
    def _():
        ncp = kc_ref.shape[2]
        nsl = (n_steps + 1) * d
        c2s = c2s_ref[...]
        lane_r = lax.broadcasted_iota(jnp.int32, (ts, nsl), 1)
        blk_r = (lane_r // d) * blocks_per_step + lane_r % d
        live = (lane_r % d < blocks_per_step) & (blk_r < ns)
        blk_f = blk_r.astype(F32)
        qpos_t = past_len + lax.broadcasted_iota(jnp.int32, (ts, 1), 0)
        cur = qpos_t // SEL_BLK
        forced = (blk_r == 0) | (blk_r == cur) | (blk_r == cur - 1)
        midx = lax.broadcasted_iota(jnp.int32, (rows, ncp), 1)
        cmask = (midx >= 1) & ((midx - 1) * CMP_STRIDE + CMP_LEN - 1 <= qpos)
        sidx = lax.broadcasted_iota(jnp.int32, (rows, ts), 1)
        new_ok = (sidx <= trow) & (sidx < t_valid)
        wl = win_ref.shape[0] // WIN_VECS
        dist_buf = trow + wl - lax.broadcasted_iota(jnp.int32, (rows, wl), 1)
        buf_ok = (dist_buf >= 0) & (dist_buf < WINDOW)
        dist_new = trow - sidx
        wnew_ok = (dist_new >= 0) & (dist_new < WINDOW) & (sidx < t_valid)
        for g in range(KV_GROUPS):
            qg = q_of(g)
            s = lax.dot_general(qg, kc_ref[0, g], _NT, preferred_element_type=F32)
            p_c = _softmax_rows(s, cmask)
            oc_sc[g] = jnp.dot(p_c.astype(BF16), kc_ref[0, KV_GROUPS + g], preferred_element_type=F32)
            psum = p_c[0:ts] + p_c[ts:2 * ts] + p_c[2 * ts:3 * ts] + p_c[3 * ts:4 * ts]
            p_hi = psum.astype(BF16)
            p_lo = (psum - p_hi.astype(F32)).astype(BF16)
            imp = (jnp.dot(p_hi, c2s, preferred_element_type=F32) + jnp.dot(p_lo, c2s, preferred_element_type=F32))
            score = jnp.where(live, jnp.where(blk_r * SEL_BLK <= qpos_t, imp + jnp.where(forced, FORCE_BONUS, 0.0),
                                              NEG_BIG), NEG_DEAD)
            sel = jnp.zeros((ts, nsl), F32)
            for _ in range(N_SELECT):
                top = jnp.max(score, axis=1, keepdims=True)
                first = jnp.min(jnp.where(score == top, blk_f, 1e9), axis=1, keepdims=True)
                hit = live & (blk_f == first)
                sel = jnp.where(hit, 1.0, sel)
                score = jnp.where(hit, NEG_DEAD, score)
            for step in range(n_steps):
                sel_sc[g, step] = jnp.concatenate([sel[:, step * d:(step + 1) * d]] * r4, axis=0)
            knew = new_ref[:, g * d:(g + 1) * d]
            vnew = new_ref[:, (KV_GROUPS + g) * d:(KV_GROUPS + g + 1) * d]
            sm = jnp.where(new_ok, lax.dot_general(qg, knew, _NT, preferred_element_type=F32), NEG)
            m0 = jnp.max(sm, axis=1, keepdims=True)
            p = jnp.where(new_ok, jnp.exp(sm - m0), 0.0)
            m_sc[g] = m0
            l_sc[g] = jnp.sum(p, axis=1, keepdims=True)
            acc_sc[g] = jnp.dot(p.astype(BF16), vnew, preferred_element_type=F32)
            kwb = win_ref[pl.ds(g, wl, stride=WIN_VECS), :].astype(BF16)
            vwb = win_ref[pl.ds(KV_GROUPS + g, wl, stride=WIN_VECS), :].astype(BF16)
            kwn = new_ref[:, (2 * KV_GROUPS + g) * d:(2 * KV_GROUPS + g + 1) * d]
            vwn = new_ref[:, (3 * KV_GROUPS + g) * d:(3 * KV_GROUPS + g + 1) * d]
            s1 = jnp.where(buf_ok, lax.dot_general(qg, kwb, _NT, preferred_element_type=F32), NEG)
            s2 = jnp.where(wnew_ok, lax.dot_general(qg, kwn, _NT, preferred_element_type=F32), NEG)
            mw = jnp.maximum(jnp.max(s1, axis=1, keepdims=True), jnp.max(s2, axis=1, keepdims=True))
            p1 = jnp.where(buf_ok, jnp.exp(s1 - mw), 0.0)
            p2 = jnp.where(wnew_ok, jnp.exp(s2 - mw), 0.0)
            den = jnp.maximum(jnp.sum(p1, axis=1, keepdims=True) + jnp.sum(p2, axis=1, keepdims=True), TINY)
            ow_sc[g] = (jnp.dot((p1 / den).astype(BF16), vwb, preferred_element_type=F32)
                        + jnp.dot((p2 / den).astype(BF16), vwn, preferred_element_type=F32))

    half_lane = lax.broadcasted_iota(jnp.int32, (rows, PAGE_SIZE), 1) < SEL_BLK
    for g in range(KV_GROUPS):
        qg = q_of(g)
        kt = jnp.concatenate([_page_vecs(pg, 2 * KV_GROUPS + g) for pg in pages], axis=0).astype(BF16)
        vt = jnp.concatenate([_page_vecs(pg, 3 * KV_GROUPS + g) for pg in pages], axis=0).astype(BF16)
        s = lax.dot_general(qg, kt, _NT, preferred_element_type=F32)
        selg = sel_sc[g, i]
        picked = jnp.concatenate(
            [jnp.where(half_lane, selg[:, 2 * k:2 * k + 1], selg[:, 2 * k + 1:2 * k + 2])
             for k in range(SAMPLE_PAGES)], axis=1)
        mask = picked > 0.5
        sm = jnp.where(mask, s, NEG)
        m_old = m_sc[g]
        m_new = jnp.maximum(m_old, jnp.max(sm, axis=1, keepdims=True))
        alpha = jnp.exp(m_old - m_new)
        p = jnp.where(mask, jnp.exp(sm - m_new), 0.0)
        m_sc[g] = m_new
        l_sc[g] = alpha * l_sc[g] + jnp.sum(p, axis=1, keepdims=True)
        acc_sc[g] = alpha * acc_sc[g] + jnp.dot(p.astype(BF16), vt, preferred_element_type=F32)

    @pl.when(i == n_steps - 1)
    def _():
        gates = jax.nn.sigmoid(gate_ref[...])
        for g in range(KV_GROUPS):
            o_s = acc_sc[g] / jnp.maximum(l_sc[g], TINY)
            o_c = oc_sc[g]
            o_w = ow_sc[g]
            for r in range(r4):
                h = g * r4 + r
                gate = lambda c: gates[:, SMALL_NG + 3 * h + c:SMALL_NG + 3 * h + c + 1]
                rs = slice(r * ts, (r + 1) * ts)
                o = gate(0) * o_c[rs] + gate(1) * o_s[rs] + gate(2) * o_w[rs]
                o_ref[0, :, h * d:(h + 1) * d] = o.astype(BF16)


def nsa_attention_sample(pt, cache2d, q_b, kvb, kcvc, win2d, zp, layer, db, t_pad, t_valid, past_len):
    ts = SAMPLE_TS
    n_steps = pt.shape[1] // SAMPLE_PAGES
    rows = Q_PER_KV * ts
    wl_rows = win2d.shape[0] // (DEPTH * db)
    ncp = kcvc.shape[2]
    ns = past_len // SEL_BLK + 1
    per_step = SAMPLE_PAGES * PAGE_SIZE // SEL_BLK
    ratio = SEL_BLK // CMP_STRIDE
    lane = np.arange((n_steps + 1) * HEAD_DIM)[None, :]
    tok = np.arange(ncp)[:, None] - 1
    blk = (lane // HEAD_DIM) * per_step + lane % HEAD_DIM
    c2s = jnp.asarray(((lane % HEAD_DIM < per_step) & (blk < ns) & (tok >= 0) & (tok >= ratio * blk - 1)
                       & (tok <= ratio * blk + ratio - 1)).astype(np.float32), dtype=BF16)
    seq_rows = lambda width, col: pl.BlockSpec((ts, width), lambda bi, i, pt: (bi * (t_pad // ts), col))
    grid_spec = pltpu.PrefetchScalarGridSpec(
        num_scalar_prefetch=1,
        grid=(db, n_steps),
        in_specs=_page_specs() + [
            seq_rows(N_WIDTH, 0),
            seq_rows(ROW_COLS, 0),
            pl.BlockSpec((1,) + kcvc.shape[1:], lambda bi, i, pt: (bi, 0, 0, 0)),
            pl.BlockSpec((wl_rows, HEAD_DIM), lambda bi, i, pt: (layer * db + bi, 0)),
            seq_rows(HEAD_DIM, COL_SMALL // HEAD_DIM),
            pl.BlockSpec(c2s.shape, lambda bi, i, pt: (0, 0))],
        out_specs=pl.BlockSpec((1, ts, N_WIDTH), lambda bi, i, pt: (bi, 0, 0)),
        scratch_shapes=[pltpu.VMEM((KV_GROUPS, n_steps, rows, HEAD_DIM), F32),
                        pltpu.VMEM((KV_GROUPS, rows, 1), F32),
                        pltpu.VMEM((KV_GROUPS, rows, 1), F32),
                        pltpu.VMEM((KV_GROUPS, rows, HEAD_DIM), F32),
                        pltpu.VMEM((KV_GROUPS, rows, HEAD_DIM), F32),
                        pltpu.VMEM((KV_GROUPS, rows, HEAD_DIM), F32)])
    return pl.pallas_call(
        functools.partial(_nsa_attn_sample_kernel, past_len=past_len, t_valid=t_valid, n_steps=n_steps),
        grid_spec=grid_spec,
        out_shape=jax.ShapeDtypeStruct((db, ts, N_WIDTH), BF16),
        compiler_params=pltpu.CompilerParams(
            dimension_semantics=("arbitrary", "arbitrary"), vmem_limit_bytes=VMEM_LIMIT_BYTES),
        name="nsa_attention_sample",
    )(pt, *([cache2d] * SAMPLE_PAGES), q_b, kvb, kcvc, win2d, zp, c2s)


def nsa_sample_pallas(zp, db, t, t_pad, layer, cache2d, pt, win_state, cmp_w):
    pe, w1, w2 = cmp_w
    past_len = pt.shape[1] * PAGE_SIZE
    cos2, sin2 = rope_tables(past_len + jnp.arange(t_pad))
    q_b, rows_f, win_f, kvb, _ = nsa_prep_pallas(zp, db, t_pad, cos2, sin2, tm=t_pad)
    kcvc = nsa_compress_sample(pt, cache2d, pe, w1, w2, db)
    win2d = win_state.reshape(-1, HEAD_DIM)
    hn = nsa_attention_sample(pt, cache2d, q_b, kvb, kcvc, win2d, zp, layer, db, t_pad, t, past_len)
    rows = rows_f.reshape(db, t_pad, 4, KV_GROUPS, HEAD_DIM)[:, :t]
    win_new = win_f.reshape(db, t_pad, 2, KV_GROUPS, HEAD_DIM)[:, :t]
    wl = win_state.shape[2]
    win = jnp.concatenate([win_state[layer], win_new], axis=1)[:, -wl:]
    return hn[:, :t], (rows, win)


MIX_L = 128
GLA_SUB = 16


def _logsig(x):
    return jnp.minimum(x, 0.0) - jnp.log1p(jnp.exp(-jnp.abs(x)))


def _prefix_sum(x, axis):
    n = x.shape[axis]
    idx = lax.broadcasted_iota(jnp.int32, x.shape, axis)
    step = 1
    while step < n:
        x = x + jnp.where(idx >= step, pltpu.roll(x, step, axis), 0.0)
        step *= 2
    return x


def _mlstm_phases(t, uqk_ref, v_ref, og_ref, small_ref, convw_ref, bias_ref, norm_ref, conv0_ref, c0_ref, n0_ref,
                  m0_ref, h_ref, c_ref, n_ref, m_ref, prev_sc, c_sc, n_sc, m_sc, *, t_valid):
    L = MIX_L
    d = HEAD_DIM

    prev_sc[...] = jnp.zeros_like(prev_sc)
    prev_sc[L - 8:, :] = conv0_ref[0]
    c_sc[...] = c0_ref[0]
    n_sc[...] = n0_ref[0]
    m_sc[...] = m0_ref[0]
    yield

    x = uqk_ref[...]
    prev = prev_sc[...]
    row = lax.broadcasted_iota(jnp.int32, x.shape, 0)
    w = convw_ref[...]
    conv = w[CONV_W - 1:CONV_W] * x
    for k in range(1, CONV_W):
        shifted = jnp.where(row >= k, pltpu.roll(x, k, 0), pltpu.roll(prev, k, 0))
        conv = conv + w[CONV_W - 1 - k:CONV_W - k] * shifted
    prev_sc[...] = x
    act = conv * jax.nn.sigmoid(conv)

    pre = small_ref[...] + bias_ref[...]
    pos_c = t * L + lax.broadcasted_iota(jnp.int32, (L, HEAD_DIM), 0)
    ig_c = jnp.where(pos_c < t_valid, pre, NEG)
    b_c = _prefix_sum(jnp.where(pos_c < t_valid, _logsig(pre), 0.0), 0)
    pre_r = jnp.transpose(pre)[0:8]
    pos_r = t * L + lax.broadcasted_iota(jnp.int32, (8, L), 1)
    ig_r = jnp.where(pos_r < t_valid, pre_r, NEG)
    b_r = _prefix_sum(jnp.where(pos_r < t_valid, _logsig(pre_r), 0.0), 1)

    li = lax.broadcasted_iota(jnp.int32, (L, L), 0)
    si = lax.broadcasted_iota(jnp.int32, (L, L), 1)
    for h in range(H_M):
        sl = slice(h * d, (h + 1) * d)
        bcol, igcol = b_c[:, H_M + h:H_M + h + 1], ig_c[:, h:h + 1]
        brow, igrow = b_r[H_M + h:H_M + h + 1, :], ig_r[h:h + 1, :]
        m0 = m_sc[h:h + 1, 0:1]
        c0 = c_sc[h]
        n0 = n_sc[h:h + 1, :]
        qf = act[:, sl]
        kf = act[:, M_WIDTH + h * d:M_WIDTH + (h + 1) * d] * (d ** -0.5)
        vf = v_ref[:, sl]
        qb, kb, vb = qf.astype(BF16), kf.astype(BF16), vf.astype(BF16)
        logw = jnp.where(si <= li, bcol - brow + igrow, NEG)
        gsum = bcol + m0
        m_row = jnp.maximum(jnp.max(logw, axis=1, keepdims=True), gsum)
        wgt = jnp.exp(logw - m_row) * lax.dot_general(qb, kb, _NT, preferred_element_type=F32)
        inter = jnp.exp(gsum - m_row)
        num = (jnp.dot(wgt.astype(BF16), vb, preferred_element_type=F32)
               + inter * lax.dot_general(qb, c0.astype(BF16), _NT, preferred_element_type=F32))
        den = jnp.sum(wgt, axis=1, keepdims=True) + inter * jnp.sum(qf * n0, axis=1, keepdims=True)
        hh = num / jnp.maximum(jnp.abs(den), jnp.exp(-m_row))
        bl = bcol[L - 1:L, :]
        m_new = jnp.maximum(jnp.max(bl - brow + igrow, axis=1, keepdims=True), bl + m0)
        wa = jnp.exp(bl - bcol + igcol - m_new)
        wc = jnp.exp(bl + m0 - m_new)
        c_sc[h] = wc * c0 + jnp.dot(jnp.transpose(vf * wa).astype(BF16), kb, preferred_element_type=F32)
        n_sc[h:h + 1, :] = wc * n0 + jnp.sum(wa * kf, axis=0, keepdims=True)
        m_sc[h:h + 1, :] = jnp.broadcast_to(m_new, (1, d))
        out = _rms(hh, norm_ref[:, sl]) * jax.nn.sigmoid(og_ref[:, sl])
        h_ref[:, sl] = out.astype(BF16)
    yield

    c_ref[0] = c_sc[...]
    n_ref[0] = n_sc[...]
    m_ref[0] = m_sc[...]
    yield


def _gla_phases(t, q_ref, k_ref, v_ref, r_ref, small_ref, w2_ref, gb_ref, norm_ref, s0_ref, h_ref, s_ref, st_sc,
                *, t_valid):
    L = MIX_L
    d = HEAD_DIM

    for h in range(H_G):
        st_sc[h] = jnp.transpose(s0_ref[0, h])
    yield

    pre = jnp.dot(small_ref[...].astype(BF16), w2_ref[...], preferred_element_type=F32) + gb_ref[...]
    pos = t * L + lax.broadcasted_iota(jnp.int32, (L, G_WIDTH), 0)
    la = jnp.where(pos < t_valid, _logsig(pre) / GLA_GATE_TEMP, 0.0)
    bc_all = _prefix_sum(la, 0)
    rowi = lax.broadcasted_iota(jnp.int32, (L, d), 0)
    li = lax.broadcasted_iota(jnp.int32, (L, L), 0)
    si = lax.broadcasted_iota(jnp.int32, (L, L), 1)
    for h in range(H_G):
        sl = slice(h * d, (h + 1) * d)
        bc = bc_all[:, sl]
        qf = q_ref[:, sl] * (d ** -0.5)
        kf = jnp.where(t * L + rowi < t_valid, k_ref[:, sl], 0.0)
        vf = v_ref[:, sl]
        qparts, kparts = [], []
        for j in range(L // GLA_SUB):
            lo, hi = j * GLA_SUB, (j + 1) * GLA_SUB
            e = bc[hi - 1:hi, :]
            qparts.append((qf * jnp.exp(jnp.where(rowi >= lo, bc - e, NEG))).astype(BF16))
            kparts.append((kf * jnp.exp(jnp.where((rowi >= lo) & (rowi < hi), e - bc, NEG))).astype(BF16))
        a = lax.dot_general(jnp.concatenate(qparts, axis=1), jnp.concatenate(kparts, axis=1), _NT,
                            preferred_element_type=F32)
        a = jnp.where(si <= li, a, 0.0)
        st = st_sc[h]
        o = (jnp.dot(a.astype(BF16), vf.astype(BF16), preferred_element_type=F32)
             + lax.dot_general((qf * jnp.exp(bc)).astype(BF16), st.astype(BF16), _NT, preferred_element_type=F32))
        bl = bc[L - 1:L, :]
        kd = (kf * jnp.exp(bl - bc)).astype(BF16)
        st_sc[h] = st * jnp.exp(bl) + jnp.dot(jnp.transpose(vf).astype(BF16), kd, preferred_element_type=F32)
        gate = r_ref[:, sl]
        h_ref[:, sl] = (_rms(o, norm_ref[:, sl]) * (gate * jax.nn.sigmoid(gate))).astype(BF16)
    yield

    for h in range(H_G):
        s_ref[0, h] = jnp.transpose(st_sc[h])
    yield


N_MLSTM_IN, N_GLA_IN, N_MLSTM_OUT, N_GLA_OUT, N_MLSTM_SCRATCH = 11, 9, 4, 2, 4


def _mixers_kernel(*refs, t_valid):
    t = pl.program_id(1)
    cuts = np.cumsum([N_MLSTM_IN, N_GLA_IN, N_MLSTM_OUT, N_GLA_OUT, N_MLSTM_SCRATCH])
    m_in, g_in, m_out, g_out, m_scr, g_scr = (refs[a:b] for a, b in zip([0, *cuts], [*cuts, len(refs)]))
    phases = [_mlstm_phases(t, *m_in, *m_out, *m_scr, t_valid=t_valid),
              _gla_phases(t, *g_in, *g_out, *g_scr, t_valid=t_valid)]

    @pl.when(t == 0)
    def _():
        for p in phases:
            next(p)

    for p in phases:
        next(p)

    @pl.when(t == pl.num_programs(1) - 1)
    def _():
        for p in phases:
            next(p)


def mixers_pallas(z, b, t, conv_buf, c0, n0, m0, s0, conv_w, m_gate_b, m_norm, g_w2, g_b, g_norm):
    t_pad = z.shape[0] // b
    L = MIX_L
    nt = t_pad // L
    lanes = lambda a: jnp.pad(a.reshape(1, -1), ((0, 0), (0, HEAD_DIM - a.size)))
    conv0 = jnp.pad(conv_buf, ((0, 0), (8 - (CONV_W - 1), 0), (0, 0)))
    n0p = jnp.pad(n0, ((0, 0), (0, 8 - H_M), (0, 0)))
    m0p = jnp.broadcast_to(jnp.pad(m0, ((0, 0), (0, 8 - H_M)))[:, :, None], (b, 8, HEAD_DIM))
    w2_pad = jnp.pad(g_w2, ((SMALL_GLR, HEAD_DIM - SMALL_GLR - GATE_RANK), (0, 0))).astype(BF16)
    rowblk = lambda width, col: pl.BlockSpec((L, width), lambda bi, ti: (bi * nt + ti, col))
    const2 = lambda shape: pl.BlockSpec(shape, lambda bi, ti: (0, 0))
    per_b3 = pl.BlockSpec((1, 8, HEAD_DIM), lambda bi, ti: (bi, 0, 0))
    per_b4 = pl.BlockSpec((1, H_M, HEAD_DIM, HEAD_DIM), lambda bi, ti: (bi, 0, 0, 0))
    small = rowblk(HEAD_DIM, COL_SMALL // HEAD_DIM)
    gcol = COL_GLA // G_WIDTH
    mlstm_in = [rowblk(2 * M_WIDTH, 0), rowblk(M_WIDTH, 2), rowblk(M_WIDTH, 3), small,
                const2((CONV_W, 2 * M_WIDTH)), const2((1, HEAD_DIM)), const2((1, M_WIDTH)),
                pl.BlockSpec((1, 8, 2 * M_WIDTH), lambda bi, ti: (bi, 0, 0)), per_b4, per_b3, per_b3]
    gla_in = [rowblk(G_WIDTH, gcol), rowblk(G_WIDTH, gcol + 1), rowblk(G_WIDTH, gcol + 2), rowblk(G_WIDTH, gcol + 3),
              small, const2((HEAD_DIM, G_WIDTH)), const2((1, G_WIDTH)), const2((1, G_WIDTH)), per_b4]
    assert (len(mlstm_in), len(gla_in)) == (N_MLSTM_IN, N_GLA_IN)
    state = jax.ShapeDtypeStruct((b, H_M, HEAD_DIM, HEAD_DIM), F32)
    tile = jax.ShapeDtypeStruct((b, 8, HEAD_DIM), F32)
    hm, c, n, m, hg, s = pl.pallas_call(
        functools.partial(_mixers_kernel, t_valid=t),
        grid=(b, nt),
        in_specs=mlstm_in + gla_in,
        out_specs=[rowblk(M_WIDTH, 0), per_b4, per_b3, per_b3, rowblk(G_WIDTH, 0), per_b4],
        out_shape=[jax.ShapeDtypeStruct((b * t_pad, M_WIDTH), BF16), state, tile, tile,
                   jax.ShapeDtypeStruct((b * t_pad, G_WIDTH), BF16), state],
        scratch_shapes=[pltpu.VMEM((L, 2 * M_WIDTH), F32),
                        pltpu.VMEM((H_M, HEAD_DIM, HEAD_DIM), F32),
                        pltpu.VMEM((8, HEAD_DIM), F32),
                        pltpu.VMEM((8, HEAD_DIM), F32),
                        pltpu.VMEM((H_G, HEAD_DIM, HEAD_DIM), F32)],
        compiler_params=pltpu.CompilerParams(
            dimension_semantics=("arbitrary", "arbitrary"), vmem_limit_bytes=VMEM_LIMIT_BYTES),
        name="mixers",
    )(z, z, z, z, conv_w, lanes(m_gate_b), m_norm.reshape(1, -1), conv0, c0, n0p, m0p,
      z, z, z, z, z, w2_pad, g_b.reshape(1, -1), g_norm.reshape(1, -1), s0)
    valid = lambda a: a.reshape(b, t_pad, -1)[:, :t]
    return valid(hm), valid(hg), c, n[:, :H_M], m[:, :H_M, 0], s


def rmsnorm(x, g):
    xf = x.astype(F32)
    y = xf * lax.rsqrt(jnp.mean(xf * xf, axis=-1, keepdims=True) + EPS)
    return (y * g.astype(F32)).astype(x.dtype)


def rope(x, pos):
    half = x.shape[-1] // 2
    inv_freq = jnp.exp(-math.log(ROPE_THETA) * jnp.arange(half, dtype=F32) / half)
    ang = pos.astype(F32)[:, None] * inv_freq[None, :]
    cos = jnp.cos(ang)[:, None, :]
    sin = jnp.sin(ang)[:, None, :]
    xf = x.astype(F32)
    x1, x2 = xf[..., :half], xf[..., half:]
    return jnp.concatenate([x1 * cos - x2 * sin, x2 * cos + x1 * sin], axis=-1).astype(x.dtype)


def masked_softmax(s, mask):
    s = jnp.where(mask, s, -jnp.inf)
    m = jnp.max(s, axis=-1, keepdims=True)
    m = jnp.where(jnp.isfinite(m), m, 0.0)
    p = jnp.exp(s - m)
    return p / jnp.maximum(jnp.sum(p, axis=-1, keepdims=True), TINY)


def chunked_scan(step, carry, xs, chunk):
    b, t = xs[0].shape[:2]
    c = chunk if t % chunk == 0 else t
    n = t // c
    split = lambda a: jnp.moveaxis(a.reshape((b, n, c) + a.shape[2:]), 1, 0)
    carry, ys = lax.scan(step, carry, tuple(split(a) for a in xs))
    return carry, jnp.moveaxis(ys, 0, 1).reshape((b, t) + ys.shape[3:])


def mlstm_chunk(carry, inp):
    c0, n0, m0 = carry
    q, k, v, li, lf = inp
    L = q.shape[1]
    b = jnp.cumsum(lf, axis=1).transpose(0, 2, 1)
    ig = li.transpose(0, 2, 1)
    causal = jnp.tril(jnp.ones((L, L), dtype=bool))
    logw = jnp.where(causal, b[..., :, None] - b[..., None, :] + ig[..., None, :], -jnp.inf)
    g = b + m0[..., None]
    m_row = jnp.maximum(jnp.max(logw, axis=-1), g)
    w = jnp.exp(logw - m_row[..., None]) * jnp.einsum('blhd,bshd->bhls', q, k)
    inter = jnp.exp(g - m_row)
    num = jnp.einsum('bhls,bshd->bhld', w, v) + inter[..., None] * jnp.einsum('bhed,blhd->bhle', c0, q)
    den = jnp.sum(w, axis=-1) + inter * jnp.einsum('bhd,blhd->bhl', n0, q)
    h = num / jnp.maximum(jnp.abs(den), jnp.exp(-m_row))[..., None]
    bl = b[..., -1]
    a = bl[..., None] - b + ig
    m_new = jnp.maximum(jnp.max(a, axis=-1), bl + m0)
    wa = jnp.exp(a - m_new[..., None])
    wc = jnp.exp(bl + m0 - m_new)
    c_new = wc[..., None, None] * c0 + jnp.einsum('bhs,bshe,bshd->bhed', wa, v, k)
    n_new = wc[..., None] * n0 + jnp.einsum('bhs,bshd->bhd', wa, k)
    return (c_new, n_new, m_new), h.transpose(0, 2, 1, 3)


def gla_chunk(s0, inp):
    q, k, v, la = inp
    L = q.shape[1]
    bc = jnp.cumsum(la, axis=1)
    causal = jnp.tril(jnp.ones((L, L), dtype=bool))[None, :, :, None, None]
    decay = jnp.exp(jnp.where(causal, bc[:, :, None] - bc[:, None, :], -jnp.inf))
    a = jnp.einsum('bthk,bshk,btshk->bhts', q, k, decay)
    o = jnp.einsum('bhts,bshv->bthv', a, v) + jnp.einsum('bthk,bhkv->bthv', q * jnp.exp(bc), s0)
    bl = bc[:, -1]
    s_new = jnp.exp(bl)[..., None] * s0 + jnp.einsum('bshk,bshv->bhkv', k * jnp.exp(bl[:, None] - bc), v)
    return s_new, o


def nsa_compress(x, pe, w1, w2):
    b, l = x.shape[:2]
    nb = l // CMP_STRIDE
    xb = x[:, :nb * CMP_STRIDE].reshape(b, nb, CMP_STRIDE, KV_GROUPS, HEAD_DIM)
    first = jnp.einsum('bnjgd,jde->bnge', xb, w1[:CMP_STRIDE])
    second = jnp.einsum('bnjgd,jde->bnge', xb, w1[CMP_STRIDE:])
    hid = first[:, :-1] + second[:, 1:] + jnp.einsum('jd,jde->e', pe, w1)
    return jnp.einsum('bnge,ed->bngd', jax.nn.gelu(hid), w2)


def nsa_context(rows, cmp_params):
    pe, w1, w2 = cmp_params
    b, l = rows.shape[:2]
    kc = nsa_compress(rows[:, :, 0], pe[0], w1[0], w2[0])
    vc = nsa_compress(rows[:, :, 1], pe[1], w1[1], w2[1])
    nc = kc.shape[1]
    ns = -(-l // SEL_BLK)
    sel = jnp.pad(rows[:, :, 2:4], ((0, 0), (0, ns * SEL_BLK - l), (0, 0), (0, 0), (0, 0)))
    sel = sel.reshape(b, ns, SEL_BLK, 2, KV_GROUPS, HEAD_DIM).transpose(3, 0, 4, 1, 2, 5)
    cmp_start = jnp.arange(nc) * CMP_STRIDE
    cmp_end = cmp_start + CMP_LEN - 1
    sel_start = jnp.arange(ns) * SEL_BLK
    cmp_to_sel = ((cmp_start[:, None] < sel_start[None, :] + SEL_BLK)
                  & (cmp_end[:, None] >= sel_start[None, :])).astype(F32)
    return kc, vc, cmp_end, cmp_to_sel, sel[0], sel[1]


def nsa_query_block(q, qpos, gates, kc, vc, cmp_end, cmp_to_sel, ksb, vsb, kw, vw, kwpos):
    b, t = q.shape[:2]
    qg = q.reshape(b, t, KV_GROUPS, Q_PER_KV, HEAD_DIM) * (HEAD_DIM ** -0.5)
    s = jnp.einsum('btgrd,bngd->btgrn', qg, kc).astype(F32)
    p_c = masked_softmax(s, (cmp_end[None, :] <= qpos[:, None])[None, :, None, None, :])
    o_c = jnp.einsum('btgrn,bngd->btgrd', p_c.astype(vc.dtype), vc)
    ns = ksb.shape[2]
    imp = jnp.einsum('btgn,nj->btgj', jnp.sum(p_c, axis=3), cmp_to_sel)
    j = jnp.arange(ns)
    cur = qpos // SEL_BLK
    elig = (j * SEL_BLK)[None, :] <= qpos[:, None]
    forced = (j[None, :] == 0) | (j[None, :] == cur[:, None]) | (j[None, :] == cur[:, None] - 1)
    score = jnp.where(elig[None, :, None, :], imp + jnp.where(forced, FORCE_BONUS, 0.0)[None, :, None, :], NEG_BIG)
    _, idx = lax.top_k(score, min(N_SELECT, ns))
    n = idx.shape[-1]
    idx_g = idx.transpose(0, 2, 1, 3).reshape(b, KV_GROUPS, t * n)
    take = jax.vmap(jax.vmap(lambda blocks, ids: blocks[ids]))
    gk = take(ksb, idx_g).reshape(b, KV_GROUPS, t, n * SEL_BLK, HEAD_DIM)
    gv = take(vsb, idx_g).reshape(b, KV_GROUPS, t, n * SEL_BLK, HEAD_DIM)
    kpos = (idx[..., None] * SEL_BLK + jnp.arange(SEL_BLK)).reshape(b, t, KV_GROUPS, n * SEL_BLK)
    s = jnp.einsum('btgrd,bgtmd->btgrm', qg, gk).astype(F32)
    p_s = masked_softmax(s, (kpos <= qpos[None, :, None, None])[:, :, :, None, :])
    o_s = jnp.einsum('btgrm,bgtmd->btgrd', p_s.astype(gv.dtype), gv)
    s = jnp.einsum('btgrd,bkgd->btgrk', qg, kw).astype(F32)
    d = qpos[:, None] - kwpos[None, :]
    wmask = (kwpos[None, :] >= 0) & (d >= 0) & (d < WINDOW)
    p_w = masked_softmax(s, wmask[None, :, None, None, :])
    o_w = jnp.einsum('btgrk,bkgd->btgrd', p_w.astype(vw.dtype), vw)
    g = jax.nn.sigmoid(gates.astype(F32)).reshape(b, t, KV_GROUPS, Q_PER_KV, 3)
    o = g[..., 0:1] * o_c + g[..., 1:2] * o_s + g[..., 2:3] * o_w
    return o.reshape(b, t, H_N * HEAD_DIM).astype(q.dtype)


def nsa_prep(n_q, n_kv, n_g, pos):
    b, t = n_q.shape[:2]
    q = rope(n_q.reshape(b, t, H_N, HEAD_DIM), pos)
    kv = n_kv.reshape(b, t, 6, KV_GROUPS, HEAD_DIM)
    rows = jnp.stack([rope(kv[:, :, 0], pos), kv[:, :, 1], rope(kv[:, :, 2], pos), kv[:, :, 3]], axis=2)
    win = jnp.stack([rope(kv[:, :, 4], pos), kv[:, :, 5]], axis=2)
    return q, n_g.reshape(b, t, H_N, 3), rows, win


def nsa_prompt(n_q, n_kv, n_g, cmp_params):
    b, s = n_q.shape[:2]
    pos = jnp.arange(s)
    q, gates, rows, win = nsa_prep(n_q, n_kv, n_g, pos)
    kc, vc, cmp_end, cmp_to_sel, ksb, vsb = nsa_context(rows, cmp_params)
    nb = s // QBLK
    kwp = jnp.pad(win, ((0, 0), (WINDOW, 0), (0, 0), (0, 0), (0, 0)))

    def one_block(args):
        qi, gi, bi = args
        start = bi * QBLK
        qpos = start + jnp.arange(QBLK)
        band = lax.dynamic_slice_in_dim(kwp, start, WINDOW + QBLK, axis=1)
        kwpos = start - WINDOW + jnp.arange(WINDOW + QBLK)
        return nsa_query_block(qi, qpos, gi, kc, vc, cmp_end, cmp_to_sel, ksb, vsb,
                               band[:, :, 0], band[:, :, 1], kwpos)

    blocks = lambda a: jnp.moveaxis(a.reshape((b, nb, QBLK) + a.shape[2:]), 1, 0)
    out = lax.map(one_block, (blocks(q), blocks(gates), jnp.arange(nb)))
    out = jnp.moveaxis(out, 0, 1).reshape(b, s, H_N * HEAD_DIM)
    return out, (rows, win[:, -min(WINDOW, s):])


def nsa_sample(n_q, n_kv, n_g, past_rows, win_buf, cmp_params):
    b, t = n_q.shape[:2]
    past_len = past_rows.shape[1]
    pos = past_len + jnp.arange(t)
    q, gates, rows, win = nsa_prep(n_q, n_kv, n_g, pos)
    full = jnp.concatenate([past_rows.astype(rows.dtype), rows], axis=1)
    kc, vc, cmp_end, cmp_to_sel, ksb, vsb = nsa_context(full, cmp_params)
    wl = win_buf.shape[1]
    kw_all = jnp.concatenate([win_buf.astype(win.dtype), win], axis=1)
    kwpos = past_len - wl + jnp.arange(wl + t)
    out = nsa_query_block(q, pos, gates, kc, vc, cmp_end, cmp_to_sel, ksb, vsb,
                          kw_all[:, :, 0], kw_all[:, :, 1], kwpos)
    return out, (rows, kw_all[:, -wl:])


def to_heads(a, n_heads):
    return a.reshape(a.shape[0], a.shape[1], n_heads, -1).astype(F32)


def split_cols(z):
    cut = lambda a, n: z[..., a:a + n]
    u_qk, m_v, m_o = cut(0, 2 * M_WIDTH), cut(2 * M_WIDTH, M_WIDTH), cut(3 * M_WIDTH, M_WIDTH)
    n_q, n_kv = cut(COL_NQ, N_WIDTH), cut(COL_NKV, 6 * KV_GROUPS * HEAD_DIM)
    g_q, g_k, g_v, g_r = (cut(COL_GLA + i * G_WIDTH, G_WIDTH) for i in range(4))
    m_if = cut(COL_SMALL + SMALL_MIF, 2 * H_M)
    n_g = cut(COL_SMALL + SMALL_NG, 3 * H_N)
    g_lr = cut(COL_SMALL + SMALL_GLR, GATE_RANK)
    return u_qk, m_v, m_o, m_if, n_q, n_kv, n_g, g_q, g_k, g_v, g_r, g_lr


W_IN_ORDER = (0, 1, 2, 4, 5, 7, 8, 9, 10, 3, 6, 11)


def _regroup_w_in_kernel(w_ref, o_ref):
    o = (0,) + SPLIT_OFFSETS + (IN_COLS,)
    at = 0
    for i in W_IN_ORDER:
        width = o[i + 1] - o[i]
        o_ref[:, at:at + width] = w_ref[:, o[i]:o[i + 1]].astype(BF16)
        at += width
    o_ref[:, at:] = jnp.zeros((o_ref.shape[0], IN_COLS_PAD - at), BF16)


def regroup_w_in(w_in):
    depth, k, n = w_in.shape
    tm = 256
    return pl.pallas_call(
        _regroup_w_in_kernel,
        grid=(depth, k // tm),
        in_specs=[pl.BlockSpec((None, tm, n), lambda l, i: (l, i, 0))],
        out_specs=pl.BlockSpec((None, tm, IN_COLS_PAD), lambda l, i: (l, i, 0)),
        out_shape=jax.ShapeDtypeStruct((depth, k, IN_COLS_PAD), BF16),
        compiler_params=pltpu.CompilerParams(
            dimension_semantics=("arbitrary", "arbitrary"), vmem_limit_bytes=VMEM_LIMIT_BYTES),
        name="regroup_w_in",
    )(w_in)


def trunk_layer(x, conv_buf, c0, n0, m0, s0, nsa_fn, layer, g_norms, w_in, conv_w, m_gate_b, m_norm,
                g_w2, g_b, g_norm, w_out, w_ff1, w_ff2):
    b, t, d = x.shape
    x2 = x.reshape(b * t, d)
    z = norm_matmul(x2, g_norms[0], w_in, layer, IN_TN)
    z3 = z.reshape(b, t, -1)
    t_pad = -(-t // MIX_L) * MIX_L
    zp = z if t_pad == t else jnp.pad(z3, ((0, 0), (0, t_pad - t), (0, 0))).reshape(b * t_pad, -1)
    hm, hg, c, n, m, s = mixers_pallas(zp, b, t, conv_buf, c0, n0, m0, s0,
                                       conv_w, m_gate_b, m_norm, g_w2, g_b, g_norm)
    new_conv = z3[:, t - (CONV_W - 1):, :2 * M_WIDTH]
    hn, nsa_state = nsa_fn(zp, b, t, t_pad)
    x2 = matmul_norm_res(hm.reshape(b * t, -1), hn.reshape(b * t, -1), hg.reshape(b * t, -1),
                         w_out, layer, g_norms[1], x2)
    x2 = ffn(x2, g_norms[2], w_ff1, w_ff2, layer, g_norms[3])
    return x2.reshape(b, t, d), (nsa_state[0], nsa_state[1], c, n, m, new_conv, s)


def kernel(x_prompt, x_sample, cache_nsa_kv, state_nsa_win, state_mlstm_C, state_mlstm_n, state_mlstm_m, state_mlstm_conv, state_gla_S, page_table, norms, w_in, mlstm_conv_w, mlstm_gate_b, mlstm_norm, nsa_cmp_pe, nsa_cmp_w1, nsa_cmp_w2, gla_gate_w2, gla_gate_b, gla_norm, w_out, w_ff1, w_ff2):
    xp, xs = x_prompt, x_sample
    bp = xp.shape[0]
    db = xs.shape[0]
    past_len = page_table.shape[1] * PAGE_SIZE
    conv0 = jnp.zeros((bp, CONV_W - 1, 2 * M_WIDTH), xp.dtype)
    c0 = jnp.zeros((bp, H_M, HEAD_DIM, HEAD_DIM), F32)
    n0 = jnp.zeros((bp, H_M, HEAD_DIM), F32)
    m0 = jnp.zeros((bp, H_M), F32)
    s0 = jnp.zeros((bp, H_G, HEAD_DIM, HEAD_DIM), F32)
    w_in_b = regroup_w_in(w_in)
    w_out_b = w_out.astype(BF16)
    w_ff1_b = w_ff1.astype(BF16)
    w_ff2_b = w_ff2.astype(BF16)
    cmp_pe = nsa_cmp_pe.reshape(DEPTH, 2, 1, CMP_LEN * HEAD_DIM)
    cmp_w1_b = nsa_cmp_w1.reshape(DEPTH, 2, CMP_LEN * HEAD_DIM, CMP_HID).astype(BF16)
    cmp_w2_b = nsa_cmp_w2.astype(BF16)
    n_pool = cache_nsa_kv.shape[1]
    cache2d = cache_nsa_kv.reshape(-1, HEAD_DIM)
    acc_p = [[] for _ in range(7)]
    acc_s = [[] for _ in range(7)]
    for l in range(DEPTH):
        weights = (l, norms[l], w_in_b, mlstm_conv_w[l], mlstm_gate_b[l], mlstm_norm[l],
                   gla_gate_w2[l], gla_gate_b[l], gla_norm[l], w_out_b, w_ff1_b, w_ff2_b)
        cmp_w = (cmp_pe[l], cmp_w1_b[l], cmp_w2_b[l])
        nsa_p = lambda zp, b, t, t_pad, cmp_w=cmp_w: nsa_prompt_pallas(zp, b, t, cmp_w)
        nsa_s = lambda zp, b, t, t_pad, l=l, cmp_w=cmp_w: nsa_sample_pallas(
            zp, b, t, t_pad, l, cache2d, page_table + l * n_pool, state_nsa_win, cmp_w)
        xp, st_p = trunk_layer(xp, conv0, c0, n0, m0, s0, nsa_p, *weights)
        xs, st_s = trunk_layer(xs, state_mlstm_conv[l], state_mlstm_C[l], state_mlstm_n[l],
                               state_mlstm_m[l], state_gla_S[l], nsa_s, *weights)
        for acc, a in zip(acc_p, st_p):
            acc.append(a.astype(xp.dtype))
        for acc, a in zip(acc_s, st_s):
            acc.append(a.astype(xs.dtype))
    nsa_rows_p, nsa_win_p, mlstm_c_p, mlstm_n_p, mlstm_m_p, mlstm_conv_p, gla_s_p = [jnp.stack(a) for a in acc_p]
    nsa_rows_s, nsa_win_s, mlstm_c_s, mlstm_n_s, mlstm_m_s, mlstm_conv_s, gla_s_s = [jnp.stack(a) for a in acc_s]
    return (xp, xs, nsa_rows_p, nsa_rows_s, nsa_win_p, nsa_win_s, mlstm_c_p, mlstm_c_s,
            mlstm_n_p, mlstm_n_s, mlstm_m_p, mlstm_m_s, mlstm_conv_p, mlstm_conv_s, gla_s_p, gla_s_s)
```

```python
import functools
import math

import jax
import jax.numpy as jnp
import numpy as np
from jax import lax
from jax.experimental import pallas as pl
from jax.experimental.pallas import tpu as pltpu

D_MODEL = 2048
DEPTH = 4
PAGE_SIZE = 128
HEAD_DIM = 128
D_MIX = D_MODEL
M_WIDTH = D_MIX // 4
G_WIDTH = D_MIX // 4
N_WIDTH = D_MIX - M_WIDTH - G_WIDTH
H_M = M_WIDTH // HEAD_DIM
H_N = N_WIDTH // HEAD_DIM
H_G = G_WIDTH // HEAD_DIM
KV_GROUPS = 2
Q_PER_KV = H_N // KV_GROUPS
D_FF = 4 * D_MODEL
CONV_W = 4
CHUNK = 64
CMP_STRIDE = 16
CMP_LEN = 2 * CMP_STRIDE
CMP_HID = 256
SEL_BLK = 64
N_SELECT = 16
WINDOW = 512
QBLK = 128
GATE_RANK = 16
GLA_GATE_TEMP = 16.0
ROPE_THETA = 10000.0
EPS = 1e-6
TINY = 1e-30
FORCE_BONUS = 1e3
NEG_BIG = -1e9
SPLIT_SIZES = (2 * M_WIDTH, M_WIDTH, M_WIDTH, 2 * H_M,
               N_WIDTH, 6 * KV_GROUPS * HEAD_DIM, 3 * H_N,
               G_WIDTH, G_WIDTH, G_WIDTH, G_WIDTH, GATE_RANK)
IN_COLS = sum(SPLIT_SIZES)
SPLIT_OFFSETS = tuple(int(o) for o in np.cumsum(SPLIT_SIZES)[:-1])

F32 = jnp.float32
BF16 = jnp.bfloat16

VMEM_LIMIT_BYTES = 56 * 1024 * 1024
IN_TN = 1024
COL_NQ = 2 * M_WIDTH + 2 * M_WIDTH
COL_NKV = COL_NQ + N_WIDTH
COL_GLA = COL_NKV + 6 * KV_GROUPS * HEAD_DIM
COL_SMALL = COL_GLA + 4 * G_WIDTH
SMALL_MIF = 0
SMALL_NG = 2 * H_M
SMALL_GLR = SMALL_NG + 3 * H_N
IN_COLS_PAD = 7168
NEG = -1e30
NSA_TQ = 128
NSA_TK = 1024


def _rms(x, g):
    return x * lax.rsqrt(jnp.mean(x * x, axis=-1, keepdims=True) + EPS) * g


def _norm_matmul_kernel(x_ref, g_ref, w_ref, o_ref, xn_ref):
    @pl.when(pl.program_id(1) == 0)
    def _():
        xn_ref[...] = _rms(x_ref[...], g_ref[...]).astype(BF16)

    o_ref[...] = jnp.dot(xn_ref[...], w_ref[...], preferred_element_type=F32)


def norm_matmul(x, g, w, layer, tn):
    m, k = x.shape
    n = w.shape[2]
    tm = min(m, 1024)
    return pl.pallas_call(
        _norm_matmul_kernel,
        grid=(m // tm, n // tn),
        in_specs=[pl.BlockSpec((tm, k), lambda i, j: (i, 0)),
                  pl.BlockSpec((1, k), lambda i, j: (0, 0)),
                  pl.BlockSpec((None, k, tn), lambda i, j: (layer, 0, j))],
        out_specs=pl.BlockSpec((tm, tn), lambda i, j: (i, j)),
        out_shape=jax.ShapeDtypeStruct((m, n), F32),
        scratch_shapes=[pltpu.VMEM((tm, k), BF16)],
        compiler_params=pltpu.CompilerParams(
            dimension_semantics=("arbitrary", "arbitrary"), vmem_limit_bytes=VMEM_LIMIT_BYTES),
        name="norm_matmul",
    )(x, g.reshape(1, k), w)


def _matmul_norm_res_kernel(a0_ref, a1_ref, a2_ref, w_ref, g_ref, r_ref, o_ref):
    k0, k1 = a0_ref.shape[1], a0_ref.shape[1] + a1_ref.shape[1]
    y = (jnp.dot(a0_ref[...], w_ref[:k0, :], preferred_element_type=F32)
         + jnp.dot(a1_ref[...], w_ref[k0:k1, :], preferred_element_type=F32)
         + jnp.dot(a2_ref[...], w_ref[k1:, :], preferred_element_type=F32))
    o_ref[...] = r_ref[...] + _rms(y, g_ref[...])


def matmul_norm_res(a0, a1, a2, w, layer, g, r):
    m = a0.shape[0]
    _, k, n = w.shape
    tm = min(m, 512)
    rows = lambda a: pl.BlockSpec((tm, a.shape[1]), lambda i: (i, 0))
    return pl.pallas_call(
        _matmul_norm_res_kernel,
        grid=(m // tm,),
        in_specs=[rows(a0), rows(a1), rows(a2),
                  pl.BlockSpec((None, k, n), lambda i: (layer, 0, 0)),
                  pl.BlockSpec((1, n), lambda i: (0, 0)),
                  pl.BlockSpec((tm, n), lambda i: (i, 0))],
        out_specs=pl.BlockSpec((tm, n), lambda i: (i, 0)),
        out_shape=jax.ShapeDtypeStruct((m, n), F32),
        compiler_params=pltpu.CompilerParams(
            dimension_semantics=("arbitrary",), vmem_limit_bytes=VMEM_LIMIT_BYTES),
        name="matmul_norm_res",
    )(a0, a1, a2, w, g.reshape(1, n), r)


def _ffn_kernel(x_ref, g2_ref, w1_ref, w2_ref, g3_ref, o_ref, xn_ref, acc_ref):
    f = pl.program_id(1)

    @pl.when(f == 0)
    def _():
        xn_ref[...] = _rms(x_ref[...], g2_ref[...]).astype(BF16)
        acc_ref[...] = jnp.zeros_like(acc_ref)

    h = jnp.dot(xn_ref[...], w1_ref[...], preferred_element_type=F32)
    a = jnp.square(jnp.maximum(h, 0.0)).astype(BF16)
    acc_ref[...] += jnp.dot(a, w2_ref[...], preferred_element_type=F32)

    @pl.when(f == pl.num_programs(1) - 1)
    def _():
        o_ref[...] = x_ref[...] + _rms(acc_ref[...], g3_ref[...])


def ffn(x, g2, w1, w2, layer, g3):
    m, d = x.shape
    dff = w1.shape[2]
    tm = min(m, 512)
    tf = 1024
    return pl.pallas_call(
        _ffn_kernel,
        grid=(m // tm, dff // tf),
        in_specs=[pl.BlockSpec((tm, d), lambda i, f: (i, 0)),
                  pl.BlockSpec((1, d), lambda i, f: (0, 0)),
                  pl.BlockSpec((None, d, tf), lambda i, f: (layer, 0, f)),
                  pl.BlockSpec((None, tf, d), lambda i, f: (layer, f, 0)),
                  pl.BlockSpec((1, d), lambda i, f: (0, 0))],
        out_specs=pl.BlockSpec((tm, d), lambda i, f: (i, 0)),
        out_shape=jax.ShapeDtypeStruct((m, d), F32),
        scratch_shapes=[pltpu.VMEM((tm, d), BF16), pltpu.VMEM((tm, d), F32)],
        compiler_params=pltpu.CompilerParams(
            dimension_semantics=("arbitrary", "arbitrary"), vmem_limit_bytes=VMEM_LIMIT_BYTES),
        name="ffn",
    )(x, g2.reshape(1, d), w1, w2, g3.reshape(1, d))


def rope_tables(pos):
    half = HEAD_DIM // 2
    inv_freq = jnp.exp(-math.log(ROPE_THETA) * jnp.arange(half, dtype=F32) / half)
    ang = pos.astype(F32)[:, None] * inv_freq[None, :]
    cos, sin = jnp.cos(ang), jnp.sin(ang)
    return jnp.concatenate([cos, cos], axis=-1), jnp.concatenate([-sin, sin], axis=-1)


def _rope(x, cos2, sin2):
    return x * cos2 + pltpu.roll(x, HEAD_DIM // 2, 1) * sin2


def _nsa_prep_kernel(nq_ref, nkv_ref, cos_ref, sin_ref, q_ref, rows_ref, win_ref, kvb_ref, cmp_ref):
    cos2 = cos_ref[...]
    sin2 = sin_ref[...]
    for h in range(H_N):
        sl = slice(h * HEAD_DIM, (h + 1) * HEAD_DIM)
        q_ref[:, sl] = (_rope(nq_ref[:, sl], cos2, sin2) * (HEAD_DIM ** -0.5)).astype(BF16)
    for c in range(6 * KV_GROUPS):
        slot = c // KV_GROUPS
        x = nkv_ref[:, c * HEAD_DIM:(c + 1) * HEAD_DIM]
        if slot % 2 == 0:
            x = _rope(x, cos2, sin2)
        xb = x.astype(BF16)
        tm = x.shape[0]
        if slot < 4:
            rows_ref[pl.ds(c, tm, stride=ROW_VECS), :] = x
        else:
            win_ref[pl.ds(c - ROW_VECS, tm, stride=WIN_VECS), :] = x
        if slot < 2:
            cmp_ref[0, c] = xb
        else:
            kvb_ref[:, (c - 4) * HEAD_DIM:(c - 3) * HEAD_DIM] = xb


def nsa_prep_pallas(z, b, s, cos2, sin2, tm=512):
    m = b * s
    nsb = s // tm
    return pl.pallas_call(
        _nsa_prep_kernel,
        grid=(m // tm,),
        in_specs=[pl.BlockSpec((tm, N_WIDTH), lambda i: (i, COL_NQ // N_WIDTH)),
                  pl.BlockSpec((tm, 1536), lambda i: (i, COL_NKV // 1536)),
                  pl.BlockSpec((tm, HEAD_DIM), lambda i: (i % nsb, 0)),
                  pl.BlockSpec((tm, HEAD_DIM), lambda i: (i % nsb, 0))],
        out_specs=[pl.BlockSpec((tm, N_WIDTH), lambda i: (i, 0)),
                   pl.BlockSpec((tm * 8, HEAD_DIM), lambda i: (i, 0)),
                   pl.BlockSpec((tm * 4, HEAD_DIM), lambda i: (i, 0)),
                   pl.BlockSpec((tm, 1024), lambda i: (i, 0)),
                   pl.BlockSpec((1, 4, tm, HEAD_DIM), lambda i: (i // nsb, 0, i % nsb, 0))],
        out_shape=[jax.ShapeDtypeStruct((m, N_WIDTH), BF16),
                   jax.ShapeDtypeStruct((m * 8, HEAD_DIM), F32),
                   jax.ShapeDtypeStruct((m * 4, HEAD_DIM), F32),
                   jax.ShapeDtypeStruct((m, 1024), BF16),
                   jax.ShapeDtypeStruct((b, 4, s, HEAD_DIM), BF16)],
        compiler_params=pltpu.CompilerParams(
            dimension_semantics=("arbitrary",), vmem_limit_bytes=VMEM_LIMIT_BYTES),
        name="nsa_prep",
    )(z, z, cos2, sin2)


def _nsa_cmp_kernel(x_ref, pe_ref, w1_ref, w2_ref, o_ref):
    x = x_ref[0, 0]
    nb = x.shape[0]
    half = CMP_STRIDE * HEAD_DIM
    first = jnp.dot(x, w1_ref[0, :half, :], preferred_element_type=F32)
    second = jnp.dot(x, w1_ref[0, half:, :], preferred_element_type=F32)
    pe = jnp.broadcast_to(pe_ref[0], (8, 2 * half)).astype(BF16)
    bias = jnp.dot(pe, w1_ref[0], preferred_element_type=F32)[0:1]
    hid = first + pltpu.roll(second, nb - 1, 0) + bias
    o_ref[0, 0] = jnp.dot(jax.nn.gelu(hid).astype(BF16), w2_ref[0], preferred_element_type=F32).astype(BF16)


def nsa_compress_prompt(cmp_in, pe, w1, w2):
    b, _, nb, kdim = cmp_in.shape
    return pl.pallas_call(
        _nsa_cmp_kernel,
        grid=(b, 4),
        in_specs=[pl.BlockSpec((1, 1, nb, kdim), lambda i, c: (i, c, 0, 0)),
                  pl.BlockSpec((1, 1, 2 * kdim), lambda i, c: (c // 2, 0, 0)),
                  pl.BlockSpec((1, 2 * kdim, CMP_HID), lambda i, c: (c // 2, 0, 0)),
                  pl.BlockSpec((1, CMP_HID, HEAD_DIM), lambda i, c: (c // 2, 0, 0))],
        out_specs=pl.BlockSpec((1, 1, nb, HEAD_DIM), lambda i, c: (i, c, 0, 0)),
        out_shape=jax.ShapeDtypeStruct((b, 4, nb, HEAD_DIM), BF16),
        compiler_params=pltpu.CompilerParams(
            dimension_semantics=("arbitrary", "arbitrary"), vmem_limit_bytes=VMEM_LIMIT_BYTES),
        name="nsa_compress",
    )(cmp_in, pe, w1, w2)


def _softmax_rows(s, mask):
    sm = jnp.where(mask, s, NEG)
    m = jnp.max(sm, axis=-1, keepdims=True)
    p = jnp.where(mask, jnp.exp(sm - m), 0.0)
    return p / jnp.maximum(jnp.sum(p, axis=-1, keepdims=True), TINY)


_NT = (((1,), (1,)), ((), ()))


def _nsa_attn_kernel(q_ref, kc_ref, vc_ref, ks_ref, vs_ref, kw_ref, vw_ref, gate_ref, o_ref):
    tq = NSA_TQ
    r4 = Q_PER_KV
    g = pl.program_id(1)
    i = pl.program_id(2)
    q4 = q_ref[...]
    q = jnp.concatenate([q4[:, r * HEAD_DIM:(r + 1) * HEAD_DIM] for r in range(r4)], axis=0)
    q0 = i * tq
    tpos = q0 + lax.broadcasted_iota(jnp.int32, (tq, 1), 0)

    ncp = kc_ref.shape[2]
    s = lax.dot_general(q, kc_ref[0, 0], _NT, preferred_element_type=F32).reshape(r4, tq, ncp)
    cend = lax.broadcasted_iota(jnp.int32, (tq, ncp), 1) * CMP_STRIDE + (CMP_LEN - 1)
    p_c = _softmax_rows(s, (cend <= tpos)[None])
    o_c = jnp.dot(p_c.reshape(r4 * tq, ncp).astype(BF16), vc_ref[0, 0], preferred_element_type=F32)

    psum = p_c[0] + p_c[1] + p_c[2] + p_c[3]
    n_i = lax.broadcasted_iota(jnp.int32, (ncp, HEAD_DIM), 0)
    j_i = lax.broadcasted_iota(jnp.int32, (ncp, HEAD_DIM), 1)
    ratio = SEL_BLK // CMP_STRIDE
    c2s = jnp.where((n_i >= ratio * j_i - 1) & (n_i <= ratio * j_i + ratio - 1) & (n_i < ncp - 1)
                    & (j_i < ncp // ratio), 1.0, 0.0).astype(BF16)
    p_hi = psum.astype(BF16)
    p_lo = (psum - p_hi.astype(F32)).astype(BF16)
    imp = (jnp.dot(p_hi, c2s, preferred_element_type=F32)
           + jnp.dot(p_lo, c2s, preferred_element_type=F32))

    ns = ncp // ratio
    imp_t = jnp.transpose(imp)[:ns]
    jj = lax.broadcasted_iota(jnp.int32, (ns, tq), 0)
    tt = q0 + lax.broadcasted_iota(jnp.int32, (ns, tq), 1)
    cur = tt // SEL_BLK
    forced = (jj == 0) | (jj == cur) | (jj == cur - 1)
    score = jnp.where(jj * SEL_BLK <= tt, imp_t + jnp.where(forced, FORCE_BONUS, 0.0), NEG_BIG)
    rank = jnp.zeros((ns, tq), F32)
    for jp in range(ns):
        row = score[jp:jp + 1, :]
        rank = rank + jnp.where(row > score, 1.0, jnp.where((row == score) & (jj > jp), 1.0, 0.0))
    sel_t = jnp.where(rank < float(N_SELECT), 1.0, 0.0)
    sel_t = jnp.concatenate([sel_t, jnp.zeros((HEAD_DIM - ns, tq), F32)], axis=0)
    sel = jnp.transpose(sel_t)

    tk = NSA_TK
    n_tiles = (q0 + tq + tk - 1) // tk
    unpicked = jnp.where(sel > 0.5, 0.0, NEG).astype(BF16)
    q_aug = jnp.concatenate([q, jnp.concatenate([unpicked] * r4, axis=0)], axis=1)
    blk_lane = lax.broadcasted_iota(jnp.int32, (tk, HEAD_DIM), 1)
    blk_of_key = lax.broadcasted_iota(jnp.int32, (tk, HEAD_DIM), 0) // SEL_BLK

    hc = r4
    chains = r4 // hc
    q_chain = [q_aug[c * hc * tq:(c + 1) * hc * tq] for c in range(chains)]

    def tile_update(states, kt, causal):
        k0 = pl.multiple_of(kt * tk, tk)
        onehot = jnp.where(blk_lane == kt * (tk // SEL_BLK) + blk_of_key, 1.0, 0.0).astype(BF16)
        k_aug = jnp.concatenate([ks_ref[pl.ds(k0, tk), :], onehot], axis=1)
        v_tile = vs_ref[pl.ds(k0, tk), :]
        out = []
        for c in range(chains):
            m_run, l_run, acc = states[c]
            s = lax.dot_general(q_chain[c], k_aug, _NT, preferred_element_type=F32).reshape(hc, tq, tk)
            if causal is not None:
                s = jnp.where(causal, s, NEG)
            m_new = jnp.maximum(m_run, jnp.max(s, axis=-1, keepdims=True))
            alpha = jnp.exp(m_run - m_new)
            p = jnp.exp(s - m_new)
            l_new = alpha * l_run + jnp.sum(p, axis=-1, keepdims=True)
            pv = jnp.dot(p.reshape(hc * tq, tk).astype(BF16), v_tile, preferred_element_type=F32)
            out.append((m_new, l_new, alpha.reshape(hc * tq, 1) * acc + pv))
        return tuple(out)

    init = tuple((jnp.full((hc, tq, 1), NEG, F32), jnp.zeros((hc, tq, 1), F32),
                  jnp.zeros((hc * tq, HEAD_DIM), F32)) for _ in range(chains))
    states = lax.fori_loop(0, n_tiles - 1, lambda kt, st: tile_update(st, kt, None), init)
    kpos = (n_tiles - 1) * tk + lax.broadcasted_iota(jnp.int32, (tq, tk), 1)
    states = tile_update(states, n_tiles - 1, (kpos <= tpos)[None])
    o_s = jnp.concatenate([acc / jnp.maximum(l_run, TINY).reshape(hc * tq, 1) for _, l_run, acc in states], axis=0)

    wb = WINDOW + tq
    w0 = pl.multiple_of(jnp.maximum(q0 - WINDOW, 0), tq)
    s = lax.dot_general(q, kw_ref[pl.ds(w0, wb), :], _NT, preferred_element_type=F32).reshape(r4, tq, wb)
    dist = tpos - (w0 + lax.broadcasted_iota(jnp.int32, (tq, wb), 1))
    wmask = ((dist >= 0) & (dist < WINDOW))[None]
    sm = jnp.where(wmask, s, NEG)
    p_w = jnp.where(wmask, jnp.exp(sm - jnp.max(sm, axis=-1, keepdims=True)), 0.0)
    den_w = jnp.maximum(jnp.sum(p_w, axis=-1, keepdims=True), TINY).reshape(r4 * tq, 1)
    o_w = jnp.dot(p_w.reshape(r4 * tq, wb).astype(BF16), vw_ref[pl.ds(w0, wb), :],
                  preferred_element_type=F32) / den_w

    gates = jax.nn.sigmoid(gate_ref[...])
    for r in range(r4):
        def gate(c):
            lane0 = SMALL_NG + 3 * r + c
            lane1 = lane0 + 3 * r4
            return jnp.where(g == 0, gates[:, lane0:lane0 + 1], gates[:, lane1:lane1 + 1])
        rows = slice(r * tq, (r + 1) * tq)
        o = gate(0) * o_c[rows] + gate(1) * o_s[rows] + gate(2) * o_w[rows]
        o_ref[:, r * HEAD_DIM:(r + 1) * HEAD_DIM] = o.astype(BF16)


def nsa_attention_prompt(q_b, kcvc, kvb, z, b, s):
    tq = NSA_TQ
    nq = s // tq
    ncp = kcvc.shape[2]
    gw = Q_PER_KV * HEAD_DIM
    seq_block = lambda col: pl.BlockSpec((s, HEAD_DIM), lambda bi, g, i: (bi, col + g))
    return pl.pallas_call(
        _nsa_attn_kernel,
        grid=(b, KV_GROUPS, nq),
        in_specs=[pl.BlockSpec((tq, gw), lambda bi, g, i: (bi * nq + i, g)),
                  pl.BlockSpec((1, 1, ncp, HEAD_DIM), lambda bi, g, i: (bi, g, 0, 0)),
                  pl.BlockSpec((1, 1, ncp, HEAD_DIM), lambda bi, g, i: (bi, KV_GROUPS + g, 0, 0)),
                  seq_block(0), seq_block(2), seq_block(4), seq_block(6),
                  pl.BlockSpec((tq, HEAD_DIM), lambda bi, g, i: (bi * nq + i, COL_SMALL // HEAD_DIM))],
        out_specs=pl.BlockSpec((tq, gw), lambda bi, g, i: (bi * nq + i, g)),
        out_shape=jax.ShapeDtypeStruct((b * s, N_WIDTH), BF16),
        compiler_params=pltpu.CompilerParams(
            dimension_semantics=("arbitrary", "arbitrary", "arbitrary"), vmem_limit_bytes=VMEM_LIMIT_BYTES),
        name="nsa_attention",
    )(q_b, kcvc, kcvc, kvb, kvb, kvb, kvb, z)


def nsa_prompt_pallas(z, b, s, cmp_w):
    pe, w1, w2 = cmp_w
    cos2, sin2 = rope_tables(jnp.arange(s))
    q_b, rows_f, win_f, kvb, cmp_in = nsa_prep_pallas(z, b, s, cos2, sin2)
    kcvc = nsa_compress_prompt(cmp_in.reshape(b, 4, s // CMP_STRIDE, CMP_STRIDE * HEAD_DIM), pe, w1, w2)
    hn = nsa_attention_prompt(q_b, kcvc, kvb, z, b, s)
    rows = rows_f.reshape(b, s, 4, KV_GROUPS, HEAD_DIM)
    win = win_f.reshape(b, s, 2, KV_GROUPS, HEAD_DIM)[:, -min(WINDOW, s):]
    return hn, (rows, win)


SAMPLE_PAGES = 16
SAMPLE_TS = 16
ROW_COLS = 4 * KV_GROUPS * HEAD_DIM
ROW_VECS = 4 * KV_GROUPS
WIN_VECS = 2 * KV_GROUPS
NEG_DEAD = -3e38


def _page_specs():
    def spec(k):
        return pl.BlockSpec((PAGE_SIZE * ROW_VECS, HEAD_DIM), lambda bi, i, pt: (pt[bi, i * SAMPLE_PAGES + k], 0))
    return [spec(k) for k in range(SAMPLE_PAGES)]


def _page_vecs(pg, vec):
    return pg[pl.ds(vec, PAGE_SIZE, stride=ROW_VECS), :]


def _nsa_cmp_sample_kernel(pt_ref, *refs):
    del pt_ref
    pages = refs[:SAMPLE_PAGES]
    pe_ref, w1_ref, w2_ref, o_ref, carry_sc, bias_sc = refs[SAMPLE_PAGES:]
    i = pl.program_id(1)
    half = CMP_STRIDE * HEAD_DIM
    nbp = PAGE_SIZE // CMP_STRIDE
    nb = SAMPLE_PAGES * nbp

    @pl.when(i == 0)
    def _():
        carry_sc[...] = jnp.zeros_like(carry_sc)
        for kv in range(2):
            pe = jnp.broadcast_to(pe_ref[kv], (8, 2 * half)).astype(BF16)
            bias_sc[kv] = jnp.dot(pe, w1_ref[kv], preferred_element_type=F32)

    row = lax.broadcasted_iota(jnp.int32, (nb, CMP_HID), 0)
    for kv in range(2):
        def flat(c):
            piece = lambda pg, j: pg[pl.ds(j * ROW_VECS + c, nbp, stride=CMP_STRIDE * ROW_VECS), :]
            return jnp.concatenate(
                [jnp.concatenate([piece(pg, j) for j in range(CMP_STRIDE)], axis=1) for pg in pages],
                axis=0)
        x = jnp.concatenate([flat(kv * KV_GROUPS + g) for g in range(KV_GROUPS)], axis=0).astype(BF16)
        first_all = jnp.dot(x, w1_ref[kv, :half, :], preferred_element_type=F32)
        second_all = jnp.dot(x, w1_ref[kv, half:, :], preferred_element_type=F32)
        for g in range(KV_GROUPS):
            c = kv * KV_GROUPS + g
            first = first_all[g * nb:(g + 1) * nb]
            second = second_all[g * nb:(g + 1) * nb]
            shifted = jnp.where(row == 0, carry_sc[c, 7:8, :], pltpu.roll(first, 1, 0))
            hid = shifted + second + bias_sc[kv, 0:1, :]
            o_ref[0, c] = jnp.dot(jax.nn.gelu(hid).astype(BF16), w2_ref[kv],
                                  preferred_element_type=F32).astype(BF16)
            carry_sc[c] = first[nb - 8:, :]


def nsa_compress_sample(pt, cache2d, pe, w1, w2, db):
    n_pages = pt.shape[1]
    nb_all = n_pages * (PAGE_SIZE // CMP_STRIDE)
    nb = SAMPLE_PAGES * (PAGE_SIZE // CMP_STRIDE)
    const3 = lambda bi, i, pt: (0, 0, 0)
    grid_spec = pltpu.PrefetchScalarGridSpec(
        num_scalar_prefetch=1,
        grid=(db, n_pages // SAMPLE_PAGES),
        in_specs=_page_specs() + [pl.BlockSpec(pe.shape, const3), pl.BlockSpec(w1.shape, const3),
                                   pl.BlockSpec(w2.shape, const3)],
        out_specs=pl.BlockSpec((1, 2 * KV_GROUPS, nb, HEAD_DIM), lambda bi, i, pt: (bi, 0, i, 0)),
        scratch_shapes=[pltpu.VMEM((2 * KV_GROUPS, 8, CMP_HID), F32), pltpu.VMEM((2, 8, CMP_HID), F32)])
    return pl.pallas_call(
        _nsa_cmp_sample_kernel,
        grid_spec=grid_spec,
        out_shape=jax.ShapeDtypeStruct((db, 2 * KV_GROUPS, nb_all, HEAD_DIM), BF16),
        compiler_params=pltpu.CompilerParams(
            dimension_semantics=("arbitrary", "arbitrary"), vmem_limit_bytes=VMEM_LIMIT_BYTES),
        name="nsa_compress_sample",
    )(pt, *([cache2d] * SAMPLE_PAGES), pe, w1, w2)


def _nsa_attn_sample_kernel(pt_ref, *refs, past_len, t_valid, n_steps):
    del pt_ref
    pages = refs[:SAMPLE_PAGES]
    (q_ref, new_ref, kc_ref, win_ref, gate_ref, c2s_ref, o_ref,
     sel_sc, m_sc, l_sc, acc_sc, oc_sc, ow_sc) = refs[SAMPLE_PAGES:]
    i = pl.program_id(1)
    ts = SAMPLE_TS
    r4 = Q_PER_KV
    rows = r4 * ts
    d = HEAD_DIM
    blocks_per_step = SAMPLE_PAGES * PAGE_SIZE // SEL_BLK
    ns = past_len // SEL_BLK + 1
    trow = lax.broadcasted_iota(jnp.int32, (rows, 1), 0) % ts
    qpos = past_len + trow

    def q_of(g):
        return jnp.concatenate([q_ref[:, (g * r4 + r) * d:(g * r4 + r + 1) * d] for r in range(r4)], axis=0)

    @pl.when(i == 0)
    def _():
        ncp = kc_ref.shape[2]
        nsl = (n_steps + 1) * d
        c2s = c2s_ref[...]
        lane_r = lax.broadcasted_iota(jnp.int32, (ts, nsl), 1)
        blk_r = (lane_r // d) * blocks_per_step + lane_r % d
        live = (lane_r % d < blocks_per_step) & (blk_r < ns)
        blk_f = blk_r.astype(F32)
        qpos_t = past_len + lax.broadcasted_iota(jnp.int32, (ts, 1), 0)
        cur = qpos_t // SEL_BLK
        forced = (blk_r == 0) | (blk_r == cur) | (blk_r == cur - 1)
        midx = lax.broadcasted_iota(jnp.int32, (rows, ncp), 1)
        cmask = (midx >= 1) & ((midx - 1) * CMP_STRIDE + CMP_LEN - 1 <= qpos)
        sidx = lax.broadcasted_iota(jnp.int32, (rows, ts), 1)
        new_ok = (sidx <= trow) & (sidx < t_valid)
        wl = win_ref.shape[0] // WIN_VECS
        dist_buf = trow + wl - lax.broadcasted_iota(jnp.int32, (rows, wl), 1)
        buf_ok = (dist_buf >= 0) & (dist_buf < WINDOW)
        dist_new = trow - sidx
        wnew_ok = (dist_new >= 0) & (dist_new < WINDOW) & (sidx < t_valid)
        for g in range(KV_GROUPS):
            qg = q_of(g)
            s = lax.dot_general(qg, kc_ref[0, g], _NT, preferred_element_type=F32)
            p_c = _softmax_rows(s, cmask)
            oc_sc[g] = jnp.dot(p_c.astype(BF16), kc_ref[0, KV_GROUPS + g], preferred_element_type=F32)
            psum = p_c[0:ts] + p_c[ts:2 * ts] + p_c[2 * ts:3 * ts] + p_c[3 * ts:4 * ts]
            p_hi = psum.astype(BF16)
            p_lo = (psum - p_hi.astype(F32)).astype(BF16)
            imp = (jnp.dot(p_hi, c2s, preferred_element_type=F32) + jnp.dot(p_lo, c2s, preferred_element_type=F32))
            score = jnp.where(live, jnp.where(blk_r * SEL_BLK <= qpos_t, imp + jnp.where(forced, FORCE_BONUS, 0.0),
                                              NEG_BIG), NEG_DEAD)
            sel = jnp.zeros((ts, nsl), F32)
            for _ in range(N_SELECT):
                top = jnp.max(score, axis=1, keepdims=True)
                first = jnp.min(jnp.where(score == top, blk_f, 1e9), axis=1, keepdims=True)
                hit = live & (blk_f == first)
                sel = jnp.where(hit, 1.0, sel)
                score = jnp.where(hit, NEG_DEAD, score)
            for step in range(n_steps):
                sel_sc[g, step] = jnp.concatenate([sel[:, step * d:(step + 1) * d]] * r4, axis=0)
            knew = new_ref[:, g * d:(g + 1) * d]
            vnew = new_ref[:, (KV_GROUPS + g) * d:(KV_GROUPS + g + 1) * d]
            sm = jnp.where(new_ok, lax.dot_general(qg, knew, _NT, preferred_element_type=F32), NEG)
            m0 = jnp.max(sm, axis=1, keepdims=True)
            p = jnp.where(new_ok, jnp.exp(sm - m0), 0.0)
            m_sc[g] = m0
            l_sc[g] = jnp.sum(p, axis=1, keepdims=True)
            acc_sc[g] = jnp.dot(p.astype(BF16), vnew, preferred_element_type=F32)
            kwb = win_ref[pl.ds(g, wl, stride=WIN_VECS), :].astype(BF16)
            vwb = win_ref[pl.ds(KV_GROUPS + g, wl, stride=WIN_VECS), :].astype(BF16)
            kwn = new_ref[:, (2 * KV_GROUPS + g) * d:(2 * KV_GROUPS + g + 1) * d]
            vwn = new_ref[:, (3 * KV_GROUPS + g) * d:(3 * KV_GROUPS + g + 1) * d]
            s1 = jnp.where(buf_ok, lax.dot_general(qg, kwb, _NT, preferred_element_type=F32), NEG)
            s2 = jnp.where(wnew_ok, lax.dot_general(qg, kwn, _NT, preferred_element_type=F32), NEG)
            mw = jnp.maximum(jnp.max(s1, axis=1, keepdims=True), jnp.max(s2, axis=1, keepdims=True))
            p1 = jnp.where(buf_ok, jnp.exp(s1 - mw), 0.0)
            p2 = jnp.where(wnew_ok, jnp.exp(s2 - mw), 0.0)
            den = jnp.maximum(jnp.sum(p1, axis=1, keepdims=True) + jnp.sum(p2, axis=1, keepdims=True), TINY)
            ow_sc[g] = (jnp.dot((p1 / den).astype(BF16), vwb, preferred_element_type=F32)
                        + jnp.dot((p2 / den).astype(BF16), vwn, preferred_element_type=F32))

    half_lane = lax.broadcasted_iota(jnp.int32, (rows, PAGE_SIZE), 1) < SEL_BLK
    for g in range(KV_GROUPS):
        qg = q_of(g)
        kt = jnp.concatenate([_page_vecs(pg, 2 * KV_GROUPS + g) for pg in pages], axis=0).astype(BF16)
        vt = jnp.concatenate([_page_vecs(pg, 3 * KV_GROUPS + g) for pg in pages], axis=0).astype(BF16)
        s = lax.dot_general(qg, kt, _NT, preferred_element_type=F32)
        selg = sel_sc[g, i]
        picked = jnp.concatenate(
            [jnp.where(half_lane, selg[:, 2 * k:2 * k + 1], selg[:, 2 * k + 1:2 * k + 2])
             for k in range(SAMPLE_PAGES)], axis=1)
        mask = picked > 0.5
        sm = jnp.where(mask, s, NEG)
        m_old = m_sc[g]
        m_new = jnp.maximum(m_old, jnp.max(sm, axis=1, keepdims=True))
        alpha = jnp.exp(m_old - m_new)
        p = jnp.where(mask, jnp.exp(sm - m_new), 0.0)
        m_sc[g] = m_new
        l_sc[g] = alpha * l_sc[g] + jnp.sum(p, axis=1, keepdims=True)
        acc_sc[g] = alpha * acc_sc[g] + jnp.dot(p.astype(BF16), vt, preferred_element_type=F32)

    @pl.when(i == n_steps - 1)
    def _():
        gates = jax.nn.sigmoid(gate_ref[...])
        for g in range(KV_GROUPS):
            o_s = acc_sc[g] / jnp.maximum(l_sc[g], TINY)
            o_c = oc_sc[g]
            o_w = ow_sc[g]
            for r in range(r4):
                h = g * r4 + r
                gate = lambda c: gates[:, SMALL_NG + 3 * h + c:SMALL_NG + 3 * h + c + 1]
                rs = slice(r * ts, (r + 1) * ts)
                o = gate(0) * o_c[rs] + gate(1) * o_s[rs] + gate(2) * o_w[rs]
                o_ref[0, :, h * d:(h + 1) * d] = o.astype(BF16)


def nsa_attention_sample(pt, cache2d, q_b, kvb, kcvc, win2d, zp, layer, db, t_pad, t_valid, past_len):
    ts = SAMPLE_TS
    n_steps = pt.shape[1] // SAMPLE_PAGES
    rows = Q_PER_KV * ts
    wl_rows = win2d.shape[0] // (DEPTH * db)
    ncp = kcvc.shape[2]
    ns = past_len // SEL_BLK + 1
    per_step = SAMPLE_PAGES * PAGE_SIZE // SEL_BLK
    ratio = SEL_BLK // CMP_STRIDE
    lane = np.arange((n_steps + 1) * HEAD_DIM)[None, :]
    tok = np.arange(ncp)[:, None] - 1
    blk = (lane // HEAD_DIM) * per_step + lane % HEAD_DIM
    c2s = jnp.asarray(((lane % HEAD_DIM < per_step) & (blk < ns) & (tok >= 0) & (tok >= ratio * blk - 1)
                       & (tok <= ratio * blk + ratio - 1)).astype(np.float32), dtype=BF16)
    seq_rows = lambda width, col: pl.BlockSpec((ts, width), lambda bi, i, pt: (bi * (t_pad // ts), col))
    grid_spec = pltpu.PrefetchScalarGridSpec(
        num_scalar_prefetch=1,
        grid=(db, n_steps),
        in_specs=_page_specs() + [
            seq_rows(N_WIDTH, 0),
            seq_rows(ROW_COLS, 0),
            pl.BlockSpec((1,) + kcvc.shape[1:], lambda bi, i, pt: (bi, 0, 0, 0)),
            pl.BlockSpec((wl_rows, HEAD_DIM), lambda bi, i, pt: (layer * db + bi, 0)),
            seq_rows(HEAD_DIM, COL_SMALL // HEAD_DIM),
            pl.BlockSpec(c2s.shape, lambda bi, i, pt: (0, 0))],
        out_specs=pl.BlockSpec((1, ts, N_WIDTH), lambda bi, i, pt: (bi, 0, 0)),
        scratch_shapes=[pltpu.VMEM((KV_GROUPS, n_steps, rows, HEAD_DIM), F32),
                        pltpu.VMEM((KV_GROUPS, rows, 1), F32),
                        pltpu.VMEM((KV_GROUPS, rows, 1), F32),
                        pltpu.VMEM((KV_GROUPS, rows, HEAD_DIM), F32),
                        pltpu.VMEM((KV_GROUPS, rows, HEAD_DIM), F32),
                        pltpu.VMEM((KV_GROUPS, rows, HEAD_DIM), F32)])
    return pl.pallas_call(
        functools.partial(_nsa_attn_sample_kernel, past_len=past_len, t_valid=t_valid, n_steps=n_steps),
        grid_spec=grid_spec,
        out_shape=jax.ShapeDtypeStruct((db, ts, N_WIDTH), BF16),
        compiler_params=pltpu.CompilerParams(
            dimension_semantics=("arbitrary", "arbitrary"), vmem_limit_bytes=VMEM_LIMIT_BYTES),
        name="nsa_attention_sample",
    )(pt, *([cache2d] * SAMPLE_PAGES), q_b, kvb, kcvc, win2d, zp, c2s)


def nsa_sample_pallas(zp, db, t, t_pad, layer, cache2d, pt, win_state, cmp_w):
    pe, w1, w2 = cmp_w
    past_len = pt.shape[1] * PAGE_SIZE
    cos2, sin2 = rope_tables(past_len + jnp.arange(t_pad))
    q_b, rows_f, win_f, kvb, _ = nsa_prep_pallas(zp, db, t_pad, cos2, sin2, tm=t_pad)
    kcvc = nsa_compress_sample(pt, cache2d, pe, w1, w2, db)
    win2d = win_state.reshape(-1, HEAD_DIM)
    hn = nsa_attention_sample(pt, cache2d, q_b, kvb, kcvc, win2d, zp, layer, db, t_pad, t, past_len)
    rows = rows_f.reshape(db, t_pad, 4, KV_GROUPS, HEAD_DIM)[:, :t]
    win_new = win_f.reshape(db, t_pad, 2, KV_GROUPS, HEAD_DIM)[:, :t]
    wl = win_state.shape[2]
    win = jnp.concatenate([win_state[layer], win_new], axis=1)[:, -wl:]
    return hn[:, :t], (rows, win)


MIX_L = 128
GLA_SUB = 16


def _logsig(x):
    return jnp.minimum(x, 0.0) - jnp.log1p(jnp.exp(-jnp.abs(x)))


def _prefix_sum(x, axis):
    n = x.shape[axis]
    idx = lax.broadcasted_iota(jnp.int32, x.shape, axis)
    step = 1
    while step < n:
        x = x + jnp.where(idx >= step, pltpu.roll(x, step, axis), 0.0)
        step *= 2
    return x


def _mlstm_phases(t, uqk_ref, v_ref, og_ref, small_ref, convw_ref, bias_ref, norm_ref, conv0_ref, c0_ref, n0_ref,
                  m0_ref, h_ref, c_ref, n_ref, m_ref, prev_sc, c_sc, n_sc, m_sc, *, t_valid):
    L = MIX_L
    d = HEAD_DIM

    prev_sc[...] = jnp.zeros_like(prev_sc)
    prev_sc[L - 8:, :] = conv0_ref[0]
    c_sc[...] = c0_ref[0]
    n_sc[...] = n0_ref[0]
    m_sc[...] = m0_ref[0]
    yield

    x = uqk_ref[...]
    prev = prev_sc[...]
    row = lax.broadcasted_iota(jnp.int32, x.shape, 0)
    w = convw_ref[...]
    conv = w[CONV_W - 1:CONV_W] * x
    for k in range(1, CONV_W):
        shifted = jnp.where(row >= k, pltpu.roll(x, k, 0), pltpu.roll(prev, k, 0))
        conv = conv + w[CONV_W - 1 - k:CONV_W - k] * shifted
    prev_sc[...] = x
    act = conv * jax.nn.sigmoid(conv)

    pre = small_ref[...] + bias_ref[...]
    pos_c = t * L + lax.broadcasted_iota(jnp.int32, (L, HEAD_DIM), 0)
    ig_c = jnp.where(pos_c < t_valid, pre, NEG)
    b_c = _prefix_sum(jnp.where(pos_c < t_valid, _logsig(pre), 0.0), 0)
    pre_r = jnp.transpose(pre)[0:8]
    pos_r = t * L + lax.broadcasted_iota(jnp.int32, (8, L), 1)
    ig_r = jnp.where(pos_r < t_valid, pre_r, NEG)
    b_r = _prefix_sum(jnp.where(pos_r < t_valid, _logsig(pre_r), 0.0), 1)

    li = lax.broadcasted_iota(jnp.int32, (L, L), 0)
    si = lax.broadcasted_iota(jnp.int32, (L, L), 1)
    for h in range(H_M):
        sl = slice(h * d, (h + 1) * d)
        bcol, igcol = b_c[:, H_M + h:H_M + h + 1], ig_c[:, h:h + 1]
        brow, igrow = b_r[H_M + h:H_M + h + 1, :], ig_r[h:h + 1, :]
        m0 = m_sc[h:h + 1, 0:1]
        c0 = c_sc[h]
        n0 = n_sc[h:h + 1, :]
        qf = act[:, sl]
        kf = act[:, M_WIDTH + h * d:M_WIDTH + (h + 1) * d] * (d ** -0.5)
        vf = v_ref[:, sl]
        qb, kb, vb = qf.astype(BF16), kf.astype(BF16), vf.astype(BF16)
        logw = jnp.where(si <= li, bcol - brow + igrow, NEG)
        gsum = bcol + m0
        m_row = jnp.maximum(jnp.max(logw, axis=1, keepdims=True), gsum)
        wgt = jnp.exp(logw - m_row) * lax.dot_general(qb, kb, _NT, preferred_element_type=F32)
        inter = jnp.exp(gsum - m_row)
        num = (jnp.dot(wgt.astype(BF16), vb, preferred_element_type=F32)
               + inter * lax.dot_general(qb, c0.astype(BF16), _NT, preferred_element_type=F32))
        den = jnp.sum(wgt, axis=1, keepdims=True) + inter * jnp.sum(qf * n0, axis=1, keepdims=True)
        hh = num / jnp.maximum(jnp.abs(den), jnp.exp(-m_row))
        bl = bcol[L - 1:L, :]
        m_new = jnp.maximum(jnp.max(bl - brow + igrow, axis=1, keepdims=True), bl + m0)
        wa = jnp.exp(bl - bcol + igcol - m_new)
        wc = jnp.exp(bl + m0 - m_new)
        c_sc[h] = wc * c0 + jnp.dot(jnp.transpose(vf * wa).astype(BF16), kb, preferred_element_type=F32)
        n_sc[h:h + 1, :] = wc * n0 + jnp.sum(wa * kf, axis=0, keepdims=True)
        m_sc[h:h + 1, :] = jnp.broadcast_to(m_new, (1, d))
        out = _rms(hh, norm_ref[:, sl]) * jax.nn.sigmoid(og_ref[:, sl])
        h_ref[:, sl] = out.astype(BF16)
    yield

    c_ref[0] = c_sc[...]
    n_ref[0] = n_sc[...]
    m_ref[0] = m_sc[...]
    yield


def _gla_phases(t, q_ref, k_ref, v_ref, r_ref, small_ref, w2_ref, gb_ref, norm_ref, s0_ref, h_ref, s_ref, st_sc,
                *, t_valid):
    L = MIX_L
    d = HEAD_DIM

    for h in range(H_G):
        st_sc[h] = jnp.transpose(s0_ref[0, h])
    yield

    pre = jnp.dot(small_ref[...].astype(BF16), w2_ref[...], preferred_element_type=F32) + gb_ref[...]
    pos = t * L + lax.broadcasted_iota(jnp.int32, (L, G_WIDTH), 0)
    la = jnp.where(pos < t_valid, _logsig(pre) / GLA_GATE_TEMP, 0.0)
    bc_all = _prefix_sum(la, 0)
    rowi = lax.broadcasted_iota(jnp.int32, (L, d), 0)
    li = lax.broadcasted_iota(jnp.int32, (L, L), 0)
    si = lax.broadcasted_iota(jnp.int32, (L, L), 1)
    for h in range(H_G):
        sl = slice(h * d, (h + 1) * d)
        bc = bc_all[:, sl]
        qf = q_ref[:, sl] * (d ** -0.5)
        kf = jnp.where(t * L + rowi < t_valid, k_ref[:, sl], 0.0)
        vf = v_ref[:, sl]
        qparts, kparts = [], []
        for j in range(L // GLA_SUB):
            lo, hi = j * GLA_SUB, (j + 1) * GLA_SUB
            e = bc[hi - 1:hi, :]
            qparts.append((qf * jnp.exp(jnp.where(rowi >= lo, bc - e, NEG))).astype(BF16))
            kparts.append((kf * jnp.exp(jnp.where((rowi >= lo) & (rowi < hi), e - bc, NEG))).astype(BF16))
        a = lax.dot_general(jnp.concatenate(qparts, axis=1), jnp.concatenate(kparts, axis=1), _NT,
                            preferred_element_type=F32)
        a = jnp.where(si <= li, a, 0.0)
        st = st_sc[h]
        o = (jnp.dot(a.astype(BF16), vf.astype(BF16), preferred_element_type=F32)
             + lax.dot_general((qf * jnp.exp(bc)).astype(BF16), st.astype(BF16), _NT, preferred_element_type=F32))
        bl = bc[L - 1:L, :]
        kd = (kf * jnp.exp(bl - bc)).astype(BF16)
        st_sc[h] = st * jnp.exp(bl) + jnp.dot(jnp.transpose(vf).astype(BF16), kd, preferred_element_type=F32)
        gate = r_ref[:, sl]
        h_ref[:, sl] = (_rms(o, norm_ref[:, sl]) * (gate * jax.nn.sigmoid(gate))).astype(BF16)
    yield

    for h in range(H_G):
        s_ref[0, h] = jnp.transpose(st_sc[h])
    yield


N_MLSTM_IN, N_GLA_IN, N_MLSTM_OUT, N_GLA_OUT, N_MLSTM_SCRATCH = 11, 9, 4, 2, 4


def _mixers_kernel(*refs, t_valid):
    t = pl.program_id(1)
    cuts = np.cumsum([N_MLSTM_IN, N_GLA_IN, N_MLSTM_OUT, N_GLA_OUT, N_MLSTM_SCRATCH])
    m_in, g_in, m_out, g_out, m_scr, g_scr = (refs[a:b] for a, b in zip([0, *cuts], [*cuts, len(refs)]))
    phases = [_mlstm_phases(t, *m_in, *m_out, *m_scr, t_valid=t_valid),
              _gla_phases(t, *g_in, *g_out, *g_scr, t_valid=t_valid)]

    @pl.when(t == 0)
    def _():
        for p in phases:
            next(p)

    for p in phases:
        next(p)

    @pl.when(t == pl.num_programs(1) - 1)
    def _():
        for p in phases:
            next(p)


def mixers_pallas(z, b, t, conv_buf, c0, n0, m0, s0, conv_w, m_gate_b, m_norm, g_w2, g_b, g_norm):
    t_pad = z.shape[0] // b
    L = MIX_L
    nt = t_pad // L
    lanes = lambda a: jnp.pad(a.reshape(1, -1), ((0, 0), (0, HEAD_DIM - a.size)))
    conv0 = jnp.pad(conv_buf, ((0, 0), (8 - (CONV_W - 1), 0), (0, 0)))
    n0p = jnp.pad(n0, ((0, 0), (0, 8 - H_M), (0, 0)))
    m0p = jnp.broadcast_to(jnp.pad(m0, ((0, 0), (0, 8 - H_M)))[:, :, None], (b, 8, HEAD_DIM))
    w2_pad = jnp.pad(g_w2, ((SMALL_GLR, HEAD_DIM - SMALL_GLR - GATE_RANK), (0, 0))).astype(BF16)
    rowblk = lambda width, col: pl.BlockSpec((L, width), lambda bi, ti: (bi * nt + ti, col))
    const2 = lambda shape: pl.BlockSpec(shape, lambda bi, ti: (0, 0))
    per_b3 = pl.BlockSpec((1, 8, HEAD_DIM), lambda bi, ti: (bi, 0, 0))
    per_b4 = pl.BlockSpec((1, H_M, HEAD_DIM, HEAD_DIM), lambda bi, ti: (bi, 0, 0, 0))
    small = rowblk(HEAD_DIM, COL_SMALL // HEAD_DIM)
    gcol = COL_GLA // G_WIDTH
    mlstm_in = [rowblk(2 * M_WIDTH, 0), rowblk(M_WIDTH, 2), rowblk(M_WIDTH, 3), small,
                const2((CONV_W, 2 * M_WIDTH)), const2((1, HEAD_DIM)), const2((1, M_WIDTH)),
                pl.BlockSpec((1, 8, 2 * M_WIDTH), lambda bi, ti: (bi, 0, 0)), per_b4, per_b3, per_b3]
    gla_in = [rowblk(G_WIDTH, gcol), rowblk(G_WIDTH, gcol + 1), rowblk(G_WIDTH, gcol + 2), rowblk(G_WIDTH, gcol + 3),
              small, const2((HEAD_DIM, G_WIDTH)), const2((1, G_WIDTH)), const2((1, G_WIDTH)), per_b4]
    assert (len(mlstm_in), len(gla_in)) == (N_MLSTM_IN, N_GLA_IN)
    state = jax.ShapeDtypeStruct((b, H_M, HEAD_DIM, HEAD_DIM), F32)
    tile = jax.ShapeDtypeStruct((b, 8, HEAD_DIM), F32)
    hm, c, n, m, hg, s = pl.pallas_call(
        functools.partial(_mixers_kernel, t_valid=t),
        grid=(b, nt),
        in_specs=mlstm_in + gla_in,
        out_specs=[rowblk(M_WIDTH, 0), per_b4, per_b3, per_b3, rowblk(G_WIDTH, 0), per_b4],
        out_shape=[jax.ShapeDtypeStruct((b * t_pad, M_WIDTH), BF16), state, tile, tile,
                   jax.ShapeDtypeStruct((b * t_pad, G_WIDTH), BF16), state],
        scratch_shapes=[pltpu.VMEM((L, 2 * M_WIDTH), F32),
                        pltpu.VMEM((H_M, HEAD_DIM, HEAD_DIM), F32),
                        pltpu.VMEM((8, HEAD_DIM), F32),
                        pltpu.VMEM((8, HEAD_DIM), F32),
                        pltpu.VMEM((H_G, HEAD_DIM, HEAD_DIM), F32)],
        compiler_params=pltpu.CompilerParams(
            dimension_semantics=("arbitrary", "arbitrary"), vmem_limit_bytes=VMEM_LIMIT_BYTES),
        name="mixers",
    )(z, z, z, z, conv_w, lanes(m_gate_b), m_norm.reshape(1, -1), conv0, c0, n0p, m0p,
      z, z, z, z, z, w2_pad, g_b.reshape(1, -1), g_norm.reshape(1, -1), s0)
    valid = lambda a: a.reshape(b, t_pad, -1)[:, :t]
    return valid(hm), valid(hg), c, n[:, :H_M], m[:, :H_M, 0], s


def rmsnorm(x, g):
    xf = x.astype(F32)
    y = xf * lax.rsqrt(jnp.mean(xf * xf, axis=-1, keepdims=True) + EPS)
    return (y * g.astype(F32)).astype(x.dtype)


def rope(x, pos):
    half = x.shape[-1] // 2
    inv_freq = jnp.exp(-math.log(ROPE_THETA) * jnp.arange(half, dtype=F32) / half)
    ang = pos.astype(F32)[:, None] * inv_freq[None, :]
    cos = jnp.cos(ang)[:, None, :]
    sin = jnp.sin(ang)[:, None, :]
    xf = x.astype(F32)
    x1, x2 = xf[..., :half], xf[..., half:]
    return jnp.concatenate([x1 * cos - x2 * sin, x2 * cos + x1 * sin], axis=-1).astype(x.dtype)


def masked_softmax(s, mask):
    s = jnp.where(mask, s, -jnp.inf)
    m = jnp.max(s, axis=-1, keepdims=True)
    m = jnp.where(jnp.isfinite(m), m, 0.0)
    p = jnp.exp(s - m)
    return p / jnp.maximum(jnp.sum(p, axis=-1, keepdims=True), TINY)


def chunked_scan(step, carry, xs, chunk):
    b, t = xs[0].shape[:2]
    c = chunk if t % chunk == 0 else t
    n = t // c
    split = lambda a: jnp.moveaxis(a.reshape((b, n, c) + a.shape[2:]), 1, 0)
    carry, ys = lax.scan(step, carry, tuple(split(a) for a in xs))
    return carry, jnp.moveaxis(ys, 0, 1).reshape((b, t) + ys.shape[3:])


def mlstm_chunk(carry, inp):
    c0, n0, m0 = carry
    q, k, v, li, lf = inp
    L = q.shape[1]
    b = jnp.cumsum(lf, axis=1).transpose(0, 2, 1)
    ig = li.transpose(0, 2, 1)
    causal = jnp.tril(jnp.ones((L, L), dtype=bool))
    logw = jnp.where(causal, b[..., :, None] - b[..., None, :] + ig[..., None, :], -jnp.inf)
    g = b + m0[..., None]
    m_row = jnp.maximum(jnp.max(logw, axis=-1), g)
    w = jnp.exp(logw - m_row[..., None]) * jnp.einsum('blhd,bshd->bhls', q, k)
    inter = jnp.exp(g - m_row)
    num = jnp.einsum('bhls,bshd->bhld', w, v) + inter[..., None] * jnp.einsum('bhed,blhd->bhle', c0, q)
    den = jnp.sum(w, axis=-1) + inter * jnp.einsum('bhd,blhd->bhl', n0, q)
    h = num / jnp.maximum(jnp.abs(den), jnp.exp(-m_row))[..., None]
    bl = b[..., -1]
    a = bl[..., None] - b + ig
    m_new = jnp.maximum(jnp.max(a, axis=-1), bl + m0)
    wa = jnp.exp(a - m_new[..., None])
    wc = jnp.exp(bl + m0 - m_new)
    c_new = wc[..., None, None] * c0 + jnp.einsum('bhs,bshe,bshd->bhed', wa, v, k)
    n_new = wc[..., None] * n0 + jnp.einsum('bhs,bshd->bhd', wa, k)
    return (c_new, n_new, m_new), h.transpose(0, 2, 1, 3)


def gla_chunk(s0, inp):
    q, k, v, la = inp
    L = q.shape[1]
    bc = jnp.cumsum(la, axis=1)
    causal = jnp.tril(jnp.ones((L, L), dtype=bool))[None, :, :, None, None]
    decay = jnp.exp(jnp.where(causal, bc[:, :, None] - bc[:, None, :], -jnp.inf))
    a = jnp.einsum('bthk,bshk,btshk->bhts', q, k, decay)
    o = jnp.einsum('bhts,bshv->bthv', a, v) + jnp.einsum('bthk,bhkv->bthv', q * jnp.exp(bc), s0)
    bl = bc[:, -1]
    s_new = jnp.exp(bl)[..., None] * s0 + jnp.einsum('bshk,bshv->bhkv', k * jnp.exp(bl[:, None] - bc), v)
    return s_new, o


def nsa_compress(x, pe, w1, w2):
    b, l = x.shape[:2]
    nb = l // CMP_STRIDE
    xb = x[:, :nb * CMP_STRIDE].reshape(b, nb, CMP_STRIDE, KV_GROUPS, HEAD_DIM)
    first = jnp.einsum('bnjgd,jde->bnge', xb, w1[:CMP_STRIDE])
    second = jnp.einsum('bnjgd,jde->bnge', xb, w1[CMP_STRIDE:])
    hid = first[:, :-1] + second[:, 1:] + jnp.einsum('jd,jde->e', pe, w1)
    return jnp.einsum('bnge,ed->bngd', jax.nn.gelu(hid), w2)


def nsa_context(rows, cmp_params):
    pe, w1, w2 = cmp_params
    b, l = rows.shape[:2]
    kc = nsa_compress(rows[:, :, 0], pe[0], w1[0], w2[0])
    vc = nsa_compress(rows[:, :, 1], pe[1], w1[1], w2[1])
    nc = kc.shape[1]
    ns = -(-l // SEL_BLK)
    sel = jnp.pad(rows[:, :, 2:4], ((0, 0), (0, ns * SEL_BLK - l), (0, 0), (0, 0), (0, 0)))
    sel = sel.reshape(b, ns, SEL_BLK, 2, KV_GROUPS, HEAD_DIM).transpose(3, 0, 4, 1, 2, 5)
    cmp_start = jnp.arange(nc) * CMP_STRIDE
    cmp_end = cmp_start + CMP_LEN - 1
    sel_start = jnp.arange(ns) * SEL_BLK
    cmp_to_sel = ((cmp_start[:, None] < sel_start[None, :] + SEL_BLK)
                  & (cmp_end[:, None] >= sel_start[None, :])).astype(F32)
    return kc, vc, cmp_end, cmp_to_sel, sel[0], sel[1]


def nsa_query_block(q, qpos, gates, kc, vc, cmp_end, cmp_to_sel, ksb, vsb, kw, vw, kwpos):
    b, t = q.shape[:2]
    qg = q.reshape(b, t, KV_GROUPS, Q_PER_KV, HEAD_DIM) * (HEAD_DIM ** -0.5)
    s = jnp.einsum('btgrd,bngd->btgrn', qg, kc).astype(F32)
    p_c = masked_softmax(s, (cmp_end[None, :] <= qpos[:, None])[None, :, None, None, :])
    o_c = jnp.einsum('btgrn,bngd->btgrd', p_c.astype(vc.dtype), vc)
    ns = ksb.shape[2]
    imp = jnp.einsum('btgn,nj->btgj', jnp.sum(p_c, axis=3), cmp_to_sel)
    j = jnp.arange(ns)
    cur = qpos // SEL_BLK
    elig = (j * SEL_BLK)[None, :] <= qpos[:, None]
    forced = (j[None, :] == 0) | (j[None, :] == cur[:, None]) | (j[None, :] == cur[:, None] - 1)
    score = jnp.where(elig[None, :, None, :], imp + jnp.where(forced, FORCE_BONUS, 0.0)[None, :, None, :], NEG_BIG)
    _, idx = lax.top_k(score, min(N_SELECT, ns))
    n = idx.shape[-1]
    idx_g = idx.transpose(0, 2, 1, 3).reshape(b, KV_GROUPS, t * n)
    take = jax.vmap(jax.vmap(lambda blocks, ids: blocks[ids]))
    gk = take(ksb, idx_g).reshape(b, KV_GROUPS, t, n * SEL_BLK, HEAD_DIM)
    gv = take(vsb, idx_g).reshape(b, KV_GROUPS, t, n * SEL_BLK, HEAD_DIM)
    kpos = (idx[..., None] * SEL_BLK + jnp.arange(SEL_BLK)).reshape(b, t, KV_GROUPS, n * SEL_BLK)
    s = jnp.einsum('btgrd,bgtmd->btgrm', qg, gk).astype(F32)
    p_s = masked_softmax(s, (kpos <= qpos[None, :, None, None])[:, :, :, None, :])
    o_s = jnp.einsum('btgrm,bgtmd->btgrd', p_s.astype(gv.dtype), gv)
    s = jnp.einsum('btgrd,bkgd->btgrk', qg, kw).astype(F32)
    d = qpos[:, None] - kwpos[None, :]
    wmask = (kwpos[None, :] >= 0) & (d >= 0) & (d < WINDOW)
    p_w = masked_softmax(s, wmask[None, :, None, None, :])
    o_w = jnp.einsum('btgrk,bkgd->btgrd', p_w.astype(vw.dtype), vw)
    g = jax.nn.sigmoid(gates.astype(F32)).reshape(b, t, KV_GROUPS, Q_PER_KV, 3)
    o = g[..., 0:1] * o_c + g[..., 1:2] * o_s + g[..., 2:3] * o_w
    return o.reshape(b, t, H_N * HEAD_DIM).astype(q.dtype)


def nsa_prep(n_q, n_kv, n_g, pos):
    b, t = n_q.shape[:2]
    q = rope(n_q.reshape(b, t, H_N, HEAD_DIM), pos)
    kv = n_kv.reshape(b, t, 6, KV_GROUPS, HEAD_DIM)
    rows = jnp.stack([rope(kv[:, :, 0], pos), kv[:, :, 1], rope(kv[:, :, 2], pos), kv[:, :, 3]], axis=2)
    win = jnp.stack([rope(kv[:, :, 4], pos), kv[:, :, 5]], axis=2)
    return q, n_g.reshape(b, t, H_N, 3), rows, win


def nsa_prompt(n_q, n_kv, n_g, cmp_params):
    b, s = n_q.shape[:2]
    pos = jnp.arange(s)
    q, gates, rows, win = nsa_prep(n_q, n_kv, n_g, pos)
    kc, vc, cmp_end, cmp_to_sel, ksb, vsb = nsa_context(rows, cmp_params)
    nb = s // QBLK
    kwp = jnp.pad(win, ((0, 0), (WINDOW, 0), (0, 0), (0, 0), (0, 0)))

    def one_block(args):
        qi, gi, bi = args
        start = bi * QBLK
        qpos = start + jnp.arange(QBLK)
        band = lax.dynamic_slice_in_dim(kwp, start, WINDOW + QBLK, axis=1)
        kwpos = start - WINDOW + jnp.arange(WINDOW + QBLK)
        return nsa_query_block(qi, qpos, gi, kc, vc, cmp_end, cmp_to_sel, ksb, vsb,
                               band[:, :, 0], band[:, :, 1], kwpos)

    blocks = lambda a: jnp.moveaxis(a.reshape((b, nb, QBLK) + a.shape[2:]), 1, 0)
    out = lax.map(one_block, (blocks(q), blocks(gates), jnp.arange(nb)))
    out = jnp.moveaxis(out, 0, 1).reshape(b, s, H_N * HEAD_DIM)
    return out, (rows, win[:, -min(WINDOW, s):])


def nsa_sample(n_q, n_kv, n_g, past_rows, win_buf, cmp_params):
    b, t = n_q.shape[:2]
    past_len = past_rows.shape[1]
    pos = past_len + jnp.arange(t)
    q, gates, rows, win = nsa_prep(n_q, n_kv, n_g, pos)
    full = jnp.concatenate([past_rows.astype(rows.dtype), rows], axis=1)
    kc, vc, cmp_end, cmp_to_sel, ksb, vsb = nsa_context(full, cmp_params)
    wl = win_buf.shape[1]
    kw_all = jnp.concatenate([win_buf.astype(win.dtype), win], axis=1)
    kwpos = past_len - wl + jnp.arange(wl + t)
    out = nsa_query_block(q, pos, gates, kc, vc, cmp_end, cmp_to_sel, ksb, vsb,
                          kw_all[:, :, 0], kw_all[:, :, 1], kwpos)
    return out, (rows, kw_all[:, -wl:])


def to_heads(a, n_heads):
    return a.reshape(a.shape[0], a.shape[1], n_heads, -1).astype(F32)


def split_cols(z):
    cut = lambda a, n: z[..., a:a + n]
    u_qk, m_v, m_o = cut(0, 2 * M_WIDTH), cut(2 * M_WIDTH, M_WIDTH), cut(3 * M_WIDTH, M_WIDTH)
    n_q, n_kv = cut(COL_NQ, N_WIDTH), cut(COL_NKV, 6 * KV_GROUPS * HEAD_DIM)
    g_q, g_k, g_v, g_r = (cut(COL_GLA + i * G_WIDTH, G_WIDTH) for i in range(4))
    m_if = cut(COL_SMALL + SMALL_MIF, 2 * H_M)
    n_g = cut(COL_SMALL + SMALL_NG, 3 * H_N)
    g_lr = cut(COL_SMALL + SMALL_GLR, GATE_RANK)
    return u_qk, m_v, m_o, m_if, n_q, n_kv, n_g, g_q, g_k, g_v, g_r, g_lr


W_IN_ORDER = (0, 1, 2, 4, 5, 7, 8, 9, 10, 3, 6, 11)


def _regroup_w_in_kernel(wt_ref, o_ref):
    o = (0,) + SPLIT_OFFSETS + (IN_COLS,)
    tk = wt_ref.shape[1]
    at = 0
    small = []
    for i in W_IN_ORDER:
        width = o[i + 1] - o[i]
        if width % HEAD_DIM == 0:
            o_ref[:, at:at + width] = jnp.transpose(wt_ref[o[i]:o[i + 1], :]).astype(BF16)
            at += width
        else:
            small.append(wt_ref[o[i]:o[i + 1], :])
    used = sum(s.shape[0] for s in small)
    small.append(jnp.zeros((HEAD_DIM - used, tk), F32))
    o_ref[:, at:at + HEAD_DIM] = jnp.transpose(jnp.concatenate(small, axis=0)).astype(BF16)
    at += HEAD_DIM
    o_ref[:, at:] = jnp.zeros((tk, IN_COLS_PAD - at), BF16)


def regroup_w_in(w_in):
    depth, k, n = w_in.shape
    tk = 256
    return pl.pallas_call(
        _regroup_w_in_kernel,
        grid=(depth, k // tk),
        in_specs=[pl.BlockSpec((None, n, tk), lambda l, i: (l, 0, i))],
        out_specs=pl.BlockSpec((None, tk, IN_COLS_PAD), lambda l, i: (l, i, 0)),
        out_shape=jax.ShapeDtypeStruct((depth, k, IN_COLS_PAD), BF16),
        compiler_params=pltpu.CompilerParams(
            dimension_semantics=("arbitrary", "arbitrary"), vmem_limit_bytes=VMEM_LIMIT_BYTES),
        name="regroup_w_in",
    )(jnp.transpose(w_in, (0, 2, 1)))


def trunk_layer(x, conv_buf, c0, n0, m0, s0, nsa_fn, layer, g_norms, w_in, conv_w, m_gate_b, m_norm,
                g_w2, g_b, g_norm, w_out, w_ff1, w_ff2):
    b, t, d = x.shape
    x2 = x.reshape(b * t, d)
    z = norm_matmul(x2, g_norms[0], w_in, layer, IN_TN)
    z3 = z.reshape(b, t, -1)
    t_pad = -(-t // MIX_L) * MIX_L
    zp = z if t_pad == t else jnp.pad(z3, ((0, 0), (0, t_pad - t), (0, 0))).reshape(b * t_pad, -1)
    hm, hg, c, n, m, s = mixers_pallas(zp, b, t, conv_buf, c0, n0, m0, s0,
                                       conv_w, m_gate_b, m_norm, g_w2, g_b, g_norm)
    new_conv = z3[:, t - (CONV_W - 1):, :2 * M_WIDTH]
    hn, nsa_state = nsa_fn(zp, b, t, t_pad)
    x2 = matmul_norm_res(hm.reshape(b * t, -1), hn.reshape(b * t, -1), hg.reshape(b * t, -1),
                         w_out, layer, g_norms[1], x2)
    x2 = ffn(x2, g_norms[2], w_ff1, w_ff2, layer, g_norms[3])
    return x2.reshape(b, t, d), (nsa_state[0], nsa_state[1], c, n, m, new_conv, s)


def kernel(x_prompt, x_sample, cache_nsa_kv, state_nsa_win, state_mlstm_C, state_mlstm_n, state_mlstm_m, state_mlstm_conv, state_gla_S, page_table, norms, w_in, mlstm_conv_w, mlstm_gate_b, mlstm_norm, nsa_cmp_pe, nsa_cmp_w1, nsa_cmp_w2, gla_gate_w2, gla_gate_b, gla_norm, w_out, w_ff1, w_ff2):
    xp, xs = x_prompt, x_sample
    bp = xp.shape[0]
    db = xs.shape[0]
    past_len = page_table.shape[1] * PAGE_SIZE
    conv0 = jnp.zeros((bp, CONV_W - 1, 2 * M_WIDTH), xp.dtype)
    c0 = jnp.zeros((bp, H_M, HEAD_DIM, HEAD_DIM), F32)
    n0 = jnp.zeros((bp, H_M, HEAD_DIM), F32)
    m0 = jnp.zeros((bp, H_M), F32)
    s0 = jnp.zeros((bp, H_G, HEAD_DIM, HEAD_DIM), F32)
    w_in_b = regroup_w_in(w_in)
    w_out_b = w_out.astype(BF16)
    w_ff1_b = w_ff1.astype(BF16)
    w_ff2_b = w_ff2.astype(BF16)
    cmp_pe = nsa_cmp_pe.reshape(DEPTH, 2, 1, CMP_LEN * HEAD_DIM)
    cmp_w1_b = nsa_cmp_w1.reshape(DEPTH, 2, CMP_LEN * HEAD_DIM, CMP_HID).astype(BF16)
    cmp_w2_b = nsa_cmp_w2.astype(BF16)
    n_pool = cache_nsa_kv.shape[1]
    cache2d = cache_nsa_kv.reshape(-1, HEAD_DIM)
    acc_p = [[] for _ in range(7)]
    acc_s = [[] for _ in range(7)]
    for l in range(DEPTH):
        weights = (l, norms[l], w_in_b, mlstm_conv_w[l], mlstm_gate_b[l], mlstm_norm[l],
                   gla_gate_w2[l], gla_gate_b[l], gla_norm[l], w_out_b, w_ff1_b, w_ff2_b)
        cmp_w = (cmp_pe[l], cmp_w1_b[l], cmp_w2_b[l])
        nsa_p = lambda zp, b, t, t_pad, cmp_w=cmp_w: nsa_prompt_pallas(zp, b, t, cmp_w)
        nsa_s = lambda zp, b, t, t_pad, l=l, cmp_w=cmp_w: nsa_sample_pallas(
            zp, b, t, t_pad, l, cache2d, page_table + l * n_pool, state_nsa_win, cmp_w)
        xp, st_p = trunk_layer(xp, conv0, c0, n0, m0, s0, nsa_p, *weights)
        xs, st_s = trunk_layer(xs, state_mlstm_conv[l], state_mlstm_C[l], state_mlstm_n[l],
                               state_mlstm_m[l], state_gla_S[l], nsa_s, *weights)
        for acc, a in zip(acc_p, st_p):
            acc.append(a.astype(xp.dtype))
        for acc, a in zip(acc_s, st_s):
            acc.append(a.astype(xs.dtype))
    nsa_rows_p, nsa_win_p, mlstm_c_p, mlstm_n_p, mlstm_m_p, mlstm_conv_p, gla_s_p = [jnp.stack(a) for a in acc_p]
    nsa_rows_s, nsa_win_s, mlstm_c_s, mlstm_n_s, mlstm_m_s, mlstm_conv_s, gla_s_s = [jnp.stack(a) for a in acc_s]
    return (xp, xs, nsa_rows_p, nsa_rows_s, nsa_win_p, nsa_win_s, mlstm_c_p, mlstm_c_s,
            mlstm_n_p, mlstm_n_s, mlstm_m_p, mlstm_m_s, mlstm_conv_p, mlstm_conv_s, gla_s_p, gla_s_s)
```

```python
import functools
import math

import jax
import jax.numpy as jnp
import numpy as np
from jax import lax
from jax.experimental import pallas as pl
from jax.experimental.pallas import tpu as pltpu

D_MODEL = 2048
DEPTH = 4
PAGE_SIZE = 128
HEAD_DIM = 128
D_MIX = D_MODEL
M_WIDTH = D_MIX // 4
G_WIDTH = D_MIX // 4
N_WIDTH = D_MIX - M_WIDTH - G_WIDTH
H_M = M_WIDTH // HEAD_DIM
H_N = N_WIDTH // HEAD_DIM
H_G = G_WIDTH // HEAD_DIM
KV_GROUPS = 2
Q_PER_KV = H_N // KV_GROUPS
D_FF = 4 * D_MODEL
CONV_W = 4
CHUNK = 64
CMP_STRIDE = 16
CMP_LEN = 2 * CMP_STRIDE
CMP_HID = 256
SEL_BLK = 64
N_SELECT = 16
WINDOW = 512
QBLK = 128
GATE_RANK = 16
GLA_GATE_TEMP = 16.0
ROPE_THETA = 10000.0
EPS = 1e-6
TINY = 1e-30
FORCE_BONUS = 1e3
NEG_BIG = -1e9
SPLIT_SIZES = (2 * M_WIDTH, M_WIDTH, M_WIDTH, 2 * H_M,
               N_WIDTH, 6 * KV_GROUPS * HEAD_DIM, 3 * H_N,
               G_WIDTH, G_WIDTH, G_WIDTH, G_WIDTH, GATE_RANK)
IN_COLS = sum(SPLIT_SIZES)
SPLIT_OFFSETS = tuple(int(o) for o in np.cumsum(SPLIT_SIZES)[:-1])

F32 = jnp.float32
BF16 = jnp.bfloat16

VMEM_LIMIT_BYTES = 56 * 1024 * 1024
IN_TN = 1024
COL_NQ = 2 * M_WIDTH + 2 * M_WIDTH
COL_NKV = COL_NQ + N_WIDTH
COL_GLA = COL_NKV + 6 * KV_GROUPS * HEAD_DIM
COL_SMALL = COL_GLA + 4 * G_WIDTH
SMALL_MIF = 0
SMALL_NG = 2 * H_M
SMALL_GLR = SMALL_NG + 3 * H_N
IN_COLS_PAD = 7168
NEG = -1e30
NSA_TQ = 256
NSA_TK = 1024


def _rms(x, g):
    return x * lax.rsqrt(jnp.mean(x * x, axis=-1, keepdims=True) + EPS) * g


def _norm_matmul_kernel(x_ref, g_ref, w_ref, o_ref, xn_ref):
    @pl.when(pl.program_id(1) == 0)
    def _():
        xn_ref[...] = _rms(x_ref[...], g_ref[...]).astype(BF16)

    o_ref[...] = jnp.dot(xn_ref[...], w_ref[...], preferred_element_type=F32)


def norm_matmul(x, g, w, layer, tn):
    m, k = x.shape
    n = w.shape[2]
    tm = min(m, 1024)
    return pl.pallas_call(
        _norm_matmul_kernel,
        grid=(m // tm, n // tn),
        in_specs=[pl.BlockSpec((tm, k), lambda i, j: (i, 0)),
                  pl.BlockSpec((1, k), lambda i, j: (0, 0)),
                  pl.BlockSpec((None, k, tn), lambda i, j: (layer, 0, j))],
        out_specs=pl.BlockSpec((tm, tn), lambda i, j: (i, j)),
        out_shape=jax.ShapeDtypeStruct((m, n), F32),
        scratch_shapes=[pltpu.VMEM((tm, k), BF16)],
        compiler_params=pltpu.CompilerParams(
            dimension_semantics=("arbitrary", "arbitrary"), vmem_limit_bytes=VMEM_LIMIT_BYTES),
        name="norm_matmul",
    )(x, g.reshape(1, k), w)


def _matmul_norm_res_kernel(a0_ref, a1_ref, a2_ref, w_ref, g_ref, r_ref, o_ref):
    k0, k1 = a0_ref.shape[1], a0_ref.shape[1] + a1_ref.shape[1]
    y = (jnp.dot(a0_ref[...], w_ref[:k0, :], preferred_element_type=F32)
         + jnp.dot(a1_ref[...], w_ref[k0:k1, :], preferred_element_type=F32)
         + jnp.dot(a2_ref[...], w_ref[k1:, :], preferred_element_type=F32))
    o_ref[...] = r_ref[...] + _rms(y, g_ref[...])


def matmul_norm_res(a0, a1, a2, w, layer, g, r):
    m = a0.shape[0]
    _, k, n = w.shape
    tm = min(m, 512)
    rows = lambda a: pl.BlockSpec((tm, a.shape[1]), lambda i: (i, 0))
    return pl.pallas_call(
        _matmul_norm_res_kernel,
        grid=(m // tm,),
        in_specs=[rows(a0), rows(a1), rows(a2),
                  pl.BlockSpec((None, k, n), lambda i: (layer, 0, 0)),
                  pl.BlockSpec((1, n), lambda i: (0, 0)),
                  pl.BlockSpec((tm, n), lambda i: (i, 0))],
        out_specs=pl.BlockSpec((tm, n), lambda i: (i, 0)),
        out_shape=jax.ShapeDtypeStruct((m, n), F32),
        compiler_params=pltpu.CompilerParams(
            dimension_semantics=("arbitrary",), vmem_limit_bytes=VMEM_LIMIT_BYTES),
        name="matmul_norm_res",
    )(a0, a1, a2, w, g.reshape(1, n), r)


def _ffn_kernel(x_ref, g2_ref, w1_ref, w2_ref, g3_ref, o_ref, xn_ref, acc_ref):
    f = pl.program_id(1)

    @pl.when(f == 0)
    def _():
        xn_ref[...] = _rms(x_ref[...], g2_ref[...]).astype(BF16)
        acc_ref[...] = jnp.zeros_like(acc_ref)

    h = jnp.dot(xn_ref[...], w1_ref[...], preferred_element_type=F32)
    a = jnp.square(jnp.maximum(h, 0.0)).astype(BF16)
    acc_ref[...] += jnp.dot(a, w2_ref[...], preferred_element_type=F32)

    @pl.when(f == pl.num_programs(1) - 1)
    def _():
        o_ref[...] = x_ref[...] + _rms(acc_ref[...], g3_ref[...])


def ffn(x, g2, w1, w2, layer, g3):
    m, d = x.shape
    dff = w1.shape[2]
    tm = min(m, 512)
    tf = 1024
    return pl.pallas_call(
        _ffn_kernel,
        grid=(m // tm, dff // tf),
        in_specs=[pl.BlockSpec((tm, d), lambda i, f: (i, 0)),
                  pl.BlockSpec((1, d), lambda i, f: (0, 0)),
                  pl.BlockSpec((None, d, tf), lambda i, f: (layer, 0, f)),
                  pl.BlockSpec((None, tf, d), lambda i, f: (layer, f, 0)),
                  pl.BlockSpec((1, d), lambda i, f: (0, 0))],
        out_specs=pl.BlockSpec((tm, d), lambda i, f: (i, 0)),
        out_shape=jax.ShapeDtypeStruct((m, d), F32),
        scratch_shapes=[pltpu.VMEM((tm, d), BF16), pltpu.VMEM((tm, d), F32)],
        compiler_params=pltpu.CompilerParams(
            dimension_semantics=("arbitrary", "arbitrary"), vmem_limit_bytes=VMEM_LIMIT_BYTES),
        name="ffn",
    )(x, g2.reshape(1, d), w1, w2, g3.reshape(1, d))


def rope_tables(pos):
    half = HEAD_DIM // 2
    inv_freq = jnp.exp(-math.log(ROPE_THETA) * jnp.arange(half, dtype=F32) / half)
    ang = pos.astype(F32)[:, None] * inv_freq[None, :]
    cos, sin = jnp.cos(ang), jnp.sin(ang)
    return jnp.concatenate([cos, cos], axis=-1), jnp.concatenate([-sin, sin], axis=-1)


def _rope(x, cos2, sin2):
    return x * cos2 + pltpu.roll(x, HEAD_DIM // 2, 1) * sin2


def _nsa_prep_kernel(nq_ref, nkv_ref, cos_ref, sin_ref, q_ref, rows_ref, win_ref, kvb_ref, cmp_ref):
    cos2 = cos_ref[...]
    sin2 = sin_ref[...]
    for h in range(H_N):
        sl = slice(h * HEAD_DIM, (h + 1) * HEAD_DIM)
        q_ref[:, sl] = (_rope(nq_ref[:, sl], cos2, sin2) * (HEAD_DIM ** -0.5)).astype(BF16)
    for c in range(6 * KV_GROUPS):
        slot = c // KV_GROUPS
        x = nkv_ref[:, c * HEAD_DIM:(c + 1) * HEAD_DIM]
        if slot % 2 == 0:
            x = _rope(x, cos2, sin2)
        xb = x.astype(BF16)
        tm = x.shape[0]
        if slot < 4:
            rows_ref[pl.ds(c, tm, stride=ROW_VECS), :] = x
        else:
            win_ref[pl.ds(c - ROW_VECS, tm, stride=WIN_VECS), :] = x
        if slot < 2:
            cmp_ref[0, c] = xb
        else:
            kvb_ref[:, (c - 4) * HEAD_DIM:(c - 3) * HEAD_DIM] = xb


def nsa_prep_pallas(z, b, s, cos2, sin2, tm=512):
    m = b * s
    nsb = s // tm
    return pl.pallas_call(
        _nsa_prep_kernel,
        grid=(m // tm,),
        in_specs=[pl.BlockSpec((tm, N_WIDTH), lambda i: (i, COL_NQ // N_WIDTH)),
                  pl.BlockSpec((tm, 1536), lambda i: (i, COL_NKV // 1536)),
                  pl.BlockSpec((tm, HEAD_DIM), lambda i: (i % nsb, 0)),
                  pl.BlockSpec((tm, HEAD_DIM), lambda i: (i % nsb, 0))],
        out_specs=[pl.BlockSpec((tm, N_WIDTH), lambda i: (i, 0)),
                   pl.BlockSpec((tm * 8, HEAD_DIM), lambda i: (i, 0)),
                   pl.BlockSpec((tm * 4, HEAD_DIM), lambda i: (i, 0)),
                   pl.BlockSpec((tm, 1024), lambda i: (i, 0)),
                   pl.BlockSpec((1, 4, tm, HEAD_DIM), lambda i: (i // nsb, 0, i % nsb, 0))],
        out_shape=[jax.ShapeDtypeStruct((m, N_WIDTH), BF16),
                   jax.ShapeDtypeStruct((m * 8, HEAD_DIM), F32),
                   jax.ShapeDtypeStruct((m * 4, HEAD_DIM), F32),
                   jax.ShapeDtypeStruct((m, 1024), BF16),
                   jax.ShapeDtypeStruct((b, 4, s, HEAD_DIM), BF16)],
        compiler_params=pltpu.CompilerParams(
            dimension_semantics=("arbitrary",), vmem_limit_bytes=VMEM_LIMIT_BYTES),
        name="nsa_prep",
    )(z, z, cos2, sin2)


def _nsa_cmp_kernel(x_ref, pe_ref, w1_ref, w2_ref, o_ref):
    x = x_ref[0, 0]
    nb = x.shape[0]
    half = CMP_STRIDE * HEAD_DIM
    first = jnp.dot(x, w1_ref[0, :half, :], preferred_element_type=F32)
    second = jnp.dot(x, w1_ref[0, half:, :], preferred_element_type=F32)
    pe = jnp.broadcast_to(pe_ref[0], (8, 2 * half)).astype(BF16)
    bias = jnp.dot(pe, w1_ref[0], preferred_element_type=F32)[0:1]
    hid = first + pltpu.roll(second, nb - 1, 0) + bias
    o_ref[0, 0] = jnp.dot(jax.nn.gelu(hid).astype(BF16), w2_ref[0], preferred_element_type=F32).astype(BF16)


def nsa_compress_prompt(cmp_in, pe, w1, w2):
    b, _, nb, kdim = cmp_in.shape
    return pl.pallas_call(
        _nsa_cmp_kernel,
        grid=(b, 4),
        in_specs=[pl.BlockSpec((1, 1, nb, kdim), lambda i, c: (i, c, 0, 0)),
                  pl.BlockSpec((1, 1, 2 * kdim), lambda i, c: (c // 2, 0, 0)),
                  pl.BlockSpec((1, 2 * kdim, CMP_HID), lambda i, c: (c // 2, 0, 0)),
                  pl.BlockSpec((1, CMP_HID, HEAD_DIM), lambda i, c: (c // 2, 0, 0))],
        out_specs=pl.BlockSpec((1, 1, nb, HEAD_DIM), lambda i, c: (i, c, 0, 0)),
        out_shape=jax.ShapeDtypeStruct((b, 4, nb, HEAD_DIM), BF16),
        compiler_params=pltpu.CompilerParams(
            dimension_semantics=("arbitrary", "arbitrary"), vmem_limit_bytes=VMEM_LIMIT_BYTES),
        name="nsa_compress",
    )(cmp_in, pe, w1, w2)


def _softmax_rows(s, mask):
    sm = jnp.where(mask, s, NEG)
    m = jnp.max(sm, axis=-1, keepdims=True)
    p = jnp.where(mask, jnp.exp(sm - m), 0.0)
    return p / jnp.maximum(jnp.sum(p, axis=-1, keepdims=True), TINY)


_NT = (((1,), (1,)), ((), ()))


def _nsa_attn_kernel(q_ref, kc_ref, vc_ref, ks_ref, vs_ref, kw_ref, vw_ref, gate_ref, o_ref):
    tq = NSA_TQ
    r4 = Q_PER_KV
    g = pl.program_id(1)
    i = pl.program_id(2)
    q4 = q_ref[...]
    q = jnp.concatenate([q4[:, r * HEAD_DIM:(r + 1) * HEAD_DIM] for r in range(r4)], axis=0)
    q0 = i * tq
    tpos = q0 + lax.broadcasted_iota(jnp.int32, (tq, 1), 0)

    ncp = kc_ref.shape[2]
    s = lax.dot_general(q, kc_ref[0, 0], _NT, preferred_element_type=F32).reshape(r4, tq, ncp)
    cend = lax.broadcasted_iota(jnp.int32, (tq, ncp), 1) * CMP_STRIDE + (CMP_LEN - 1)
    p_c = _softmax_rows(s, (cend <= tpos)[None])
    o_c = jnp.dot(p_c.reshape(r4 * tq, ncp).astype(BF16), vc_ref[0, 0], preferred_element_type=F32)

    psum = p_c[0] + p_c[1] + p_c[2] + p_c[3]
    n_i = lax.broadcasted_iota(jnp.int32, (ncp, HEAD_DIM), 0)
    j_i = lax.broadcasted_iota(jnp.int32, (ncp, HEAD_DIM), 1)
    ratio = SEL_BLK // CMP_STRIDE
    c2s = jnp.where((n_i >= ratio * j_i - 1) & (n_i <= ratio * j_i + ratio - 1) & (n_i < ncp - 1)
                    & (j_i < ncp // ratio), 1.0, 0.0).astype(BF16)
    p_hi = psum.astype(BF16)
    p_lo = (psum - p_hi.astype(F32)).astype(BF16)
    imp = (jnp.dot(p_hi, c2s, preferred_element_type=F32)
           + jnp.dot(p_lo, c2s, preferred_element_type=F32))

    ns = ncp // ratio
    imp_t = jnp.transpose(imp)[:ns]
    jj = lax.broadcasted_iota(jnp.int32, (ns, tq), 0)
    tt = q0 + lax.broadcasted_iota(jnp.int32, (ns, tq), 1)
    cur = tt // SEL_BLK
    forced = (jj == 0) | (jj == cur) | (jj == cur - 1)
    score = jnp.where(jj * SEL_BLK <= tt, imp_t + jnp.where(forced, FORCE_BONUS, 0.0), NEG_BIG)
    rank = jnp.zeros((ns, tq), F32)
    for jp in range(ns):
        row = score[jp:jp + 1, :]
        rank = rank + jnp.where(row > score, 1.0, jnp.where((row == score) & (jj > jp), 1.0, 0.0))
    sel_t = jnp.where(rank < float(N_SELECT), 1.0, 0.0)
    sel_t = jnp.concatenate([sel_t, jnp.zeros((HEAD_DIM - ns, tq), F32)], axis=0)
    sel = jnp.transpose(sel_t)

    tk = NSA_TK
    n_tiles = (q0 + tq + tk - 1) // tk
    unpicked = jnp.where(sel > 0.5, 0.0, NEG).astype(BF16)
    q_aug = jnp.concatenate([q, jnp.concatenate([unpicked] * r4, axis=0)], axis=1)
    blk_lane = lax.broadcasted_iota(jnp.int32, (tk, HEAD_DIM), 1)
    blk_of_key = lax.broadcasted_iota(jnp.int32, (tk, HEAD_DIM), 0) // SEL_BLK

    hc = r4
    chains = r4 // hc
    q_chain = [q_aug[c * hc * tq:(c + 1) * hc * tq] for c in range(chains)]

    def tile_update(states, kt, causal):
        k0 = pl.multiple_of(kt * tk, tk)
        onehot = jnp.where(blk_lane == kt * (tk // SEL_BLK) + blk_of_key, 1.0, 0.0).astype(BF16)
        k_aug = jnp.concatenate([ks_ref[pl.ds(k0, tk), :], onehot], axis=1)
        v_tile = vs_ref[pl.ds(k0, tk), :]
        out = []
        for c in range(chains):
            m_run, l_run, acc = states[c]
            s = lax.dot_general(q_chain[c], k_aug, _NT, preferred_element_type=F32).reshape(hc, tq, tk)
            if causal is not None:
                s = jnp.where(causal, s, NEG)
            m_new = jnp.maximum(m_run, jnp.max(s, axis=-1, keepdims=True))
            alpha = jnp.exp(m_run - m_new)
            p = jnp.exp(s - m_new)
            l_new = alpha * l_run + jnp.sum(p, axis=-1, keepdims=True)
            pv = jnp.dot(p.reshape(hc * tq, tk).astype(BF16), v_tile, preferred_element_type=F32)
            out.append((m_new, l_new, alpha.reshape(hc * tq, 1) * acc + pv))
        return tuple(out)

    init = tuple((jnp.full((hc, tq, 1), NEG, F32), jnp.zeros((hc, tq, 1), F32),
                  jnp.zeros((hc * tq, HEAD_DIM), F32)) for _ in range(chains))
    states = lax.fori_loop(0, n_tiles - 1, lambda kt, st: tile_update(st, kt, None), init)
    kpos = (n_tiles - 1) * tk + lax.broadcasted_iota(jnp.int32, (tq, tk), 1)
    states = tile_update(states, n_tiles - 1, (kpos <= tpos)[None])
    o_s = jnp.concatenate([acc / jnp.maximum(l_run, TINY).reshape(hc * tq, 1) for _, l_run, acc in states], axis=0)

    wb = WINDOW + tq
    w0 = pl.multiple_of(jnp.maximum(q0 - WINDOW, 0), tq)
    s = lax.dot_general(q, kw_ref[pl.ds(w0, wb), :], _NT, preferred_element_type=F32).reshape(r4, tq, wb)
    dist = tpos - (w0 + lax.broadcasted_iota(jnp.int32, (tq, wb), 1))
    wmask = ((dist >= 0) & (dist < WINDOW))[None]
    sm = jnp.where(wmask, s, NEG)
    p_w = jnp.where(wmask, jnp.exp(sm - jnp.max(sm, axis=-1, keepdims=True)), 0.0)
    den_w = jnp.maximum(jnp.sum(p_w, axis=-1, keepdims=True), TINY).reshape(r4 * tq, 1)
    o_w = jnp.dot(p_w.reshape(r4 * tq, wb).astype(BF16), vw_ref[pl.ds(w0, wb), :],
                  preferred_element_type=F32) / den_w

    gates = jax.nn.sigmoid(gate_ref[...])
    for r in range(r4):
        def gate(c):
            lane0 = SMALL_NG + 3 * r + c
            lane1 = lane0 + 3 * r4
            return jnp.where(g == 0, gates[:, lane0:lane0 + 1], gates[:, lane1:lane1 + 1])
        rows = slice(r * tq, (r + 1) * tq)
        o = gate(0) * o_c[rows] + gate(1) * o_s[rows] + gate(2) * o_w[rows]
        o_ref[:, r * HEAD_DIM:(r + 1) * HEAD_DIM] = o.astype(BF16)


def nsa_attention_prompt(q_b, kcvc, kvb, z, b, s):
    tq = NSA_TQ
    nq = s // tq
    ncp = kcvc.shape[2]
    gw = Q_PER_KV * HEAD_DIM
    seq_block = lambda col: pl.BlockSpec((s, HEAD_DIM), lambda bi, g, i: (bi, col + g))
    return pl.pallas_call(
        _nsa_attn_kernel,
        grid=(b, KV_GROUPS, nq),
        in_specs=[pl.BlockSpec((tq, gw), lambda bi, g, i: (bi * nq + i, g)),
                  pl.BlockSpec((1, 1, ncp, HEAD_DIM), lambda bi, g, i: (bi, g, 0, 0)),
                  pl.BlockSpec((1, 1, ncp, HEAD_DIM), lambda bi, g, i: (bi, KV_GROUPS + g, 0, 0)),
                  seq_block(0), seq_block(2), seq_block(4), seq_block(6),
                  pl.BlockSpec((tq, HEAD_DIM), lambda bi, g, i: (bi * nq + i, COL_SMALL // HEAD_DIM))],
        out_specs=pl.BlockSpec((tq, gw), lambda bi, g, i: (bi * nq + i, g)),
        out_shape=jax.ShapeDtypeStruct((b * s, N_WIDTH), BF16),
        compiler_params=pltpu.CompilerParams(
            dimension_semantics=("arbitrary", "arbitrary", "arbitrary"), vmem_limit_bytes=VMEM_LIMIT_BYTES),
        name="nsa_attention",
    )(q_b, kcvc, kcvc, kvb, kvb, kvb, kvb, z)


def nsa_prompt_pallas(z, b, s, cmp_w):
    pe, w1, w2 = cmp_w
    cos2, sin2 = rope_tables(jnp.arange(s))
    q_b, rows_f, win_f, kvb, cmp_in = nsa_prep_pallas(z, b, s, cos2, sin2)
    kcvc = nsa_compress_prompt(cmp_in.reshape(b, 4, s // CMP_STRIDE, CMP_STRIDE * HEAD_DIM), pe, w1, w2)
    hn = nsa_attention_prompt(q_b, kcvc, kvb, z, b, s)
    rows = rows_f.reshape(b, s, 4, KV_GROUPS, HEAD_DIM)
    win = win_f.reshape(b, s, 2, KV_GROUPS, HEAD_DIM)[:, -min(WINDOW, s):]
    return hn, (rows, win)


SAMPLE_PAGES = 16
SAMPLE_TS = 16
ROW_COLS = 4 * KV_GROUPS * HEAD_DIM
ROW_VECS = 4 * KV_GROUPS
WIN_VECS = 2 * KV_GROUPS
NEG_DEAD = -3e38


def _page_specs():
    def spec(k):
        return pl.BlockSpec((PAGE_SIZE * ROW_VECS, HEAD_DIM), lambda bi, i, pt: (pt[bi, i * SAMPLE_PAGES + k], 0))
    return [spec(k) for k in range(SAMPLE_PAGES)]


def _page_vecs(pg, vec):
    return pg[pl.ds(vec, PAGE_SIZE, stride=ROW_VECS), :]


def _nsa_cmp_sample_kernel(pt_ref, *refs):
    del pt_ref
    pages = refs[:SAMPLE_PAGES]
    pe_ref, w1_ref, w2_ref, o_ref, carry_sc, bias_sc = refs[SAMPLE_PAGES:]
    i = pl.program_id(1)
    half = CMP_STRIDE * HEAD_DIM
    nbp = PAGE_SIZE // CMP_STRIDE
    nb = SAMPLE_PAGES * nbp

    @pl.when(i == 0)
    def _():
        carry_sc[...] = jnp.zeros_like(carry_sc)
        for kv in range(2):
            pe = jnp.broadcast_to(pe_ref[kv], (8, 2 * half)).astype(BF16)
            bias_sc[kv] = jnp.dot(pe, w1_ref[kv], preferred_element_type=F32)

    row = lax.broadcasted_iota(jnp.int32, (nb, CMP_HID), 0)
    for kv in range(2):
        def flat(c):
            piece = lambda pg, j: pg[pl.ds(j * ROW_VECS + c, nbp, stride=CMP_STRIDE * ROW_VECS), :]
            return jnp.concatenate(
                [jnp.concatenate([piece(pg, j) for j in range(CMP_STRIDE)], axis=1) for pg in pages],
                axis=0)
        x = jnp.concatenate([flat(kv * KV_GROUPS + g) for g in range(KV_GROUPS)], axis=0).astype(BF16)
        first_all = jnp.dot(x, w1_ref[kv, :half, :], preferred_element_type=F32)
        second_all = jnp.dot(x, w1_ref[kv, half:, :], preferred_element_type=F32)
        for g in range(KV_GROUPS):
            c = kv * KV_GROUPS + g
            first = first_all[g * nb:(g + 1) * nb]
            second = second_all[g * nb:(g + 1) * nb]
            shifted = jnp.where(row == 0, carry_sc[c, 7:8, :], pltpu.roll(first, 1, 0))
            hid = shifted + second + bias_sc[kv, 0:1, :]
            o_ref[0, c] = jnp.dot(jax.nn.gelu(hid).astype(BF16), w2_ref[kv],
                                  preferred_element_type=F32).astype(BF16)
            carry_sc[c] = first[nb - 8:, :]


def nsa_compress_sample(pt, cache2d, pe, w1, w2, db):
    n_pages = pt.shape[1]
    nb_all = n_pages * (PAGE_SIZE // CMP_STRIDE)
    nb = SAMPLE_PAGES * (PAGE_SIZE // CMP_STRIDE)
    const3 = lambda bi, i, pt: (0, 0, 0)
    grid_spec = pltpu.PrefetchScalarGridSpec(
        num_scalar_prefetch=1,
        grid=(db, n_pages // SAMPLE_PAGES),
        in_specs=_page_specs() + [pl.BlockSpec(pe.shape, const3), pl.BlockSpec(w1.shape, const3),
                                   pl.BlockSpec(w2.shape, const3)],
        out_specs=pl.BlockSpec((1, 2 * KV_GROUPS, nb, HEAD_DIM), lambda bi, i, pt: (bi, 0, i, 0)),
        scratch_shapes=[pltpu.VMEM((2 * KV_GROUPS, 8, CMP_HID), F32), pltpu.VMEM((2, 8, CMP_HID), F32)])
    return pl.pallas_call(
        _nsa_cmp_sample_kernel,
        grid_spec=grid_spec,
        out_shape=jax.ShapeDtypeStruct((db, 2 * KV_GROUPS, nb_all, HEAD_DIM), BF16),
        compiler_params=pltpu.CompilerParams(
            dimension_semantics=("arbitrary", "arbitrary"), vmem_limit_bytes=VMEM_LIMIT_BYTES),
        name="nsa_compress_sample",
    )(pt, *([cache2d] * SAMPLE_PAGES), pe, w1, w2)


def _nsa_attn_sample_kernel(pt_ref, *refs, past_len, t_valid, n_steps):
    del pt_ref
    pages = refs[:SAMPLE_PAGES]
    (q_ref, new_ref, kc_ref, win_ref, gate_ref, c2s_ref, o_ref,
     sel_sc, m_sc, l_sc, acc_sc, oc_sc, ow_sc) = refs[SAMPLE_PAGES:]
    i = pl.program_id(1)
    ts = SAMPLE_TS
    r4 = Q_PER_KV
    rows = r4 * ts
    d = HEAD_DIM
    blocks_per_step = SAMPLE_PAGES * PAGE_SIZE // SEL_BLK
    ns = past_len // SEL_BLK + 1
    trow = lax.broadcasted_iota(jnp.int32, (rows, 1), 0) % ts
    qpos = past_len + trow

    def q_of(g):
        return jnp.concatenate([q_ref[:, (g * r4 + r) * d:(g * r4 + r + 1) * d] for r in range(r4)], axis=0)

    @pl.when(i == 0)
    def _():
        ncp = kc_ref.shape[2]
        nsl = (n_steps + 1) * d
        c2s = c2s_ref[...]
        lane_r = lax.broadcasted_iota(jnp.int32, (ts, nsl), 1)
        blk_r = (lane_r // d) * blocks_per_step + lane_r % d
        live = (lane_r % d < blocks_per_step) & (blk_r < ns)
        blk_f = blk_r.astype(F32)
        qpos_t = past_len + lax.broadcasted_iota(jnp.int32, (ts, 1), 0)
        cur = qpos_t // SEL_BLK
        forced = (blk_r == 0) | (blk_r == cur) | (blk_r == cur - 1)
        midx = lax.broadcasted_iota(jnp.int32, (rows, ncp), 1)
        cmask = (midx >= 1) & ((midx - 1) * CMP_STRIDE + CMP_LEN - 1 <= qpos)
        sidx = lax.broadcasted_iota(jnp.int32, (rows, ts), 1)
        new_ok = (sidx <= trow) & (sidx < t_valid)
        wl = win_ref.shape[0] // WIN_VECS
        dist_buf = trow + wl - lax.broadcasted_iota(jnp.int32, (rows, wl), 1)
        buf_ok = (dist_buf >= 0) & (dist_buf < WINDOW)
        dist_new = trow - sidx
        wnew_ok = (dist_new >= 0) & (dist_new < WINDOW) & (sidx < t_valid)
        for g in range(KV_GROUPS):
            qg = q_of(g)
            s = lax.dot_general(qg, kc_ref[0, g], _NT, preferred_element_type=F32)
            p_c = _softmax_rows(s, cmask)
            oc_sc[g] = jnp.dot(p_c.astype(BF16), kc_ref[0, KV_GROUPS + g], preferred_element_type=F32)
            psum = p_c[0:ts] + p_c[ts:2 * ts] + p_c[2 * ts:3 * ts] + p_c[3 * ts:4 * ts]
            p_hi = psum.astype(BF16)
            p_lo = (psum - p_hi.astype(F32)).astype(BF16)
            imp = (jnp.dot(p_hi, c2s, preferred_element_type=F32) + jnp.dot(p_lo, c2s, preferred_element_type=F32))
            score = jnp.where(live, jnp.where(blk_r * SEL_BLK <= qpos_t, imp + jnp.where(forced, FORCE_BONUS, 0.0),
                                              NEG_BIG), NEG_DEAD)
            sel = jnp.zeros((ts, nsl), F32)
            for _ in range(N_SELECT):
                top = jnp.max(score, axis=1, keepdims=True)
                first = jnp.min(jnp.where(score == top, blk_f, 1e9), axis=1, keepdims=True)
                hit = live & (blk_f == first)
                sel = jnp.where(hit, 1.0, sel)
                score = jnp.where(hit, NEG_DEAD, score)
            for step in range(n_steps):
                sel_sc[g, step] = jnp.concatenate([sel[:, step * d:(step + 1) * d]] * r4, axis=0)
            knew = new_ref[:, g * d:(g + 1) * d]
            vnew = new_ref[:, (KV_GROUPS + g) * d:(KV_GROUPS + g + 1) * d]
            sm = jnp.where(new_ok, lax.dot_general(qg, knew, _NT, preferred_element_type=F32), NEG)
            m0 = jnp.max(sm, axis=1, keepdims=True)
            p = jnp.where(new_ok, jnp.exp(sm - m0), 0.0)
            m_sc[g] = m0
            l_sc[g] = jnp.sum(p, axis=1, keepdims=True)
            acc_sc[g] = jnp.dot(p.astype(BF16), vnew, preferred_element_type=F32)
            kwb = win_ref[pl.ds(g, wl, stride=WIN_VECS), :].astype(BF16)
            vwb = win_ref[pl.ds(KV_GROUPS + g, wl, stride=WIN_VECS), :].astype(BF16)
            kwn = new_ref[:, (2 * KV_GROUPS + g) * d:(2 * KV_GROUPS + g + 1) * d]
            vwn = new_ref[:, (3 * KV_GROUPS + g) * d:(3 * KV_GROUPS + g + 1) * d]
            s1 = jnp.where(buf_ok, lax.dot_general(qg, kwb, _NT, preferred_element_type=F32), NEG)
            s2 = jnp.where(wnew_ok, lax.dot_general(qg, kwn, _NT, preferred_element_type=F32), NEG)
            mw = jnp.maximum(jnp.max(s1, axis=1, keepdims=True), jnp.max(s2, axis=1, keepdims=True))
            p1 = jnp.where(buf_ok, jnp.exp(s1 - mw), 0.0)
            p2 = jnp.where(wnew_ok, jnp.exp(s2 - mw), 0.0)
            den = jnp.maximum(jnp.sum(p1, axis=1, keepdims=True) + jnp.sum(p2, axis=1, keepdims=True), TINY)
            ow_sc[g] = (jnp.dot((p1 / den).astype(BF16), vwb, preferred_element_type=F32)
                        + jnp.dot((p2 / den).astype(BF16), vwn, preferred_element_type=F32))

    half_lane = lax.broadcasted_iota(jnp.int32, (rows, PAGE_SIZE), 1) < SEL_BLK
    for g in range(KV_GROUPS):
        qg = q_of(g)
        kt = jnp.concatenate([_page_vecs(pg, 2 * KV_GROUPS + g) for pg in pages], axis=0).astype(BF16)
        vt = jnp.concatenate([_page_vecs(pg, 3 * KV_GROUPS + g) for pg in pages], axis=0).astype(BF16)
        s = lax.dot_general(qg, kt, _NT, preferred_element_type=F32)
        selg = sel_sc[g, i]
        picked = jnp.concatenate(
            [jnp.where(half_lane, selg[:, 2 * k:2 * k + 1], selg[:, 2 * k + 1:2 * k + 2])
             for k in range(SAMPLE_PAGES)], axis=1)
        mask = picked > 0.5
        sm = jnp.where(mask, s, NEG)
        m_old = m_sc[g]
        m_new = jnp.maximum(m_old, jnp.max(sm, axis=1, keepdims=True))
        alpha = jnp.exp(m_old - m_new)
        p = jnp.where(mask, jnp.exp(sm - m_new), 0.0)
        m_sc[g] = m_new
        l_sc[g] = alpha * l_sc[g] + jnp.sum(p, axis=1, keepdims=True)
        acc_sc[g] = alpha * acc_sc[g] + jnp.dot(p.astype(BF16), vt, preferred_element_type=F32)

    @pl.when(i == n_steps - 1)
    def _():
        gates = jax.nn.sigmoid(gate_ref[...])
        for g in range(KV_GROUPS):
            o_s = acc_sc[g] / jnp.maximum(l_sc[g], TINY)
            o_c = oc_sc[g]
            o_w = ow_sc[g]
            for r in range(r4):
                h = g * r4 + r
                gate = lambda c: gates[:, SMALL_NG + 3 * h + c:SMALL_NG + 3 * h + c + 1]
                rs = slice(r * ts, (r + 1) * ts)
                o = gate(0) * o_c[rs] + gate(1) * o_s[rs] + gate(2) * o_w[rs]
                o_ref[0, :, h * d:(h + 1) * d] = o.astype(BF16)


def nsa_attention_sample(pt, cache2d, q_b, kvb, kcvc, win2d, zp, layer, db, t_pad, t_valid, past_len):
    ts = SAMPLE_TS
    n_steps = pt.shape[1] // SAMPLE_PAGES
    rows = Q_PER_KV * ts
    wl_rows = win2d.shape[0] // (DEPTH * db)
    ncp = kcvc.shape[2]
    ns = past_len // SEL_BLK + 1
    per_step = SAMPLE_PAGES * PAGE_SIZE // SEL_BLK
    ratio = SEL_BLK // CMP_STRIDE
    lane = np.arange((n_steps + 1) * HEAD_DIM)[None, :]
    tok = np.arange(ncp)[:, None] - 1
    blk = (lane // HEAD_DIM) * per_step + lane % HEAD_DIM
    c2s = jnp.asarray(((lane % HEAD_DIM < per_step) & (blk < ns) & (tok >= 0) & (tok >= ratio * blk - 1)
                       & (tok <= ratio * blk + ratio - 1)).astype(np.float32), dtype=BF16)
    seq_rows = lambda width, col: pl.BlockSpec((ts, width), lambda bi, i, pt: (bi * (t_pad // ts), col))
    grid_spec = pltpu.PrefetchScalarGridSpec(
        num_scalar_prefetch=1,
        grid=(db, n_steps),
        in_specs=_page_specs() + [
            seq_rows(N_WIDTH, 0),
            seq_rows(ROW_COLS, 0),
            pl.BlockSpec((1,) + kcvc.shape[1:], lambda bi, i, pt: (bi, 0, 0, 0)),
            pl.BlockSpec((wl_rows, HEAD_DIM), lambda bi, i, pt: (layer * db + bi, 0)),
            seq_rows(HEAD_DIM, COL_SMALL // HEAD_DIM),
            pl.BlockSpec(c2s.shape, lambda bi, i, pt: (0, 0))],
        out_specs=pl.BlockSpec((1, ts, N_WIDTH), lambda bi, i, pt: (bi, 0, 0)),
        scratch_shapes=[pltpu.VMEM((KV_GROUPS, n_steps, rows, HEAD_DIM), F32),
                        pltpu.VMEM((KV_GROUPS, rows, 1), F32),
                        pltpu.VMEM((KV_GROUPS, rows, 1), F32),
                        pltpu.VMEM((KV_GROUPS, rows, HEAD_DIM), F32),
                        pltpu.VMEM((KV_GROUPS, rows, HEAD_DIM), F32),
                        pltpu.VMEM((KV_GROUPS, rows, HEAD_DIM), F32)])
    return pl.pallas_call(
        functools.partial(_nsa_attn_sample_kernel, past_len=past_len, t_valid=t_valid, n_steps=n_steps),
        grid_spec=grid_spec,
        out_shape=jax.ShapeDtypeStruct((db, ts, N_WIDTH), BF16),
        compiler_params=pltpu.CompilerParams(
            dimension_semantics=("arbitrary", "arbitrary"), vmem_limit_bytes=VMEM_LIMIT_BYTES),
        name="nsa_attention_sample",
    )(pt, *([cache2d] * SAMPLE_PAGES), q_b, kvb, kcvc, win2d, zp, c2s)


def nsa_sample_pallas(zp, db, t, t_pad, layer, cache2d, pt, win_state, cmp_w):
    pe, w1, w2 = cmp_w
    past_len = pt.shape[1] * PAGE_SIZE
    cos2, sin2 = rope_tables(past_len + jnp.arange(t_pad))
    q_b, rows_f, win_f, kvb, _ = nsa_prep_pallas(zp, db, t_pad, cos2, sin2, tm=t_pad)
    kcvc = nsa_compress_sample(pt, cache2d, pe, w1, w2, db)
    win2d = win_state.reshape(-1, HEAD_DIM)
    hn = nsa_attention_sample(pt, cache2d, q_b, kvb, kcvc, win2d, zp, layer, db, t_pad, t, past_len)
    rows = rows_f.reshape(db, t_pad, 4, KV_GROUPS, HEAD_DIM)[:, :t]
    win_new = win_f.reshape(db, t_pad, 2, KV_GROUPS, HEAD_DIM)[:, :t]
    wl = win_state.shape[2]
    win = jnp.concatenate([win_state[layer], win_new], axis=1)[:, -wl:]
    return hn[:, :t], (rows, win)


MIX_L = 128
GLA_SUB = 16


def _logsig(x):
    return jnp.minimum(x, 0.0) - jnp.log1p(jnp.exp(-jnp.abs(x)))


def _prefix_sum(x, axis):
    n = x.shape[axis]
    idx = lax.broadcasted_iota(jnp.int32, x.shape, axis)
    step = 1
    while step < n:
        x = x + jnp.where(idx >= step, pltpu.roll(x, step, axis), 0.0)
        step *= 2
    return x


def _mlstm_phases(t, uqk_ref, v_ref, og_ref, small_ref, convw_ref, bias_ref, norm_ref, conv0_ref, c0_ref, n0_ref,
                  m0_ref, h_ref, c_ref, n_ref, m_ref, prev_sc, c_sc, n_sc, m_sc, *, t_valid):
    L = MIX_L
    d = HEAD_DIM

    prev_sc[...] = jnp.zeros_like(prev_sc)
    prev_sc[L - 8:, :] = conv0_ref[0]
    c_sc[...] = c0_ref[0]
    n_sc[...] = n0_ref[0]
    m_sc[...] = m0_ref[0]
    yield

    x = uqk_ref[...]
    prev = prev_sc[...]
    row = lax.broadcasted_iota(jnp.int32, x.shape, 0)
    w = convw_ref[...]
    conv = w[CONV_W - 1:CONV_W] * x
    for k in range(1, CONV_W):
        shifted = jnp.where(row >= k, pltpu.roll(x, k, 0), pltpu.roll(prev, k, 0))
        conv = conv + w[CONV_W - 1 - k:CONV_W - k] * shifted
    prev_sc[...] = x
    act = conv * jax.nn.sigmoid(conv)

    pre = small_ref[...] + bias_ref[...]
    pos_c = t * L + lax.broadcasted_iota(jnp.int32, (L, HEAD_DIM), 0)
    ig_c = jnp.where(pos_c < t_valid, pre, NEG)
    b_c = _prefix_sum(jnp.where(pos_c < t_valid, _logsig(pre), 0.0), 0)
    pre_r = jnp.transpose(pre)[0:8]
    pos_r = t * L + lax.broadcasted_iota(jnp.int32, (8, L), 1)
    ig_r = jnp.where(pos_r < t_valid, pre_r, NEG)
    b_r = _prefix_sum(jnp.where(pos_r < t_valid, _logsig(pre_r), 0.0), 1)

    li = lax.broadcasted_iota(jnp.int32, (L, L), 0)
    si = lax.broadcasted_iota(jnp.int32, (L, L), 1)
    for h in range(H_M):
        sl = slice(h * d, (h + 1) * d)
        bcol, igcol = b_c[:, H_M + h:H_M + h + 1], ig_c[:, h:h + 1]
        brow, igrow = b_r[H_M + h:H_M + h + 1, :], ig_r[h:h + 1, :]
        m0 = m_sc[h:h + 1, 0:1]
        c0 = c_sc[h]
        n0 = n_sc[h:h + 1, :]
        qf = act[:, sl]
        kf = act[:, M_WIDTH + h * d:M_WIDTH + (h + 1) * d] * (d ** -0.5)
        vf = v_ref[:, sl]
        qb, kb, vb = qf.astype(BF16), kf.astype(BF16), vf.astype(BF16)
        logw = jnp.where(si <= li, bcol - brow + igrow, NEG)
        gsum = bcol + m0
        m_row = jnp.maximum(jnp.max(logw, axis=1, keepdims=True), gsum)
        wgt = jnp.exp(logw - m_row) * lax.dot_general(qb, kb, _NT, preferred_element_type=F32)
        inter = jnp.exp(gsum - m_row)
        num = (jnp.dot(wgt.astype(BF16), vb, preferred_element_type=F32)
               + inter * lax.dot_general(qb, c0.astype(BF16), _NT, preferred_element_type=F32))
        den = jnp.sum(wgt, axis=1, keepdims=True) + inter * jnp.sum(qf * n0, axis=1, keepdims=True)
        hh = num / jnp.maximum(jnp.abs(den), jnp.exp(-m_row))
        bl = bcol[L - 1:L, :]
        m_new = jnp.maximum(jnp.max(bl - brow + igrow, axis=1, keepdims=True), bl + m0)
        wa = jnp.exp(bl - bcol + igcol - m_new)
        wc = jnp.exp(bl + m0 - m_new)
        c_sc[h] = wc * c0 + jnp.dot(jnp.transpose(vf * wa).astype(BF16), kb, preferred_element_type=F32)
        n_sc[h:h + 1, :] = wc * n0 + jnp.sum(wa * kf, axis=0, keepdims=True)
        m_sc[h:h + 1, :] = jnp.broadcast_to(m_new, (1, d))
        out = _rms(hh, norm_ref[:, sl]) * jax.nn.sigmoid(og_ref[:, sl])
        h_ref[:, sl] = out.astype(BF16)
    yield

    c_ref[0] = c_sc[...]
    n_ref[0] = n_sc[...]
    m_ref[0] = m_sc[...]
    yield


def _gla_phases(t, q_ref, k_ref, v_ref, r_ref, small_ref, w2_ref, gb_ref, norm_ref, s0_ref, h_ref, s_ref, st_sc,
                *, t_valid):
    L = MIX_L
    d = HEAD_DIM

    for h in range(H_G):
        st_sc[h] = jnp.transpose(s0_ref[0, h])
    yield

    pre = jnp.dot(small_ref[...].astype(BF16), w2_ref[...], preferred_element_type=F32) + gb_ref[...]
    pos = t * L + lax.broadcasted_iota(jnp.int32, (L, G_WIDTH), 0)
    la = jnp.where(pos < t_valid, _logsig(pre) / GLA_GATE_TEMP, 0.0)
    bc_all = _prefix_sum(la, 0)
    rowi = lax.broadcasted_iota(jnp.int32, (L, d), 0)
    li = lax.broadcasted_iota(jnp.int32, (L, L), 0)
    si = lax.broadcasted_iota(jnp.int32, (L, L), 1)
    for h in range(H_G):
        sl = slice(h * d, (h + 1) * d)
        bc = bc_all[:, sl]
        qf = q_ref[:, sl] * (d ** -0.5)
        kf = jnp.where(t * L + rowi < t_valid, k_ref[:, sl], 0.0)
        vf = v_ref[:, sl]
        qparts, kparts = [], []
        for j in range(L // GLA_SUB):
            lo, hi = j * GLA_SUB, (j + 1) * GLA_SUB
            e = bc[hi - 1:hi, :]
            qparts.append((qf * jnp.exp(jnp.where(rowi >= lo, bc - e, NEG))).astype(BF16))
            kparts.append((kf * jnp.exp(jnp.where((rowi >= lo) & (rowi < hi), e - bc, NEG))).astype(BF16))
        a = lax.dot_general(jnp.concatenate(qparts, axis=1), jnp.concatenate(kparts, axis=1), _NT,
                            preferred_element_type=F32)
        a = jnp.where(si <= li, a, 0.0)
        st = st_sc[h]
        o = (jnp.dot(a.astype(BF16), vf.astype(BF16), preferred_element_type=F32)
             + lax.dot_general((qf * jnp.exp(bc)).astype(BF16), st.astype(BF16), _NT, preferred_element_type=F32))
        bl = bc[L - 1:L, :]
        kd = (kf * jnp.exp(bl - bc)).astype(BF16)
        st_sc[h] = st * jnp.exp(bl) + jnp.dot(jnp.transpose(vf).astype(BF16), kd, preferred_element_type=F32)
        gate = r_ref[:, sl]
        h_ref[:, sl] = (_rms(o, norm_ref[:, sl]) * (gate * jax.nn.sigmoid(gate))).astype(BF16)
    yield

    for h in range(H_G):
        s_ref[0, h] = jnp.transpose(st_sc[h])
    yield


N_MLSTM_IN, N_GLA_IN, N_MLSTM_OUT, N_GLA_OUT, N_MLSTM_SCRATCH = 11, 9, 4, 2, 4


def _mixers_kernel(*refs, t_valid):
    t = pl.program_id(1)
    cuts = np.cumsum([N_MLSTM_IN, N_GLA_IN, N_MLSTM_OUT, N_GLA_OUT, N_MLSTM_SCRATCH])
    m_in, g_in, m_out, g_out, m_scr, g_scr = (refs[a:b] for a, b in zip([0, *cuts], [*cuts, len(refs)]))
    phases = [_mlstm_phases(t, *m_in, *m_out, *m_scr, t_valid=t_valid),
              _gla_phases(t, *g_in, *g_out, *g_scr, t_valid=t_valid)]

    @pl.when(t == 0)
    def _():
        for p in phases:
            next(p)

    for p in phases:
        next(p)

    @pl.when(t == pl.num_programs(1) - 1)
    def _():
        for p in phases:
            next(p)


def mixers_pallas(z, b, t, conv_buf, c0, n0, m0, s0, conv_w, m_gate_b, m_norm, g_w2, g_b, g_norm):
    t_pad = z.shape[0] // b
    L = MIX_L
    nt = t_pad // L
    lanes = lambda a: jnp.pad(a.reshape(1, -1), ((0, 0), (0, HEAD_DIM - a.size)))
    conv0 = jnp.pad(conv_buf, ((0, 0), (8 - (CONV_W - 1), 0), (0, 0)))
    n0p = jnp.pad(n0, ((0, 0), (0, 8 - H_M), (0, 0)))
    m0p = jnp.broadcast_to(jnp.pad(m0, ((0, 0), (0, 8 - H_M)))[:, :, None], (b, 8, HEAD_DIM))
    w2_pad = jnp.pad(g_w2, ((SMALL_GLR, HEAD_DIM - SMALL_GLR - GATE_RANK), (0, 0))).astype(BF16)
    rowblk = lambda width, col: pl.BlockSpec((L, width), lambda bi, ti: (bi * nt + ti, col))
    const2 = lambda shape: pl.BlockSpec(shape, lambda bi, ti: (0, 0))
    per_b3 = pl.BlockSpec((1, 8, HEAD_DIM), lambda bi, ti: (bi, 0, 0))
    per_b4 = pl.BlockSpec((1, H_M, HEAD_DIM, HEAD_DIM), lambda bi, ti: (bi, 0, 0, 0))
    small = rowblk(HEAD_DIM, COL_SMALL // HEAD_DIM)
    gcol = COL_GLA // G_WIDTH
    mlstm_in = [rowblk(2 * M_WIDTH, 0), rowblk(M_WIDTH, 2), rowblk(M_WIDTH, 3), small,
                const2((CONV_W, 2 * M_WIDTH)), const2((1, HEAD_DIM)), const2((1, M_WIDTH)),
                pl.BlockSpec((1, 8, 2 * M_WIDTH), lambda bi, ti: (bi, 0, 0)), per_b4, per_b3, per_b3]
    gla_in = [rowblk(G_WIDTH, gcol), rowblk(G_WIDTH, gcol + 1), rowblk(G_WIDTH, gcol + 2), rowblk(G_WIDTH, gcol + 3),
              small, const2((HEAD_DIM, G_WIDTH)), const2((1, G_WIDTH)), const2((1, G_WIDTH)), per_b4]
    assert (len(mlstm_in), len(gla_in)) == (N_MLSTM_IN, N_GLA_IN)
    state = jax.ShapeDtypeStruct((b, H_M, HEAD_DIM, HEAD_DIM), F32)
    tile = jax.ShapeDtypeStruct((b, 8, HEAD_DIM), F32)
    hm, c, n, m, hg, s = pl.pallas_call(
        functools.partial(_mixers_kernel, t_valid=t),
        grid=(b, nt),
        in_specs=mlstm_in + gla_in,
        out_specs=[rowblk(M_WIDTH, 0), per_b4, per_b3, per_b3, rowblk(G_WIDTH, 0), per_b4],
        out_shape=[jax.ShapeDtypeStruct((b * t_pad, M_WIDTH), BF16), state, tile, tile,
                   jax.ShapeDtypeStruct((b * t_pad, G_WIDTH), BF16), state],
        scratch_shapes=[pltpu.VMEM((L, 2 * M_WIDTH), F32),
                        pltpu.VMEM((H_M, HEAD_DIM, HEAD_DIM), F32),
                        pltpu.VMEM((8, HEAD_DIM), F32),
                        pltpu.VMEM((8, HEAD_DIM), F32),
                        pltpu.VMEM((H_G, HEAD_DIM, HEAD_DIM), F32)],
        compiler_params=pltpu.CompilerParams(
            dimension_semantics=("arbitrary", "arbitrary"), vmem_limit_bytes=VMEM_LIMIT_BYTES),
        name="mixers",
    )(z, z, z, z, conv_w, lanes(m_gate_b), m_norm.reshape(1, -1), conv0, c0, n0p, m0p,
      z, z, z, z, z, w2_pad, g_b.reshape(1, -1), g_norm.reshape(1, -1), s0)
    valid = lambda a: a.reshape(b, t_pad, -1)[:, :t]
    return valid(hm), valid(hg), c, n[:, :H_M], m[:, :H_M, 0], s


def split_cols(z):
    cut = lambda a, n: z[..., a:a + n]
    u_qk, m_v, m_o = cut(0, 2 * M_WIDTH), cut(2 * M_WIDTH, M_WIDTH), cut(3 * M_WIDTH, M_WIDTH)
    n_q, n_kv = cut(COL_NQ, N_WIDTH), cut(COL_NKV, 6 * KV_GROUPS * HEAD_DIM)
    g_q, g_k, g_v, g_r = (cut(COL_GLA + i * G_WIDTH, G_WIDTH) for i in range(4))
    m_if = cut(COL_SMALL + SMALL_MIF, 2 * H_M)
    n_g = cut(COL_SMALL + SMALL_NG, 3 * H_N)
    g_lr = cut(COL_SMALL + SMALL_GLR, GATE_RANK)
    return u_qk, m_v, m_o, m_if, n_q, n_kv, n_g, g_q, g_k, g_v, g_r, g_lr


W_IN_ORDER = (0, 1, 2, 4, 5, 7, 8, 9, 10, 3, 6, 11)


def _regroup_w_in_kernel(wt_ref, o_ref):
    o = (0,) + SPLIT_OFFSETS + (IN_COLS,)
    tk = wt_ref.shape[1]
    at = 0
    small = []
    for i in W_IN_ORDER:
        width = o[i + 1] - o[i]
        if width % HEAD_DIM == 0:
            o_ref[:, at:at + width] = jnp.transpose(wt_ref[o[i]:o[i + 1], :]).astype(BF16)
            at += width
        else:
            small.append(wt_ref[o[i]:o[i + 1], :])
    used = sum(s.shape[0] for s in small)
    small.append(jnp.zeros((HEAD_DIM - used, tk), F32))
    o_ref[:, at:at + HEAD_DIM] = jnp.transpose(jnp.concatenate(small, axis=0)).astype(BF16)
    at += HEAD_DIM
    o_ref[:, at:] = jnp.zeros((tk, IN_COLS_PAD - at), BF16)


def regroup_w_in(w_in):
    depth, k, n = w_in.shape
    tk = 256
    return pl.pallas_call(
        _regroup_w_in_kernel,
        grid=(depth, k // tk),
        in_specs=[pl.BlockSpec((None, n, tk), lambda l, i: (l, 0, i))],
        out_specs=pl.BlockSpec((None, tk, IN_COLS_PAD), lambda l, i: (l, i, 0)),
        out_shape=jax.ShapeDtypeStruct((depth, k, IN_COLS_PAD), BF16),
        compiler_params=pltpu.CompilerParams(
            dimension_semantics=("arbitrary", "arbitrary"), vmem_limit_bytes=VMEM_LIMIT_BYTES),
        name="regroup_w_in",
    )(jnp.transpose(w_in, (0, 2, 1)))


def trunk_layer(x, conv_buf, c0, n0, m0, s0, nsa_fn, layer, g_norms, w_in, conv_w, m_gate_b, m_norm,
                g_w2, g_b, g_norm, w_out, w_ff1, w_ff2):
    b, t, d = x.shape
    x2 = x.reshape(b * t, d)
    z = norm_matmul(x2, g_norms[0], w_in, layer, IN_TN)
    z3 = z.reshape(b, t, -1)
    t_pad = -(-t // MIX_L) * MIX_L
    zp = z if t_pad == t else jnp.pad(z3, ((0, 0), (0, t_pad - t), (0, 0))).reshape(b * t_pad, -1)
    hm, hg, c, n, m, s = mixers_pallas(zp, b, t, conv_buf, c0, n0, m0, s0,
                                       conv_w, m_gate_b, m_norm, g_w2, g_b, g_norm)
    new_conv = z3[:, t - (CONV_W - 1):, :2 * M_WIDTH]
    hn, nsa_state = nsa_fn(zp, b, t, t_pad)
    x2 = matmul_norm_res(hm.reshape(b * t, -1), hn.reshape(b * t, -1), hg.reshape(b * t, -1),
                         w_out, layer, g_norms[1], x2)
    x2 = ffn(x2, g_norms[2], w_ff1, w_ff2, layer, g_norms[3])
    return x2.reshape(b, t, d), (nsa_state[0], nsa_state[1], c, n, m, new_conv, s)


def kernel(x_prompt, x_sample, cache_nsa_kv, state_nsa_win, state_mlstm_C, state_mlstm_n, state_mlstm_m, state_mlstm_conv, state_gla_S, page_table, norms, w_in, mlstm_conv_w, mlstm_gate_b, mlstm_norm, nsa_cmp_pe, nsa_cmp_w1, nsa_cmp_w2, gla_gate_w2, gla_gate_b, gla_norm, w_out, w_ff1, w_ff2):
    xp, xs = x_prompt, x_sample
    bp = xp.shape[0]
    conv0 = jnp.zeros((bp, CONV_W - 1, 2 * M_WIDTH), xp.dtype)
    c0 = jnp.zeros((bp, H_M, HEAD_DIM, HEAD_DIM), F32)
    n0 = jnp.zeros((bp, H_M, HEAD_DIM), F32)
    m0 = jnp.zeros((bp, H_M), F32)
    s0 = jnp.zeros((bp, H_G, HEAD_DIM, HEAD_DIM), F32)
    w_in_b = regroup_w_in(w_in)
    w_out_b = w_out.astype(BF16)
    w_ff1_b = w_ff1.astype(BF16)
    w_ff2_b = w_ff2.astype(BF16)
    cmp_pe = nsa_cmp_pe.reshape(DEPTH, 2, 1, CMP_LEN * HEAD_DIM)
    cmp_w1_b = nsa_cmp_w1.reshape(DEPTH, 2, CMP_LEN * HEAD_DIM, CMP_HID).astype(BF16)
    cmp_w2_b = nsa_cmp_w2.astype(BF16)
    n_pool = cache_nsa_kv.shape[1]
    cache2d = cache_nsa_kv.reshape(-1, HEAD_DIM)
    acc_p = [[] for _ in range(7)]
    acc_s = [[] for _ in range(7)]
    for l in range(DEPTH):
        weights = (l, norms[l], w_in_b, mlstm_conv_w[l], mlstm_gate_b[l], mlstm_norm[l],
                   gla_gate_w2[l], gla_gate_b[l], gla_norm[l], w_out_b, w_ff1_b, w_ff2_b)
        cmp_w = (cmp_pe[l], cmp_w1_b[l], cmp_w2_b[l])
        nsa_p = lambda zp, b, t, t_pad, cmp_w=cmp_w: nsa_prompt_pallas(zp, b, t, cmp_w)
        nsa_s = lambda zp, b, t, t_pad, l=l, cmp_w=cmp_w: nsa_sample_pallas(
            zp, b, t, t_pad, l, cache2d, page_table + l * n_pool, state_nsa_win, cmp_w)
        xp, st_p = trunk_layer(xp, conv0, c0, n0, m0, s0, nsa_p, *weights)
        xs, st_s = trunk_layer(xs, state_mlstm_conv[l], state_mlstm_C[l], state_mlstm_n[l],
                               state_mlstm_m[l], state_gla_S[l], nsa_s, *weights)
        for acc, a in zip(acc_p, st_p):
            acc.append(a.astype(xp.dtype))
        for acc, a in zip(acc_s, st_s):
            acc.append(a.astype(xs.dtype))
    nsa_rows_p, nsa_win_p, mlstm_c_p, mlstm_n_p, mlstm_m_p, mlstm_conv_p, gla_s_p = [jnp.stack(a) for a in acc_p]
    nsa_rows_s, nsa_win_s, mlstm_c_s, mlstm_n_s, mlstm_m_s, mlstm_conv_s, gla_s_s = [jnp.stack(a) for a in acc_s]
    return (xp, xs, nsa_rows_p, nsa_rows_s, nsa_win_p, nsa_win_s, mlstm_c_p, mlstm_c_s,
            mlstm_n_p, mlstm_n_s, mlstm_m_p, mlstm_m_s, mlstm_conv_p, mlstm_conv_s, gla_s_p, gla_s_s)
```

```python
import functools
import math

import jax
import jax.numpy as jnp
import numpy as np
from jax import lax
from jax.experimental import pallas as pl
from jax.experimental.pallas import tpu as pltpu

D_MODEL = 2048
DEPTH = 4
PAGE_SIZE = 128
HEAD_DIM = 128
D_MIX = D_MODEL
M_WIDTH = D_MIX // 4
G_WIDTH = D_MIX // 4
N_WIDTH = D_MIX - M_WIDTH - G_WIDTH
H_M = M_WIDTH // HEAD_DIM
H_N = N_WIDTH // HEAD_DIM
H_G = G_WIDTH // HEAD_DIM
KV_GROUPS = 2
Q_PER_KV = H_N // KV_GROUPS
D_FF = 4 * D_MODEL
CONV_W = 4
CHUNK = 64
CMP_STRIDE = 16
CMP_LEN = 2 * CMP_STRIDE
CMP_HID = 256
SEL_BLK = 64
N_SELECT = 16
WINDOW = 512
QBLK = 128
GATE_RANK = 16
GLA_GATE_TEMP = 16.0
ROPE_THETA = 10000.0
EPS = 1e-6
TINY = 1e-30
FORCE_BONUS = 1e3
NEG_BIG = -1e9
SPLIT_SIZES = (2 * M_WIDTH, M_WIDTH, M_WIDTH, 2 * H_M,
               N_WIDTH, 6 * KV_GROUPS * HEAD_DIM, 3 * H_N,
               G_WIDTH, G_WIDTH, G_WIDTH, G_WIDTH, GATE_RANK)
IN_COLS = sum(SPLIT_SIZES)
SPLIT_OFFSETS = tuple(int(o) for o in np.cumsum(SPLIT_SIZES)[:-1])

F32 = jnp.float32
BF16 = jnp.bfloat16

VMEM_LIMIT_BYTES = 56 * 1024 * 1024
IN_TN = 1024
COL_NQ = 2 * M_WIDTH + 2 * M_WIDTH
COL_NKV = COL_NQ + N_WIDTH
COL_GLA = COL_NKV + 6 * KV_GROUPS * HEAD_DIM
COL_SMALL = COL_GLA + 4 * G_WIDTH
SMALL_MIF = 0
SMALL_NG = 2 * H_M
SMALL_GLR = SMALL_NG + 3 * H_N
IN_COLS_PAD = 7168
NEG = -1e30
NSA_TQ = 256
NSA_TK = 1024


def _rms(x, g):
    return x * lax.rsqrt(jnp.mean(x * x, axis=-1, keepdims=True) + EPS) * g


def _norm_matmul_kernel(x_ref, g_ref, w_ref, o_ref, xn_ref):
    @pl.when(pl.program_id(1) == 0)
    def _():
        xn_ref[...] = _rms(x_ref[...], g_ref[...]).astype(BF16)

    o_ref[...] = jnp.dot(xn_ref[...], w_ref[...], preferred_element_type=F32)


def norm_matmul(x, g, w, layer, tn):
    m, k = x.shape
    n = w.shape[2]
    tm = min(m, 1024)
    return pl.pallas_call(
        _norm_matmul_kernel,
        grid=(m // tm, n // tn),
        in_specs=[pl.BlockSpec((tm, k), lambda i, j: (i, 0)),
                  pl.BlockSpec((1, k), lambda i, j: (0, 0)),
                  pl.BlockSpec((None, k, tn), lambda i, j: (layer, 0, j))],
        out_specs=pl.BlockSpec((tm, tn), lambda i, j: (i, j)),
        out_shape=jax.ShapeDtypeStruct((m, n), F32),
        scratch_shapes=[pltpu.VMEM((tm, k), BF16)],
        compiler_params=pltpu.CompilerParams(
            dimension_semantics=("arbitrary", "arbitrary"), vmem_limit_bytes=VMEM_LIMIT_BYTES),
        name="norm_matmul",
    )(x, g.reshape(1, k), w)


def _matmul_norm_res_kernel(a0_ref, a1_ref, a2_ref, w_ref, g_ref, r_ref, o_ref):
    k0, k1 = a0_ref.shape[1], a0_ref.shape[1] + a1_ref.shape[1]
    y = (jnp.dot(a0_ref[...], w_ref[:k0, :], preferred_element_type=F32)
         + jnp.dot(a1_ref[...], w_ref[k0:k1, :], preferred_element_type=F32)
         + jnp.dot(a2_ref[...], w_ref[k1:, :], preferred_element_type=F32))
    o_ref[...] = r_ref[...] + _rms(y, g_ref[...])


def matmul_norm_res(a0, a1, a2, w, layer, g, r):
    m = a0.shape[0]
    _, k, n = w.shape
    tm = min(m, 512)
    rows = lambda a: pl.BlockSpec((tm, a.shape[1]), lambda i: (i, 0))
    return pl.pallas_call(
        _matmul_norm_res_kernel,
        grid=(m // tm,),
        in_specs=[rows(a0), rows(a1), rows(a2),
                  pl.BlockSpec((None, k, n), lambda i: (layer, 0, 0)),
                  pl.BlockSpec((1, n), lambda i: (0, 0)),
                  pl.BlockSpec((tm, n), lambda i: (i, 0))],
        out_specs=pl.BlockSpec((tm, n), lambda i: (i, 0)),
        out_shape=jax.ShapeDtypeStruct((m, n), F32),
        compiler_params=pltpu.CompilerParams(
            dimension_semantics=("arbitrary",), vmem_limit_bytes=VMEM_LIMIT_BYTES),
        name="matmul_norm_res",
    )(a0, a1, a2, w, g.reshape(1, n), r)


def _ffn_kernel(x_ref, g2_ref, w1_ref, w2_ref, g3_ref, o_ref, xn_ref, acc_ref):
    f = pl.program_id(1)

    @pl.when(f == 0)
    def _():
        xn_ref[...] = _rms(x_ref[...], g2_ref[...]).astype(BF16)
        acc_ref[...] = jnp.zeros_like(acc_ref)

    h = jnp.dot(xn_ref[...], w1_ref[...], preferred_element_type=F32)
    a = jnp.square(jnp.maximum(h, 0.0)).astype(BF16)
    acc_ref[...] += jnp.dot(a, w2_ref[...], preferred_element_type=F32)

    @pl.when(f == pl.num_programs(1) - 1)
    def _():
        o_ref[...] = x_ref[...] + _rms(acc_ref[...], g3_ref[...])


def ffn(x, g2, w1, w2, layer, g3):
    m, d = x.shape
    dff = w1.shape[2]
    tm = min(m, 512)
    tf = 1024
    return pl.pallas_call(
        _ffn_kernel,
        grid=(m // tm, dff // tf),
        in_specs=[pl.BlockSpec((tm, d), lambda i, f: (i, 0)),
                  pl.BlockSpec((1, d), lambda i, f: (0, 0)),
                  pl.BlockSpec((None, d, tf), lambda i, f: (layer, 0, f)),
                  pl.BlockSpec((None, tf, d), lambda i, f: (layer, f, 0)),
                  pl.BlockSpec((1, d), lambda i, f: (0, 0))],
        out_specs=pl.BlockSpec((tm, d), lambda i, f: (i, 0)),
        out_shape=jax.ShapeDtypeStruct((m, d), F32),
        scratch_shapes=[pltpu.VMEM((tm, d), BF16), pltpu.VMEM((tm, d), F32)],
        compiler_params=pltpu.CompilerParams(
            dimension_semantics=("arbitrary", "arbitrary"), vmem_limit_bytes=VMEM_LIMIT_BYTES),
        name="ffn",
    )(x, g2.reshape(1, d), w1, w2, g3.reshape(1, d))


def rope_tables(pos):
    half = HEAD_DIM // 2
    inv_freq = jnp.exp(-math.log(ROPE_THETA) * jnp.arange(half, dtype=F32) / half)
    ang = pos.astype(F32)[:, None] * inv_freq[None, :]
    cos, sin = jnp.cos(ang), jnp.sin(ang)
    return jnp.concatenate([cos, cos], axis=-1), jnp.concatenate([-sin, sin], axis=-1)


def _rope(x, cos2, sin2):
    return x * cos2 + pltpu.roll(x, HEAD_DIM // 2, 1) * sin2


def _nsa_prep_kernel(nq_ref, nkv_ref, cos_ref, sin_ref, q_ref, rows_ref, win_ref, kvb_ref, cmp_ref):
    cos2 = cos_ref[...]
    sin2 = sin_ref[...]
    for h in range(H_N):
        sl = slice(h * HEAD_DIM, (h + 1) * HEAD_DIM)
        q_ref[:, sl] = (_rope(nq_ref[:, sl], cos2, sin2) * (HEAD_DIM ** -0.5)).astype(BF16)
    for c in range(6 * KV_GROUPS):
        slot = c // KV_GROUPS
        x = nkv_ref[:, c * HEAD_DIM:(c + 1) * HEAD_DIM]
        if slot % 2 == 0:
            x = _rope(x, cos2, sin2)
        xb = x.astype(BF16)
        tm = x.shape[0]
        if slot < 4:
            rows_ref[pl.ds(c, tm, stride=ROW_VECS), :] = x
        else:
            win_ref[pl.ds(c - ROW_VECS, tm, stride=WIN_VECS), :] = x
        if slot < 2:
            cmp_ref[0, c] = xb
        else:
            kvb_ref[:, (c - 4) * HEAD_DIM:(c - 3) * HEAD_DIM] = xb


def nsa_prep_pallas(z, b, s, cos2, sin2, tm=512, layer=0, depth=1, rows_all=None):
    m = b * s
    nsb = s // tm
    steps = m // tm
    kern = _nsa_prep_kernel if rows_all is None else (lambda *refs: _nsa_prep_kernel(*refs[:4], *refs[5:]))
    extra_specs = [] if rows_all is None else [pl.BlockSpec(memory_space=pl.ANY)]
    extra_args = [] if rows_all is None else [rows_all]
    return pl.pallas_call(
        kern,
        grid=(steps,),
        in_specs=[pl.BlockSpec((tm, N_WIDTH), lambda i: (i, COL_NQ // N_WIDTH)),
                  pl.BlockSpec((tm, 1536), lambda i: (i, COL_NKV // 1536)),
                  pl.BlockSpec((tm, HEAD_DIM), lambda i: (i % nsb, 0)),
                  pl.BlockSpec((tm, HEAD_DIM), lambda i: (i % nsb, 0))] + extra_specs,
        out_specs=[pl.BlockSpec((tm, N_WIDTH), lambda i: (i, 0)),
                   pl.BlockSpec((tm * 8, HEAD_DIM), lambda i: (layer * steps + i, 0)),
                   pl.BlockSpec((tm * 4, HEAD_DIM), lambda i: (i, 0)),
                   pl.BlockSpec((tm, 1024), lambda i: (i, 0)),
                   pl.BlockSpec((1, 4, tm, HEAD_DIM), lambda i: (i // nsb, 0, i % nsb, 0))],
        out_shape=[jax.ShapeDtypeStruct((m, N_WIDTH), BF16),
                   jax.ShapeDtypeStruct((depth * m * 8, HEAD_DIM), F32),
                   jax.ShapeDtypeStruct((m * 4, HEAD_DIM), F32),
                   jax.ShapeDtypeStruct((m, 1024), BF16),
                   jax.ShapeDtypeStruct((b, 4, s, HEAD_DIM), BF16)],
        input_output_aliases={} if rows_all is None else {4: 1},
        compiler_params=pltpu.CompilerParams(
            dimension_semantics=("arbitrary",), vmem_limit_bytes=VMEM_LIMIT_BYTES),
        name="nsa_prep",
    )(z, z, cos2, sin2, *extra_args)


def _nsa_cmp_kernel(x_ref, pe_ref, w1_ref, w2_ref, o_ref):
    x = x_ref[0, 0]
    nb = x.shape[0]
    half = CMP_STRIDE * HEAD_DIM
    first = jnp.dot(x, w1_ref[0, :half, :], preferred_element_type=F32)
    second = jnp.dot(x, w1_ref[0, half:, :], preferred_element_type=F32)
    pe = jnp.broadcast_to(pe_ref[0], (8, 2 * half)).astype(BF16)
    bias = jnp.dot(pe, w1_ref[0], preferred_element_type=F32)[0:1]
    hid = first + pltpu.roll(second, nb - 1, 0) + bias
    o_ref[0, 0] = jnp.dot(jax.nn.gelu(hid).astype(BF16), w2_ref[0], preferred_element_type=F32).astype(BF16)


def nsa_compress_prompt(cmp_in, pe, w1, w2):
    b, _, nb, kdim = cmp_in.shape
    return pl.pallas_call(
        _nsa_cmp_kernel,
        grid=(b, 4),
        in_specs=[pl.BlockSpec((1, 1, nb, kdim), lambda i, c: (i, c, 0, 0)),
                  pl.BlockSpec((1, 1, 2 * kdim), lambda i, c: (c // 2, 0, 0)),
                  pl.BlockSpec((1, 2 * kdim, CMP_HID), lambda i, c: (c // 2, 0, 0)),
                  pl.BlockSpec((1, CMP_HID, HEAD_DIM), lambda i, c: (c // 2, 0, 0))],
        out_specs=pl.BlockSpec((1, 1, nb, HEAD_DIM), lambda i, c: (i, c, 0, 0)),
        out_shape=jax.ShapeDtypeStruct((b, 4, nb, HEAD_DIM), BF16),
        compiler_params=pltpu.CompilerParams(
            dimension_semantics=("arbitrary", "arbitrary"), vmem_limit_bytes=VMEM_LIMIT_BYTES),
        name="nsa_compress",
    )(cmp_in, pe, w1, w2)


def _softmax_rows(s, mask):
    sm = jnp.where(mask, s, NEG)
    m = jnp.max(sm, axis=-1, keepdims=True)
    p = jnp.where(mask, jnp.exp(sm - m), 0.0)
    return p / jnp.maximum(jnp.sum(p, axis=-1, keepdims=True), TINY)


_NT = (((1,), (1,)), ((), ()))


def _nsa_attn_kernel(q_ref, kc_ref, vc_ref, ks_ref, vs_ref, kw_ref, vw_ref, gate_ref, o_ref):
    tq = NSA_TQ
    r4 = Q_PER_KV
    g = pl.program_id(1)
    i = pl.program_id(2)
    q4 = q_ref[...]
    q = jnp.concatenate([q4[:, r * HEAD_DIM:(r + 1) * HEAD_DIM] for r in range(r4)], axis=0)
    q0 = i * tq
    tpos = q0 + lax.broadcasted_iota(jnp.int32, (tq, 1), 0)

    ncp = kc_ref.shape[2]
    s = lax.dot_general(q, kc_ref[0, 0], _NT, preferred_element_type=F32).reshape(r4, tq, ncp)
    cend = lax.broadcasted_iota(jnp.int32, (tq, ncp), 1) * CMP_STRIDE + (CMP_LEN - 1)
    p_c = _softmax_rows(s, (cend <= tpos)[None])
    o_c = jnp.dot(p_c.reshape(r4 * tq, ncp).astype(BF16), vc_ref[0, 0], preferred_element_type=F32)

    psum = p_c[0] + p_c[1] + p_c[2] + p_c[3]
    n_i = lax.broadcasted_iota(jnp.int32, (ncp, HEAD_DIM), 0)
    j_i = lax.broadcasted_iota(jnp.int32, (ncp, HEAD_DIM), 1)
    ratio = SEL_BLK // CMP_STRIDE
    c2s = jnp.where((n_i >= ratio * j_i - 1) & (n_i <= ratio * j_i + ratio - 1) & (n_i < ncp - 1)
                    & (j_i < ncp // ratio), 1.0, 0.0).astype(BF16)
    p_hi = psum.astype(BF16)
    p_lo = (psum - p_hi.astype(F32)).astype(BF16)
    imp = (jnp.dot(p_hi, c2s, preferred_element_type=F32)
           + jnp.dot(p_lo, c2s, preferred_element_type=F32))

    ns = ncp // ratio
    imp_t = jnp.transpose(imp)[:ns]
    jj = lax.broadcasted_iota(jnp.int32, (ns, tq), 0)
    tt = q0 + lax.broadcasted_iota(jnp.int32, (ns, tq), 1)
    cur = tt // SEL_BLK
    forced = (jj == 0) | (jj == cur) | (jj == cur - 1)
    score = jnp.where(jj * SEL_BLK <= tt, imp_t + jnp.where(forced, FORCE_BONUS, 0.0), NEG_BIG)
    rank = jnp.zeros((ns, tq), F32)
    for jp in range(ns):
        row = score[jp:jp + 1, :]
        rank = rank + jnp.where(row > score, 1.0, jnp.where((row == score) & (jj > jp), 1.0, 0.0))
    sel_t = jnp.where(rank < float(N_SELECT), 1.0, 0.0)
    sel_t = jnp.concatenate([sel_t, jnp.zeros((HEAD_DIM - ns, tq), F32)], axis=0)
    sel = jnp.transpose(sel_t)

    tk = NSA_TK
    n_tiles = (q0 + tq + tk - 1) // tk
    unpicked = jnp.where(sel > 0.5, 0.0, NEG).astype(BF16)
    q_aug = jnp.concatenate([q, jnp.concatenate([unpicked] * r4, axis=0)], axis=1)
    blk_lane = lax.broadcasted_iota(jnp.int32, (tk, HEAD_DIM), 1)
    blk_of_key = lax.broadcasted_iota(jnp.int32, (tk, HEAD_DIM), 0) // SEL_BLK

    hc = r4
    chains = r4 // hc
    q_chain = [q_aug[c * hc * tq:(c + 1) * hc * tq] for c in range(chains)]

    def tile_update(states, kt, causal):
        k0 = pl.multiple_of(kt * tk, tk)
        onehot = jnp.where(blk_lane == kt * (tk // SEL_BLK) + blk_of_key, 1.0, 0.0).astype(BF16)
        k_aug = jnp.concatenate([ks_ref[pl.ds(k0, tk), :], onehot], axis=1)
        v_tile = vs_ref[pl.ds(k0, tk), :]
        out = []
        for c in range(chains):
            m_run, l_run, acc = states[c]
            s = lax.dot_general(q_chain[c], k_aug, _NT, preferred_element_type=F32).reshape(hc, tq, tk)
            if causal is not None:
                s = jnp.where(causal, s, NEG)
            m_new = jnp.maximum(m_run, jnp.max(s, axis=-1, keepdims=True))
            alpha = jnp.exp(m_run - m_new)
            p = jnp.exp(s - m_new)
            l_new = alpha * l_run + jnp.sum(p, axis=-1, keepdims=True)
            pv = jnp.dot(p.reshape(hc * tq, tk).astype(BF16), v_tile, preferred_element_type=F32)
            out.append((m_new, l_new, alpha.reshape(hc * tq, 1) * acc + pv))
        return tuple(out)

    init = tuple((jnp.full((hc, tq, 1), NEG, F32), jnp.zeros((hc, tq, 1), F32),
                  jnp.zeros((hc * tq, HEAD_DIM), F32)) for _ in range(chains))
    states = lax.fori_loop(0, n_tiles - 1, lambda kt, st: tile_update(st, kt, None), init)
    kpos = (n_tiles - 1) * tk + lax.broadcasted_iota(jnp.int32, (tq, tk), 1)
    states = tile_update(states, n_tiles - 1, (kpos <= tpos)[None])
    o_s = jnp.concatenate([acc / jnp.maximum(l_run, TINY).reshape(hc * tq, 1) for _, l_run, acc in states], axis=0)

    wb = WINDOW + tq
    w0 = pl.multiple_of(jnp.maximum(q0 - WINDOW, 0), tq)
    s = lax.dot_general(q, kw_ref[pl.ds(w0, wb), :], _NT, preferred_element_type=F32).reshape(r4, tq, wb)
    dist = tpos - (w0 + lax.broadcasted_iota(jnp.int32, (tq, wb), 1))
    wmask = ((dist >= 0) & (dist < WINDOW))[None]
    sm = jnp.where(wmask, s, NEG)
    p_w = jnp.where(wmask, jnp.exp(sm - jnp.max(sm, axis=-1, keepdims=True)), 0.0)
    den_w = jnp.maximum(jnp.sum(p_w, axis=-1, keepdims=True), TINY).reshape(r4 * tq, 1)
    o_w = jnp.dot(p_w.reshape(r4 * tq, wb).astype(BF16), vw_ref[pl.ds(w0, wb), :],
                  preferred_element_type=F32) / den_w

    gates = jax.nn.sigmoid(gate_ref[...])
    for r in range(r4):
        def gate(c):
            lane0 = SMALL_NG + 3 * r + c
            lane1 = lane0 + 3 * r4
            return jnp.where(g == 0, gates[:, lane0:lane0 + 1], gates[:, lane1:lane1 + 1])
        rows = slice(r * tq, (r + 1) * tq)
        o = gate(0) * o_c[rows] + gate(1) * o_s[rows] + gate(2) * o_w[rows]
        o_ref[:, r * HEAD_DIM:(r + 1) * HEAD_DIM] = o.astype(BF16)


def nsa_attention_prompt(q_b, kcvc, kvb, z, b, s):
    tq = NSA_TQ
    nq = s // tq
    ncp = kcvc.shape[2]
    gw = Q_PER_KV * HEAD_DIM
    seq_block = lambda col: pl.BlockSpec((s, HEAD_DIM), lambda bi, g, i: (bi, col + g))
    return pl.pallas_call(
        _nsa_attn_kernel,
        grid=(b, KV_GROUPS, nq),
        in_specs=[pl.BlockSpec((tq, gw), lambda bi, g, i: (bi * nq + i, g)),
                  pl.BlockSpec((1, 1, ncp, HEAD_DIM), lambda bi, g, i: (bi, g, 0, 0)),
                  pl.BlockSpec((1, 1, ncp, HEAD_DIM), lambda bi, g, i: (bi, KV_GROUPS + g, 0, 0)),
                  seq_block(0), seq_block(2), seq_block(4), seq_block(6),
                  pl.BlockSpec((tq, HEAD_DIM), lambda bi, g, i: (bi * nq + i, COL_SMALL // HEAD_DIM))],
        out_specs=pl.BlockSpec((tq, gw), lambda bi, g, i: (bi * nq + i, g)),
        out_shape=jax.ShapeDtypeStruct((b * s, N_WIDTH), BF16),
        compiler_params=pltpu.CompilerParams(
            dimension_semantics=("arbitrary", "arbitrary", "arbitrary"), vmem_limit_bytes=VMEM_LIMIT_BYTES),
        name="nsa_attention",
    )(q_b, kcvc, kcvc, kvb, kvb, kvb, kvb, z)


def nsa_prompt_pallas(z, b, s, cmp_w, layer, rows_all):
    pe, w1, w2 = cmp_w
    cos2, sin2 = rope_tables(jnp.arange(s))
    q_b, rows_all, win_f, kvb, cmp_in = nsa_prep_pallas(z, b, s, cos2, sin2, layer=layer, depth=DEPTH,
                                                        rows_all=rows_all)
    kcvc = nsa_compress_prompt(cmp_in.reshape(b, 4, s // CMP_STRIDE, CMP_STRIDE * HEAD_DIM), pe, w1, w2)
    hn = nsa_attention_prompt(q_b, kcvc, kvb, z, b, s)
    win = win_f.reshape(b, s, 2, KV_GROUPS, HEAD_DIM)[:, -min(WINDOW, s):]
    return hn, (rows_all, win)


SAMPLE_PAGES = 16
SAMPLE_TS = 16
ROW_COLS = 4 * KV_GROUPS * HEAD_DIM
ROW_VECS = 4 * KV_GROUPS
WIN_VECS = 2 * KV_GROUPS
NEG_DEAD = -3e38


def _page_specs():
    def spec(k):
        return pl.BlockSpec((PAGE_SIZE * ROW_VECS, HEAD_DIM), lambda bi, i, pt: (pt[bi, i * SAMPLE_PAGES + k], 0))
    return [spec(k) for k in range(SAMPLE_PAGES)]


def _page_vecs(pg, vec):
    return pg[pl.ds(vec, PAGE_SIZE, stride=ROW_VECS), :]


def _nsa_cmp_sample_kernel(pt_ref, *refs):
    del pt_ref
    pages = refs[:SAMPLE_PAGES]
    pe_ref, w1_ref, w2_ref, o_ref, carry_sc, bias_sc = refs[SAMPLE_PAGES:]
    i = pl.program_id(1)
    half = CMP_STRIDE * HEAD_DIM
    nbp = PAGE_SIZE // CMP_STRIDE
    nb = SAMPLE_PAGES * nbp

    @pl.when(i == 0)
    def _():
        carry_sc[...] = jnp.zeros_like(carry_sc)
        for kv in range(2):
            pe = jnp.broadcast_to(pe_ref[kv], (8, 2 * half)).astype(BF16)
            bias_sc[kv] = jnp.dot(pe, w1_ref[kv], preferred_element_type=F32)

    row = lax.broadcasted_iota(jnp.int32, (nb, CMP_HID), 0)
    for kv in range(2):
        def flat(c):
            piece = lambda pg, j: pg[pl.ds(j * ROW_VECS + c, nbp, stride=CMP_STRIDE * ROW_VECS), :]
            return jnp.concatenate(
                [jnp.concatenate([piece(pg, j) for j in range(CMP_STRIDE)], axis=1) for pg in pages],
                axis=0)
        x = jnp.concatenate([flat(kv * KV_GROUPS + g) for g in range(KV_GROUPS)], axis=0).astype(BF16)
        first_all = jnp.dot(x, w1_ref[kv, :half, :], preferred_element_type=F32)
        second_all = jnp.dot(x, w1_ref[kv, half:, :], preferred_element_type=F32)
        for g in range(KV_GROUPS):
            c = kv * KV_GROUPS + g
            first = first_all[g * nb:(g + 1) * nb]
            second = second_all[g * nb:(g + 1) * nb]
            shifted = jnp.where(row == 0, carry_sc[c, 7:8, :], pltpu.roll(first, 1, 0))
            hid = shifted + second + bias_sc[kv, 0:1, :]
            o_ref[0, c] = jnp.dot(jax.nn.gelu(hid).astype(BF16), w2_ref[kv],
                                  preferred_element_type=F32).astype(BF16)
            carry_sc[c] = first[nb - 8:, :]


def nsa_compress_sample(pt, cache2d, pe, w1, w2, db):
    n_pages = pt.shape[1]
    nb_all = n_pages * (PAGE_SIZE // CMP_STRIDE)
    nb = SAMPLE_PAGES * (PAGE_SIZE // CMP_STRIDE)
    const3 = lambda bi, i, pt: (0, 0, 0)
    grid_spec = pltpu.PrefetchScalarGridSpec(
        num_scalar_prefetch=1,
        grid=(db, n_pages // SAMPLE_PAGES),
        in_specs=_page_specs() + [pl.BlockSpec(pe.shape, const3), pl.BlockSpec(w1.shape, const3),
                                   pl.BlockSpec(w2.shape, const3)],
        out_specs=pl.BlockSpec((1, 2 * KV_GROUPS, nb, HEAD_DIM), lambda bi, i, pt: (bi, 0, i, 0)),
        scratch_shapes=[pltpu.VMEM((2 * KV_GROUPS, 8, CMP_HID), F32), pltpu.VMEM((2, 8, CMP_HID), F32)])
    return pl.pallas_call(
        _nsa_cmp_sample_kernel,
        grid_spec=grid_spec,
        out_shape=jax.ShapeDtypeStruct((db, 2 * KV_GROUPS, nb_all, HEAD_DIM), BF16),
        compiler_params=pltpu.CompilerParams(
            dimension_semantics=("arbitrary", "arbitrary"), vmem_limit_bytes=VMEM_LIMIT_BYTES),
        name="nsa_compress_sample",
    )(pt, *([cache2d] * SAMPLE_PAGES), pe, w1, w2)


def _nsa_attn_sample_kernel(pt_ref, *refs, past_len, t_valid, n_steps):
    del pt_ref
    pages = refs[:SAMPLE_PAGES]
    (q_ref, new_ref, kc_ref, win_ref, gate_ref, c2s_ref, o_ref,
     sel_sc, m_sc, l_sc, acc_sc, oc_sc, ow_sc) = refs[SAMPLE_PAGES:]
    i = pl.program_id(1)
    ts = SAMPLE_TS
    r4 = Q_PER_KV
    rows = r4 * ts
    d = HEAD_DIM
    blocks_per_step = SAMPLE_PAGES * PAGE_SIZE // SEL_BLK
    ns = past_len // SEL_BLK + 1
    trow = lax.broadcasted_iota(jnp.int32, (rows, 1), 0) % ts
    qpos = past_len + trow

    def q_of(g):
        return jnp.concatenate([q_ref[:, (g * r4 + r) * d:(g * r4 + r + 1) * d] for r in range(r4)], axis=0)

    @pl.when(i == 0)
    def _():
        ncp = kc_ref.shape[2]
        nsl = (n_steps + 1) * d
        c2s = c2s_ref[...]
        lane_r = lax.broadcasted_iota(jnp.int32, (ts, nsl), 1)
        blk_r = (lane_r // d) * blocks_per_step + lane_r % d
        live = (lane_r % d < blocks_per_step) & (blk_r < ns)
        blk_f = blk_r.astype(F32)
        qpos_t = past_len + lax.broadcasted_iota(jnp.int32, (ts, 1), 0)
        cur = qpos_t // SEL_BLK
        forced = (blk_r == 0) | (blk_r == cur) | (blk_r == cur - 1)
        midx = lax.broadcasted_iota(jnp.int32, (rows, ncp), 1)
        cmask = (midx >= 1) & ((midx - 1) * CMP_STRIDE + CMP_LEN - 1 <= qpos)
        sidx = lax.broadcasted_iota(jnp.int32, (rows, ts), 1)
        new_ok = (sidx <= trow) & (sidx < t_valid)
        wl = win_ref.shape[0] // WIN_VECS
        dist_buf = trow + wl - lax.broadcasted_iota(jnp.int32, (rows, wl), 1)
        buf_ok = (dist_buf >= 0) & (dist_buf < WINDOW)
        dist_new = trow - sidx
        wnew_ok = (dist_new >= 0) & (dist_new < WINDOW) & (sidx < t_valid)
        for g in range(KV_GROUPS):
            qg = q_of(g)
            s = lax.dot_general(qg, kc_ref[0, g], _NT, preferred_element_type=F32)
            p_c = _softmax_rows(s, cmask)
            oc_sc[g] = jnp.dot(p_c.astype(BF16), kc_ref[0, KV_GROUPS + g], preferred_element_type=F32)
            psum = p_c[0:ts] + p_c[ts:2 * ts] + p_c[2 * ts:3 * ts] + p_c[3 * ts:4 * ts]
            p_hi = psum.astype(BF16)
            p_lo = (psum - p_hi.astype(F32)).astype(BF16)
            imp = (jnp.dot(p_hi, c2s, preferred_element_type=F32) + jnp.dot(p_lo, c2s, preferred_element_type=F32))
            score = jnp.where(live, jnp.where(blk_r * SEL_BLK <= qpos_t, imp + jnp.where(forced, FORCE_BONUS, 0.0),
                                              NEG_BIG), NEG_DEAD)
            sel = jnp.zeros((ts, nsl), F32)
            for _ in range(N_SELECT):
                top = jnp.max(score, axis=1, keepdims=True)
                first = jnp.min(jnp.where(score == top, blk_f, 1e9), axis=1, keepdims=True)
                hit = live & (blk_f == first)
                sel = jnp.where(hit, 1.0, sel)
                score = jnp.where(hit, NEG_DEAD, score)
            for step in range(n_steps):
                sel_sc[g, step] = jnp.concatenate([sel[:, step * d:(step + 1) * d]] * r4, axis=0)
            knew = new_ref[:, g * d:(g + 1) * d]
            vnew = new_ref[:, (KV_GROUPS + g) * d:(KV_GROUPS + g + 1) * d]
            sm = jnp.where(new_ok, lax.dot_general(qg, knew, _NT, preferred_element_type=F32), NEG)
            m0 = jnp.max(sm, axis=1, keepdims=True)
            p = jnp.where(new_ok, jnp.exp(sm - m0), 0.0)
            m_sc[g] = m0
            l_sc[g] = jnp.sum(p, axis=1, keepdims=True)
            acc_sc[g] = jnp.dot(p.astype(BF16), vnew, preferred_element_type=F32)
            kwb = win_ref[pl.ds(g, wl, stride=WIN_VECS), :].astype(BF16)
            vwb = win_ref[pl.ds(KV_GROUPS + g, wl, stride=WIN_VECS), :].astype(BF16)
            kwn = new_ref[:, (2 * KV_GROUPS + g) * d:(2 * KV_GROUPS + g + 1) * d]
            vwn = new_ref[:, (3 * KV_GROUPS + g) * d:(3 * KV_GROUPS + g + 1) * d]
            s1 = jnp.where(buf_ok, lax.dot_general(qg, kwb, _NT, preferred_element_type=F32), NEG)
            s2 = jnp.where(wnew_ok, lax.dot_general(qg, kwn, _NT, preferred_element_type=F32), NEG)
            mw = jnp.maximum(jnp.max(s1, axis=1, keepdims=True), jnp.max(s2, axis=1, keepdims=True))
            p1 = jnp.where(buf_ok, jnp.exp(s1 - mw), 0.0)
            p2 = jnp.where(wnew_ok, jnp.exp(s2 - mw), 0.0)
            den = jnp.maximum(jnp.sum(p1, axis=1, keepdims=True) + jnp.sum(p2, axis=1, keepdims=True), TINY)
            ow_sc[g] = (jnp.dot((p1 / den).astype(BF16), vwb, preferred_element_type=F32)
                        + jnp.dot((p2 / den).astype(BF16), vwn, preferred_element_type=F32))

    half_lane = lax.broadcasted_iota(jnp.int32, (rows, PAGE_SIZE), 1) < SEL_BLK
    for g in range(KV_GROUPS):
        qg = q_of(g)
        kt = jnp.concatenate([_page_vecs(pg, 2 * KV_GROUPS + g) for pg in pages], axis=0).astype(BF16)
        vt = jnp.concatenate([_page_vecs(pg, 3 * KV_GROUPS + g) for pg in pages], axis=0).astype(BF16)
        s = lax.dot_general(qg, kt, _NT, preferred_element_type=F32)
        selg = sel_sc[g, i]
        picked = jnp.concatenate(
            [jnp.where(half_lane, selg[:, 2 * k:2 * k + 1], selg[:, 2 * k + 1:2 * k + 2])
             for k in range(SAMPLE_PAGES)], axis=1)
        mask = picked > 0.5
        sm = jnp.where(mask, s, NEG)
        m_old = m_sc[g]
        m_new = jnp.maximum(m_old, jnp.max(sm, axis=1, keepdims=True))
        alpha = jnp.exp(m_old - m_new)
        p = jnp.where(mask, jnp.exp(sm - m_new), 0.0)
        m_sc[g] = m_new
        l_sc[g] = alpha * l_sc[g] + jnp.sum(p, axis=1, keepdims=True)
        acc_sc[g] = alpha * acc_sc[g] + jnp.dot(p.astype(BF16), vt, preferred_element_type=F32)

    @pl.when(i == n_steps - 1)
    def _():
        gates = jax.nn.sigmoid(gate_ref[...])
        for g in range(KV_GROUPS):
            o_s = acc_sc[g] / jnp.maximum(l_sc[g], TINY)
            o_c = oc_sc[g]
            o_w = ow_sc[g]
            for r in range(r4):
                h = g * r4 + r
                gate = lambda c: gates[:, SMALL_NG + 3 * h + c:SMALL_NG + 3 * h + c + 1]
                rs = slice(r * ts, (r + 1) * ts)
                o = gate(0) * o_c[rs] + gate(1) * o_s[rs] + gate(2) * o_w[rs]
                o_ref[0, :, h * d:(h + 1) * d] = o.astype(BF16)


def nsa_attention_sample(pt, cache2d, q_b, kvb, kcvc, win2d, zp, layer, db, t_pad, t_valid, past_len):
    ts = SAMPLE_TS
    n_steps = pt.shape[1] // SAMPLE_PAGES
    rows = Q_PER_KV * ts
    wl_rows = win2d.shape[0] // (DEPTH * db)
    ncp = kcvc.shape[2]
    ns = past_len // SEL_BLK + 1
    per_step = SAMPLE_PAGES * PAGE_SIZE // SEL_BLK
    ratio = SEL_BLK // CMP_STRIDE
    lane = np.arange((n_steps + 1) * HEAD_DIM)[None, :]
    tok = np.arange(ncp)[:, None] - 1
    blk = (lane // HEAD_DIM) * per_step + lane % HEAD_DIM
    c2s = jnp.asarray(((lane % HEAD_DIM < per_step) & (blk < ns) & (tok >= 0) & (tok >= ratio * blk - 1)
                       & (tok <= ratio * blk + ratio - 1)).astype(np.float32), dtype=BF16)
    seq_rows = lambda width, col: pl.BlockSpec((ts, width), lambda bi, i, pt: (bi * (t_pad // ts), col))
    grid_spec = pltpu.PrefetchScalarGridSpec(
        num_scalar_prefetch=1,
        grid=(db, n_steps),
        in_specs=_page_specs() + [
            seq_rows(N_WIDTH, 0),
            seq_rows(ROW_COLS, 0),
            pl.BlockSpec((1,) + kcvc.shape[1:], lambda bi, i, pt: (bi, 0, 0, 0)),
            pl.BlockSpec((wl_rows, HEAD_DIM), lambda bi, i, pt: (layer * db + bi, 0)),
            seq_rows(HEAD_DIM, COL_SMALL // HEAD_DIM),
            pl.BlockSpec(c2s.shape, lambda bi, i, pt: (0, 0))],
        out_specs=pl.BlockSpec((1, ts, N_WIDTH), lambda bi, i, pt: (bi, 0, 0)),
        scratch_shapes=[pltpu.VMEM((KV_GROUPS, n_steps, rows, HEAD_DIM), F32),
                        pltpu.VMEM((KV_GROUPS, rows, 1), F32),
                        pltpu.VMEM((KV_GROUPS, rows, 1), F32),
                        pltpu.VMEM((KV_GROUPS, rows, HEAD_DIM), F32),
                        pltpu.VMEM((KV_GROUPS, rows, HEAD_DIM), F32),
                        pltpu.VMEM((KV_GROUPS, rows, HEAD_DIM), F32)])
    return pl.pallas_call(
        functools.partial(_nsa_attn_sample_kernel, past_len=past_len, t_valid=t_valid, n_steps=n_steps),
        grid_spec=grid_spec,
        out_shape=jax.ShapeDtypeStruct((db, ts, N_WIDTH), BF16),
        compiler_params=pltpu.CompilerParams(
            dimension_semantics=("arbitrary", "arbitrary"), vmem_limit_bytes=VMEM_LIMIT_BYTES),
        name="nsa_attention_sample",
    )(pt, *([cache2d] * SAMPLE_PAGES), q_b, kvb, kcvc, win2d, zp, c2s)


def nsa_sample_pallas(zp, db, t, t_pad, layer, cache2d, pt, win_state, cmp_w):
    pe, w1, w2 = cmp_w
    past_len = pt.shape[1] * PAGE_SIZE
    cos2, sin2 = rope_tables(past_len + jnp.arange(t_pad))
    q_b, rows_f, win_f, kvb, _ = nsa_prep_pallas(zp, db, t_pad, cos2, sin2, tm=t_pad)
    kcvc = nsa_compress_sample(pt, cache2d, pe, w1, w2, db)
    win2d = win_state.reshape(-1, HEAD_DIM)
    hn = nsa_attention_sample(pt, cache2d, q_b, kvb, kcvc, win2d, zp, layer, db, t_pad, t, past_len)
    rows = rows_f.reshape(db, t_pad, 4, KV_GROUPS, HEAD_DIM)[:, :t]
    win_new = win_f.reshape(db, t_pad, 2, KV_GROUPS, HEAD_DIM)[:, :t]
    wl = win_state.shape[2]
    win = jnp.concatenate([win_state[layer], win_new], axis=1)[:, -wl:]
    return hn[:, :t], (rows, win)


MIX_L = 128
GLA_SUB = 16


def _logsig(x):
    return jnp.minimum(x, 0.0) - jnp.log1p(jnp.exp(-jnp.abs(x)))


def _prefix_sum(x, axis):
    n = x.shape[axis]
    idx = lax.broadcasted_iota(jnp.int32, x.shape, axis)
    step = 1
    while step < n:
        x = x + jnp.where(idx >= step, pltpu.roll(x, step, axis), 0.0)
        step *= 2
    return x


def _mlstm_phases(t, uqk_ref, v_ref, og_ref, small_ref, convw_ref, bias_ref, norm_ref, conv0_ref, c0_ref, n0_ref,
                  m0_ref, h_ref, c_ref, n_ref, m_ref, prev_sc, c_sc, n_sc, m_sc, *, t_valid):
    L = MIX_L
    d = HEAD_DIM

    prev_sc[...] = jnp.zeros_like(prev_sc)
    prev_sc[L - 8:, :] = conv0_ref[0]
    c_sc[...] = c0_ref[0]
    n_sc[...] = n0_ref[0]
    m_sc[...] = m0_ref[0]
    yield

    x = uqk_ref[...]
    prev = prev_sc[...]
    row = lax.broadcasted_iota(jnp.int32, x.shape, 0)
    w = convw_ref[...]
    conv = w[CONV_W - 1:CONV_W] * x
    for k in range(1, CONV_W):
        shifted = jnp.where(row >= k, pltpu.roll(x, k, 0), pltpu.roll(prev, k, 0))
        conv = conv + w[CONV_W - 1 - k:CONV_W - k] * shifted
    prev_sc[...] = x
    act = conv * jax.nn.sigmoid(conv)

    pre = small_ref[...] + bias_ref[...]
    pos_c = t * L + lax.broadcasted_iota(jnp.int32, (L, HEAD_DIM), 0)
    ig_c = jnp.where(pos_c < t_valid, pre, NEG)
    b_c = _prefix_sum(jnp.where(pos_c < t_valid, _logsig(pre), 0.0), 0)
    pre_r = jnp.transpose(pre)[0:8]
    pos_r = t * L + lax.broadcasted_iota(jnp.int32, (8, L), 1)
    ig_r = jnp.where(pos_r < t_valid, pre_r, NEG)
    b_r = _prefix_sum(jnp.where(pos_r < t_valid, _logsig(pre_r), 0.0), 1)

    li = lax.broadcasted_iota(jnp.int32, (L, L), 0)
    si = lax.broadcasted_iota(jnp.int32, (L, L), 1)
    for h in range(H_M):
        sl = slice(h * d, (h + 1) * d)
        bcol, igcol = b_c[:, H_M + h:H_M + h + 1], ig_c[:, h:h + 1]
        brow, igrow = b_r[H_M + h:H_M + h + 1, :], ig_r[h:h + 1, :]
        m0 = m_sc[h:h + 1, 0:1]
        c0 = c_sc[h]
        n0 = n_sc[h:h + 1, :]
        qf = act[:, sl]
        kf = act[:, M_WIDTH + h * d:M_WIDTH + (h + 1) * d] * (d ** -0.5)
        vf = v_ref[:, sl]
        qb, kb, vb = qf.astype(BF16), kf.astype(BF16), vf.astype(BF16)
        logw = jnp.where(si <= li, bcol - brow + igrow, NEG)
        gsum = bcol + m0
        m_row = jnp.maximum(jnp.max(logw, axis=1, keepdims=True), gsum)
        wgt = jnp.exp(logw - m_row) * lax.dot_general(qb, kb, _NT, preferred_element_type=F32)
        inter = jnp.exp(gsum - m_row)
        num = (jnp.dot(wgt.astype(BF16), vb, preferred_element_type=F32)
               + inter * lax.dot_general(qb, c0.astype(BF16), _NT, preferred_element_type=F32))
        den = jnp.sum(wgt, axis=1, keepdims=True) + inter * jnp.sum(qf * n0, axis=1, keepdims=True)
        hh = num / jnp.maximum(jnp.abs(den), jnp.exp(-m_row))
        bl = bcol[L - 1:L, :]
        m_new = jnp.maximum(jnp.max(bl - brow + igrow, axis=1, keepdims=True), bl + m0)
        wa = jnp.exp(bl - bcol + igcol - m_new)
        wc = jnp.exp(bl + m0 - m_new)
        c_sc[h] = wc * c0 + jnp.dot(jnp.transpose(vf * wa).astype(BF16), kb, preferred_element_type=F32)
        n_sc[h:h + 1, :] = wc * n0 + jnp.sum(wa * kf, axis=0, keepdims=True)
        m_sc[h:h + 1, :] = jnp.broadcast_to(m_new, (1, d))
        out = _rms(hh, norm_ref[:, sl]) * jax.nn.sigmoid(og_ref[:, sl])
        h_ref[:, sl] = out.astype(BF16)
    yield

    c_ref[0] = c_sc[...]
    n_ref[0] = n_sc[...]
    m_ref[0] = m_sc[...]
    yield


def _gla_phases(t, q_ref, k_ref, v_ref, r_ref, small_ref, w2_ref, gb_ref, norm_ref, s0_ref, h_ref, s_ref, st_sc,
                *, t_valid):
    L = MIX_L
    d = HEAD_DIM

    for h in range(H_G):
        st_sc[h] = jnp.transpose(s0_ref[0, h])
    yield

    pre = jnp.dot(small_ref[...].astype(BF16), w2_ref[...], preferred_element_type=F32) + gb_ref[...]
    pos = t * L + lax.broadcasted_iota(jnp.int32, (L, G_WIDTH), 0)
    la = jnp.where(pos < t_valid, _logsig(pre) / GLA_GATE_TEMP, 0.0)
    bc_all = _prefix_sum(la, 0)
    rowi = lax.broadcasted_iota(jnp.int32, (L, d), 0)
    li = lax.broadcasted_iota(jnp.int32, (L, L), 0)
    si = lax.broadcasted_iota(jnp.int32, (L, L), 1)
    for h in range(H_G):
        sl = slice(h * d, (h + 1) * d)
        bc = bc_all[:, sl]
        qf = q_ref[:, sl] * (d ** -0.5)
        kf = jnp.where(t * L + rowi < t_valid, k_ref[:, sl], 0.0)
        vf = v_ref[:, sl]
        qparts, kparts = [], []
        for j in range(L // GLA_SUB):
            lo, hi = j * GLA_SUB, (j + 1) * GLA_SUB
            e = bc[hi - 1:hi, :]
            qj = qf[lo:] * jnp.exp(bc[lo:] - e)
            kj = kf[lo:hi] * jnp.exp(e - bc[lo:hi])
            zeros = lambda n: [jnp.zeros((n, d), F32)] if n else []
            qparts.append(jnp.concatenate(zeros(lo) + [qj], axis=0).astype(BF16))
            kparts.append(jnp.concatenate(zeros(lo) + [kj] + zeros(L - hi), axis=0).astype(BF16))
        a = lax.dot_general(jnp.concatenate(qparts, axis=1), jnp.concatenate(kparts, axis=1), _NT,
                            preferred_element_type=F32)
        a = jnp.where(si <= li, a, 0.0)
        st = st_sc[h]
        o = (jnp.dot(a.astype(BF16), vf.astype(BF16), preferred_element_type=F32)
             + lax.dot_general((qf * jnp.exp(bc)).astype(BF16), st.astype(BF16), _NT, preferred_element_type=F32))
        bl = bc[L - 1:L, :]
        kd = (kf * jnp.exp(bl - bc)).astype(BF16)
        st_sc[h] = st * jnp.exp(bl) + jnp.dot(jnp.transpose(vf).astype(BF16), kd, preferred_element_type=F32)
        gate = r_ref[:, sl]
        h_ref[:, sl] = (_rms(o, norm_ref[:, sl]) * (gate * jax.nn.sigmoid(gate))).astype(BF16)
    yield

    for h in range(H_G):
        s_ref[0, h] = jnp.transpose(st_sc[h])
    yield


N_MLSTM_IN, N_GLA_IN, N_MLSTM_OUT, N_GLA_OUT, N_MLSTM_SCRATCH = 11, 9, 4, 2, 4


MIX_SEQS = 2
N_GLA_SCRATCH = 1


def _mixers_kernel(*refs, t_valid):
    t = pl.program_id(1)
    n_in, n_scr = N_MLSTM_IN + N_GLA_IN, N_MLSTM_SCRATCH + N_GLA_SCRATCH
    n_out = N_MLSTM_OUT + N_GLA_OUT
    all_in, outs, all_scr = refs[:n_in * MIX_SEQS], refs[n_in * MIX_SEQS:][:n_out], refs[n_in * MIX_SEQS + n_out:]
    hm_ref, c_ref, n_ref, m_ref, hg_ref, s_ref = outs
    phases = []
    for s in range(MIX_SEQS):
        ins, scr = all_in[s * n_in:(s + 1) * n_in], all_scr[s * n_scr:(s + 1) * n_scr]
        one = pl.ds(s, 1)
        phases.append(_mlstm_phases(t, *ins[:N_MLSTM_IN], hm_ref.at[s], c_ref.at[one], n_ref.at[one], m_ref.at[one],
                                    *scr[:N_MLSTM_SCRATCH], t_valid=t_valid))
        phases.append(_gla_phases(t, *ins[N_MLSTM_IN:], hg_ref.at[s], s_ref.at[one], *scr[N_MLSTM_SCRATCH:],
                                  t_valid=t_valid))

    @pl.when(t == 0)
    def _():
        for p in phases:
            next(p)

    for p in phases:
        next(p)

    @pl.when(t == pl.num_programs(1) - 1)
    def _():
        for p in phases:
            next(p)


def mixers_pallas(z, b, t, conv_buf, c0, n0, m0, s0, conv_w, m_gate_b, m_norm, g_w2, g_b, g_norm):
    t_pad = z.shape[0] // b
    L = MIX_L
    nt = t_pad // L
    lanes = lambda a: jnp.pad(a.reshape(1, -1), ((0, 0), (0, HEAD_DIM - a.size)))
    conv0 = jnp.pad(conv_buf, ((0, 0), (8 - (CONV_W - 1), 0), (0, 0)))
    n0p = jnp.pad(n0, ((0, 0), (0, 8 - H_M), (0, 0)))
    m0p = jnp.broadcast_to(jnp.pad(m0, ((0, 0), (0, 8 - H_M)))[:, :, None], (b, 8, HEAD_DIM))
    w2_pad = jnp.pad(g_w2, ((SMALL_GLR, HEAD_DIM - SMALL_GLR - GATE_RANK), (0, 0))).astype(BF16)
    nseq = MIX_SEQS
    assert b % nseq == 0, (b, nseq)
    const2 =lambda shape: pl.BlockSpec(shape, lambda bi, ti: (0, 0))
    gcol = COL_GLA // G_WIDTH
    in_specs, operands = [], []
    for q in range(nseq):
        seq = lambda bi, q=q: bi * nseq + q
        rowblk = lambda width, col, seq=seq: pl.BlockSpec((L, width), lambda bi, ti: (seq(bi) * nt + ti, col))
        per_b3 = pl.BlockSpec((1, 8, HEAD_DIM), lambda bi, ti, seq=seq: (seq(bi), 0, 0))
        per_b4 = pl.BlockSpec((1, H_M, HEAD_DIM, HEAD_DIM), lambda bi, ti, seq=seq: (seq(bi), 0, 0, 0))
        small = rowblk(HEAD_DIM, COL_SMALL // HEAD_DIM)
        mlstm_in = [rowblk(2 * M_WIDTH, 0), rowblk(M_WIDTH, 2), rowblk(M_WIDTH, 3), small,
                    const2((CONV_W, 2 * M_WIDTH)), const2((1, HEAD_DIM)), const2((1, M_WIDTH)),
                    pl.BlockSpec((1, 8, 2 * M_WIDTH), lambda bi, ti, seq=seq: (seq(bi), 0, 0)),
                    per_b4, per_b3, per_b3]
        gla_in = [rowblk(G_WIDTH, gcol), rowblk(G_WIDTH, gcol + 1), rowblk(G_WIDTH, gcol + 2),
                  rowblk(G_WIDTH, gcol + 3), small, const2((HEAD_DIM, G_WIDTH)), const2((1, G_WIDTH)),
                  const2((1, G_WIDTH)), per_b4]
        assert (len(mlstm_in), len(gla_in)) == (N_MLSTM_IN, N_GLA_IN)
        in_specs += mlstm_in + gla_in
        operands += [z, z, z, z, conv_w, lanes(m_gate_b), m_norm.reshape(1, -1), conv0, c0, n0p, m0p,
                     z, z, z, z, z, w2_pad, g_b.reshape(1, -1), g_norm.reshape(1, -1), s0]
    nb = b // nseq
    out_rows = lambda width: pl.BlockSpec((nseq, L, width), lambda bi, ti: (bi, ti, 0))
    out_b3 = pl.BlockSpec((nseq, 8, HEAD_DIM), lambda bi, ti: (bi, 0, 0))
    out_b4 = pl.BlockSpec((nseq, H_M, HEAD_DIM, HEAD_DIM), lambda bi, ti: (bi, 0, 0, 0))
    out_specs = [out_rows(M_WIDTH), out_b4, out_b3, out_b3, out_rows(G_WIDTH), out_b4]
    state = jax.ShapeDtypeStruct((b, H_M, HEAD_DIM, HEAD_DIM), F32)
    tile = jax.ShapeDtypeStruct((b, 8, HEAD_DIM), F32)
    out_shape = [jax.ShapeDtypeStruct((b, t_pad, M_WIDTH), BF16), state, tile, tile,
                 jax.ShapeDtypeStruct((b, t_pad, G_WIDTH), BF16), state]
    seq_scratch = [pltpu.VMEM((L, 2 * M_WIDTH), F32),
                   pltpu.VMEM((H_M, HEAD_DIM, HEAD_DIM), F32),
                   pltpu.VMEM((8, HEAD_DIM), F32),
                   pltpu.VMEM((8, HEAD_DIM), F32),
                   pltpu.VMEM((H_G, HEAD_DIM, HEAD_DIM), F32)]
    hm, c, n, m, hg, s = pl.pallas_call(
        functools.partial(_mixers_kernel, t_valid=t),
        grid=(nb, nt),
        in_specs=in_specs,
        out_specs=out_specs,
        out_shape=out_shape,
        scratch_shapes=seq_scratch * nseq,
        compiler_params=pltpu.CompilerParams(
            dimension_semantics=("arbitrary", "arbitrary"), vmem_limit_bytes=VMEM_LIMIT_BYTES),
        name="mixers",
    )(*operands)
    return hm[:, :t], hg[:, :t], c, n[:, :H_M], m[:, :H_M, 0], s


def split_cols(z):
    cut = lambda a, n: z[..., a:a + n]
    u_qk, m_v, m_o = cut(0, 2 * M_WIDTH), cut(2 * M_WIDTH, M_WIDTH), cut(3 * M_WIDTH, M_WIDTH)
    n_q, n_kv = cut(COL_NQ, N_WIDTH), cut(COL_NKV, 6 * KV_GROUPS * HEAD_DIM)
    g_q, g_k, g_v, g_r = (cut(COL_GLA + i * G_WIDTH, G_WIDTH) for i in range(4))
    m_if = cut(COL_SMALL + SMALL_MIF, 2 * H_M)
    n_g = cut(COL_SMALL + SMALL_NG, 3 * H_N)
    g_lr = cut(COL_SMALL + SMALL_GLR, GATE_RANK)
    return u_qk, m_v, m_o, m_if, n_q, n_kv, n_g, g_q, g_k, g_v, g_r, g_lr


W_IN_ORDER = (0, 1, 2, 4, 5, 7, 8, 9, 10, 3, 6, 11)


def _regroup_w_in_kernel(wt_ref, o_ref):
    o = (0,) + SPLIT_OFFSETS + (IN_COLS,)
    tk = wt_ref.shape[1]
    at = 0
    small = []
    for i in W_IN_ORDER:
        width = o[i + 1] - o[i]
        if width % HEAD_DIM == 0:
            o_ref[:, at:at + width] = jnp.transpose(wt_ref[o[i]:o[i + 1], :]).astype(BF16)
            at += width
        else:
            small.append(wt_ref[o[i]:o[i + 1], :])
    used = sum(s.shape[0] for s in small)
    small.append(jnp.zeros((HEAD_DIM - used, tk), F32))
    o_ref[:, at:at + HEAD_DIM] = jnp.transpose(jnp.concatenate(small, axis=0)).astype(BF16)
    at += HEAD_DIM
    o_ref[:, at:] = jnp.zeros((tk, IN_COLS_PAD - at), BF16)


def regroup_w_in(w_in):
    depth, k, n = w_in.shape
    tk = 256
    return pl.pallas_call(
        _regroup_w_in_kernel,
        grid=(depth, k // tk),
        in_specs=[pl.BlockSpec((None, n, tk), lambda l, i: (l, 0, i))],
        out_specs=pl.BlockSpec((None, tk, IN_COLS_PAD), lambda l, i: (l, i, 0)),
        out_shape=jax.ShapeDtypeStruct((depth, k, IN_COLS_PAD), BF16),
        compiler_params=pltpu.CompilerParams(
            dimension_semantics=("arbitrary", "arbitrary"), vmem_limit_bytes=VMEM_LIMIT_BYTES),
        name="regroup_w_in",
    )(jnp.transpose(w_in, (0, 2, 1)))


def trunk_layer(x, conv_buf, c0, n0, m0, s0, nsa_fn, layer, g_norms, w_in, conv_w, m_gate_b, m_norm,
                g_w2, g_b, g_norm, w_out, w_ff1, w_ff2):
    b, t, d = x.shape
    x2 = x.reshape(b * t, d)
    z = norm_matmul(x2, g_norms[0], w_in, layer, IN_TN)
    z3 = z.reshape(b, t, -1)
    t_pad = -(-t // MIX_L) * MIX_L
    zp = z if t_pad == t else jnp.pad(z3, ((0, 0), (0, t_pad - t), (0, 0))).reshape(b * t_pad, -1)
    hm, hg, c, n, m, s = mixers_pallas(zp, b, t, conv_buf, c0, n0, m0, s0,
                                       conv_w, m_gate_b, m_norm, g_w2, g_b, g_norm)
    new_conv = z3[:, t - (CONV_W - 1):, :2 * M_WIDTH]
    hn, nsa_state = nsa_fn(zp, b, t, t_pad)
    x2 = matmul_norm_res(hm.reshape(b * t, -1), hn.reshape(b * t, -1), hg.reshape(b * t, -1),
                         w_out, layer, g_norms[1], x2)
    x2 = ffn(x2, g_norms[2], w_ff1, w_ff2, layer, g_norms[3])
    return x2.reshape(b, t, d), (nsa_state[0], nsa_state[1], c, n, m, new_conv, s)


def kernel(x_prompt, x_sample, cache_nsa_kv, state_nsa_win, state_mlstm_C, state_mlstm_n, state_mlstm_m, state_mlstm_conv, state_gla_S, page_table, norms, w_in, mlstm_conv_w, mlstm_gate_b, mlstm_norm, nsa_cmp_pe, nsa_cmp_w1, nsa_cmp_w2, gla_gate_w2, gla_gate_b, gla_norm, w_out, w_ff1, w_ff2):
    xp, xs = x_prompt, x_sample
    bp = xp.shape[0]
    conv0 = jnp.zeros((bp, CONV_W - 1, 2 * M_WIDTH), xp.dtype)
    c0 = jnp.zeros((bp, H_M, HEAD_DIM, HEAD_DIM), F32)
    n0 = jnp.zeros((bp, H_M, HEAD_DIM), F32)
    m0 = jnp.zeros((bp, H_M), F32)
    s0 = jnp.zeros((bp, H_G, HEAD_DIM, HEAD_DIM), F32)
    w_in_b = regroup_w_in(w_in)
    w_out_b = w_out.astype(BF16)
    w_ff1_b = w_ff1.astype(BF16)
    w_ff2_b = w_ff2.astype(BF16)
    cmp_pe = nsa_cmp_pe.reshape(DEPTH, 2, 1, CMP_LEN * HEAD_DIM)
    cmp_w1_b = nsa_cmp_w1.reshape(DEPTH, 2, CMP_LEN * HEAD_DIM, CMP_HID).astype(BF16)
    cmp_w2_b = nsa_cmp_w2.astype(BF16)
    n_pool = cache_nsa_kv.shape[1]
    cache2d = cache_nsa_kv.reshape(-1, HEAD_DIM)
    acc_p = [[] for _ in range(6)]
    acc_s = [[] for _ in range(7)]
    rows_all = None
    for l in range(DEPTH):
        weights = (l, norms[l], w_in_b, mlstm_conv_w[l], mlstm_gate_b[l], mlstm_norm[l],
                   gla_gate_w2[l], gla_gate_b[l], gla_norm[l], w_out_b, w_ff1_b, w_ff2_b)
        cmp_w = (cmp_pe[l], cmp_w1_b[l], cmp_w2_b[l])
        nsa_p = lambda zp, b, t, t_pad, l=l, cmp_w=cmp_w, rows_all=rows_all: nsa_prompt_pallas(
            zp, b, t, cmp_w, l, rows_all)
        nsa_s = lambda zp, b, t, t_pad, l=l, cmp_w=cmp_w: nsa_sample_pallas(
            zp, b, t, t_pad, l, cache2d, page_table + l * n_pool, state_nsa_win, cmp_w)
        xp, st_p = trunk_layer(xp, conv0, c0, n0, m0, s0, nsa_p, *weights)
        xs, st_s = trunk_layer(xs, state_mlstm_conv[l], state_mlstm_C[l], state_mlstm_n[l],
                               state_mlstm_m[l], state_gla_S[l], nsa_s, *weights)
        rows_all = st_p[0]
        for acc, a in zip(acc_p, st_p[1:]):
            acc.append(a.astype(xp.dtype))
        for acc, a in zip(acc_s, st_s):
            acc.append(a.astype(xs.dtype))
    nsa_rows_p = rows_all.reshape((DEPTH, bp, xp.shape[1], 4, KV_GROUPS, HEAD_DIM))
    nsa_win_p, mlstm_c_p, mlstm_n_p, mlstm_m_p, mlstm_conv_p, gla_s_p = [jnp.stack(a) for a in acc_p]
    nsa_rows_s, nsa_win_s, mlstm_c_s, mlstm_n_s, mlstm_m_s, mlstm_conv_s, gla_s_s = [jnp.stack(a) for a in acc_s]
    return (xp, xs, nsa_rows_p, nsa_rows_s, nsa_win_p, nsa_win_s, mlstm_c_p, mlstm_c_s,
            mlstm_n_p, mlstm_n_s, mlstm_m_p, mlstm_m_s, mlstm_conv_p, mlstm_conv_s, gla_s_p, gla_s_s)
```

```python
import functools
import math

import jax
import jax.numpy as jnp
import numpy as np
from jax import lax
from jax.experimental import pallas as pl
from jax.experimental.pallas import tpu as pltpu

D_MODEL = 2048
DEPTH = 4
PAGE_SIZE = 128
HEAD_DIM = 128
D_MIX = D_MODEL
M_WIDTH = D_MIX // 4
G_WIDTH = D_MIX // 4
N_WIDTH = D_MIX - M_WIDTH - G_WIDTH
H_M = M_WIDTH // HEAD_DIM
H_N = N_WIDTH // HEAD_DIM
H_G = G_WIDTH // HEAD_DIM
KV_GROUPS = 2
Q_PER_KV = H_N // KV_GROUPS
D_FF = 4 * D_MODEL
CONV_W = 4
CHUNK = 64
CMP_STRIDE = 16
CMP_LEN = 2 * CMP_STRIDE
CMP_HID = 256
SEL_BLK = 64
N_SELECT = 16
WINDOW = 512
QBLK = 128
GATE_RANK = 16
GLA_GATE_TEMP = 16.0
ROPE_THETA = 10000.0
EPS = 1e-6
TINY = 1e-30
FORCE_BONUS = 1e3
NEG_BIG = -1e9
SPLIT_SIZES = (2 * M_WIDTH, M_WIDTH, M_WIDTH, 2 * H_M,
               N_WIDTH, 6 * KV_GROUPS * HEAD_DIM, 3 * H_N,
               G_WIDTH, G_WIDTH, G_WIDTH, G_WIDTH, GATE_RANK)
IN_COLS = sum(SPLIT_SIZES)
SPLIT_OFFSETS = tuple(int(o) for o in np.cumsum(SPLIT_SIZES)[:-1])

F32 = jnp.float32
BF16 = jnp.bfloat16

VMEM_LIMIT_BYTES = 56 * 1024 * 1024
IN_TN = 1024
COL_NQ = 2 * M_WIDTH + 2 * M_WIDTH
COL_NKV = COL_NQ + N_WIDTH
COL_GLA = COL_NKV + 6 * KV_GROUPS * HEAD_DIM
COL_SMALL = COL_GLA + 4 * G_WIDTH
SMALL_MIF = 0
SMALL_NG = 2 * H_M
SMALL_GLR = SMALL_NG + 3 * H_N
IN_COLS_PAD = 7168
NEG = -1e30
NSA_TQ = 256
NSA_TK = 1024


def _rms(x, g):
    return x * lax.rsqrt(jnp.mean(x * x, axis=-1, keepdims=True) + EPS) * g


def _norm_matmul_kernel(x_ref, g_ref, w_ref, o_ref, xn_ref):
    @pl.when(pl.program_id(1) == 0)
    def _():
        xn_ref[...] = _rms(x_ref[...], g_ref[...]).astype(BF16)

    o_ref[...] = jnp.dot(xn_ref[...], w_ref[...], preferred_element_type=F32)


def norm_matmul(x, g, w, layer, tn):
    m, k = x.shape
    n = w.shape[2]
    tm = min(m, 1024)
    return pl.pallas_call(
        _norm_matmul_kernel,
        grid=(m // tm, n // tn),
        in_specs=[pl.BlockSpec((tm, k), lambda i, j: (i, 0)),
                  pl.BlockSpec((1, k), lambda i, j: (0, 0)),
                  pl.BlockSpec((None, k, tn), lambda i, j: (layer, 0, j))],
        out_specs=pl.BlockSpec((tm, tn), lambda i, j: (i, j)),
        out_shape=jax.ShapeDtypeStruct((m, n), F32),
        scratch_shapes=[pltpu.VMEM((tm, k), BF16)],
        compiler_params=pltpu.CompilerParams(
            dimension_semantics=("arbitrary", "arbitrary"), vmem_limit_bytes=VMEM_LIMIT_BYTES),
        name="norm_matmul",
    )(x, g.reshape(1, k), w)


def _matmul_norm_res_kernel(a0_ref, a1_ref, a2_ref, w_ref, g_ref, r_ref, o_ref):
    k0, k1 = a0_ref.shape[1], a0_ref.shape[1] + a1_ref.shape[1]
    y = (jnp.dot(a0_ref[...], w_ref[:k0, :], preferred_element_type=F32)
         + jnp.dot(a1_ref[...], w_ref[k0:k1, :], preferred_element_type=F32)
         + jnp.dot(a2_ref[...], w_ref[k1:, :], preferred_element_type=F32))
    o_ref[...] = r_ref[...] + _rms(y, g_ref[...])


def matmul_norm_res(a0, a1, a2, w, layer, g, r):
    m = a0.shape[0]
    _, k, n = w.shape
    tm = min(m, 512)
    rows = lambda a: pl.BlockSpec((tm, a.shape[1]), lambda i: (i, 0))
    return pl.pallas_call(
        _matmul_norm_res_kernel,
        grid=(m // tm,),
        in_specs=[rows(a0), rows(a1), rows(a2),
                  pl.BlockSpec((None, k, n), lambda i: (layer, 0, 0)),
                  pl.BlockSpec((1, n), lambda i: (0, 0)),
                  pl.BlockSpec((tm, n), lambda i: (i, 0))],
        out_specs=pl.BlockSpec((tm, n), lambda i: (i, 0)),
        out_shape=jax.ShapeDtypeStruct((m, n), F32),
        compiler_params=pltpu.CompilerParams(
            dimension_semantics=("arbitrary",), vmem_limit_bytes=VMEM_LIMIT_BYTES),
        name="matmul_norm_res",
    )(a0, a1, a2, w, g.reshape(1, n), r)


def _ffn_kernel(x_ref, g2_ref, w1_ref, w2_ref, g3_ref, o_ref, xn_ref, acc_ref):
    f = pl.program_id(1)

    @pl.when(f == 0)
    def _():
        xn_ref[...] = _rms(x_ref[...], g2_ref[...]).astype(BF16)
        acc_ref[...] = jnp.zeros_like(acc_ref)

    h = jnp.dot(xn_ref[...], w1_ref[...], preferred_element_type=F32)
    a = jnp.square(jnp.maximum(h, 0.0)).astype(BF16)
    acc_ref[...] += jnp.dot(a, w2_ref[...], preferred_element_type=F32)

    @pl.when(f == pl.num_programs(1) - 1)
    def _():
        o_ref[...] = x_ref[...] + _rms(acc_ref[...], g3_ref[...])


def ffn(x, g2, w1, w2, layer, g3):
    m, d = x.shape
    dff = w1.shape[2]
    tm = min(m, 512)
    tf = 1024
    return pl.pallas_call(
        _ffn_kernel,
        grid=(m // tm, dff // tf),
        in_specs=[pl.BlockSpec((tm, d), lambda i, f: (i, 0)),
                  pl.BlockSpec((1, d), lambda i, f: (0, 0)),
                  pl.BlockSpec((None, d, tf), lambda i, f: (layer, 0, f)),
                  pl.BlockSpec((None, tf, d), lambda i, f: (layer, f, 0)),
                  pl.BlockSpec((1, d), lambda i, f: (0, 0))],
        out_specs=pl.BlockSpec((tm, d), lambda i, f: (i, 0)),
        out_shape=jax.ShapeDtypeStruct((m, d), F32),
        scratch_shapes=[pltpu.VMEM((tm, d), BF16), pltpu.VMEM((tm, d), F32)],
        compiler_params=pltpu.CompilerParams(
            dimension_semantics=("arbitrary", "arbitrary"), vmem_limit_bytes=VMEM_LIMIT_BYTES),
        name="ffn",
    )(x, g2.reshape(1, d), w1, w2, g3.reshape(1, d))


def rope_tables(pos):
    half = HEAD_DIM // 2
    inv_freq = jnp.exp(-math.log(ROPE_THETA) * jnp.arange(half, dtype=F32) / half)
    ang = pos.astype(F32)[:, None] * inv_freq[None, :]
    cos, sin = jnp.cos(ang), jnp.sin(ang)
    return jnp.concatenate([cos, cos], axis=-1), jnp.concatenate([-sin, sin], axis=-1)


def _rope(x, cos2, sin2):
    return x * cos2 + pltpu.roll(x, HEAD_DIM // 2, 1) * sin2


def _nsa_prep_kernel(nq_ref, nkv_ref, cos_ref, sin_ref, q_ref, rows_ref, win_ref, kvb_ref, cmp_ref):
    cos2 = cos_ref[...]
    sin2 = sin_ref[...]
    for h in range(H_N):
        sl = slice(h * HEAD_DIM, (h + 1) * HEAD_DIM)
        q_ref[:, sl] = (_rope(nq_ref[:, sl], cos2, sin2) * (HEAD_DIM ** -0.5)).astype(BF16)
    for c in range(6 * KV_GROUPS):
        slot = c // KV_GROUPS
        x = nkv_ref[:, c * HEAD_DIM:(c + 1) * HEAD_DIM]
        if slot % 2 == 0:
            x = _rope(x, cos2, sin2)
        xb = x.astype(BF16)
        tm = x.shape[0]
        if slot < 4:
            rows_ref[pl.ds(c, tm, stride=ROW_VECS), :] = x
        else:
            win_ref[pl.ds(c - ROW_VECS, tm, stride=WIN_VECS), :] = x
        if slot < 2:
            cmp_ref[0, c] = xb
        else:
            kvb_ref[:, (c - 4) * HEAD_DIM:(c - 3) * HEAD_DIM] = xb


def nsa_prep_pallas(z, b, s, cos2, sin2, tm=512, layer=0, depth=1, rows_all=None):
    m = b * s
    nsb = s // tm
    steps = m // tm
    kern = _nsa_prep_kernel if rows_all is None else (lambda *refs: _nsa_prep_kernel(*refs[:4], *refs[5:]))
    extra_specs = [] if rows_all is None else [pl.BlockSpec(memory_space=pl.ANY)]
    extra_args = [] if rows_all is None else [rows_all]
    return pl.pallas_call(
        kern,
        grid=(steps,),
        in_specs=[pl.BlockSpec((tm, N_WIDTH), lambda i: (i, COL_NQ // N_WIDTH)),
                  pl.BlockSpec((tm, 1536), lambda i: (i, COL_NKV // 1536)),
                  pl.BlockSpec((tm, HEAD_DIM), lambda i: (i % nsb, 0)),
                  pl.BlockSpec((tm, HEAD_DIM), lambda i: (i % nsb, 0))] + extra_specs,
        out_specs=[pl.BlockSpec((tm, N_WIDTH), lambda i: (i, 0)),
                   pl.BlockSpec((tm * 8, HEAD_DIM), lambda i: (layer * steps + i, 0)),
                   pl.BlockSpec((tm * 4, HEAD_DIM), lambda i: (i, 0)),
                   pl.BlockSpec((tm, 1024), lambda i: (i, 0)),
                   pl.BlockSpec((1, 4, tm, HEAD_DIM), lambda i: (i // nsb, 0, i % nsb, 0))],
        out_shape=[jax.ShapeDtypeStruct((m, N_WIDTH), BF16),
                   jax.ShapeDtypeStruct((depth * m * 8, HEAD_DIM), F32),
                   jax.ShapeDtypeStruct((m * 4, HEAD_DIM), F32),
                   jax.ShapeDtypeStruct((m, 1024), BF16),
                   jax.ShapeDtypeStruct((b, 4, s, HEAD_DIM), BF16)],
        input_output_aliases={} if rows_all is None else {4: 1},
        compiler_params=pltpu.CompilerParams(
            dimension_semantics=("arbitrary",), vmem_limit_bytes=VMEM_LIMIT_BYTES),
        name="nsa_prep",
    )(z, z, cos2, sin2, *extra_args)


def _nsa_cmp_kernel(x_ref, pe_ref, w1_ref, w2_ref, o_ref):
    x = x_ref[0, 0]
    nb = x.shape[0]
    half = CMP_STRIDE * HEAD_DIM
    first = jnp.dot(x, w1_ref[0, :half, :], preferred_element_type=F32)
    second = jnp.dot(x, w1_ref[0, half:, :], preferred_element_type=F32)
    pe = jnp.broadcast_to(pe_ref[0], (8, 2 * half)).astype(BF16)
    bias = jnp.dot(pe, w1_ref[0], preferred_element_type=F32)[0:1]
    hid = first + pltpu.roll(second, nb - 1, 0) + bias
    o_ref[0, 0] = jnp.dot(jax.nn.gelu(hid).astype(BF16), w2_ref[0], preferred_element_type=F32).astype(BF16)


def nsa_compress_prompt(cmp_in, pe, w1, w2):
    b, _, nb, kdim = cmp_in.shape
    return pl.pallas_call(
        _nsa_cmp_kernel,
        grid=(b, 4),
        in_specs=[pl.BlockSpec((1, 1, nb, kdim), lambda i, c: (i, c, 0, 0)),
                  pl.BlockSpec((1, 1, 2 * kdim), lambda i, c: (c // 2, 0, 0)),
                  pl.BlockSpec((1, 2 * kdim, CMP_HID), lambda i, c: (c // 2, 0, 0)),
                  pl.BlockSpec((1, CMP_HID, HEAD_DIM), lambda i, c: (c // 2, 0, 0))],
        out_specs=pl.BlockSpec((1, 1, nb, HEAD_DIM), lambda i, c: (i, c, 0, 0)),
        out_shape=jax.ShapeDtypeStruct((b, 4, nb, HEAD_DIM), BF16),
        compiler_params=pltpu.CompilerParams(
            dimension_semantics=("arbitrary", "arbitrary"), vmem_limit_bytes=VMEM_LIMIT_BYTES),
        name="nsa_compress",
    )(cmp_in, pe, w1, w2)


def _softmax_rows(s, mask):
    sm = jnp.where(mask, s, NEG)
    m = jnp.max(sm, axis=-1, keepdims=True)
    p = jnp.where(mask, jnp.exp(sm - m), 0.0)
    return p / jnp.maximum(jnp.sum(p, axis=-1, keepdims=True), TINY)


_NT = (((1,), (1,)), ((), ()))


def _nsa_attn_kernel(q_ref, kc_ref, vc_ref, ks_ref, vs_ref, kw_ref, vw_ref, gate_ref, o_ref):
    tq = NSA_TQ
    r4 = Q_PER_KV
    g = pl.program_id(1)
    i = pl.program_id(2)
    q4 = q_ref[...]
    q = jnp.concatenate([q4[:, r * HEAD_DIM:(r + 1) * HEAD_DIM] for r in range(r4)], axis=0)
    q0 = i * tq
    tpos = q0 + lax.broadcasted_iota(jnp.int32, (tq, 1), 0)

    ncp = kc_ref.shape[2]
    s = lax.dot_general(q, kc_ref[0, 0], _NT, preferred_element_type=F32).reshape(r4, tq, ncp)
    cend = lax.broadcasted_iota(jnp.int32, (tq, ncp), 1) * CMP_STRIDE + (CMP_LEN - 1)
    p_c = _softmax_rows(s, (cend <= tpos)[None])
    o_c = jnp.dot(p_c.reshape(r4 * tq, ncp).astype(BF16), vc_ref[0, 0], preferred_element_type=F32)

    psum = p_c[0] + p_c[1] + p_c[2] + p_c[3]
    n_i = lax.broadcasted_iota(jnp.int32, (ncp, HEAD_DIM), 0)
    j_i = lax.broadcasted_iota(jnp.int32, (ncp, HEAD_DIM), 1)
    ratio = SEL_BLK // CMP_STRIDE
    c2s = jnp.where((n_i >= ratio * j_i - 1) & (n_i <= ratio * j_i + ratio - 1) & (n_i < ncp - 1)
                    & (j_i < ncp // ratio), 1.0, 0.0).astype(BF16)
    p_hi = psum.astype(BF16)
    p_lo = (psum - p_hi.astype(F32)).astype(BF16)
    imp = (jnp.dot(p_hi, c2s, preferred_element_type=F32)
           + jnp.dot(p_lo, c2s, preferred_element_type=F32))

    ns = ncp // ratio
    imp_t = jnp.transpose(imp)[:ns]
    jj = lax.broadcasted_iota(jnp.int32, (ns, tq), 0)
    tt = q0 + lax.broadcasted_iota(jnp.int32, (ns, tq), 1)
    cur = tt // SEL_BLK
    forced = (jj == 0) | (jj == cur) | (jj == cur - 1)
    score = jnp.where(jj * SEL_BLK <= tt, imp_t + jnp.where(forced, FORCE_BONUS, 0.0), NEG_BIG)
    rank = jnp.zeros((ns, tq), F32)
    for jp in range(ns):
        row = score[jp:jp + 1, :]
        rank = rank + jnp.where(row > score, 1.0, jnp.where((row == score) & (jj > jp), 1.0, 0.0))
    sel_t = jnp.where(rank < float(N_SELECT), 1.0, 0.0)
    sel_t = jnp.concatenate([sel_t, jnp.zeros((HEAD_DIM - ns, tq), F32)], axis=0)
    sel = jnp.transpose(sel_t)

    tk = NSA_TK
    n_tiles = (q0 + tq + tk - 1) // tk
    unpicked = jnp.where(sel > 0.5, 0.0, NEG).astype(BF16)
    q_aug = jnp.concatenate([q, jnp.concatenate([unpicked] * r4, axis=0)], axis=1)
    blk_lane = lax.broadcasted_iota(jnp.int32, (tk, HEAD_DIM), 1)
    blk_of_key = lax.broadcasted_iota(jnp.int32, (tk, HEAD_DIM), 0) // SEL_BLK

    hc = r4
    chains = r4 // hc
    q_chain = [q_aug[c * hc * tq:(c + 1) * hc * tq] for c in range(chains)]

    def tile_update(states, kt, causal):
        k0 = pl.multiple_of(kt * tk, tk)
        onehot = jnp.where(blk_lane == kt * (tk // SEL_BLK) + blk_of_key, 1.0, 0.0).astype(BF16)
        k_aug = jnp.concatenate([ks_ref[pl.ds(k0, tk), :], onehot], axis=1)
        v_tile = vs_ref[pl.ds(k0, tk), :]
        out = []
        for c in range(chains):
            m_run, l_run, acc = states[c]
            s = lax.dot_general(q_chain[c], k_aug, _NT, preferred_element_type=F32).reshape(hc, tq, tk)
            if causal is not None:
                s = jnp.where(causal, s, NEG)
            m_new = jnp.maximum(m_run, jnp.max(s, axis=-1, keepdims=True))
            alpha = jnp.exp(m_run - m_new)
            p = jnp.exp(s - m_new)
            l_new = alpha * l_run + jnp.sum(p, axis=-1, keepdims=True)
            pv = jnp.dot(p.reshape(hc * tq, tk).astype(BF16), v_tile, preferred_element_type=F32)
            out.append((m_new, l_new, alpha.reshape(hc * tq, 1) * acc + pv))
        return tuple(out)

    init = tuple((jnp.full((hc, tq, 1), NEG, F32), jnp.zeros((hc, tq, 1), F32),
                  jnp.zeros((hc * tq, HEAD_DIM), F32)) for _ in range(chains))
    states = lax.fori_loop(0, n_tiles - 1, lambda kt, st: tile_update(st, kt, None), init)
    kpos = (n_tiles - 1) * tk + lax.broadcasted_iota(jnp.int32, (tq, tk), 1)
    states = tile_update(states, n_tiles - 1, (kpos <= tpos)[None])
    o_s = jnp.concatenate([acc / jnp.maximum(l_run, TINY).reshape(hc * tq, 1) for _, l_run, acc in states], axis=0)

    wb = WINDOW + tq
    w0 = pl.multiple_of(jnp.maximum(q0 - WINDOW, 0), tq)
    s = lax.dot_general(q, kw_ref[pl.ds(w0, wb), :], _NT, preferred_element_type=F32).reshape(r4, tq, wb)
    dist = tpos - (w0 + lax.broadcasted_iota(jnp.int32, (tq, wb), 1))
    wmask = ((dist >= 0) & (dist < WINDOW))[None]
    sm = jnp.where(wmask, s, NEG)
    p_w = jnp.where(wmask, jnp.exp(sm - jnp.max(sm, axis=-1, keepdims=True)), 0.0)
    den_w = jnp.maximum(jnp.sum(p_w, axis=-1, keepdims=True), TINY).reshape(r4 * tq, 1)
    o_w = jnp.dot(p_w.reshape(r4 * tq, wb).astype(BF16), vw_ref[pl.ds(w0, wb), :],
                  preferred_element_type=F32) / den_w

    gates = jax.nn.sigmoid(gate_ref[...])
    for r in range(r4):
        def gate(c):
            lane0 = SMALL_NG + 3 * r + c
            lane1 = lane0 + 3 * r4
            return jnp.where(g == 0, gates[:, lane0:lane0 + 1], gates[:, lane1:lane1 + 1])
        rows = slice(r * tq, (r + 1) * tq)
        o = gate(0) * o_c[rows] + gate(1) * o_s[rows] + gate(2) * o_w[rows]
        o_ref[:, r * HEAD_DIM:(r + 1) * HEAD_DIM] = o.astype(BF16)


def nsa_attention_prompt(q_b, kcvc, kvb, z, b, s):
    tq = NSA_TQ
    nq = s // tq
    ncp = kcvc.shape[2]
    gw = Q_PER_KV * HEAD_DIM
    seq_block = lambda col: pl.BlockSpec((s, HEAD_DIM), lambda bi, g, i: (bi, col + g))
    return pl.pallas_call(
        _nsa_attn_kernel,
        grid=(b, KV_GROUPS, nq),
        in_specs=[pl.BlockSpec((tq, gw), lambda bi, g, i: (bi * nq + i, g)),
                  pl.BlockSpec((1, 1, ncp, HEAD_DIM), lambda bi, g, i: (bi, g, 0, 0)),
                  pl.BlockSpec((1, 1, ncp, HEAD_DIM), lambda bi, g, i: (bi, KV_GROUPS + g, 0, 0)),
                  seq_block(0), seq_block(2), seq_block(4), seq_block(6),
                  pl.BlockSpec((tq, HEAD_DIM), lambda bi, g, i: (bi * nq + i, COL_SMALL // HEAD_DIM))],
        out_specs=pl.BlockSpec((tq, gw), lambda bi, g, i: (bi * nq + i, g)),
        out_shape=jax.ShapeDtypeStruct((b * s, N_WIDTH), BF16),
        compiler_params=pltpu.CompilerParams(
            dimension_semantics=("arbitrary", "arbitrary", "arbitrary"), vmem_limit_bytes=VMEM_LIMIT_BYTES),
        name="nsa_attention",
    )(q_b, kcvc, kcvc, kvb, kvb, kvb, kvb, z)


def nsa_prompt_pallas(z, b, s, cmp_w, layer, rows_all):
    pe, w1, w2 = cmp_w
    cos2, sin2 = rope_tables(jnp.arange(s))
    q_b, rows_all, win_f, kvb, cmp_in = nsa_prep_pallas(z, b, s, cos2, sin2, layer=layer, depth=DEPTH,
                                                        rows_all=rows_all)
    kcvc = nsa_compress_prompt(cmp_in.reshape(b, 4, s // CMP_STRIDE, CMP_STRIDE * HEAD_DIM), pe, w1, w2)
    hn = nsa_attention_prompt(q_b, kcvc, kvb, z, b, s)
    win = win_f.reshape(b, s, 2, KV_GROUPS, HEAD_DIM)[:, -min(WINDOW, s):]
    return hn, (rows_all, win)


SAMPLE_PAGES = 16
SAMPLE_TS = 16
SAMPLE_ROWS = 8
ROW_COLS = 4 * KV_GROUPS * HEAD_DIM
ROW_VECS = 4 * KV_GROUPS
WIN_VECS = 2 * KV_GROUPS
NEG_DEAD = -3e38


def _page_specs():
    def spec(k):
        return pl.BlockSpec((PAGE_SIZE * ROW_VECS, HEAD_DIM), lambda bi, i, pt: (pt[bi, i * SAMPLE_PAGES + k], 0))
    return [spec(k) for k in range(SAMPLE_PAGES)]


def _page_vecs(pg, vec):
    return pg[pl.ds(vec, PAGE_SIZE, stride=ROW_VECS), :]


def _nsa_cmp_sample_kernel(pt_ref, *refs):
    del pt_ref
    pages = refs[:SAMPLE_PAGES]
    pe_ref, w1_ref, w2_ref, o_ref, carry_sc, bias_sc = refs[SAMPLE_PAGES:]
    i = pl.program_id(1)
    half = CMP_STRIDE * HEAD_DIM
    nbp = PAGE_SIZE // CMP_STRIDE
    nb = SAMPLE_PAGES * nbp

    @pl.when(i == 0)
    def _():
        carry_sc[...] = jnp.zeros_like(carry_sc)
        for kv in range(2):
            pe = jnp.broadcast_to(pe_ref[kv], (8, 2 * half)).astype(BF16)
            bias_sc[kv] = jnp.dot(pe, w1_ref[kv], preferred_element_type=F32)

    row = lax.broadcasted_iota(jnp.int32, (nb, CMP_HID), 0)
    for kv in range(2):
        def flat(c):
            piece = lambda pg, j: pg[pl.ds(j * ROW_VECS + c, nbp, stride=CMP_STRIDE * ROW_VECS), :]
            return jnp.concatenate(
                [jnp.concatenate([piece(pg, j) for j in range(CMP_STRIDE)], axis=1) for pg in pages],
                axis=0)
        x = jnp.concatenate([flat(kv * KV_GROUPS + g) for g in range(KV_GROUPS)], axis=0).astype(BF16)
        first_all = jnp.dot(x, w1_ref[kv, :half, :], preferred_element_type=F32)
        second_all = jnp.dot(x, w1_ref[kv, half:, :], preferred_element_type=F32)
        for g in range(KV_GROUPS):
            c = kv * KV_GROUPS + g
            first = first_all[g * nb:(g + 1) * nb]
            second = second_all[g * nb:(g + 1) * nb]
            shifted = jnp.where(row == 0, carry_sc[c, 7:8, :], pltpu.roll(first, 1, 0))
            hid = shifted + second + bias_sc[kv, 0:1, :]
            o_ref[0, c] = jnp.dot(jax.nn.gelu(hid).astype(BF16), w2_ref[kv],
                                  preferred_element_type=F32).astype(BF16)
            carry_sc[c] = first[nb - 8:, :]


def nsa_compress_sample(pt, cache2d, pe, w1, w2, db):
    n_pages = pt.shape[1]
    nb_all = n_pages * (PAGE_SIZE // CMP_STRIDE)
    nb = SAMPLE_PAGES * (PAGE_SIZE // CMP_STRIDE)
    const3 = lambda bi, i, pt: (0, 0, 0)
    grid_spec = pltpu.PrefetchScalarGridSpec(
        num_scalar_prefetch=1,
        grid=(db, n_pages // SAMPLE_PAGES),
        in_specs=_page_specs() + [pl.BlockSpec(pe.shape, const3), pl.BlockSpec(w1.shape, const3),
                                   pl.BlockSpec(w2.shape, const3)],
        out_specs=pl.BlockSpec((1, 2 * KV_GROUPS, nb, HEAD_DIM), lambda bi, i, pt: (bi, 0, i, 0)),
        scratch_shapes=[pltpu.VMEM((2 * KV_GROUPS, 8, CMP_HID), F32), pltpu.VMEM((2, 8, CMP_HID), F32)])
    return pl.pallas_call(
        _nsa_cmp_sample_kernel,
        grid_spec=grid_spec,
        out_shape=jax.ShapeDtypeStruct((db, 2 * KV_GROUPS, nb_all, HEAD_DIM), BF16),
        compiler_params=pltpu.CompilerParams(
            dimension_semantics=("arbitrary", "arbitrary"), vmem_limit_bytes=VMEM_LIMIT_BYTES),
        name="nsa_compress_sample",
    )(pt, *([cache2d] * SAMPLE_PAGES), pe, w1, w2)


def _nsa_attn_sample_kernel(pt_ref, *refs, past_len, t_valid, n_steps):
    del pt_ref
    pages = refs[:SAMPLE_PAGES]
    (q_ref, new_ref, kc_ref, win_ref, gate_ref, c2s_ref, o_ref,
     sel_sc, m_sc, l_sc, acc_sc, oc_sc, ow_sc) = refs[SAMPLE_PAGES:]
    i = pl.program_id(1)
    tb = SAMPLE_TS
    ts = SAMPLE_ROWS
    r4 = Q_PER_KV
    rows = r4 * ts
    d = HEAD_DIM
    blocks_per_step = SAMPLE_PAGES * PAGE_SIZE // SEL_BLK
    ns = past_len // SEL_BLK + 1
    trow = lax.broadcasted_iota(jnp.int32, (rows, 1), 0) % ts
    qpos = past_len + trow

    def q_of(g):
        q32 = q_ref[:, g * r4 * d:(g + 1) * r4 * d].astype(F32)
        return jnp.concatenate([q32[:ts, r * d:(r + 1) * d] for r in range(r4)], axis=0).astype(BF16)

    @pl.when(i == 0)
    def _():
        ncp = kc_ref.shape[2]
        nsl = (n_steps + 1) * d
        c2s = c2s_ref[...]
        lane_r = lax.broadcasted_iota(jnp.int32, (ts, nsl), 1)
        blk_r = (lane_r // d) * blocks_per_step + lane_r % d
        live = (lane_r % d < blocks_per_step) & (blk_r < ns)
        blk_f = blk_r.astype(F32)
        qpos_t = past_len + lax.broadcasted_iota(jnp.int32, (ts, 1), 0)
        cur = qpos_t // SEL_BLK
        forced = (blk_r == 0) | (blk_r == cur) | (blk_r == cur - 1)
        midx = lax.broadcasted_iota(jnp.int32, (rows, ncp), 1)
        cmask = (midx >= 1) & ((midx - 1) * CMP_STRIDE + CMP_LEN - 1 <= qpos)
        sidx = lax.broadcasted_iota(jnp.int32, (rows, tb), 1)
        new_ok = (sidx <= trow) & (sidx < t_valid)
        wl = win_ref.shape[0] // WIN_VECS
        dist_buf = trow + wl - lax.broadcasted_iota(jnp.int32, (rows, wl), 1)
        buf_ok = (dist_buf >= 0) & (dist_buf < WINDOW)
        dist_new = trow - sidx
        wnew_ok = (dist_new >= 0) & (dist_new < WINDOW) & (sidx < t_valid)
        p_parts = []
        for g in range(KV_GROUPS):
            s = lax.dot_general(q_of(g), kc_ref[0, g], _NT, preferred_element_type=F32)
            p_c = _softmax_rows(s, cmask)
            oc_sc[g] = jnp.dot(p_c.astype(BF16), kc_ref[0, KV_GROUPS + g], preferred_element_type=F32)
            psum = p_c[0:ts] + p_c[ts:2 * ts] + p_c[2 * ts:3 * ts] + p_c[3 * ts:4 * ts]
            p_hi = psum.astype(BF16)
            p_parts += [p_hi.astype(F32), psum - p_hi.astype(F32)]
        imp_parts = jnp.dot(jnp.concatenate(p_parts, axis=0).astype(BF16), c2s, preferred_element_type=F32)
        for g in range(KV_GROUPS):
            qg = q_of(g)
            imp = imp_parts[2 * g * ts:(2 * g + 1) * ts] + imp_parts[(2 * g + 1) * ts:(2 * g + 2) * ts]
            score = jnp.where(live, jnp.where(blk_r * SEL_BLK <= qpos_t, imp + jnp.where(forced, FORCE_BONUS, 0.0),
                                              NEG_BIG), NEG_DEAD)
            sel = jnp.zeros((ts, nsl), F32)
            for _ in range(N_SELECT):
                top = jnp.max(score, axis=1, keepdims=True)
                first = jnp.min(jnp.where(score == top, blk_f, 1e9), axis=1, keepdims=True)
                hit = live & (blk_f == first)
                sel = jnp.where(hit, 1.0, sel)
                score = jnp.where(hit, NEG_DEAD, score)
            for step in range(n_steps):
                sel_sc[g, step] = jnp.concatenate([sel[:, step * d:(step + 1) * d]] * r4, axis=0)
            knew = new_ref[:, g * d:(g + 1) * d]
            vnew = new_ref[:, (KV_GROUPS + g) * d:(KV_GROUPS + g + 1) * d]
            sm = jnp.where(new_ok, lax.dot_general(qg, knew, _NT, preferred_element_type=F32), NEG)
            m0 = jnp.max(sm, axis=1, keepdims=True)
            p = jnp.where(new_ok, jnp.exp(sm - m0), 0.0)
            m_sc[g] = m0
            l_sc[g] = jnp.sum(p, axis=1, keepdims=True)
            acc_sc[g] = jnp.dot(p.astype(BF16), vnew, preferred_element_type=F32)
            kwb = win_ref[pl.ds(g, wl, stride=WIN_VECS), :].astype(BF16)
            vwb = win_ref[pl.ds(KV_GROUPS + g, wl, stride=WIN_VECS), :].astype(BF16)
            kwn = new_ref[:, (2 * KV_GROUPS + g) * d:(2 * KV_GROUPS + g + 1) * d]
            vwn = new_ref[:, (3 * KV_GROUPS + g) * d:(3 * KV_GROUPS + g + 1) * d]
            s1 = jnp.where(buf_ok, lax.dot_general(qg, kwb, _NT, preferred_element_type=F32), NEG)
            s2 = jnp.where(wnew_ok, lax.dot_general(qg, kwn, _NT, preferred_element_type=F32), NEG)
            mw = jnp.maximum(jnp.max(s1, axis=1, keepdims=True), jnp.max(s2, axis=1, keepdims=True))
            p1 = jnp.where(buf_ok, jnp.exp(s1 - mw), 0.0)
            p2 = jnp.where(wnew_ok, jnp.exp(s2 - mw), 0.0)
            den = jnp.maximum(jnp.sum(p1, axis=1, keepdims=True) + jnp.sum(p2, axis=1, keepdims=True), TINY)
            ow_sc[g] = (jnp.dot((p1 / den).astype(BF16), vwb, preferred_element_type=F32)
                        + jnp.dot((p2 / den).astype(BF16), vwn, preferred_element_type=F32))

    half_lane = lax.broadcasted_iota(jnp.int32, (rows, PAGE_SIZE), 1) < SEL_BLK
    for g in range(KV_GROUPS):
        qg = q_of(g)
        kt = jnp.concatenate([_page_vecs(pg, 2 * KV_GROUPS + g) for pg in pages], axis=0).astype(BF16)
        vt = jnp.concatenate([_page_vecs(pg, 3 * KV_GROUPS + g) for pg in pages], axis=0).astype(BF16)
        s = lax.dot_general(qg, kt, _NT, preferred_element_type=F32)
        selg = sel_sc[g, i]
        picked = jnp.concatenate(
            [jnp.where(half_lane, selg[:, 2 * k:2 * k + 1], selg[:, 2 * k + 1:2 * k + 2])
             for k in range(SAMPLE_PAGES)], axis=1)
        mask = picked > 0.5
        sm = jnp.where(mask, s, NEG)
        m_old = m_sc[g]
        m_new = jnp.maximum(m_old, jnp.max(sm, axis=1, keepdims=True))
        alpha = jnp.exp(m_old - m_new)
        p = jnp.where(mask, jnp.exp(sm - m_new), 0.0)
        m_sc[g] = m_new
        l_sc[g] = alpha * l_sc[g] + jnp.sum(p, axis=1, keepdims=True)
        acc_sc[g] = alpha * acc_sc[g] + jnp.dot(p.astype(BF16), vt, preferred_element_type=F32)

    @pl.when(i == n_steps - 1)
    def _():
        gates = jax.nn.sigmoid(gate_ref[...])[:ts]
        for g in range(KV_GROUPS):
            o_s = acc_sc[g] / jnp.maximum(l_sc[g], TINY)
            o_c = oc_sc[g]
            o_w = ow_sc[g]
            for r in range(r4):
                h = g * r4 + r
                gate = lambda c: gates[:, SMALL_NG + 3 * h + c:SMALL_NG + 3 * h + c + 1]
                rs = slice(r * ts, (r + 1) * ts)
                o = gate(0) * o_c[rs] + gate(1) * o_s[rs] + gate(2) * o_w[rs]
                o_ref[0, :, h * d:(h + 1) * d] = jnp.concatenate(
                    [o, jnp.zeros((tb - ts, d), F32)], axis=0).astype(BF16)


def nsa_attention_sample(pt, cache2d, q_b, kvb, kcvc, win2d, zp, layer, db, t_pad, t_valid, past_len):
    ts = SAMPLE_TS
    assert t_valid <= SAMPLE_ROWS, (t_valid, SAMPLE_ROWS)
    n_steps = pt.shape[1] // SAMPLE_PAGES
    rows = Q_PER_KV * SAMPLE_ROWS
    wl_rows = win2d.shape[0] // (DEPTH * db)
    ncp = kcvc.shape[2]
    ns = past_len // SEL_BLK + 1
    per_step = SAMPLE_PAGES * PAGE_SIZE // SEL_BLK
    ratio = SEL_BLK // CMP_STRIDE
    lane = np.arange((n_steps + 1) * HEAD_DIM)[None, :]
    tok = np.arange(ncp)[:, None] - 1
    blk = (lane // HEAD_DIM) * per_step + lane % HEAD_DIM
    c2s = jnp.asarray(((lane % HEAD_DIM < per_step) & (blk < ns) & (tok >= 0) & (tok >= ratio * blk - 1)
                       & (tok <= ratio * blk + ratio - 1)).astype(np.float32), dtype=BF16)
    seq_rows = lambda width, col: pl.BlockSpec((ts, width), lambda bi, i, pt: (bi * (t_pad // ts), col))
    grid_spec = pltpu.PrefetchScalarGridSpec(
        num_scalar_prefetch=1,
        grid=(db, n_steps),
        in_specs=_page_specs() + [
            seq_rows(N_WIDTH, 0),
            seq_rows(ROW_COLS, 0),
            pl.BlockSpec((1,) + kcvc.shape[1:], lambda bi, i, pt: (bi, 0, 0, 0)),
            pl.BlockSpec((wl_rows, HEAD_DIM), lambda bi, i, pt: (layer * db + bi, 0)),
            seq_rows(HEAD_DIM, COL_SMALL // HEAD_DIM),
            pl.BlockSpec(c2s.shape, lambda bi, i, pt: (0, 0))],
        out_specs=pl.BlockSpec((1, ts, N_WIDTH), lambda bi, i, pt: (bi, 0, 0)),
        scratch_shapes=[pltpu.VMEM((KV_GROUPS, n_steps, rows, HEAD_DIM), F32),
                        pltpu.VMEM((KV_GROUPS, rows, 1), F32),
                        pltpu.VMEM((KV_GROUPS, rows, 1), F32),
                        pltpu.VMEM((KV_GROUPS, rows, HEAD_DIM), F32),
                        pltpu.VMEM((KV_GROUPS, rows, HEAD_DIM), F32),
                        pltpu.VMEM((KV_GROUPS, rows, HEAD_DIM), F32)])
    return pl.pallas_call(
        functools.partial(_nsa_attn_sample_kernel, past_len=past_len, t_valid=t_valid, n_steps=n_steps),
        grid_spec=grid_spec,
        out_shape=jax.ShapeDtypeStruct((db, ts, N_WIDTH), BF16),
        compiler_params=pltpu.CompilerParams(
            dimension_semantics=("arbitrary", "arbitrary"), vmem_limit_bytes=VMEM_LIMIT_BYTES),
        name="nsa_attention_sample",
    )(pt, *([cache2d] * SAMPLE_PAGES), q_b, kvb, kcvc, win2d, zp, c2s)


def nsa_sample_pallas(zp, db, t, t_pad, layer, cache2d, pt, win_state, cmp_w):
    pe, w1, w2 = cmp_w
    past_len = pt.shape[1] * PAGE_SIZE
    cos2, sin2 = rope_tables(past_len + jnp.arange(t_pad))
    q_b, rows_f, win_f, kvb, _ = nsa_prep_pallas(zp, db, t_pad, cos2, sin2, tm=t_pad)
    kcvc = nsa_compress_sample(pt, cache2d, pe, w1, w2, db)
    win2d = win_state.reshape(-1, HEAD_DIM)
    hn = nsa_attention_sample(pt, cache2d, q_b, kvb, kcvc, win2d, zp, layer, db, t_pad, t, past_len)
    rows = rows_f.reshape(db, t_pad, 4, KV_GROUPS, HEAD_DIM)[:, :t]
    win_new = win_f.reshape(db, t_pad, 2, KV_GROUPS, HEAD_DIM)[:, :t]
    wl = win_state.shape[2]
    win = jnp.concatenate([win_state[layer], win_new], axis=1)[:, -wl:]
    return hn[:, :t], (rows, win)


MIX_L = 128
GLA_SUB = 16


def _logsig(x):
    return jnp.minimum(x, 0.0) - jnp.log1p(jnp.exp(-jnp.abs(x)))


def _prefix_sum(x, axis):
    n = x.shape[axis]
    idx = lax.broadcasted_iota(jnp.int32, x.shape, axis)
    step = 1
    while step < n:
        x = x + jnp.where(idx >= step, pltpu.roll(x, step, axis), 0.0)
        step *= 2
    return x


def _mlstm_phases(t, uqk_ref, v_ref, og_ref, small_ref, convw_ref, bias_ref, norm_ref, conv0_ref, c0_ref, n0_ref,
                  m0_ref, h_ref, c_ref, n_ref, m_ref, prev_sc, c_sc, n_sc, m_sc, *, t_valid):
    L = MIX_L
    d = HEAD_DIM

    prev_sc[...] = jnp.zeros_like(prev_sc)
    prev_sc[L - 8:, :] = conv0_ref[0]
    c_sc[...] = c0_ref[0]
    n_sc[...] = n0_ref[0]
    m_sc[...] = m0_ref[0]
    yield

    x = uqk_ref[...]
    prev = prev_sc[...]
    row = lax.broadcasted_iota(jnp.int32, x.shape, 0)
    w = convw_ref[...]
    conv = w[CONV_W - 1:CONV_W] * x
    for k in range(1, CONV_W):
        shifted = jnp.where(row >= k, pltpu.roll(x, k, 0), pltpu.roll(prev, k, 0))
        conv = conv + w[CONV_W - 1 - k:CONV_W - k] * shifted
    prev_sc[...] = x
    act = conv * jax.nn.sigmoid(conv)

    pre = small_ref[...] + bias_ref[...]
    pos_c = t * L + lax.broadcasted_iota(jnp.int32, (L, HEAD_DIM), 0)
    ig_c = jnp.where(pos_c < t_valid, pre, NEG)
    b_c = _prefix_sum(jnp.where(pos_c < t_valid, _logsig(pre), 0.0), 0)
    pre_r = jnp.transpose(pre)[0:8]
    pos_r = t * L + lax.broadcasted_iota(jnp.int32, (8, L), 1)
    ig_r = jnp.where(pos_r < t_valid, pre_r, NEG)
    b_r = _prefix_sum(jnp.where(pos_r < t_valid, _logsig(pre_r), 0.0), 1)

    li = lax.broadcasted_iota(jnp.int32, (L, L), 0)
    si = lax.broadcasted_iota(jnp.int32, (L, L), 1)
    for h in range(H_M):
        sl = slice(h * d, (h + 1) * d)
        bcol, igcol = b_c[:, H_M + h:H_M + h + 1], ig_c[:, h:h + 1]
        brow, igrow = b_r[H_M + h:H_M + h + 1, :], ig_r[h:h + 1, :]
        m0 = m_sc[h:h + 1, 0:1]
        c0 = c_sc[h]
        n0 = n_sc[h:h + 1, :]
        qf = act[:, sl]
        kf = act[:, M_WIDTH + h * d:M_WIDTH + (h + 1) * d] * (d ** -0.5)
        vf = v_ref[:, sl]
        qb, kb, vb = qf.astype(BF16), kf.astype(BF16), vf.astype(BF16)
        logw = jnp.where(si <= li, bcol - brow + igrow, NEG)
        gsum = bcol + m0
        m_row = jnp.maximum(jnp.max(logw, axis=1, keepdims=True), gsum)
        wgt = jnp.exp(logw - m_row) * lax.dot_general(qb, kb, _NT, preferred_element_type=F32)
        inter = jnp.exp(gsum - m_row)
        num = (jnp.dot(wgt.astype(BF16), vb, preferred_element_type=F32)
               + inter * lax.dot_general(qb, c0.astype(BF16), _NT, preferred_element_type=F32))
        den = jnp.sum(wgt, axis=1, keepdims=True) + inter * jnp.sum(qf * n0, axis=1, keepdims=True)
        hh = num / jnp.maximum(jnp.abs(den), jnp.exp(-m_row))
        bl = bcol[L - 1:L, :]
        m_new = jnp.maximum(jnp.max(bl - brow + igrow, axis=1, keepdims=True), bl + m0)
        wa = jnp.exp(bl - bcol + igcol - m_new)
        wc = jnp.exp(bl + m0 - m_new)
        c_sc[h] = wc * c0 + jnp.dot(jnp.transpose(vf * wa).astype(BF16), kb, preferred_element_type=F32)
        n_sc[h:h + 1, :] = wc * n0 + jnp.sum(wa * kf, axis=0, keepdims=True)
        m_sc[h:h + 1, :] = jnp.broadcast_to(m_new, (1, d))
        out = _rms(hh, norm_ref[:, sl]) * jax.nn.sigmoid(og_ref[:, sl])
        h_ref[:, sl] = out.astype(BF16)
    yield

    c_ref[0] = c_sc[...]
    n_ref[0] = n_sc[...]
    m_ref[0] = m_sc[...]
    yield


def _gla_phases(t, q_ref, k_ref, v_ref, r_ref, small_ref, w2_ref, gb_ref, norm_ref, s0_ref, h_ref, s_ref, st_sc,
                *, t_valid):
    L = MIX_L
    d = HEAD_DIM

    for h in range(H_G):
        st_sc[h] = jnp.transpose(s0_ref[0, h])
    yield

    pre = jnp.dot(small_ref[...].astype(BF16), w2_ref[...], preferred_element_type=F32) + gb_ref[...]
    pos = t * L + lax.broadcasted_iota(jnp.int32, (L, G_WIDTH), 0)
    la = jnp.where(pos < t_valid, _logsig(pre) / GLA_GATE_TEMP, 0.0)
    bc_all = _prefix_sum(la, 0)
    rowi = lax.broadcasted_iota(jnp.int32, (L, d), 0)
    li = lax.broadcasted_iota(jnp.int32, (L, L), 0)
    si = lax.broadcasted_iota(jnp.int32, (L, L), 1)
    for h in range(H_G):
        sl = slice(h * d, (h + 1) * d)
        bc = bc_all[:, sl]
        qf = q_ref[:, sl] * (d ** -0.5)
        kf = jnp.where(t * L + rowi < t_valid, k_ref[:, sl], 0.0)
        vf = v_ref[:, sl]
        qparts, kparts = [], []
        for j in range(L // GLA_SUB):
            lo, hi = j * GLA_SUB, (j + 1) * GLA_SUB
            e = bc[hi - 1:hi, :]
            qj = qf[lo:] * jnp.exp(bc[lo:] - e)
            kj = kf[lo:hi] * jnp.exp(e - bc[lo:hi])
            zeros = lambda n: [jnp.zeros((n, d), F32)] if n else []
            qparts.append(jnp.concatenate(zeros(lo) + [qj], axis=0).astype(BF16))
            kparts.append(jnp.concatenate(zeros(lo) + [kj] + zeros(L - hi), axis=0).astype(BF16))
        a = lax.dot_general(jnp.concatenate(qparts, axis=1), jnp.concatenate(kparts, axis=1), _NT,
                            preferred_element_type=F32)
        a = jnp.where(si <= li, a, 0.0)
        st = st_sc[h]
        o = (jnp.dot(a.astype(BF16), vf.astype(BF16), preferred_element_type=F32)
             + lax.dot_general((qf * jnp.exp(bc)).astype(BF16), st.astype(BF16), _NT, preferred_element_type=F32))
        bl = bc[L - 1:L, :]
        kd = (kf * jnp.exp(bl - bc)).astype(BF16)
        st_sc[h] = st * jnp.exp(bl) + jnp.dot(jnp.transpose(vf).astype(BF16), kd, preferred_element_type=F32)
        gate = r_ref[:, sl]
        h_ref[:, sl] = (_rms(o, norm_ref[:, sl]) * (gate * jax.nn.sigmoid(gate))).astype(BF16)
    yield

    for h in range(H_G):
        s_ref[0, h] = jnp.transpose(st_sc[h])
    yield


N_MLSTM_IN, N_GLA_IN, N_MLSTM_OUT, N_GLA_OUT, N_MLSTM_SCRATCH = 11, 9, 4, 2, 4


MIX_SEQS = 2
N_GLA_SCRATCH = 1


def _mixers_kernel(*refs, t_valid):
    t = pl.program_id(1)
    n_in, n_scr = N_MLSTM_IN + N_GLA_IN, N_MLSTM_SCRATCH + N_GLA_SCRATCH
    n_out = N_MLSTM_OUT + N_GLA_OUT
    all_in, outs, all_scr = refs[:n_in * MIX_SEQS], refs[n_in * MIX_SEQS:][:n_out], refs[n_in * MIX_SEQS + n_out:]
    hm_ref, c_ref, n_ref, m_ref, hg_ref, s_ref = outs
    phases = []
    for s in range(MIX_SEQS):
        ins, scr = all_in[s * n_in:(s + 1) * n_in], all_scr[s * n_scr:(s + 1) * n_scr]
        one = pl.ds(s, 1)
        phases.append(_mlstm_phases(t, *ins[:N_MLSTM_IN], hm_ref.at[s], c_ref.at[one], n_ref.at[one], m_ref.at[one],
                                    *scr[:N_MLSTM_SCRATCH], t_valid=t_valid))
        phases.append(_gla_phases(t, *ins[N_MLSTM_IN:], hg_ref.at[s], s_ref.at[one], *scr[N_MLSTM_SCRATCH:],
                                  t_valid=t_valid))

    @pl.when(t == 0)
    def _():
        for p in phases:
            next(p)

    for p in phases:
        next(p)

    @pl.when(t == pl.num_programs(1) - 1)
    def _():
        for p in phases:
            next(p)


def mixers_pallas(z, b, t, conv_buf, c0, n0, m0, s0, conv_w, m_gate_b, m_norm, g_w2, g_b, g_norm):
    t_pad = z.shape[0] // b
    L = MIX_L
    nt = t_pad // L
    lanes = lambda a: jnp.pad(a.reshape(1, -1), ((0, 0), (0, HEAD_DIM - a.size)))
    conv0 = jnp.pad(conv_buf, ((0, 0), (8 - (CONV_W - 1), 0), (0, 0)))
    n0p = jnp.pad(n0, ((0, 0), (0, 8 - H_M), (0, 0)))
    m0p = jnp.broadcast_to(jnp.pad(m0, ((0, 0), (0, 8 - H_M)))[:, :, None], (b, 8, HEAD_DIM))
    w2_pad = jnp.pad(g_w2, ((SMALL_GLR, HEAD_DIM - SMALL_GLR - GATE_RANK), (0, 0))).astype(BF16)
    nseq = MIX_SEQS
    assert b % nseq == 0, (b, nseq)
    const2 =lambda shape: pl.BlockSpec(shape, lambda bi, ti: (0, 0))
    gcol = COL_GLA // G_WIDTH
    in_specs, operands = [], []
    for q in range(nseq):
        seq = lambda bi, q=q: bi * nseq + q
        rowblk = lambda width, col, seq=seq: pl.BlockSpec((L, width), lambda bi, ti: (seq(bi) * nt + ti, col))
        per_b3 = pl.BlockSpec((1, 8, HEAD_DIM), lambda bi, ti, seq=seq: (seq(bi), 0, 0))
        per_b4 = pl.BlockSpec((1, H_M, HEAD_DIM, HEAD_DIM), lambda bi, ti, seq=seq: (seq(bi), 0, 0, 0))
        small = rowblk(HEAD_DIM, COL_SMALL // HEAD_DIM)
        mlstm_in = [rowblk(2 * M_WIDTH, 0), rowblk(M_WIDTH, 2), rowblk(M_WIDTH, 3), small,
                    const2((CONV_W, 2 * M_WIDTH)), const2((1, HEAD_DIM)), const2((1, M_WIDTH)),
                    pl.BlockSpec((1, 8, 2 * M_WIDTH), lambda bi, ti, seq=seq: (seq(bi), 0, 0)),
                    per_b4, per_b3, per_b3]
        gla_in = [rowblk(G_WIDTH, gcol), rowblk(G_WIDTH, gcol + 1), rowblk(G_WIDTH, gcol + 2),
                  rowblk(G_WIDTH, gcol + 3), small, const2((HEAD_DIM, G_WIDTH)), const2((1, G_WIDTH)),
                  const2((1, G_WIDTH)), per_b4]
        assert (len(mlstm_in), len(gla_in)) == (N_MLSTM_IN, N_GLA_IN)
        in_specs += mlstm_in + gla_in
        operands += [z, z, z, z, conv_w, lanes(m_gate_b), m_norm.reshape(1, -1), conv0, c0, n0p, m0p,
                     z, z, z, z, z, w2_pad, g_b.reshape(1, -1), g_norm.reshape(1, -1), s0]
    nb = b // nseq
    out_rows = lambda width: pl.BlockSpec((nseq, L, width), lambda bi, ti: (bi, ti, 0))
    out_b3 = pl.BlockSpec((nseq, 8, HEAD_DIM), lambda bi, ti: (bi, 0, 0))
    out_b4 = pl.BlockSpec((nseq, H_M, HEAD_DIM, HEAD_DIM), lambda bi, ti: (bi, 0, 0, 0))
    out_specs = [out_rows(M_WIDTH), out_b4, out_b3, out_b3, out_rows(G_WIDTH), out_b4]
    state = jax.ShapeDtypeStruct((b, H_M, HEAD_DIM, HEAD_DIM), F32)
    tile = jax.ShapeDtypeStruct((b, 8, HEAD_DIM), F32)
    out_shape = [jax.ShapeDtypeStruct((b, t_pad, M_WIDTH), BF16), state, tile, tile,
                 jax.ShapeDtypeStruct((b, t_pad, G_WIDTH), BF16), state]
    seq_scratch = [pltpu.VMEM((L, 2 * M_WIDTH), F32),
                   pltpu.VMEM((H_M, HEAD_DIM, HEAD_DIM), F32),
                   pltpu.VMEM((8, HEAD_DIM), F32),
                   pltpu.VMEM((8, HEAD_DIM), F32),
                   pltpu.VMEM((H_G, HEAD_DIM, HEAD_DIM), F32)]
    hm, c, n, m, hg, s = pl.pallas_call(
        functools.partial(_mixers_kernel, t_valid=t),
        grid=(nb, nt),
        in_specs=in_specs,
        out_specs=out_specs,
        out_shape=out_shape,
        scratch_shapes=seq_scratch * nseq,
        compiler_params=pltpu.CompilerParams(
            dimension_semantics=("arbitrary", "arbitrary"), vmem_limit_bytes=VMEM_LIMIT_BYTES),
        name="mixers",
    )(*operands)
    return hm[:, :t], hg[:, :t], c, n[:, :H_M], m[:, :H_M, 0], s


def split_cols(z):
    cut = lambda a, n: z[..., a:a + n]
    u_qk, m_v, m_o = cut(0, 2 * M_WIDTH), cut(2 * M_WIDTH, M_WIDTH), cut(3 * M_WIDTH, M_WIDTH)
    n_q, n_kv = cut(COL_NQ, N_WIDTH), cut(COL_NKV, 6 * KV_GROUPS * HEAD_DIM)
    g_q, g_k, g_v, g_r = (cut(COL_GLA + i * G_WIDTH, G_WIDTH) for i in range(4))
    m_if = cut(COL_SMALL + SMALL_MIF, 2 * H_M)
    n_g = cut(COL_SMALL + SMALL_NG, 3 * H_N)
    g_lr = cut(COL_SMALL + SMALL_GLR, GATE_RANK)
    return u_qk, m_v, m_o, m_if, n_q, n_kv, n_g, g_q, g_k, g_v, g_r, g_lr


W_IN_ORDER = (0, 1, 2, 4, 5, 7, 8, 9, 10, 3, 6, 11)


def _regroup_w_in_kernel(wt_ref, o_ref):
    o = (0,) + SPLIT_OFFSETS + (IN_COLS,)
    tk = wt_ref.shape[1]
    at = 0
    small = []
    for i in W_IN_ORDER:
        width = o[i + 1] - o[i]
        if width % HEAD_DIM == 0:
            o_ref[:, at:at + width] = jnp.transpose(wt_ref[o[i]:o[i + 1], :]).astype(BF16)
            at += width
        else:
            small.append(wt_ref[o[i]:o[i + 1], :])
    used = sum(s.shape[0] for s in small)
    small.append(jnp.zeros((HEAD_DIM - used, tk), F32))
    o_ref[:, at:at + HEAD_DIM] = jnp.transpose(jnp.concatenate(small, axis=0)).astype(BF16)
    at += HEAD_DIM
    o_ref[:, at:] = jnp.zeros((tk, IN_COLS_PAD - at), BF16)


def regroup_w_in(w_in):
    depth, k, n = w_in.shape
    tk = 256
    return pl.pallas_call(
        _regroup_w_in_kernel,
        grid=(depth, k // tk),
        in_specs=[pl.BlockSpec((None, n, tk), lambda l, i: (l, 0, i))],
        out_specs=pl.BlockSpec((None, tk, IN_COLS_PAD), lambda l, i: (l, i, 0)),
        out_shape=jax.ShapeDtypeStruct((depth, k, IN_COLS_PAD), BF16),
        compiler_params=pltpu.CompilerParams(
            dimension_semantics=("arbitrary", "arbitrary"), vmem_limit_bytes=VMEM_LIMIT_BYTES),
        name="regroup_w_in",
    )(jnp.transpose(w_in, (0, 2, 1)))


def trunk_layer(x, conv_buf, c0, n0, m0, s0, nsa_fn, layer, g_norms, w_in, conv_w, m_gate_b, m_norm,
                g_w2, g_b, g_norm, w_out, w_ff1, w_ff2):
    b, t, d = x.shape
    x2 = x.reshape(b * t, d)
    z = norm_matmul(x2, g_norms[0], w_in, layer, IN_TN)
    z3 = z.reshape(b, t, -1)
    t_pad = -(-t // MIX_L) * MIX_L
    zp = z if t_pad == t else jnp.pad(z3, ((0, 0), (0, t_pad - t), (0, 0))).reshape(b * t_pad, -1)
    hm, hg, c, n, m, s = mixers_pallas(zp, b, t, conv_buf, c0, n0, m0, s0,
                                       conv_w, m_gate_b, m_norm, g_w2, g_b, g_norm)
    new_conv = z3[:, t - (CONV_W - 1):, :2 * M_WIDTH]
    hn, nsa_state = nsa_fn(zp, b, t, t_pad)
    x2 = matmul_norm_res(hm.reshape(b * t, -1), hn.reshape(b * t, -1), hg.reshape(b * t, -1),
                         w_out, layer, g_norms[1], x2)
    x2 = ffn(x2, g_norms[2], w_ff1, w_ff2, layer, g_norms[3])
    return x2.reshape(b, t, d), (nsa_state[0], nsa_state[1], c, n, m, new_conv, s)


def kernel(x_prompt, x_sample, cache_nsa_kv, state_nsa_win, state_mlstm_C, state_mlstm_n, state_mlstm_m, state_mlstm_conv, state_gla_S, page_table, norms, w_in, mlstm_conv_w, mlstm_gate_b, mlstm_norm, nsa_cmp_pe, nsa_cmp_w1, nsa_cmp_w2, gla_gate_w2, gla_gate_b, gla_norm, w_out, w_ff1, w_ff2):
    xp, xs = x_prompt, x_sample
    bp = xp.shape[0]
    conv0 = jnp.zeros((bp, CONV_W - 1, 2 * M_WIDTH), xp.dtype)
    c0 = jnp.zeros((bp, H_M, HEAD_DIM, HEAD_DIM), F32)
    n0 = jnp.zeros((bp, H_M, HEAD_DIM), F32)
    m0 = jnp.zeros((bp, H_M), F32)
    s0 = jnp.zeros((bp, H_G, HEAD_DIM, HEAD_DIM), F32)
    w_in_b = regroup_w_in(w_in)
    w_out_b = w_out.astype(BF16)
    w_ff1_b = w_ff1.astype(BF16)
    w_ff2_b = w_ff2.astype(BF16)
    cmp_pe = nsa_cmp_pe.reshape(DEPTH, 2, 1, CMP_LEN * HEAD_DIM)
    cmp_w1_b = nsa_cmp_w1.reshape(DEPTH, 2, CMP_LEN * HEAD_DIM, CMP_HID).astype(BF16)
    cmp_w2_b = nsa_cmp_w2.astype(BF16)
    n_pool = cache_nsa_kv.shape[1]
    cache2d = cache_nsa_kv.reshape(-1, HEAD_DIM)
    acc_p = [[] for _ in range(6)]
    acc_s = [[] for _ in range(7)]
    rows_all = None
    for l in range(DEPTH):
        weights = (l, norms[l], w_in_b, mlstm_conv_w[l], mlstm_gate_b[l], mlstm_norm[l],
                   gla_gate_w2[l], gla_gate_b[l], gla_norm[l], w_out_b, w_ff1_b, w_ff2_b)
        cmp_w = (cmp_pe[l], cmp_w1_b[l], cmp_w2_b[l])
        nsa_p = lambda zp, b, t, t_pad, l=l, cmp_w=cmp_w, rows_all=rows_all: nsa_prompt_pallas(
            zp, b, t, cmp_w, l, rows_all)
        nsa_s = lambda zp, b, t, t_pad, l=l, cmp_w=cmp_w: nsa_sample_pallas(
            zp, b, t, t_pad, l, cache2d, page_table + l * n_pool, state_nsa_win, cmp_w)
        xp, st_p = trunk_layer(xp, conv0, c0, n0, m0, s0, nsa_p, *weights)
        xs, st_s = trunk_layer(xs, state_mlstm_conv[l], state_mlstm_C[l], state_mlstm_n[l],
                               state_mlstm_m[l], state_gla_S[l], nsa_s, *weights)
        rows_all = st_p[0]
        for acc, a in zip(acc_p, st_p[1:]):
            acc.append(a.astype(xp.dtype))
        for acc, a in zip(acc_s, st_s):
            acc.append(a.astype(xs.dtype))
    nsa_rows_p = rows_all.reshape((DEPTH, bp, xp.shape[1], 4, KV_GROUPS, HEAD_DIM))
    nsa_win_p, mlstm_c_p, mlstm_n_p, mlstm_m_p, mlstm_conv_p, gla_s_p = [jnp.stack(a) for a in acc_p]
    nsa_rows_s, nsa_win_s, mlstm_c_s, mlstm_n_s, mlstm_m_s, mlstm_conv_s, gla_s_s = [jnp.stack(a) for a in acc_s]
    return (xp, xs, nsa_rows_p, nsa_rows_s, nsa_win_p, nsa_win_s, mlstm_c_p, mlstm_c_s,
            mlstm_n_p, mlstm_n_s, mlstm_m_p, mlstm_m_s, mlstm_conv_p, mlstm_conv_s, gla_s_p, gla_s_s)
```
